```python
import jax, jax.numpy as jnp
from jax import lax
import numpy as np

D_MODEL = 1024
BATCH = 4
SEQ = 4096
DEPTH = 2

CHUNK = 64
GLA_HEADS = 4
GLA_DK = 128
GLA_DV = 256
GLA_GATE_RANK = 16
GLA_TAU = 16.0
SG_WIDTH = 1024
SG_GROUPS = 4
SG_BLOCK = 128
MLA_HEADS = 8
MLA_Q_RANK = 384
MLA_KV_RANK = 256
MLA_NOPE = 128
MLA_ROPE = 64
MLA_V = 128
MLA_QK = MLA_NOPE + MLA_ROPE
ROPE_THETA = 10000.0
Q_BLOCK = 128
N_BRANCHES = 3
BRANCH_WIDTH = 1024
N_EXPERTS = 64
N_GROUPS = 8
TOPK_GROUPS = 4
TOP_K = 8
D_EXPERT = 256
D_SHARED = 256
ROUTED_SCALE = 2.5
EXPERT_BLOCK = 128
PLE_DIM = 256
ALPHA = (2 * DEPTH) ** 0.25
BETA = (8 * DEPTH) ** -0.25
IN_SPLIT = (GLA_HEADS * GLA_DK, GLA_HEADS * GLA_DK, GLA_HEADS * GLA_DV, GLA_HEADS * GLA_DV,
            GLA_GATE_RANK, 2 * SG_WIDTH, MLA_Q_RANK, MLA_KV_RANK + MLA_ROPE, N_BRANCHES * D_MODEL)
D_IN = sum(IN_SPLIT)

kernel_name = 'hybrid_gla_gmlp_mla_moe_deepnorm'


def layer_norm(x, g, b, eps=1e-5):
    xf = x.astype(jnp.float32)
    mu = jnp.mean(xf, -1, keepdims=True)
    var = jnp.mean(jnp.square(xf - mu), -1, keepdims=True)
    return ((xf - mu) * lax.rsqrt(var + eps)).astype(x.dtype) * g + b


def rms_norm(x, g, eps=1e-6):
    xf = x.astype(jnp.float32)
    return (xf * lax.rsqrt(jnp.mean(jnp.square(xf), -1, keepdims=True) + eps)).astype(x.dtype) * g


def rope(x, pos):
    half = x.shape[-1] // 2
    inv = ROPE_THETA ** (-jnp.arange(half, dtype=jnp.float32) / half)
    ang = pos.astype(jnp.float32)[:, None] * inv
    cos = jnp.cos(ang)[:, None, :]
    sin = jnp.sin(ang)[:, None, :]
    xf = x.astype(jnp.float32)
    x1, x2 = xf[..., :half], xf[..., half:]
    return jnp.concatenate([x1 * cos - x2 * sin, x1 * sin + x2 * cos], -1).astype(x.dtype)


def gla_chunked(q, k, v, glog):
    B, S, H, DK = q.shape
    DV = v.shape[-1]
    nc = S // CHUNK
    rs = lambda t: t.astype(jnp.float32).reshape(B, nc, CHUNK, H, t.shape[-1])
    q, k, v, glog = rs(q) * (DK ** -0.5), rs(k), rs(v), rs(glog)
    b = jnp.cumsum(glog, axis=2)
    b_last = b[:, :, -1:]
    q_dec = q * jnp.exp(b)
    k_inv = k * jnp.exp(-b)
    k_end = k * jnp.exp(b_last - b)
    causal = jnp.tril(jnp.ones((CHUNK, CHUNK), bool))
    att = jnp.where(causal, jnp.einsum('bcthd,bcshd->bchts', q_dec, k_inv), 0.0)
    o_intra = jnp.einsum('bchts,bcshv->bcthv', att, v)
    d_state = jnp.einsum('bcshk,bcshv->bchkv', k_end, v)
    decay = jnp.exp(b_last[:, :, 0])

    def step(state, inp):
        dec, ds = inp
        return dec[..., None] * state + ds, state

    init = jnp.zeros((B, H, DK, DV), jnp.float32)
    _, s_prev = lax.scan(step, init, (jnp.moveaxis(decay, 1, 0), jnp.moveaxis(d_state, 1, 0)))
    s_prev = jnp.moveaxis(s_prev, 0, 1)
    o_inter = jnp.einsum('bcthk,bchkv->bcthv', q_dec, s_prev)
    return (o_intra + o_inter).reshape(B, S, H, DV)


def spatial_gating(z, ln_g, ln_b, w_s, b_s):
    B, S, _ = z.shape
    u, v = jnp.split(jax.nn.gelu(z, approximate=False), 2, axis=-1)
    v = layer_norm(v, ln_g, ln_b).reshape(B, S // SG_BLOCK, SG_BLOCK, SG_GROUPS, SG_WIDTH // SG_GROUPS)
    w = jnp.where(jnp.tril(jnp.ones((SG_BLOCK, SG_BLOCK), bool)), w_s, 0)
    sv = jnp.einsum('gts,bnsgc->bntgc', w, v) + b_s.T[:, :, None]
    return u * sv.reshape(B, S, SG_WIDTH)


def mla_attention(q_down, kv_down, pos, qn_g, w_uq, kvn_g, w_ukv):
    B, S, _ = q_down.shape
    q = (rms_norm(q_down, qn_g) @ w_uq).reshape(B, S, MLA_HEADS, MLA_QK)
    c_kv, k_rope = jnp.split(kv_down, [MLA_KV_RANK], axis=-1)
    kv = (rms_norm(c_kv, kvn_g) @ w_ukv).reshape(B, S, MLA_HEADS, MLA_NOPE + MLA_V)
    k_nope, v = jnp.split(kv, [MLA_NOPE], axis=-1)
    q = jnp.concatenate([q[..., :MLA_NOPE], rope(q[..., MLA_NOPE:], pos)], -1)
    k_rope = jnp.broadcast_to(rope(k_rope[:, :, None, :], pos), (B, S, MLA_HEADS, MLA_ROPE))
    k = jnp.concatenate([k_nope, k_rope], -1)
    n_qb = S // Q_BLOCK
    qb = jnp.moveaxis(q.reshape(B, n_qb, Q_BLOCK, MLA_HEADS, MLA_QK), 1, 0)
    key_chunk = jnp.arange(S) // CHUNK
    scale = MLA_QK ** -0.5

    def attend(args):
        q_blk, qi = args
        s = jnp.einsum('bqhd,bkhd->bhqk', q_blk, k).astype(jnp.float32) * scale
        q_chunk = (qi * Q_BLOCK + jnp.arange(Q_BLOCK)) // CHUNK
        s = jnp.where(key_chunk[None, :] <= q_chunk[:, None], s, -jnp.inf)
        pr = jax.nn.softmax(s, axis=-1).astype(v.dtype)
        return jnp.einsum('bhqk,bkhd->bqhd', pr, v)

    o = lax.map(attend, (qb, jnp.arange(n_qb)))
    return jnp.moveaxis(o, 0, 1).reshape(B, S, MLA_HEADS * MLA_V)


def token_mixers(h, pos, w_in, fg_w, fg_b, gla_g, sg_g, sg_bn, sg_w, sg_b,
                 qn_g, w_uq, kvn_g, w_ukv, w_branch, w_out):
    B, S, _ = h.shape
    offs = np.cumsum(IN_SPLIT)[:-1].tolist()
    gq, gk, gv, gr, gfr, sgz, mq, mkv, gates = jnp.split(h @ w_in, offs, axis=-1)
    glog = jax.nn.log_sigmoid((gfr @ fg_w + fg_b).astype(jnp.float32)) / GLA_TAU
    o_a = gla_chunked(gq.reshape(B, S, GLA_HEADS, GLA_DK), gk.reshape(B, S, GLA_HEADS, GLA_DK),
                      gv.reshape(B, S, GLA_HEADS, GLA_DV), glog.reshape(B, S, GLA_HEADS, GLA_DK))
    o_a = rms_norm(o_a.astype(h.dtype), gla_g.reshape(GLA_HEADS, GLA_DV)).reshape(B, S, -1) * jax.nn.silu(gr)
    o_b = spatial_gating(sgz, sg_g, sg_bn, sg_w, sg_b)
    o_c = mla_attention(mq, mkv, pos, qn_g, w_uq, kvn_g, w_ukv)
    branches = jnp.stack([o_a, o_b, o_c], axis=2)
    proj = jnp.einsum('bsnc,ncd->bsnd', branches, w_branch)
    gate = jax.nn.sigmoid(gates).reshape(B, S, N_BRANCHES, D_MODEL)
    return (gate * proj).sum(2) @ w_out


def routed_experts(xt, idx, wts, w_gate, w_up, w_down):
    T, D = xt.shape
    A = T * TOP_K
    nb = -(-A // EXPERT_BLOCK) + N_EXPERTS
    e_flat = idx.reshape(A)
    tok = jnp.repeat(jnp.arange(T, dtype=jnp.int32), TOP_K)
    order = jnp.argsort(e_flat)
    e_sorted = e_flat[order]
    counts = jnp.bincount(e_flat, length=N_EXPERTS)
    padded = (counts + EXPERT_BLOCK - 1) // EXPERT_BLOCK * EXPERT_BLOCK
    pad_end = jnp.cumsum(padded)
    start = jnp.cumsum(counts) - counts
    dest = (pad_end - padded)[e_sorted] + jnp.arange(A) - start[e_sorted]
    slot_tok = jnp.full((nb * EXPERT_BLOCK,), T, jnp.int32).at[dest].set(tok[order])
    slot_w = jnp.zeros((nb * EXPERT_BLOCK,), wts.dtype).at[dest].set(wts.reshape(A)[order])
    blk_expert = jnp.minimum(jnp.searchsorted(pad_end, jnp.arange(nb) * EXPERT_BLOCK, side='right'),
                             N_EXPERTS - 1)
    x_pad = jnp.concatenate([xt, jnp.zeros((1, D), xt.dtype)], 0)

    def expert_block(args):
        tok_b, w_b, e = args
        xb = x_pad[tok_b]
        hb = jax.nn.silu(xb @ w_gate[e]) * (xb @ w_up[e])
        return (hb @ w_down[e]) * w_b[:, None]

    yb = lax.map(expert_block, (slot_tok.reshape(nb, EXPERT_BLOCK), slot_w.reshape(nb, EXPERT_BLOCK), blk_expert))
    y = jnp.zeros((T + 1, D), xt.dtype).at[slot_tok].add(yb.reshape(-1, D))
    return y[:T]


def moe_ffn(h, router_w, router_b, w_gate, w_up, w_down, sw_gate, sw_up, sw_down):
    B, S, D = h.shape
    T = B * S
    xt = h.reshape(T, D)
    scores = jax.nn.sigmoid((xt @ router_w).astype(jnp.float32))
    biased = scores + router_b.astype(jnp.float32)
    grp = biased.reshape(T, N_GROUPS, N_EXPERTS // N_GROUPS)
    grp_score = lax.top_k(grp, 2)[0].sum(-1)
    _, top_grp = lax.top_k(grp_score, TOPK_GROUPS)
    grp_mask = jax.nn.one_hot(top_grp, N_GROUPS).sum(1) > 0
    exp_mask = jnp.repeat(grp_mask, N_EXPERTS // N_GROUPS, axis=1)
    _, idx = lax.top_k(jnp.where(exp_mask, biased, -jnp.inf), TOP_K)
    wts = jnp.take_along_axis(scores, idx, axis=1)
    wts = wts / wts.sum(-1, keepdims=True) * ROUTED_SCALE
    routed = routed_experts(xt, idx, wts.astype(xt.dtype), w_gate, w_up, w_down)
    shared = (jax.nn.silu(xt @ sw_gate) * (xt @ sw_up)) @ sw_down
    return (routed + shared).reshape(B, S, D)


def setup_inputs(seed: int = 0) -> dict:
    key = jax.random.key(seed)
    ks = iter(jax.random.split(key, 40))
    L, D = DEPTH, D_MODEL

    def nrm(shape, scale):
        return jax.random.normal(next(ks), shape, jnp.float32) * scale

    def gain(shape):
        return 1.0 + nrm(shape, 0.02)

    return {
        'x': nrm((BATCH, SEQ, D), 1.0),
        'p': nrm((L, BATCH, SEQ, PLE_DIM), 1.0),
        'ln_in_g': gain((D,)),
        'ln_in_b': nrm((D,), 0.02),
        'w_in': nrm((L, D, D_IN), D ** -0.5),
        'gla_fg_w': nrm((L, GLA_GATE_RANK, GLA_HEADS * GLA_DK), GLA_GATE_RANK ** -0.5),
        'gla_fg_b': nrm((L, GLA_HEADS * GLA_DK), 0.1),
        'gla_norm_g': gain((L, GLA_HEADS * GLA_DV)),
        'sg_norm_g': gain((L, SG_WIDTH)),
        'sg_norm_b': nrm((L, SG_WIDTH), 0.02),
        'sg_w': nrm((L, SG_GROUPS, SG_BLOCK, SG_BLOCK), SG_BLOCK ** -0.5),
        'sg_b': gain((L, SG_GROUPS, SG_BLOCK)),
        'mla_qn_g': gain((L, MLA_Q_RANK)),
        'mla_w_uq': nrm((L, MLA_Q_RANK, MLA_HEADS * MLA_QK), MLA_Q_RANK ** -0.5),
        'mla_kvn_g': gain((L, MLA_KV_RANK)),
        'mla_w_ukv': nrm((L, MLA_KV_RANK, MLA_HEADS * (MLA_NOPE + MLA_V)), MLA_KV_RANK ** -0.5),
        'w_branch': nrm((L, N_BRANCHES, BRANCH_WIDTH, D), BRANCH_WIDTH ** -0.5),
        'w_out': nrm((L, D, D), BETA * D ** -0.5),
        'ln1_g': gain((L, D)),
        'ln1_b': nrm((L, D), 0.02),
        'router_w': nrm((L, D, N_EXPERTS), D ** -0.5),
        'router_b': nrm((L, N_EXPERTS), 0.01),
        'exp_w_gate': nrm((L, N_EXPERTS, D, D_EXPERT), D ** -0.5),
        'exp_w_up': nrm((L, N_EXPERTS, D, D_EXPERT), D ** -0.5),
        'exp_w_down': nrm((L, N_EXPERTS, D_EXPERT, D), BETA * D_EXPERT ** -0.5),
        'sh_w_gate': nrm((L, D, D_SHARED), D ** -0.5),
        'sh_w_up': nrm((L, D, D_SHARED), D ** -0.5),
        'sh_w_down': nrm((L, D_SHARED, D), BETA * D_SHARED ** -0.5),
        'ple_w_in': nrm((L, PLE_DIM, D), BETA * PLE_DIM ** -0.5),
        'ple_w_gate': nrm((L, D, D), D ** -0.5),
        'ln2_g': gain((L, D)),
        'ln2_b': nrm((L, D), 0.02),
    }


def reference(x, p, ln_in_g, ln_in_b, w_in, gla_fg_w, gla_fg_b, gla_norm_g, sg_norm_g, sg_norm_b,
              sg_w, sg_b, mla_qn_g, mla_w_uq, mla_kvn_g, mla_w_ukv, w_branch, w_out, ln1_g, ln1_b,
              router_w, router_b, exp_w_gate, exp_w_up, exp_w_down, sh_w_gate, sh_w_up, sh_w_down,
              ple_w_in, ple_w_gate, ln2_g, ln2_b):
    S = x.shape[1]
    pos = jnp.arange(S, dtype=jnp.int32)
    h = layer_norm(x, ln_in_g, ln_in_b)
    for i in range(DEPTH):
        mix = token_mixers(h, pos, w_in[i], gla_fg_w[i], gla_fg_b[i], gla_norm_g[i],
                           sg_norm_g[i], sg_norm_b[i], sg_w[i], sg_b[i],
                           mla_qn_g[i], mla_w_uq[i], mla_kvn_g[i], mla_w_ukv[i],
                           w_branch[i], w_out[i])
        h = layer_norm(ALPHA * h + mix, ln1_g[i], ln1_b[i])
        ffn = moe_ffn(h, router_w[i], router_b[i], exp_w_gate[i], exp_w_up[i], exp_w_down[i],
                      sh_w_gate[i], sh_w_up[i], sh_w_down[i])
        ple = (p[i] @ ple_w_in[i]) * jax.nn.sigmoid(h @ ple_w_gate[i])
        h = layer_norm(ALPHA * h + ffn + ple, ln2_g[i], ln2_b[i])
    return h
```

```python
import functools

import jax
import jax.numpy as jnp
from jax import lax
from jax.experimental import pallas as pl
from jax.experimental.pallas import tpu as pltpu

F32 = jnp.float32
BF16 = jnp.bfloat16

D_MODEL = 1024
DEPTH = 2
CHUNK = 64
GLA_HEADS, GLA_DK, GLA_DV, GLA_GATE_RANK, GLA_TAU = 4, 128, 256, 16, 16.0
SG_WIDTH, SG_GROUPS, SG_BLOCK = 1024, 4, 128
MLA_HEADS, MLA_Q_RANK, MLA_KV_RANK = 8, 384, 256
MLA_NOPE, MLA_ROPE, MLA_V = 128, 64, 128
MLA_QK = MLA_NOPE + MLA_ROPE
ROPE_THETA = 10000.0
N_BRANCHES = 3
N_EXPERTS, N_GROUPS, TOPK_GROUPS, TOP_K = 64, 8, 4, 8
GROUP_SIZE = N_EXPERTS // N_GROUPS
D_EXPERT, D_SHARED = 256, 256
ROUTED_SCALE = 2.5
PLE_DIM = 256
ALPHA = (2 * DEPTH) ** 0.25

LANES = 128
SUBLANES = 8
VMEM_LIMIT_BYTES = 56 * 1024 * 1024

P_SMALL = 1024
P_GFR = 384
P_CKV = 512
P_KROPE = 768
P_GATES = 1024
P_SG = 4096
P_GV = 6144
P_GR = 7168
P_GQ = 8192
P_GK = 8704
P_TOTAL = 9216
MLA_HEAD_PAD = 256

TM_LN = 512
TM_PROJ, TN_PROJ = 1024, 1024
R_GLA = 512
TM_SG = 256
TM_MLA = 512
TQ_ATT = 512
TM_MERGE = 512
TM_MOE = 512
CH_MOE = 128
EP_MOE = 2
TM_TAIL = 512


def _cparams(*sem):
    return pltpu.CompilerParams(dimension_semantics=sem, vmem_limit_bytes=VMEM_LIMIT_BYTES)


def _dot(a, b):
    return jnp.dot(a, b, preferred_element_type=F32)


def _dot_nt(a, b):
    return lax.dot_general(a, b, (((1,), (1,)), ((), ())), preferred_element_type=F32)


def _dot_tn(a, b):
    return lax.dot_general(a, b, (((0,), (0,)), ((), ())), preferred_element_type=F32)


def _layer_norm(x, g, b, eps=1e-5):
    mu = jnp.mean(x, axis=-1, keepdims=True)
    xc = x - mu
    var = jnp.mean(xc * xc, axis=-1, keepdims=True)
    return xc * lax.rsqrt(var + eps) * g + b


def _rms_norm(x, g, eps=1e-6):
    return x * lax.rsqrt(jnp.mean(x * x, axis=-1, keepdims=True) + eps) * g


def _sigmoid(x):
    return 1.0 / (1.0 + jnp.exp(-x))


def _silu(x):
    return x * _sigmoid(x)


def _split_bf16(x):
    hi = x.astype(BF16)
    lo = (x - hi.astype(F32)).astype(BF16)
    return hi, lo


def _ln_kernel(x_ref, g_ref, b_ref, o32_ref, o16_ref):
    y = _layer_norm(x_ref[...], g_ref[...], b_ref[...])
    o32_ref[...] = y
    o16_ref[...] = y.astype(BF16)


def _ln_call(x, g, b):
    t, d = x.shape
    row = pl.BlockSpec((TM_LN, d), lambda i: (i, 0))
    vec = pl.BlockSpec((1, d), lambda i: (0, 0))
    return pl.pallas_call(
        _ln_kernel,
        grid=(t // TM_LN,),
        in_specs=[row, vec, vec],
        out_specs=[row, row],
        out_shape=[jax.ShapeDtypeStruct((t, d), F32), jax.ShapeDtypeStruct((t, d), BF16)],
        compiler_params=_cparams("parallel"),
        name="ln_in",
    )(x, g.reshape(1, d), b.reshape(1, d))


def _proj_kernel(x_ref, w_ref, o_ref):
    o_ref[...] = _dot(x_ref[...], w_ref[...]).astype(o_ref.dtype)


def _proj_call(h16, w):
    t, k = h16.shape
    n = w.shape[1]
    return pl.pallas_call(
        _proj_kernel,
        grid=(t // TM_PROJ, n // TN_PROJ),
        in_specs=[pl.BlockSpec((TM_PROJ, k), lambda i, j: (i, 0)),
                  pl.BlockSpec((k, TN_PROJ), lambda i, j: (0, j))],
        out_specs=pl.BlockSpec((TM_PROJ, TN_PROJ), lambda i, j: (i, j)),
        out_shape=jax.ShapeDtypeStruct((t, n), BF16),
        compiler_params=_cparams("parallel", "arbitrary"),
        name="in_proj",
    )(h16, w)


def _prep_w_in(w_in):
    d = w_in.shape[0]
    gq, gk = w_in[:, 0:512], w_in[:, 512:1024]
    gv, gr = w_in[:, 1024:2048], w_in[:, 2048:3072]
    gfr = w_in[:, 3072:3088]
    sgz = w_in[:, 3088:5136]
    mq = w_in[:, 5136:5520]
    ckv = w_in[:, 5520:5776]
    kr = w_in[:, 5776:5840]
    gates = w_in[:, 5840:8912]
    half = MLA_ROPE // 2
    kr_rot = jnp.concatenate([-kr[:, half:], kr[:, :half]], axis=1)
    z = lambda n: jnp.zeros((d, n), w_in.dtype)
    small = jnp.concatenate([mq, gfr, z(112), ckv, kr, kr_rot, z(128)], axis=1)
    return jnp.concatenate([small, gates, sgz, gv, gr, gq, gk], axis=1).astype(BF16)


def _gla_kernel(q_ref, k_ref, v_ref, gr_ref, gf_ref, fgw_ref, fgb_ref, g_ref, o_ref,
                st_ref, qd_ref, ki_ref, ke_ref, dec_ref):
    rows = q_ref.shape[0]

    @pl.when(pl.program_id(1) == 0)
    def _():
        st_ref[...] = jnp.zeros_like(st_ref)

    pre = _dot(gf_ref[...], fgw_ref[...]) + fgb_ref[...]
    glog = -(jnp.maximum(-pre, 0.0) + jnp.log1p(jnp.exp(-jnp.abs(pre)))) * (1.0 / GLA_TAU)

    r = lax.broadcasted_iota(jnp.int32, (rows, rows), 0)
    c = lax.broadcasted_iota(jnp.int32, (rows, rows), 1)
    same = lax.shift_right_logical(r, 6) == lax.shift_right_logical(c, 6)
    tri = jnp.where(same & (c <= r), 1.0, 0.0).astype(BF16)
    blk = jnp.where(same, 1.0, 0.0).astype(BF16)
    hi, lo = _split_bf16(glog)
    bcum = _dot(tri, hi) + _dot(tri, lo)
    btot = _dot(blk, hi) + _dot(blk, lo)

    q = q_ref[...].astype(F32) * (GLA_DK ** -0.5)
    k = k_ref[...].astype(F32)
    qd_ref[...] = (q * jnp.exp(bcum)).astype(BF16)
    ki_ref[...] = (k * jnp.exp(-bcum)).astype(BF16)
    ke_ref[...] = (k * jnp.exp(btot - bcum)).astype(BF16)
    dec_ref[...] = jnp.exp(btot)

    tr = lax.broadcasted_iota(jnp.int32, (CHUNK, CHUNK), 0)
    tc = lax.broadcasted_iota(jnp.int32, (CHUNK, CHUNK), 1)
    causal = tc <= tr

    def chunk_body(ci, carry):
        r0 = pl.multiple_of(ci * CHUNK, CHUNK)
        rs = pl.ds(r0, CHUNK)
        for hd in range(GLA_HEADS):
            ks = slice(hd * GLA_DK, (hd + 1) * GLA_DK)
            vs = slice(hd * GLA_DV, (hd + 1) * GLA_DV)
            qd = qd_ref[rs, ks]
            ki = ki_ref[rs, ks]
            ke = ke_ref[rs, ks]
            v = v_ref[rs, vs]
            att = jnp.where(causal, _dot_nt(qd, ki), 0.0)
            st = st_ref[hd]
            o = _dot(att.astype(BF16), v) + _dot_nt(qd, st.astype(BF16))
            dec = dec_ref[pl.ds(r0, 1), ks]
            st_ref[hd] = st * dec + _dot_tn(v, ke)
            on = _rms_norm(o, g_ref[:, vs])
            gate = gr_ref[rs, vs].astype(F32)
            o_ref[rs, vs] = (on * _silu(gate)).astype(o_ref.dtype)
        return carry

    lax.fori_loop(0, rows // CHUNK, chunk_body, 0)


def _gla_call(proj, fg_w, fg_b, gla_g, batch, seq):
    t = proj.shape[0]
    nr = seq // R_GLA
    row = lambda i, j: i * nr + j
    fgw = jnp.zeros((LANES, GLA_HEADS * GLA_DK), F32).at[:GLA_GATE_RANK].set(fg_w).astype(BF16)
    hk = GLA_HEADS * GLA_DK
    hv = GLA_HEADS * GLA_DV
    return pl.pallas_call(
        _gla_kernel,
        grid=(batch, nr),
        in_specs=[
            pl.BlockSpec((R_GLA, hk), lambda i, j: (row(i, j), P_GQ // hk)),
            pl.BlockSpec((R_GLA, hk), lambda i, j: (row(i, j), P_GK // hk)),
            pl.BlockSpec((R_GLA, hv), lambda i, j: (row(i, j), P_GV // hv)),
            pl.BlockSpec((R_GLA, hv), lambda i, j: (row(i, j), P_GR // hv)),
            pl.BlockSpec((R_GLA, LANES), lambda i, j: (row(i, j), P_GFR // LANES)),
            pl.BlockSpec((LANES, hk), lambda i, j: (0, 0)),
            pl.BlockSpec((1, hk), lambda i, j: (0, 0)),
            pl.BlockSpec((1, hv), lambda i, j: (0, 0)),
        ],
        out_specs=pl.BlockSpec((R_GLA, hv), lambda i, j: (row(i, j), 0)),
        out_shape=jax.ShapeDtypeStruct((t, hv), BF16),
        scratch_shapes=[
            pltpu.VMEM((GLA_HEADS, GLA_DV, GLA_DK), F32),
            pltpu.VMEM((R_GLA, hk), BF16),
            pltpu.VMEM((R_GLA, hk), BF16),
            pltpu.VMEM((R_GLA, hk), BF16),
            pltpu.VMEM((R_GLA, hk), F32),
        ],
        compiler_params=_cparams("parallel", "arbitrary"),
        name="gla",
    )(proj, proj, proj, proj, proj, fgw, fg_b.reshape(1, hk), gla_g.reshape(1, hv))


def _gelu(x):
    return 0.5 * x * (1.0 + lax.erf(x * (2.0 ** -0.5)))


def _sg_kernel(u_ref, v_ref, g_ref, b_ref, ws_ref, bias_ref, o_ref):
    rows = u_ref.shape[0]
    gw = SG_WIDTH // SG_GROUPS
    u = _gelu(u_ref[...].astype(F32))
    v = _gelu(v_ref[...].astype(F32))
    vn = _layer_norm(v, g_ref[...], b_ref[...]).astype(BF16)
    tr = lax.broadcasted_iota(jnp.int32, (SG_BLOCK, SG_BLOCK), 0)
    tc = lax.broadcasted_iota(jnp.int32, (SG_BLOCK, SG_BLOCK), 1)
    tril = tc <= tr
    for g in range(SG_GROUPS):
        w = jnp.where(tril, ws_ref[g], 0.0).astype(BF16)
        cs = slice(g * gw, (g + 1) * gw)
        for blk in range(rows // SG_BLOCK):
            rs = slice(blk * SG_BLOCK, (blk + 1) * SG_BLOCK)
            sv = _dot(w, vn[rs, cs]) + bias_ref[:, cs]
            o_ref[rs, cs] = (u[rs, cs] * sv).astype(o_ref.dtype)


def _sg_call(proj, sg_g, sg_bn, sg_w, sg_b):
    t = proj.shape[0]
    gw = SG_WIDTH // SG_GROUPS
    bias = jnp.repeat(sg_b.T, gw, axis=1)
    return pl.pallas_call(
        _sg_kernel,
        grid=(t // TM_SG,),
        in_specs=[
            pl.BlockSpec((TM_SG, SG_WIDTH), lambda i: (i, P_SG // SG_WIDTH)),
            pl.BlockSpec((TM_SG, SG_WIDTH), lambda i: (i, P_SG // SG_WIDTH + 1)),
            pl.BlockSpec((1, SG_WIDTH), lambda i: (0, 0)),
            pl.BlockSpec((1, SG_WIDTH), lambda i: (0, 0)),
            pl.BlockSpec((SG_GROUPS, SG_BLOCK, SG_BLOCK), lambda i: (0, 0, 0)),
            pl.BlockSpec((SG_BLOCK, SG_WIDTH), lambda i: (0, 0)),
        ],
        out_specs=pl.BlockSpec((TM_SG, SG_WIDTH), lambda i: (i, 0)),
        out_shape=jax.ShapeDtypeStruct((t, SG_WIDTH), BF16),
        compiler_params=_cparams("parallel"),
        name="spatial_gating",
    )(proj, proj, sg_g.reshape(1, -1), sg_bn.reshape(1, -1), sg_w, bias)


def _mla_prep_kernel(sm_ref, cs_ref, qg_ref, kg_ref, wq_ref, wkv_ref, q_ref, k_ref, v_ref):
    rows = sm_ref.shape[0]
    mq = sm_ref[:, 0:MLA_Q_RANK].astype(F32)
    qn = _rms_norm(mq, qg_ref[...]).astype(BF16)
    qf = _dot(qn, wq_ref[...])
    ckv = sm_ref[:, P_CKV:P_CKV + MLA_KV_RANK].astype(F32)
    cn = _rms_norm(ckv, kg_ref[...]).astype(BF16)
    kv = _dot(cn, wkv_ref[...])

    cs = cs_ref[...]
    lane = lax.broadcasted_iota(jnp.int32, (rows, LANES), 1)
    low = lane < MLA_ROPE

    def rope(pair):
        t = pair * cs
        return jnp.where(low, t + pltpu.roll(t, MLA_ROPE, 1), 0.0)

    scale = MLA_QK ** -0.5
    kr = rope(sm_ref[:, P_KROPE:P_KROPE + LANES].astype(F32)).astype(BF16)
    for h in range(MLA_HEADS):
        c0 = h * MLA_HEAD_PAD
        q_ref[:, c0:c0 + MLA_NOPE] = (qf[:, c0:c0 + MLA_NOPE] * scale).astype(BF16)
        q_ref[:, c0 + MLA_NOPE:c0 + MLA_HEAD_PAD] = (
            rope(qf[:, c0 + MLA_NOPE:c0 + MLA_HEAD_PAD]) * scale).astype(BF16)
        k_ref[:, c0:c0 + MLA_NOPE] = kv[:, h * MLA_NOPE:(h + 1) * MLA_NOPE].astype(BF16)
        k_ref[:, c0 + MLA_NOPE:c0 + MLA_HEAD_PAD] = kr
    v_ref[...] = kv[:, MLA_HEADS * MLA_NOPE:].astype(BF16)


def _mla_prep_call(proj, qn_g, w_uq, kvn_g, w_ukv, seq):
    t = proj.shape[0]
    half = MLA_ROPE // 2
    wq = w_uq.reshape(MLA_Q_RANK, MLA_HEADS, MLA_QK)
    wr = wq[:, :, MLA_NOPE:]
    wr_rot = jnp.concatenate([-wr[:, :, half:], wr[:, :, :half]], axis=2)
    wq_ext = jnp.concatenate([wq[:, :, :MLA_NOPE], wr, wr_rot], axis=2)
    wq_ext = wq_ext.reshape(MLA_Q_RANK, MLA_HEADS * MLA_HEAD_PAD).astype(BF16)
    wkv = w_ukv.reshape(MLA_KV_RANK, MLA_HEADS, MLA_NOPE + MLA_V)
    wkv = jnp.concatenate([wkv[:, :, :MLA_NOPE].reshape(MLA_KV_RANK, -1),
                           wkv[:, :, MLA_NOPE:].reshape(MLA_KV_RANK, -1)], axis=1).astype(BF16)
    inv = ROPE_THETA ** (-jnp.arange(half, dtype=F32) / half)
    ang = jnp.arange(seq, dtype=F32)[:, None] * inv
    cs = jnp.concatenate([jnp.cos(ang), jnp.cos(ang), jnp.sin(ang), jnp.sin(ang)], axis=1)
    ns = seq // TM_MLA
    hq = MLA_HEADS * MLA_HEAD_PAD
    hv = MLA_HEADS * MLA_V
    return pl.pallas_call(
        _mla_prep_kernel,
        grid=(t // TM_MLA,),
        in_specs=[
            pl.BlockSpec((TM_MLA, P_SMALL), lambda i: (i, 0)),
            pl.BlockSpec((TM_MLA, LANES), lambda i: (i % ns, 0)),
            pl.BlockSpec((1, MLA_Q_RANK), lambda i: (0, 0)),
            pl.BlockSpec((1, MLA_KV_RANK), lambda i: (0, 0)),
            pl.BlockSpec((MLA_Q_RANK, hq), lambda i: (0, 0)),
            pl.BlockSpec((MLA_KV_RANK, MLA_HEADS * (MLA_NOPE + MLA_V)), lambda i: (0, 0)),
        ],
        out_specs=[
            pl.BlockSpec((TM_MLA, hq), lambda i: (i, 0)),
            pl.BlockSpec((TM_MLA, hq), lambda i: (i, 0)),
            pl.BlockSpec((TM_MLA, hv), lambda i: (i, 0)),
        ],
        out_shape=[jax.ShapeDtypeStruct((t, hq), BF16), jax.ShapeDtypeStruct((t, hq), BF16),
                   jax.ShapeDtypeStruct((t, hv), BF16)],
        compiler_params=_cparams("parallel"),
        name="mla_prep",
    )(proj, cs, qn_g.reshape(1, -1), kvn_g.reshape(1, -1), wq_ext, wkv)


def _attn_kernel(q_ref, k_ref, v_ref, o_ref, m_ref, l_ref, acc_ref):
    tq = q_ref.shape[0]
    i = pl.program_id(2)
    q = q_ref[...]
    m_ref[...] = jnp.full_like(m_ref, -jnp.inf)
    l_ref[...] = jnp.zeros_like(l_ref)
    acc_ref[...] = jnp.zeros_like(acc_ref)

    def step(j, mask):
        r0 = pl.multiple_of(j * tq, tq)
        s = _dot_nt(q, k_ref[pl.ds(r0, tq), :])
        if mask is not None:
            s = jnp.where(mask, s, -jnp.inf)
        m_old = m_ref[...]
        m_new = jnp.maximum(m_old, jnp.max(s, axis=-1, keepdims=True))
        alpha = jnp.exp(m_old - m_new)
        p = jnp.exp(s - m_new)
        l_ref[...] = alpha * l_ref[...] + jnp.sum(p, axis=-1, keepdims=True)
        acc_ref[...] = alpha * acc_ref[...] + _dot(p.astype(BF16), v_ref[pl.ds(r0, tq), :])
        m_ref[...] = m_new

    def body(j, carry):
        step(j, None)
        return carry

    lax.fori_loop(0, i, body, 0)
    rr = lax.broadcasted_iota(jnp.int32, (tq, tq), 0)
    cc = lax.broadcasted_iota(jnp.int32, (tq, tq), 1)
    step(i, lax.shift_right_logical(cc, 6) <= lax.shift_right_logical(rr, 6))
    o_ref[...] = (acc_ref[...] / l_ref[...]).astype(o_ref.dtype)


def _attn_call(q, k, v, batch, seq):
    t = q.shape[0]
    nq = seq // TQ_ATT
    return pl.pallas_call(
        _attn_kernel,
        grid=(batch, MLA_HEADS, nq),
        in_specs=[
            pl.BlockSpec((TQ_ATT, MLA_HEAD_PAD), lambda b, h, i: (b * nq + i, h)),
            pl.BlockSpec((seq, MLA_HEAD_PAD), lambda b, h, i: (b, h)),
            pl.BlockSpec((seq, MLA_V), lambda b, h, i: (b, h)),
        ],
        out_specs=pl.BlockSpec((TQ_ATT, MLA_V), lambda b, h, i: (b * nq + i, h)),
        out_shape=jax.ShapeDtypeStruct((t, MLA_HEADS * MLA_V), BF16),
        scratch_shapes=[pltpu.VMEM((TQ_ATT, 1), F32), pltpu.VMEM((TQ_ATT, 1), F32),
                        pltpu.VMEM((TQ_ATT, MLA_V), F32)],
        compiler_params=_cparams("parallel", "parallel", "arbitrary"),
        name="mla_attention",
    )(q, k, v)


def _merge_kernel(oa_ref, ob_ref, oc_ref, g0_ref, g1_ref, g2_ref, h_ref, wb_ref, wo_ref,
                  lg_ref, lb_ref, o32_ref, o16_ref):
    merged = _sigmoid(g0_ref[...].astype(F32)) * _dot(oa_ref[...], wb_ref[0])
    merged += _sigmoid(g1_ref[...].astype(F32)) * _dot(ob_ref[...], wb_ref[1])
    merged += _sigmoid(g2_ref[...].astype(F32)) * _dot(oc_ref[...], wb_ref[2])
    mix = _dot(merged.astype(BF16), wo_ref[...])
    y = _layer_norm(ALPHA * h_ref[...] + mix, lg_ref[...], lb_ref[...])
    o32_ref[...] = y
    o16_ref[...] = y.astype(BF16)


def _merge_call(o_a, o_b, o_c, proj, h32, w_branch, w_out, ln_g, ln_b):
    t, d = h32.shape
    row = pl.BlockSpec((TM_MERGE, d), lambda i: (i, 0))
    gate = lambda n: pl.BlockSpec((TM_MERGE, d), lambda i: (i, P_GATES // d + n))
    vec = pl.BlockSpec((1, d), lambda i: (0, 0))
    return pl.pallas_call(
        _merge_kernel,
        grid=(t // TM_MERGE,),
        in_specs=[row, row, row, gate(0), gate(1), gate(2), row,
                  pl.BlockSpec((N_BRANCHES, d, d), lambda i: (0, 0, 0)),
                  pl.BlockSpec((d, d), lambda i: (0, 0)), vec, vec],
        out_specs=[row, row],
        out_shape=[jax.ShapeDtypeStruct((t, d), F32), jax.ShapeDtypeStruct((t, d), BF16)],
        compiler_params=_cparams("parallel"),
        name="merge",
    )(o_a, o_b, o_c, proj, proj, proj, h32, w_branch.astype(BF16), w_out.astype(BF16),
      ln_g.reshape(1, d), ln_b.reshape(1, d))


def _first_argmax_mask(vals, iota, n):
    m = jnp.max(vals, axis=0, keepdims=True)
    idx = jnp.min(jnp.where(vals == m, iota, n), axis=0, keepdims=True)
    return iota == idx


def _router_kernel(h_ref, wt_ref, b_ref, comb_ref, rank_ref, cnt_ref):
    tm = h_ref.shape[0]
    h = h_ref[...]
    h_hi, h_lo = _split_bf16(h)
    w = wt_ref[...]
    w_hi, w_lo = _split_bf16(w)
    logits = _dot_nt(w_hi, h_hi) + _dot_nt(w_hi, h_lo) + _dot_nt(w_lo, h_hi)
    scores = _sigmoid(logits)
    biased = scores + b_ref[...]

    neg = -jnp.inf
    sub = lax.broadcasted_iota(jnp.int32, (GROUP_SIZE, tm), 0)
    grp_rows = []
    for g in range(N_GROUPS):
        blk = biased[g * GROUP_SIZE:(g + 1) * GROUP_SIZE, :]
        m1 = jnp.max(blk, axis=0, keepdims=True)
        first = _first_argmax_mask(blk, sub, GROUP_SIZE)
        m2 = jnp.max(jnp.where(first, neg, blk), axis=0, keepdims=True)
        grp_rows.append(m1 + m2)
    gs = jnp.concatenate(grp_rows, axis=0)
    gsel = jnp.zeros((N_GROUPS, tm), jnp.bool_)
    gi = lax.broadcasted_iota(jnp.int32, (N_GROUPS, tm), 0)
    for _ in range(TOPK_GROUPS):
        pick = _first_argmax_mask(gs, gi, N_GROUPS)
        gsel = gsel | pick
        gs = jnp.where(pick, neg, gs)
    emask = jnp.concatenate(
        [jnp.broadcast_to(gsel[g:g + 1, :], (GROUP_SIZE, tm)) for g in range(N_GROUPS)], axis=0)
    cand = jnp.where(emask, biased, neg)
    ei = lax.broadcasted_iota(jnp.int32, (N_EXPERTS, tm), 0)
    chosen = jnp.zeros((N_EXPERTS, tm), jnp.bool_)
    for _ in range(TOP_K):
        pick = _first_argmax_mask(cand, ei, N_EXPERTS)
        chosen = chosen | pick
        cand = jnp.where(pick, neg, cand)
    wsel = jnp.where(chosen, scores, 0.0)
    comb_ref[...] = wsel / jnp.sum(wsel, axis=0, keepdims=True) * ROUTED_SCALE

    r = lax.broadcasted_iota(jnp.int32, (tm, tm), 0)
    c = lax.broadcasted_iota(jnp.int32, (tm, tm), 1)
    upper = jnp.where(r < c, 1.0, 0.0).astype(BF16)
    sel = jnp.where(chosen, 1.0, 0.0)
    rank_ref[...] = _dot(sel.astype(BF16), upper)
    cnt = jnp.sum(sel, axis=1, keepdims=True)
    cnt_ref[...] = jnp.broadcast_to(cnt, (N_EXPERTS, LANES)).astype(jnp.int32)


def _router_call(h32, router_w, router_b):
    t, d = h32.shape
    nt = t // TM_MOE
    bias = jnp.broadcast_to(router_b.reshape(N_EXPERTS, 1), (N_EXPERTS, TM_MOE))
    comb, rank, cnt = pl.pallas_call(
        _router_kernel,
        grid=(nt,),
        in_specs=[pl.BlockSpec((TM_MOE, d), lambda i: (i, 0)),
                  pl.BlockSpec((N_EXPERTS, d), lambda i: (0, 0)),
                  pl.BlockSpec((N_EXPERTS, TM_MOE), lambda i: (0, 0))],
        out_specs=[pl.BlockSpec((N_EXPERTS, TM_MOE), lambda i: (0, i)),
                   pl.BlockSpec((N_EXPERTS, TM_MOE), lambda i: (0, i)),
                   pl.BlockSpec((N_EXPERTS, LANES), lambda i: (i, 0))],
        out_shape=[jax.ShapeDtypeStruct((N_EXPERTS, t), F32),
                   jax.ShapeDtypeStruct((N_EXPERTS, t), F32),
                   jax.ShapeDtypeStruct((nt * N_EXPERTS, LANES), jnp.int32)],
        compiler_params=_cparams("parallel"),
        name="router",
    )(h32, router_w.T, bias)
    return comb, rank, cnt[:, 0].reshape(nt, N_EXPERTS)


def _moe_kernel(cnt_ref, x_ref, comb_ref, rank_ref, wgu_ref, wd_ref, y_ref):
    tm = x_ref.shape[0]
    i = pl.program_id(0)
    p = pl.program_id(1)

    @pl.when(p == 0)
    def _():
        y_ref[...] = jnp.zeros_like(y_ref)

    cmax = cnt_ref[i, EP_MOE * p]
    for ee in range(1, EP_MOE):
        cmax = jnp.maximum(cmax, cnt_ref[i, EP_MOE * p + ee])

    for c in range(tm // CH_MOE):
        @pl.when(cmax > c * CH_MOE)
        def _():
            slot = (lax.broadcasted_iota(jnp.int32, (CH_MOE, tm), 0) + c * CH_MOE).astype(F32)
            pws, outs = [], []
            for ee in range(EP_MOE):
                e = EP_MOE * p + ee
                rrow = rank_ref[pl.ds(e, 1), :]
                wrow = comb_ref[pl.ds(e, 1), :]
                hit = (rrow == slot) & (wrow > 0.0)
                onehot = jnp.where(hit, 1.0, 0.0).astype(BF16)
                pws.append(jnp.where(hit, wrow, 0.0).astype(BF16))
                xg = _dot(onehot, x_ref[...]).astype(BF16)
                gu = _dot(xg, wgu_ref[ee])
                hmid = _silu(gu[:, :D_EXPERT]) * gu[:, D_EXPERT:]
                outs.append(_dot(hmid.astype(BF16), wd_ref[ee]).astype(BF16))
            pw = jnp.concatenate(pws, axis=0)
            out = jnp.concatenate(outs, axis=0)
            y_ref[...] += _dot_tn(pw, out)


def _moe_call(h16, comb, rank, cnt, w_gate, w_up, w_down):
    t, d = h16.shape
    nt = t // TM_MOE
    wgu = jnp.concatenate([w_gate, w_up], axis=2).astype(BF16)
    wd = w_down.astype(BF16)
    grid_spec = pltpu.PrefetchScalarGridSpec(
        num_scalar_prefetch=1,
        grid=(nt, N_EXPERTS // EP_MOE),
        in_specs=[
            pl.BlockSpec((TM_MOE, d), lambda i, p, cnt: (i, 0)),
            pl.BlockSpec((N_EXPERTS, TM_MOE), lambda i, p, cnt: (0, i)),
            pl.BlockSpec((N_EXPERTS, TM_MOE), lambda i, p, cnt: (0, i)),
            pl.BlockSpec((EP_MOE, d, 2 * D_EXPERT), lambda i, p, cnt: (p, 0, 0)),
            pl.BlockSpec((EP_MOE, D_EXPERT, d), lambda i, p, cnt: (p, 0, 0)),
        ],
        out_specs=pl.BlockSpec((TM_MOE, d), lambda i, p, cnt: (i, 0)),
    )
    return pl.pallas_call(
        _moe_kernel,
        grid_spec=grid_spec,
        out_shape=jax.ShapeDtypeStruct((t, d), F32),
        compiler_params=_cparams("parallel", "arbitrary"),
        name="moe_routed",
    )(cnt, h16, comb, rank, wgu, wd)


def _tail_kernel(y_ref, h16_ref, h32_ref, p_ref, swgu_ref, swd_ref, pwi_ref, pwg_ref,
                 lg_ref, lb_ref, o32_ref, o16_ref):
    x = h16_ref[...]
    gu = _dot(x, swgu_ref[...])
    shared = _dot((_silu(gu[:, :D_SHARED]) * gu[:, D_SHARED:]).astype(BF16), swd_ref[...])
    ple = _dot(p_ref[...].astype(BF16), pwi_ref[...]) * _sigmoid(_dot(x, pwg_ref[...]))
    z = ALPHA * h32_ref[...] + y_ref[...] + shared + ple
    y = _layer_norm(z, lg_ref[...], lb_ref[...])
    o32_ref[...] = y
    o16_ref[...] = y.astype(BF16)


def _tail_call(y, h16, h32, p, sw_gate, sw_up, sw_down, ple_w_in, ple_w_gate, ln_g, ln_b):
    t, d = h32.shape
    row = pl.BlockSpec((TM_TAIL, d), lambda i: (i, 0))
    vec = pl.BlockSpec((1, d), lambda i: (0, 0))
    full = lambda a, b: pl.BlockSpec((a, b), lambda i: (0, 0))
    swgu = jnp.concatenate([sw_gate, sw_up], axis=1).astype(BF16)
    return pl.pallas_call(
        _tail_kernel,
        grid=(t // TM_TAIL,),
        in_specs=[row, row, row, pl.BlockSpec((TM_TAIL, PLE_DIM), lambda i: (i, 0)),
                  full(d, 2 * D_SHARED), full(D_SHARED, d), full(PLE_DIM, d), full(d, d), vec, vec],
        out_specs=[row, row],
        out_shape=[jax.ShapeDtypeStruct((t, d), F32), jax.ShapeDtypeStruct((t, d), BF16)],
        compiler_params=_cparams("parallel"),
        name="ffn_tail",
    )(y, h16, h32, p, swgu, sw_down.astype(BF16), ple_w_in.astype(BF16), ple_w_gate.astype(BF16),
      ln_g.reshape(1, d), ln_b.reshape(1, d))


def kernel(x, p, ln_in_g, ln_in_b, w_in, gla_fg_w, gla_fg_b, gla_norm_g, sg_norm_g, sg_norm_b, sg_w, sg_b, mla_qn_g, mla_w_uq, mla_kvn_g, mla_w_ukv, w_branch, w_out, ln1_g, ln1_b, router_w, router_b, exp_w_gate, exp_w_up, exp_w_down, sh_w_gate, sh_w_up, sh_w_down, ple_w_in, ple_w_gate, ln2_g, ln2_b):
    batch, seq, d = x.shape
    t = batch * seq
    depth = w_in.shape[0]
    h32, h16 = _ln_call(x.reshape(t, d), ln_in_g, ln_in_b)
    for i in range(depth):
        proj = _proj_call(h16, _prep_w_in(w_in[i]))
        o_a = _gla_call(proj, gla_fg_w[i], gla_fg_b[i], gla_norm_g[i], batch, seq)
        o_b = _sg_call(proj, sg_norm_g[i], sg_norm_b[i], sg_w[i], sg_b[i])
        q, k, v = _mla_prep_call(proj, mla_qn_g[i], mla_w_uq[i], mla_kvn_g[i], mla_w_ukv[i], seq)
        o_c = _attn_call(q, k, v, batch, seq)
        h32, h16 = _merge_call(o_a, o_b, o_c, proj, h32, w_branch[i], w_out[i], ln1_g[i], ln1_b[i])
        comb, rank, cnt = _router_call(h32, router_w[i], router_b[i])
        y = _moe_call(h16, comb, rank, cnt, exp_w_gate[i], exp_w_up[i], exp_w_down[i])
        h32, h16 = _tail_call(y, h16, h32, p[i].reshape(t, -1), sh_w_gate[i], sh_w_up[i],
                              sh_w_down[i], ple_w_in[i], ple_w_gate[i], ln2_g[i], ln2_b[i])
    return h32.reshape(batch, seq, d)
```

```python
import functools

import jax
import jax.numpy as jnp
from jax import lax
from jax.experimental import pallas as pl
from jax.experimental.pallas import tpu as pltpu

F32 = jnp.float32
BF16 = jnp.bfloat16

D_MODEL = 1024
DEPTH = 2
CHUNK = 64
GLA_HEADS, GLA_DK, GLA_DV, GLA_GATE_RANK, GLA_TAU = 4, 128, 256, 16, 16.0
SG_WIDTH, SG_GROUPS, SG_BLOCK = 1024, 4, 128
MLA_HEADS, MLA_Q_RANK, MLA_KV_RANK = 8, 384, 256
MLA_NOPE, MLA_ROPE, MLA_V = 128, 64, 128
MLA_QK = MLA_NOPE + MLA_ROPE
ROPE_THETA = 10000.0
N_BRANCHES = 3
N_EXPERTS, N_GROUPS, TOPK_GROUPS, TOP_K = 64, 8, 4, 8
GROUP_SIZE = N_EXPERTS // N_GROUPS
D_EXPERT, D_SHARED = 256, 256
ROUTED_SCALE = 2.5
PLE_DIM = 256
ALPHA = (2 * DEPTH) ** 0.25

LANES = 128
SUBLANES = 8
VMEM_LIMIT_BYTES = 56 * 1024 * 1024

P_SMALL = 1024
P_GFR = 384
P_CKV = 512
P_KROPE = 768
P_GATES = 1024
P_SG = 4096
P_GV = 6144
P_GR = 7168
P_GQ = 8192
P_GK = 8704
P_TOTAL = 9216
MLA_HEAD_PAD = 256

TM_LN = 512
TM_PROJ, TN_PROJ = 1024, 1024
R_GLA = 512
TM_SG = 256
TQ_ATT = 512
TM_MLA = TQ_ATT
HEADS_ATT = 2
LOG2_E = 1.4426950408889634
TM_MERGE = 512
TM_MOE = 512
CH_MOE = 128
EP_MOE = 2
TM_TAIL = 512


def _cparams(*sem):
    return pltpu.CompilerParams(dimension_semantics=sem, vmem_limit_bytes=VMEM_LIMIT_BYTES)


def _dot(a, b):
    return jnp.dot(a, b, preferred_element_type=F32)


def _dot_nt(a, b):
    return lax.dot_general(a, b, (((1,), (1,)), ((), ())), preferred_element_type=F32)


def _dot_tn(a, b):
    return lax.dot_general(a, b, (((0,), (0,)), ((), ())), preferred_element_type=F32)


def _layer_norm(x, g, b, eps=1e-5):
    mu = jnp.mean(x, axis=-1, keepdims=True)
    xc = x - mu
    var = jnp.mean(xc * xc, axis=-1, keepdims=True)
    return xc * lax.rsqrt(var + eps) * g + b


def _rms_norm(x, g, eps=1e-6):
    return x * lax.rsqrt(jnp.mean(x * x, axis=-1, keepdims=True) + eps) * g


def _sigmoid(x):
    return 1.0 / (1.0 + jnp.exp(-x))


def _silu(x):
    return x * _sigmoid(x)


def _split_bf16(x):
    hi = x.astype(BF16)
    lo = (x - hi.astype(F32)).astype(BF16)
    return hi, lo


def _ln_kernel(x_ref, g_ref, b_ref, o32_ref, o16_ref):
    y = _layer_norm(x_ref[...], g_ref[...], b_ref[...])
    o32_ref[...] = y
    o16_ref[...] = y.astype(BF16)


def _ln_call(x, g, b):
    t, d = x.shape
    row = pl.BlockSpec((TM_LN, d), lambda i: (i, 0))
    vec = pl.BlockSpec((1, d), lambda i: (0, 0))
    return pl.pallas_call(
        _ln_kernel,
        grid=(t // TM_LN,),
        in_specs=[row, vec, vec],
        out_specs=[row, row],
        out_shape=[jax.ShapeDtypeStruct((t, d), F32), jax.ShapeDtypeStruct((t, d), BF16)],
        compiler_params=_cparams("parallel"),
        name="ln_in",
    )(x, g.reshape(1, d), b.reshape(1, d))


def _proj_kernel(x_ref, w_ref, o_ref):
    o_ref[...] = _dot(x_ref[...], w_ref[...]).astype(o_ref.dtype)


def _proj_call(h16, w):
    t, k = h16.shape
    n = w.shape[1]
    return pl.pallas_call(
        _proj_kernel,
        grid=(t // TM_PROJ, n // TN_PROJ),
        in_specs=[pl.BlockSpec((TM_PROJ, k), lambda i, j: (i, 0)),
                  pl.BlockSpec((k, TN_PROJ), lambda i, j: (0, j))],
        out_specs=pl.BlockSpec((TM_PROJ, TN_PROJ), lambda i, j: (i, j)),
        out_shape=jax.ShapeDtypeStruct((t, n), BF16),
        compiler_params=_cparams("parallel", "arbitrary"),
        name="in_proj",
    )(h16, w)


def _prep_w_in(w_in):
    d = w_in.shape[0]
    gq, gk = w_in[:, 0:512], w_in[:, 512:1024]
    gv, gr = w_in[:, 1024:2048], w_in[:, 2048:3072]
    gfr = w_in[:, 3072:3088]
    sgz = w_in[:, 3088:5136]
    mq = w_in[:, 5136:5520]
    ckv = w_in[:, 5520:5776]
    kr = w_in[:, 5776:5840]
    gates = w_in[:, 5840:8912]
    half = MLA_ROPE // 2
    kr_rot = jnp.concatenate([-kr[:, half:], kr[:, :half]], axis=1)
    z = lambda n: jnp.zeros((d, n), w_in.dtype)
    small = jnp.concatenate([mq, gfr, z(112), ckv, kr, kr_rot, z(128)], axis=1)
    return jnp.concatenate([small, gates, sgz, gv, gr, gq, gk], axis=1).astype(BF16)


def _gla_kernel(q_ref, k_ref, v_ref, gr_ref, gf_ref, fgw_ref, fgb_ref, g_ref, o_ref,
                st_ref, qd_ref, ki_ref, ke_ref, dec_ref):
    rows = q_ref.shape[0]

    @pl.when(pl.program_id(1) == 0)
    def _():
        st_ref[...] = jnp.zeros_like(st_ref)

    pre = _dot(gf_ref[...], fgw_ref[...]) + fgb_ref[...]
    glog = -(jnp.maximum(-pre, 0.0) + jnp.log1p(jnp.exp(-jnp.abs(pre)))) * (1.0 / GLA_TAU)

    r = lax.broadcasted_iota(jnp.int32, (rows, rows), 0)
    c = lax.broadcasted_iota(jnp.int32, (rows, rows), 1)
    same = lax.shift_right_logical(r, 6) == lax.shift_right_logical(c, 6)
    tri = jnp.where(same & (c <= r), 1.0, 0.0).astype(BF16)
    blk = jnp.where(same, 1.0, 0.0).astype(BF16)
    hi, lo = _split_bf16(glog)
    bcum = _dot(tri, hi) + _dot(tri, lo)
    btot = _dot(blk, hi) + _dot(blk, lo)

    q = q_ref[...].astype(F32) * (GLA_DK ** -0.5)
    k = k_ref[...].astype(F32)
    qd_ref[...] = (q * jnp.exp(bcum)).astype(BF16)
    ki_ref[...] = (k * jnp.exp(-bcum)).astype(BF16)
    ke_ref[...] = (k * jnp.exp(btot - bcum)).astype(BF16)
    dec_ref[...] = jnp.exp(btot)

    tr = lax.broadcasted_iota(jnp.int32, (CHUNK, CHUNK), 0)
    tc = lax.broadcasted_iota(jnp.int32, (CHUNK, CHUNK), 1)
    causal = tc <= tr

    def chunk_body(ci, carry):
        r0 = pl.multiple_of(ci * CHUNK, CHUNK)
        rs = pl.ds(r0, CHUNK)
        for hd in range(GLA_HEADS):
            ks = slice(hd * GLA_DK, (hd + 1) * GLA_DK)
            vs = slice(hd * GLA_DV, (hd + 1) * GLA_DV)
            qd = qd_ref[rs, ks]
            ki = ki_ref[rs, ks]
            ke = ke_ref[rs, ks]
            v = v_ref[rs, vs]
            att = jnp.where(causal, _dot_nt(qd, ki), 0.0)
            st = st_ref[hd]
            o = _dot(att.astype(BF16), v) + _dot_nt(qd, st.astype(BF16))
            dec = dec_ref[pl.ds(r0, 1), ks]
            st_ref[hd] = st * dec + _dot_tn(v, ke)
            on = _rms_norm(o, g_ref[:, vs])
            gate = gr_ref[rs, vs].astype(F32)
            o_ref[rs, vs] = (on * _silu(gate)).astype(o_ref.dtype)
        return carry

    lax.fori_loop(0, rows // CHUNK, chunk_body, 0)


def _gla_call(proj, fg_w, fg_b, gla_g, batch, seq):
    t = proj.shape[0]
    nr = seq // R_GLA
    row = lambda i, j: i * nr + j
    fgw = jnp.zeros((LANES, GLA_HEADS * GLA_DK), F32).at[:GLA_GATE_RANK].set(fg_w).astype(BF16)
    hk = GLA_HEADS * GLA_DK
    hv = GLA_HEADS * GLA_DV
    return pl.pallas_call(
        _gla_kernel,
        grid=(batch, nr),
        in_specs=[
            pl.BlockSpec((R_GLA, hk), lambda i, j: (row(i, j), P_GQ // hk)),
            pl.BlockSpec((R_GLA, hk), lambda i, j: (row(i, j), P_GK // hk)),
            pl.BlockSpec((R_GLA, hv), lambda i, j: (row(i, j), P_GV // hv)),
            pl.BlockSpec((R_GLA, hv), lambda i, j: (row(i, j), P_GR // hv)),
            pl.BlockSpec((R_GLA, LANES), lambda i, j: (row(i, j), P_GFR // LANES)),
            pl.BlockSpec((LANES, hk), lambda i, j: (0, 0)),
            pl.BlockSpec((1, hk), lambda i, j: (0, 0)),
            pl.BlockSpec((1, hv), lambda i, j: (0, 0)),
        ],
        out_specs=pl.BlockSpec((R_GLA, hv), lambda i, j: (row(i, j), 0)),
        out_shape=jax.ShapeDtypeStruct((t, hv), BF16),
        scratch_shapes=[
            pltpu.VMEM((GLA_HEADS, GLA_DV, GLA_DK), F32),
            pltpu.VMEM((R_GLA, hk), BF16),
            pltpu.VMEM((R_GLA, hk), BF16),
            pltpu.VMEM((R_GLA, hk), BF16),
            pltpu.VMEM((R_GLA, hk), F32),
        ],
        compiler_params=_cparams("parallel", "arbitrary"),
        name="gla",
    )(proj, proj, proj, proj, proj, fgw, fg_b.reshape(1, hk), gla_g.reshape(1, hv))


def _gelu(x):
    return 0.5 * x * (1.0 + lax.erf(x * (2.0 ** -0.5)))


def _sg_kernel(u_ref, v_ref, g_ref, b_ref, ws_ref, bias_ref, o_ref):
    rows = u_ref.shape[0]
    gw = SG_WIDTH // SG_GROUPS
    u = _gelu(u_ref[...].astype(F32))
    v = _gelu(v_ref[...].astype(F32))
    vn = _layer_norm(v, g_ref[...], b_ref[...]).astype(BF16)
    tr = lax.broadcasted_iota(jnp.int32, (SG_BLOCK, SG_BLOCK), 0)
    tc = lax.broadcasted_iota(jnp.int32, (SG_BLOCK, SG_BLOCK), 1)
    tril = tc <= tr
    for g in range(SG_GROUPS):
        w = jnp.where(tril, ws_ref[g], 0.0).astype(BF16)
        cs = slice(g * gw, (g + 1) * gw)
        for blk in range(rows // SG_BLOCK):
            rs = slice(blk * SG_BLOCK, (blk + 1) * SG_BLOCK)
            sv = _dot(w, vn[rs, cs]) + bias_ref[:, cs]
            o_ref[rs, cs] = (u[rs, cs] * sv).astype(o_ref.dtype)


def _sg_call(proj, sg_g, sg_bn, sg_w, sg_b):
    t = proj.shape[0]
    gw = SG_WIDTH // SG_GROUPS
    bias = jnp.repeat(sg_b.T, gw, axis=1)
    return pl.pallas_call(
        _sg_kernel,
        grid=(t // TM_SG,),
        in_specs=[
            pl.BlockSpec((TM_SG, SG_WIDTH), lambda i: (i, P_SG // SG_WIDTH)),
            pl.BlockSpec((TM_SG, SG_WIDTH), lambda i: (i, P_SG // SG_WIDTH + 1)),
            pl.BlockSpec((1, SG_WIDTH), lambda i: (0, 0)),
            pl.BlockSpec((1, SG_WIDTH), lambda i: (0, 0)),
            pl.BlockSpec((SG_GROUPS, SG_BLOCK, SG_BLOCK), lambda i: (0, 0, 0)),
            pl.BlockSpec((SG_BLOCK, SG_WIDTH), lambda i: (0, 0)),
        ],
        out_specs=pl.BlockSpec((TM_SG, SG_WIDTH), lambda i: (i, 0)),
        out_shape=jax.ShapeDtypeStruct((t, SG_WIDTH), BF16),
        compiler_params=_cparams("parallel"),
        name="spatial_gating",
    )(proj, proj, sg_g.reshape(1, -1), sg_bn.reshape(1, -1), sg_w, bias)


def _mla_prep_kernel(sm_ref, cs_ref, qg_ref, kg_ref, wq_ref, wkv_ref, q_ref, k_ref, vt_ref):
    rows = sm_ref.shape[0]
    mq = sm_ref[:, 0:MLA_Q_RANK].astype(F32)
    qn = _rms_norm(mq, qg_ref[...]).astype(BF16)
    qf = _dot(qn, wq_ref[...])
    ckv = sm_ref[:, P_CKV:P_CKV + MLA_KV_RANK].astype(F32)
    cn = _rms_norm(ckv, kg_ref[...]).astype(BF16)
    kv = _dot(cn, wkv_ref[...])

    cs = cs_ref[...]
    lane = lax.broadcasted_iota(jnp.int32, (rows, LANES), 1)
    low = lane < MLA_ROPE

    def rope(pair):
        t = pair * cs
        return jnp.where(low, t + pltpu.roll(t, MLA_ROPE, 1), 0.0)

    scale = (MLA_QK ** -0.5) * LOG2_E
    kr = rope(sm_ref[:, P_KROPE:P_KROPE + LANES].astype(F32)).astype(BF16)
    for h in range(MLA_HEADS):
        c0 = h * MLA_HEAD_PAD
        q_ref[:, c0:c0 + MLA_NOPE] = (qf[:, c0:c0 + MLA_NOPE] * scale).astype(BF16)
        q_ref[:, c0 + MLA_NOPE:c0 + MLA_HEAD_PAD] = (
            rope(qf[:, c0 + MLA_NOPE:c0 + MLA_HEAD_PAD]) * scale).astype(BF16)
        k_ref[:, c0:c0 + MLA_NOPE] = kv[:, h * MLA_NOPE:(h + 1) * MLA_NOPE].astype(BF16)
        k_ref[:, c0 + MLA_NOPE:c0 + MLA_HEAD_PAD] = kr
    vt_ref[0] = kv[:, MLA_HEADS * MLA_NOPE:].T.astype(BF16)


def _mla_prep_call(proj, qn_g, w_uq, kvn_g, w_ukv, seq):
    t = proj.shape[0]
    half = MLA_ROPE // 2
    wq = w_uq.reshape(MLA_Q_RANK, MLA_HEADS, MLA_QK)
    wr = wq[:, :, MLA_NOPE:]
    wr_rot = jnp.concatenate([-wr[:, :, half:], wr[:, :, :half]], axis=2)
    wq_ext = jnp.concatenate([wq[:, :, :MLA_NOPE], wr, wr_rot], axis=2)
    wq_ext = wq_ext.reshape(MLA_Q_RANK, MLA_HEADS * MLA_HEAD_PAD).astype(BF16)
    wkv = w_ukv.reshape(MLA_KV_RANK, MLA_HEADS, MLA_NOPE + MLA_V)
    wkv = jnp.concatenate([wkv[:, :, :MLA_NOPE].reshape(MLA_KV_RANK, -1),
                           wkv[:, :, MLA_NOPE:].reshape(MLA_KV_RANK, -1)], axis=1).astype(BF16)
    inv = ROPE_THETA ** (-jnp.arange(half, dtype=F32) / half)
    ang = jnp.arange(seq, dtype=F32)[:, None] * inv
    cs = jnp.concatenate([jnp.cos(ang), jnp.cos(ang), jnp.sin(ang), jnp.sin(ang)], axis=1)
    ns = seq // TM_MLA
    hq = MLA_HEADS * MLA_HEAD_PAD
    hv = MLA_HEADS * MLA_V
    return pl.pallas_call(
        _mla_prep_kernel,
        grid=(t // TM_MLA,),
        in_specs=[
            pl.BlockSpec((TM_MLA, P_SMALL), lambda i: (i, 0)),
            pl.BlockSpec((TM_MLA, LANES), lambda i: (i % ns, 0)),
            pl.BlockSpec((1, MLA_Q_RANK), lambda i: (0, 0)),
            pl.BlockSpec((1, MLA_KV_RANK), lambda i: (0, 0)),
            pl.BlockSpec((MLA_Q_RANK, hq), lambda i: (0, 0)),
            pl.BlockSpec((MLA_KV_RANK, MLA_HEADS * (MLA_NOPE + MLA_V)), lambda i: (0, 0)),
        ],
        out_specs=[
            pl.BlockSpec((TM_MLA, hq), lambda i: (i, 0)),
            pl.BlockSpec((TM_MLA, hq), lambda i: (i, 0)),
            pl.BlockSpec((1, hv, TM_MLA), lambda i: (i, 0, 0)),
        ],
        out_shape=[jax.ShapeDtypeStruct((t, hq), BF16), jax.ShapeDtypeStruct((t, hq), BF16),
                   jax.ShapeDtypeStruct((t // TM_MLA, hv, TM_MLA), BF16)],
        compiler_params=_cparams("parallel"),
        name="mla_prep",
    )(proj, cs, qn_g.reshape(1, -1), kvn_g.reshape(1, -1), wq_ext, wkv)


def _attn_kernel(q_ref, k_ref, vt_ref, o_ref, acc_ref):
    tq = q_ref.shape[0]
    i = pl.program_id(2)
    acc_ref[...] = jnp.zeros_like(acc_ref)

    def step(j, ms, ls, mask):
        r0 = pl.multiple_of(j * tq, tq)
        new_m, new_l = [], []
        for hh in range(HEADS_ATT):
            qs = slice(hh * MLA_HEAD_PAD, (hh + 1) * MLA_HEAD_PAD)
            s = _dot_nt(k_ref[pl.ds(r0, tq), qs], q_ref[:, qs])
            if mask is not None:
                s = jnp.where(mask, s, -jnp.inf)
            m_new = jnp.maximum(ms[hh], jnp.max(s, axis=0, keepdims=True))
            alpha = jnp.exp2(ms[hh] - m_new)
            p = jnp.exp2(s - m_new)
            new_l.append(alpha * ls[hh] + jnp.sum(p, axis=0, keepdims=True))
            vt = vt_ref[j, hh * MLA_V:(hh + 1) * MLA_V, :]
            acc_ref[hh] = alpha * acc_ref[hh] + _dot(vt, p.astype(BF16))
            new_m.append(m_new)
        return tuple(new_m), tuple(new_l)

    m0 = tuple(jnp.full((1, tq), -jnp.inf, F32) for _ in range(HEADS_ATT))
    l0 = tuple(jnp.zeros((1, tq), F32) for _ in range(HEADS_ATT))
    ms, ls = lax.fori_loop(0, i, lambda j, c: step(j, c[0], c[1], None), (m0, l0))
    kk = lax.broadcasted_iota(jnp.int32, (tq, tq), 0)
    qq = lax.broadcasted_iota(jnp.int32, (tq, tq), 1)
    visible = lax.shift_right_logical(kk, 6) <= lax.shift_right_logical(qq, 6)
    ms, ls = step(i, ms, ls, visible)
    for hh in range(HEADS_ATT):
        o_ref[:, hh * MLA_V:(hh + 1) * MLA_V] = (acc_ref[hh] / ls[hh]).T.astype(o_ref.dtype)


def _attn_call(q, k, vt, batch, seq):
    t = q.shape[0]
    nq = seq // TQ_ATT
    qw = HEADS_ATT * MLA_HEAD_PAD
    vw = HEADS_ATT * MLA_V
    return pl.pallas_call(
        _attn_kernel,
        grid=(batch, MLA_HEADS // HEADS_ATT, nq),
        in_specs=[
            pl.BlockSpec((TQ_ATT, qw), lambda b, h, i: (b * nq + i, h)),
            pl.BlockSpec((seq, qw), lambda b, h, i: (b, h)),
            pl.BlockSpec((nq, vw, TQ_ATT), lambda b, h, i: (b, h, 0)),
        ],
        out_specs=pl.BlockSpec((TQ_ATT, vw), lambda b, h, i: (b * nq + i, h)),
        out_shape=jax.ShapeDtypeStruct((t, MLA_HEADS * MLA_V), BF16),
        scratch_shapes=[pltpu.VMEM((HEADS_ATT, MLA_V, TQ_ATT), F32)],
        compiler_params=_cparams("parallel", "parallel", "arbitrary"),
        name="mla_attention",
    )(q, k, vt)


def _merge_kernel(oa_ref, ob_ref, oc_ref, g0_ref, g1_ref, g2_ref, h_ref, wb_ref, wo_ref,
                  lg_ref, lb_ref, o32_ref, o16_ref):
    merged = _sigmoid(g0_ref[...].astype(F32)) * _dot(oa_ref[...], wb_ref[0])
    merged += _sigmoid(g1_ref[...].astype(F32)) * _dot(ob_ref[...], wb_ref[1])
    merged += _sigmoid(g2_ref[...].astype(F32)) * _dot(oc_ref[...], wb_ref[2])
    mix = _dot(merged.astype(BF16), wo_ref[...])
    y = _layer_norm(ALPHA * h_ref[...] + mix, lg_ref[...], lb_ref[...])
    o32_ref[...] = y
    o16_ref[...] = y.astype(BF16)


def _merge_call(o_a, o_b, o_c, proj, h32, w_branch, w_out, ln_g, ln_b):
    t, d = h32.shape
    row = pl.BlockSpec((TM_MERGE, d), lambda i: (i, 0))
    gate = lambda n: pl.BlockSpec((TM_MERGE, d), lambda i: (i, P_GATES // d + n))
    vec = pl.BlockSpec((1, d), lambda i: (0, 0))
    return pl.pallas_call(
        _merge_kernel,
        grid=(t // TM_MERGE,),
        in_specs=[row, row, row, gate(0), gate(1), gate(2), row,
                  pl.BlockSpec((N_BRANCHES, d, d), lambda i: (0, 0, 0)),
                  pl.BlockSpec((d, d), lambda i: (0, 0)), vec, vec],
        out_specs=[row, row],
        out_shape=[jax.ShapeDtypeStruct((t, d), F32), jax.ShapeDtypeStruct((t, d), BF16)],
        compiler_params=_cparams("parallel"),
        name="merge",
    )(o_a, o_b, o_c, proj, proj, proj, h32, w_branch.astype(BF16), w_out.astype(BF16),
      ln_g.reshape(1, d), ln_b.reshape(1, d))


def _first_argmax_mask(vals, iota, n):
    m = jnp.max(vals, axis=0, keepdims=True)
    idx = jnp.min(jnp.where(vals == m, iota, n), axis=0, keepdims=True)
    return iota == idx


def _router_kernel(h_ref, wt_ref, b_ref, comb_ref, rank_ref, cnt_ref):
    tm = h_ref.shape[0]
    h = h_ref[...]
    h_hi, h_lo = _split_bf16(h)
    w = wt_ref[...]
    w_hi, w_lo = _split_bf16(w)
    logits = _dot_nt(w_hi, h_hi) + _dot_nt(w_hi, h_lo) + _dot_nt(w_lo, h_hi)
    scores = _sigmoid(logits)
    biased = scores + b_ref[...]

    neg = -jnp.inf
    sub = lax.broadcasted_iota(jnp.int32, (GROUP_SIZE, tm), 0)
    grp_rows = []
    for g in range(N_GROUPS):
        blk = biased[g * GROUP_SIZE:(g + 1) * GROUP_SIZE, :]
        m1 = jnp.max(blk, axis=0, keepdims=True)
        first = _first_argmax_mask(blk, sub, GROUP_SIZE)
        m2 = jnp.max(jnp.where(first, neg, blk), axis=0, keepdims=True)
        grp_rows.append(m1 + m2)
    gs = jnp.concatenate(grp_rows, axis=0)
    gsel = jnp.zeros((N_GROUPS, tm), jnp.bool_)
    gi = lax.broadcasted_iota(jnp.int32, (N_GROUPS, tm), 0)
    for _ in range(TOPK_GROUPS):
        pick = _first_argmax_mask(gs, gi, N_GROUPS)
        gsel = gsel | pick
        gs = jnp.where(pick, neg, gs)
    emask = jnp.concatenate(
        [jnp.broadcast_to(gsel[g:g + 1, :], (GROUP_SIZE, tm)) for g in range(N_GROUPS)], axis=0)
    cand = jnp.where(emask, biased, neg)
    ei = lax.broadcasted_iota(jnp.int32, (N_EXPERTS, tm), 0)
    chosen = jnp.zeros((N_EXPERTS, tm), jnp.bool_)
    for _ in range(TOP_K):
        pick = _first_argmax_mask(cand, ei, N_EXPERTS)
        chosen = chosen | pick
        cand = jnp.where(pick, neg, cand)
    wsel = jnp.where(chosen, scores, 0.0)
    comb_ref[...] = wsel / jnp.sum(wsel, axis=0, keepdims=True) * ROUTED_SCALE

    r = lax.broadcasted_iota(jnp.int32, (tm, tm), 0)
    c = lax.broadcasted_iota(jnp.int32, (tm, tm), 1)
    upper = jnp.where(r < c, 1.0, 0.0).astype(BF16)
    sel = jnp.where(chosen, 1.0, 0.0)
    rank_ref[...] = _dot(sel.astype(BF16), upper)
    cnt = jnp.sum(sel, axis=1, keepdims=True)
    cnt_ref[...] = jnp.broadcast_to(cnt, (N_EXPERTS, LANES)).astype(jnp.int32)


def _router_call(h32, router_w, router_b):
    t, d = h32.shape
    nt = t // TM_MOE
    bias = jnp.broadcast_to(router_b.reshape(N_EXPERTS, 1), (N_EXPERTS, TM_MOE))
    comb, rank, cnt = pl.pallas_call(
        _router_kernel,
        grid=(nt,),
        in_specs=[pl.BlockSpec((TM_MOE, d), lambda i: (i, 0)),
                  pl.BlockSpec((N_EXPERTS, d), lambda i: (0, 0)),
                  pl.BlockSpec((N_EXPERTS, TM_MOE), lambda i: (0, 0))],
        out_specs=[pl.BlockSpec((N_EXPERTS, TM_MOE), lambda i: (0, i)),
                   pl.BlockSpec((N_EXPERTS, TM_MOE), lambda i: (0, i)),
                   pl.BlockSpec((N_EXPERTS, LANES), lambda i: (i, 0))],
        out_shape=[jax.ShapeDtypeStruct((N_EXPERTS, t), F32),
                   jax.ShapeDtypeStruct((N_EXPERTS, t), F32),
                   jax.ShapeDtypeStruct((nt * N_EXPERTS, LANES), jnp.int32)],
        compiler_params=_cparams("parallel"),
        name="router",
    )(h32, router_w.T, bias)
    return comb, rank, cnt[:, 0].reshape(nt, N_EXPERTS)


def _moe_kernel(cnt_ref, x_ref, comb_ref, rank_ref, wgu_ref, wd_ref, y_ref):
    tm = x_ref.shape[0]
    i = pl.program_id(0)
    p = pl.program_id(1)

    @pl.when(p == 0)
    def _():
        y_ref[...] = jnp.zeros_like(y_ref)

    cmax = cnt_ref[i, EP_MOE * p]
    for ee in range(1, EP_MOE):
        cmax = jnp.maximum(cmax, cnt_ref[i, EP_MOE * p + ee])

    for c in range(tm // CH_MOE):
        @pl.when(cmax > c * CH_MOE)
        def _():
            slot = (lax.broadcasted_iota(jnp.int32, (CH_MOE, tm), 0) + c * CH_MOE).astype(F32)
            pws, outs = [], []
            for ee in range(EP_MOE):
                e = EP_MOE * p + ee
                rrow = rank_ref[pl.ds(e, 1), :]
                wrow = comb_ref[pl.ds(e, 1), :]
                hit = (rrow == slot) & (wrow > 0.0)
                onehot = jnp.where(hit, 1.0, 0.0).astype(BF16)
                pws.append(jnp.where(hit, wrow, 0.0).astype(BF16))
                xg = _dot(onehot, x_ref[...]).astype(BF16)
                gu = _dot(xg, wgu_ref[ee])
                hmid = _silu(gu[:, :D_EXPERT]) * gu[:, D_EXPERT:]
                outs.append(_dot(hmid.astype(BF16), wd_ref[ee]).astype(BF16))
            pw = jnp.concatenate(pws, axis=0)
            out = jnp.concatenate(outs, axis=0)
            y_ref[...] += _dot_tn(pw, out)


def _moe_call(h16, comb, rank, cnt, w_gate, w_up, w_down):
    t, d = h16.shape
    nt = t // TM_MOE
    wgu = jnp.concatenate([w_gate, w_up], axis=2).astype(BF16)
    wd = w_down.astype(BF16)
    grid_spec = pltpu.PrefetchScalarGridSpec(
        num_scalar_prefetch=1,
        grid=(nt, N_EXPERTS // EP_MOE),
        in_specs=[
            pl.BlockSpec((TM_MOE, d), lambda i, p, cnt: (i, 0)),
            pl.BlockSpec((N_EXPERTS, TM_MOE), lambda i, p, cnt: (0, i)),
            pl.BlockSpec((N_EXPERTS, TM_MOE), lambda i, p, cnt: (0, i)),
            pl.BlockSpec((EP_MOE, d, 2 * D_EXPERT), lambda i, p, cnt: (p, 0, 0)),
            pl.BlockSpec((EP_MOE, D_EXPERT, d), lambda i, p, cnt: (p, 0, 0)),
        ],
        out_specs=pl.BlockSpec((TM_MOE, d), lambda i, p, cnt: (i, 0)),
    )
    return pl.pallas_call(
        _moe_kernel,
        grid_spec=grid_spec,
        out_shape=jax.ShapeDtypeStruct((t, d), F32),
        compiler_params=_cparams("parallel", "arbitrary"),
        name="moe_routed",
    )(cnt, h16, comb, rank, wgu, wd)


def _tail_kernel(y_ref, h16_ref, h32_ref, p_ref, swgu_ref, swd_ref, pwi_ref, pwg_ref,
                 lg_ref, lb_ref, o32_ref, o16_ref):
    x = h16_ref[...]
    gu = _dot(x, swgu_ref[...])
    shared = _dot((_silu(gu[:, :D_SHARED]) * gu[:, D_SHARED:]).astype(BF16), swd_ref[...])
    ple = _dot(p_ref[...].astype(BF16), pwi_ref[...]) * _sigmoid(_dot(x, pwg_ref[...]))
    z = ALPHA * h32_ref[...] + y_ref[...] + shared + ple
    y = _layer_norm(z, lg_ref[...], lb_ref[...])
    o32_ref[...] = y
    o16_ref[...] = y.astype(BF16)


def _tail_call(y, h16, h32, p, sw_gate, sw_up, sw_down, ple_w_in, ple_w_gate, ln_g, ln_b):
    t, d = h32.shape
    row = pl.BlockSpec((TM_TAIL, d), lambda i: (i, 0))
    vec = pl.BlockSpec((1, d), lambda i: (0, 0))
    full = lambda a, b: pl.BlockSpec((a, b), lambda i: (0, 0))
    swgu = jnp.concatenate([sw_gate, sw_up], axis=1).astype(BF16)
    return pl.pallas_call(
        _tail_kernel,
        grid=(t // TM_TAIL,),
        in_specs=[row, row, row, pl.BlockSpec((TM_TAIL, PLE_DIM), lambda i: (i, 0)),
                  full(d, 2 * D_SHARED), full(D_SHARED, d), full(PLE_DIM, d), full(d, d), vec, vec],
        out_specs=[row, row],
        out_shape=[jax.ShapeDtypeStruct((t, d), F32), jax.ShapeDtypeStruct((t, d), BF16)],
        compiler_params=_cparams("parallel"),
        name="ffn_tail",
    )(y, h16, h32, p, swgu, sw_down.astype(BF16), ple_w_in.astype(BF16), ple_w_gate.astype(BF16),
      ln_g.reshape(1, d), ln_b.reshape(1, d))


def kernel(x, p, ln_in_g, ln_in_b, w_in, gla_fg_w, gla_fg_b, gla_norm_g, sg_norm_g, sg_norm_b, sg_w, sg_b, mla_qn_g, mla_w_uq, mla_kvn_g, mla_w_ukv, w_branch, w_out, ln1_g, ln1_b, router_w, router_b, exp_w_gate, exp_w_up, exp_w_down, sh_w_gate, sh_w_up, sh_w_down, ple_w_in, ple_w_gate, ln2_g, ln2_b):
    batch, seq, d = x.shape
    t = batch * seq
    depth = w_in.shape[0]
    h32, h16 = _ln_call(x.reshape(t, d), ln_in_g, ln_in_b)
    for i in range(depth):
        proj = _proj_call(h16, _prep_w_in(w_in[i]))
        o_a = _gla_call(proj, gla_fg_w[i], gla_fg_b[i], gla_norm_g[i], batch, seq)
        o_b = _sg_call(proj, sg_norm_g[i], sg_norm_b[i], sg_w[i], sg_b[i])
        q, k, vt = _mla_prep_call(proj, mla_qn_g[i], mla_w_uq[i], mla_kvn_g[i], mla_w_ukv[i], seq)
        o_c = _attn_call(q, k, vt, batch, seq)
        h32, h16 = _merge_call(o_a, o_b, o_c, proj, h32, w_branch[i], w_out[i], ln1_g[i], ln1_b[i])
        comb, rank, cnt = _router_call(h32, router_w[i], router_b[i])
        y = _moe_call(h16, comb, rank, cnt, exp_w_gate[i], exp_w_up[i], exp_w_down[i])
        h32, h16 = _tail_call(y, h16, h32, p[i].reshape(t, -1), sh_w_gate[i], sh_w_up[i],
                              sh_w_down[i], ple_w_in[i], ple_w_gate[i], ln2_g[i], ln2_b[i])
    return h32.reshape(batch, seq, d)
```

```python
import functools

import jax
import jax.numpy as jnp
from jax import lax
from jax.experimental import pallas as pl
from jax.experimental.pallas import tpu as pltpu

F32 = jnp.float32
BF16 = jnp.bfloat16

D_MODEL = 1024
DEPTH = 2
CHUNK = 64
GLA_HEADS, GLA_DK, GLA_DV, GLA_GATE_RANK, GLA_TAU = 4, 128, 256, 16, 16.0
SG_WIDTH, SG_GROUPS, SG_BLOCK = 1024, 4, 128
MLA_HEADS, MLA_Q_RANK, MLA_KV_RANK = 8, 384, 256
MLA_NOPE, MLA_ROPE, MLA_V = 128, 64, 128
MLA_QK = MLA_NOPE + MLA_ROPE
ROPE_THETA = 10000.0
N_BRANCHES = 3
N_EXPERTS, N_GROUPS, TOPK_GROUPS, TOP_K = 64, 8, 4, 8
GROUP_SIZE = N_EXPERTS // N_GROUPS
D_EXPERT, D_SHARED = 256, 256
ROUTED_SCALE = 2.5
PLE_DIM = 256
ALPHA = (2 * DEPTH) ** 0.25

LANES = 128
SUBLANES = 8
VMEM_LIMIT_BYTES = 56 * 1024 * 1024

P_SMALL = 1024
P_GFR = 384
P_CKV = 512
P_KROPE = 768
P_GATES = 1024
P_SG = 4096
P_GV = 6144
P_GR = 7168
P_GQ = 8192
P_GK = 8704
P_TOTAL = 9216
MLA_HEAD_PAD = 256

TM_LN = 512
TM_PROJ, TN_PROJ = 1024, 1024
R_GLA = 512
TM_SG = 256
TQ_ATT = 512
TM_MLA = TQ_ATT
HEADS_ATT = 2
LOG2_E = 1.4426950408889634
TM_MERGE = 512
TM_MOE = 512
SEG_ALIGN_LOG2 = 4
SEG_ALIGN = 1 << SEG_ALIGN_LOG2
SLOT_CHUNK_LOG2 = 6
SLOT_CHUNK = 1 << SLOT_CHUNK_LOG2
SORT_BLK = 512
ROW_BLK = 256
LOCAL_ROWS = -(-(TM_MOE * TOP_K + N_EXPERTS * (SEG_ALIGN - 1) + SLOT_CHUNK) // SORT_BLK) * SORT_BLK


def _cparams(*sem):
    return pltpu.CompilerParams(dimension_semantics=sem, vmem_limit_bytes=VMEM_LIMIT_BYTES)


def _dot(a, b):
    return jnp.dot(a, b, preferred_element_type=F32)


def _dot_nt(a, b):
    return lax.dot_general(a, b, (((1,), (1,)), ((), ())), preferred_element_type=F32)


def _dot_tn(a, b):
    return lax.dot_general(a, b, (((0,), (0,)), ((), ())), preferred_element_type=F32)


def _layer_norm(x, g, b, eps=1e-5):
    mu = jnp.mean(x, axis=-1, keepdims=True)
    xc = x - mu
    var = jnp.mean(xc * xc, axis=-1, keepdims=True)
    return xc * lax.rsqrt(var + eps) * g + b


def _rms_norm(x, g, eps=1e-6):
    return x * lax.rsqrt(jnp.mean(x * x, axis=-1, keepdims=True) + eps) * g


def _sigmoid(x):
    return 1.0 / (1.0 + jnp.exp(-x))


def _silu(x):
    return x * _sigmoid(x)


def _split_bf16(x):
    hi = x.astype(BF16)
    lo = (x - hi.astype(F32)).astype(BF16)
    return hi, lo


def _ln_kernel(x_ref, g_ref, b_ref, o32_ref, o16_ref):
    y = _layer_norm(x_ref[...], g_ref[...], b_ref[...])
    o32_ref[...] = y
    o16_ref[...] = y.astype(BF16)


def _ln_call(x, g, b):
    t, d = x.shape
    row = pl.BlockSpec((TM_LN, d), lambda i: (i, 0))
    vec = pl.BlockSpec((1, d), lambda i: (0, 0))
    return pl.pallas_call(
        _ln_kernel,
        grid=(t // TM_LN,),
        in_specs=[row, vec, vec],
        out_specs=[row, row],
        out_shape=[jax.ShapeDtypeStruct((t, d), F32), jax.ShapeDtypeStruct((t, d), BF16)],
        compiler_params=_cparams("parallel"),
        name="ln_in",
    )(x, g.reshape(1, d), b.reshape(1, d))


def _proj_kernel(x_ref, w_ref, o_ref):
    o_ref[...] = _dot(x_ref[...], w_ref[...]).astype(o_ref.dtype)


def _proj_call(h16, w):
    t, k = h16.shape
    n = w.shape[1]
    return pl.pallas_call(
        _proj_kernel,
        grid=(t // TM_PROJ, n // TN_PROJ),
        in_specs=[pl.BlockSpec((TM_PROJ, k), lambda i, j: (i, 0)),
                  pl.BlockSpec((k, TN_PROJ), lambda i, j: (0, j))],
        out_specs=pl.BlockSpec((TM_PROJ, TN_PROJ), lambda i, j: (i, j)),
        out_shape=jax.ShapeDtypeStruct((t, n), BF16),
        compiler_params=_cparams("parallel", "arbitrary"),
        name="in_proj",
    )(h16, w)


def _prep_w_in(w_in):
    d = w_in.shape[0]
    gq, gk = w_in[:, 0:512], w_in[:, 512:1024]
    gv, gr = w_in[:, 1024:2048], w_in[:, 2048:3072]
    gfr = w_in[:, 3072:3088]
    sgz = w_in[:, 3088:5136]
    mq = w_in[:, 5136:5520]
    ckv = w_in[:, 5520:5776]
    kr = w_in[:, 5776:5840]
    gates = w_in[:, 5840:8912]
    half = MLA_ROPE // 2
    kr_rot = jnp.concatenate([-kr[:, half:], kr[:, :half]], axis=1)
    z = lambda n: jnp.zeros((d, n), w_in.dtype)
    small = jnp.concatenate([mq, gfr, z(112), ckv, kr, kr_rot, z(128)], axis=1)
    return jnp.concatenate([small, gates, sgz, gv, gr, gq, gk], axis=1).astype(BF16)


def _gla_kernel(q_ref, k_ref, v_ref, gr_ref, gf_ref, fgw_ref, fgb_ref, g_ref, o_ref,
                st_ref, qd_ref, ki_ref, ke_ref, dec_ref):
    rows = q_ref.shape[0]

    @pl.when(pl.program_id(1) == 0)
    def _():
        st_ref[...] = jnp.zeros_like(st_ref)

    pre = _dot(gf_ref[...], fgw_ref[...]) + fgb_ref[...]
    glog = -(jnp.maximum(-pre, 0.0) + jnp.log1p(jnp.exp(-jnp.abs(pre)))) * (1.0 / GLA_TAU)

    r = lax.broadcasted_iota(jnp.int32, (rows, rows), 0)
    c = lax.broadcasted_iota(jnp.int32, (rows, rows), 1)
    same = lax.shift_right_logical(r, 6) == lax.shift_right_logical(c, 6)
    tri = jnp.where(same & (c <= r), 1.0, 0.0).astype(BF16)
    blk = jnp.where(same, 1.0, 0.0).astype(BF16)
    hi, lo = _split_bf16(glog)
    bcum = _dot(tri, hi) + _dot(tri, lo)
    btot = _dot(blk, hi) + _dot(blk, lo)

    q = q_ref[...].astype(F32) * (GLA_DK ** -0.5)
    k = k_ref[...].astype(F32)
    qd_ref[...] = (q * jnp.exp(bcum)).astype(BF16)
    ki_ref[...] = (k * jnp.exp(-bcum)).astype(BF16)
    ke_ref[...] = (k * jnp.exp(btot - bcum)).astype(BF16)
    dec_ref[...] = jnp.exp(btot)

    tr = lax.broadcasted_iota(jnp.int32, (CHUNK, CHUNK), 0)
    tc = lax.broadcasted_iota(jnp.int32, (CHUNK, CHUNK), 1)
    causal = tc <= tr

    def chunk_body(ci, carry):
        r0 = pl.multiple_of(ci * CHUNK, CHUNK)
        rs = pl.ds(r0, CHUNK)
        for hd in range(GLA_HEADS):
            ks = slice(hd * GLA_DK, (hd + 1) * GLA_DK)
            vs = slice(hd * GLA_DV, (hd + 1) * GLA_DV)
            qd = qd_ref[rs, ks]
            ki = ki_ref[rs, ks]
            ke = ke_ref[rs, ks]
            v = v_ref[rs, vs]
            att = jnp.where(causal, _dot_nt(qd, ki), 0.0)
            st = st_ref[hd]
            o = _dot(att.astype(BF16), v) + _dot_nt(qd, st.astype(BF16))
            dec = dec_ref[pl.ds(r0, 1), ks]
            st_ref[hd] = st * dec + _dot_tn(v, ke)
            on = _rms_norm(o, g_ref[:, vs])
            gate = gr_ref[rs, vs].astype(F32)
            o_ref[rs, vs] = (on * _silu(gate)).astype(o_ref.dtype)
        return carry

    lax.fori_loop(0, rows // CHUNK, chunk_body, 0)


def _gla_call(proj, fg_w, fg_b, gla_g, batch, seq):
    t = proj.shape[0]
    nr = seq // R_GLA
    row = lambda i, j: i * nr + j
    fgw = jnp.zeros((LANES, GLA_HEADS * GLA_DK), F32).at[:GLA_GATE_RANK].set(fg_w).astype(BF16)
    hk = GLA_HEADS * GLA_DK
    hv = GLA_HEADS * GLA_DV
    return pl.pallas_call(
        _gla_kernel,
        grid=(batch, nr),
        in_specs=[
            pl.BlockSpec((R_GLA, hk), lambda i, j: (row(i, j), P_GQ // hk)),
            pl.BlockSpec((R_GLA, hk), lambda i, j: (row(i, j), P_GK // hk)),
            pl.BlockSpec((R_GLA, hv), lambda i, j: (row(i, j), P_GV // hv)),
            pl.BlockSpec((R_GLA, hv), lambda i, j: (row(i, j), P_GR // hv)),
            pl.BlockSpec((R_GLA, LANES), lambda i, j: (row(i, j), P_GFR // LANES)),
            pl.BlockSpec((LANES, hk), lambda i, j: (0, 0)),
            pl.BlockSpec((1, hk), lambda i, j: (0, 0)),
            pl.BlockSpec((1, hv), lambda i, j: (0, 0)),
        ],
        out_specs=pl.BlockSpec((R_GLA, hv), lambda i, j: (row(i, j), 0)),
        out_shape=jax.ShapeDtypeStruct((t, hv), BF16),
        scratch_shapes=[
            pltpu.VMEM((GLA_HEADS, GLA_DV, GLA_DK), F32),
            pltpu.VMEM((R_GLA, hk), BF16),
            pltpu.VMEM((R_GLA, hk), BF16),
            pltpu.VMEM((R_GLA, hk), BF16),
            pltpu.VMEM((R_GLA, hk), F32),
        ],
        compiler_params=_cparams("parallel", "arbitrary"),
        name="gla",
    )(proj, proj, proj, proj, proj, fgw, fg_b.reshape(1, hk), gla_g.reshape(1, hv))


def _gelu(x):
    return 0.5 * x * (1.0 + lax.erf(x * (2.0 ** -0.5)))


def _sg_kernel(u_ref, v_ref, g_ref, b_ref, ws_ref, bias_ref, o_ref):
    rows = u_ref.shape[0]
    gw = SG_WIDTH // SG_GROUPS
    u = _gelu(u_ref[...].astype(F32))
    v = _gelu(v_ref[...].astype(F32))
    vn = _layer_norm(v, g_ref[...], b_ref[...]).astype(BF16)
    tr = lax.broadcasted_iota(jnp.int32, (SG_BLOCK, SG_BLOCK), 0)
    tc = lax.broadcasted_iota(jnp.int32, (SG_BLOCK, SG_BLOCK), 1)
    tril = tc <= tr
    for g in range(SG_GROUPS):
        w = jnp.where(tril, ws_ref[g], 0.0).astype(BF16)
        cs = slice(g * gw, (g + 1) * gw)
        for blk in range(rows // SG_BLOCK):
            rs = slice(blk * SG_BLOCK, (blk + 1) * SG_BLOCK)
            sv = _dot(w, vn[rs, cs]) + bias_ref[:, cs]
            o_ref[rs, cs] = (u[rs, cs] * sv).astype(o_ref.dtype)


def _sg_call(proj, sg_g, sg_bn, sg_w, sg_b):
    t = proj.shape[0]
    gw = SG_WIDTH // SG_GROUPS
    bias = jnp.repeat(sg_b.T, gw, axis=1)
    return pl.pallas_call(
        _sg_kernel,
        grid=(t // TM_SG,),
        in_specs=[
            pl.BlockSpec((TM_SG, SG_WIDTH), lambda i: (i, P_SG // SG_WIDTH)),
            pl.BlockSpec((TM_SG, SG_WIDTH), lambda i: (i, P_SG // SG_WIDTH + 1)),
            pl.BlockSpec((1, SG_WIDTH), lambda i: (0, 0)),
            pl.BlockSpec((1, SG_WIDTH), lambda i: (0, 0)),
            pl.BlockSpec((SG_GROUPS, SG_BLOCK, SG_BLOCK), lambda i: (0, 0, 0)),
            pl.BlockSpec((SG_BLOCK, SG_WIDTH), lambda i: (0, 0)),
        ],
        out_specs=pl.BlockSpec((TM_SG, SG_WIDTH), lambda i: (i, 0)),
        out_shape=jax.ShapeDtypeStruct((t, SG_WIDTH), BF16),
        compiler_params=_cparams("parallel"),
        name="spatial_gating",
    )(proj, proj, sg_g.reshape(1, -1), sg_bn.reshape(1, -1), sg_w, bias)


def _mla_prep_kernel(sm_ref, cs_ref, qg_ref, kg_ref, wq_ref, wkv_ref, q_ref, k_ref, vt_ref):
    rows = sm_ref.shape[0]
    mq = sm_ref[:, 0:MLA_Q_RANK].astype(F32)
    qn = _rms_norm(mq, qg_ref[...]).astype(BF16)
    qf = _dot(qn, wq_ref[...])
    ckv = sm_ref[:, P_CKV:P_CKV + MLA_KV_RANK].astype(F32)
    cn = _rms_norm(ckv, kg_ref[...]).astype(BF16)
    kv = _dot(cn, wkv_ref[...])

    cs = cs_ref[...]
    lane = lax.broadcasted_iota(jnp.int32, (rows, LANES), 1)
    low = lane < MLA_ROPE

    def rope(pair):
        t = pair * cs
        return jnp.where(low, t + pltpu.roll(t, MLA_ROPE, 1), 0.0)

    scale = (MLA_QK ** -0.5) * LOG2_E
    kr = rope(sm_ref[:, P_KROPE:P_KROPE + LANES].astype(F32)).astype(BF16)
    for h in range(MLA_HEADS):
        c0 = h * MLA_HEAD_PAD
        q_ref[:, c0:c0 + MLA_NOPE] = (qf[:, c0:c0 + MLA_NOPE] * scale).astype(BF16)
        q_ref[:, c0 + MLA_NOPE:c0 + MLA_HEAD_PAD] = (
            rope(qf[:, c0 + MLA_NOPE:c0 + MLA_HEAD_PAD]) * scale).astype(BF16)
        k_ref[:, c0:c0 + MLA_NOPE] = kv[:, h * MLA_NOPE:(h + 1) * MLA_NOPE].astype(BF16)
        k_ref[:, c0 + MLA_NOPE:c0 + MLA_HEAD_PAD] = kr
    vt_ref[0] = kv[:, MLA_HEADS * MLA_NOPE:].T.astype(BF16)


def _mla_prep_call(proj, qn_g, w_uq, kvn_g, w_ukv, seq):
    t = proj.shape[0]
    half = MLA_ROPE // 2
    wq = w_uq.reshape(MLA_Q_RANK, MLA_HEADS, MLA_QK)
    wr = wq[:, :, MLA_NOPE:]
    wr_rot = jnp.concatenate([-wr[:, :, half:], wr[:, :, :half]], axis=2)
    wq_ext = jnp.concatenate([wq[:, :, :MLA_NOPE], wr, wr_rot], axis=2)
    wq_ext = wq_ext.reshape(MLA_Q_RANK, MLA_HEADS * MLA_HEAD_PAD).astype(BF16)
    wkv = w_ukv.reshape(MLA_KV_RANK, MLA_HEADS, MLA_NOPE + MLA_V)
    wkv = jnp.concatenate([wkv[:, :, :MLA_NOPE].reshape(MLA_KV_RANK, -1),
                           wkv[:, :, MLA_NOPE:].reshape(MLA_KV_RANK, -1)], axis=1).astype(BF16)
    inv = ROPE_THETA ** (-jnp.arange(half, dtype=F32) / half)
    ang = jnp.arange(seq, dtype=F32)[:, None] * inv
    cs = jnp.concatenate([jnp.cos(ang), jnp.cos(ang), jnp.sin(ang), jnp.sin(ang)], axis=1)
    ns = seq // TM_MLA
    hq = MLA_HEADS * MLA_HEAD_PAD
    hv = MLA_HEADS * MLA_V
    return pl.pallas_call(
        _mla_prep_kernel,
        grid=(t // TM_MLA,),
        in_specs=[
            pl.BlockSpec((TM_MLA, P_SMALL), lambda i: (i, 0)),
            pl.BlockSpec((TM_MLA, LANES), lambda i: (i % ns, 0)),
            pl.BlockSpec((1, MLA_Q_RANK), lambda i: (0, 0)),
            pl.BlockSpec((1, MLA_KV_RANK), lambda i: (0, 0)),
            pl.BlockSpec((MLA_Q_RANK, hq), lambda i: (0, 0)),
            pl.BlockSpec((MLA_KV_RANK, MLA_HEADS * (MLA_NOPE + MLA_V)), lambda i: (0, 0)),
        ],
        out_specs=[
            pl.BlockSpec((TM_MLA, hq), lambda i: (i, 0)),
            pl.BlockSpec((TM_MLA, hq), lambda i: (i, 0)),
            pl.BlockSpec((1, hv, TM_MLA), lambda i: (i, 0, 0)),
        ],
        out_shape=[jax.ShapeDtypeStruct((t, hq), BF16), jax.ShapeDtypeStruct((t, hq), BF16),
                   jax.ShapeDtypeStruct((t // TM_MLA, hv, TM_MLA), BF16)],
        compiler_params=_cparams("parallel"),
        name="mla_prep",
    )(proj, cs, qn_g.reshape(1, -1), kvn_g.reshape(1, -1), wq_ext, wkv)


def _attn_kernel(q_ref, k_ref, vt_ref, o_ref, acc_ref):
    tq = q_ref.shape[0]
    i = pl.program_id(2)
    acc_ref[...] = jnp.zeros_like(acc_ref)

    def step(j, ms, ls, mask):
        r0 = pl.multiple_of(j * tq, tq)
        new_m, new_l = [], []
        for hh in range(HEADS_ATT):
            qs = slice(hh * MLA_HEAD_PAD, (hh + 1) * MLA_HEAD_PAD)
            s = _dot_nt(k_ref[pl.ds(r0, tq), qs], q_ref[:, qs])
            if mask is not None:
                s = jnp.where(mask, s, -jnp.inf)
            m_new = jnp.maximum(ms[hh], jnp.max(s, axis=0, keepdims=True))
            alpha = jnp.exp2(ms[hh] - m_new)
            p = jnp.exp2(s - m_new)
            new_l.append(alpha * ls[hh] + jnp.sum(p, axis=0, keepdims=True))
            vt = vt_ref[j, hh * MLA_V:(hh + 1) * MLA_V, :]
            acc_ref[hh] = alpha * acc_ref[hh] + _dot(vt, p.astype(BF16))
            new_m.append(m_new)
        return tuple(new_m), tuple(new_l)

    m0 = tuple(jnp.full((1, tq), -jnp.inf, F32) for _ in range(HEADS_ATT))
    l0 = tuple(jnp.zeros((1, tq), F32) for _ in range(HEADS_ATT))
    ms, ls = lax.fori_loop(0, i, lambda j, c: step(j, c[0], c[1], None), (m0, l0))
    kk = lax.broadcasted_iota(jnp.int32, (tq, tq), 0)
    qq = lax.broadcasted_iota(jnp.int32, (tq, tq), 1)
    visible = lax.shift_right_logical(kk, 6) <= lax.shift_right_logical(qq, 6)
    ms, ls = step(i, ms, ls, visible)
    for hh in range(HEADS_ATT):
        o_ref[:, hh * MLA_V:(hh + 1) * MLA_V] = (acc_ref[hh] / ls[hh]).T.astype(o_ref.dtype)


def _attn_call(q, k, vt, batch, seq):
    t = q.shape[0]
    nq = seq // TQ_ATT
    qw = HEADS_ATT * MLA_HEAD_PAD
    vw = HEADS_ATT * MLA_V
    return pl.pallas_call(
        _attn_kernel,
        grid=(batch, MLA_HEADS // HEADS_ATT, nq),
        in_specs=[
            pl.BlockSpec((TQ_ATT, qw), lambda b, h, i: (b * nq + i, h)),
            pl.BlockSpec((seq, qw), lambda b, h, i: (b, h)),
            pl.BlockSpec((nq, vw, TQ_ATT), lambda b, h, i: (b, h, 0)),
        ],
        out_specs=pl.BlockSpec((TQ_ATT, vw), lambda b, h, i: (b * nq + i, h)),
        out_shape=jax.ShapeDtypeStruct((t, MLA_HEADS * MLA_V), BF16),
        scratch_shapes=[pltpu.VMEM((HEADS_ATT, MLA_V, TQ_ATT), F32)],
        compiler_params=_cparams("parallel", "parallel", "arbitrary"),
        name="mla_attention",
    )(q, k, vt)


def _merge_kernel(oa_ref, ob_ref, oc_ref, g0_ref, g1_ref, g2_ref, h_ref, wb_ref, wo_ref,
                  lg_ref, lb_ref, o32_ref, o16_ref):
    merged = _sigmoid(g0_ref[...].astype(F32)) * _dot(oa_ref[...], wb_ref[0])
    merged += _sigmoid(g1_ref[...].astype(F32)) * _dot(ob_ref[...], wb_ref[1])
    merged += _sigmoid(g2_ref[...].astype(F32)) * _dot(oc_ref[...], wb_ref[2])
    mix = _dot(merged.astype(BF16), wo_ref[...])
    y = _layer_norm(ALPHA * h_ref[...] + mix, lg_ref[...], lb_ref[...])
    o32_ref[...] = y
    o16_ref[...] = y.astype(BF16)


def _merge_call(o_a, o_b, o_c, proj, h32, w_branch, w_out, ln_g, ln_b):
    t, d = h32.shape
    row = pl.BlockSpec((TM_MERGE, d), lambda i: (i, 0))
    gate = lambda n: pl.BlockSpec((TM_MERGE, d), lambda i: (i, P_GATES // d + n))
    vec = pl.BlockSpec((1, d), lambda i: (0, 0))
    return pl.pallas_call(
        _merge_kernel,
        grid=(t // TM_MERGE,),
        in_specs=[row, row, row, gate(0), gate(1), gate(2), row,
                  pl.BlockSpec((N_BRANCHES, d, d), lambda i: (0, 0, 0)),
                  pl.BlockSpec((d, d), lambda i: (0, 0)), vec, vec],
        out_specs=[row, row],
        out_shape=[jax.ShapeDtypeStruct((t, d), F32), jax.ShapeDtypeStruct((t, d), BF16)],
        compiler_params=_cparams("parallel"),
        name="merge",
    )(o_a, o_b, o_c, proj, proj, proj, h32, w_branch.astype(BF16), w_out.astype(BF16),
      ln_g.reshape(1, d), ln_b.reshape(1, d))


def _first_argmax_mask(vals, iota, n):
    m = jnp.max(vals, axis=0, keepdims=True)
    idx = jnp.min(jnp.where(vals == m, iota, n), axis=0, keepdims=True)
    return iota == idx


def _router_kernel(h_ref, wt_ref, b_ref, comb_ref, rank_ref, cnt_ref):
    tm = h_ref.shape[0]
    h = h_ref[...]
    h_hi, h_lo = _split_bf16(h)
    w = wt_ref[...]
    w_hi, w_lo = _split_bf16(w)
    logits = _dot_nt(w_hi, h_hi) + _dot_nt(w_hi, h_lo) + _dot_nt(w_lo, h_hi)
    scores = _sigmoid(logits)
    biased = scores + b_ref[...]

    neg = -jnp.inf
    sub = lax.broadcasted_iota(jnp.int32, (GROUP_SIZE, tm), 0)
    grp_rows = []
    for g in range(N_GROUPS):
        blk = biased[g * GROUP_SIZE:(g + 1) * GROUP_SIZE, :]
        m1 = jnp.max(blk, axis=0, keepdims=True)
        first = _first_argmax_mask(blk, sub, GROUP_SIZE)
        m2 = jnp.max(jnp.where(first, neg, blk), axis=0, keepdims=True)
        grp_rows.append(m1 + m2)
    gs = jnp.concatenate(grp_rows, axis=0)
    gsel = jnp.zeros((N_GROUPS, tm), jnp.bool_)
    gi = lax.broadcasted_iota(jnp.int32, (N_GROUPS, tm), 0)
    for _ in range(TOPK_GROUPS):
        pick = _first_argmax_mask(gs, gi, N_GROUPS)
        gsel = gsel | pick
        gs = jnp.where(pick, neg, gs)
    emask = jnp.concatenate(
        [jnp.broadcast_to(gsel[g:g + 1, :], (GROUP_SIZE, tm)) for g in range(N_GROUPS)], axis=0)
    cand = jnp.where(emask, biased, neg)
    ei = lax.broadcasted_iota(jnp.int32, (N_EXPERTS, tm), 0)
    chosen = jnp.zeros((N_EXPERTS, tm), jnp.bool_)
    for _ in range(TOP_K):
        pick = _first_argmax_mask(cand, ei, N_EXPERTS)
        chosen = chosen | pick
        cand = jnp.where(pick, neg, cand)
    wsel = jnp.where(chosen, scores, 0.0)
    comb_ref[...] = wsel / jnp.sum(wsel, axis=0, keepdims=True) * ROUTED_SCALE

    r = lax.broadcasted_iota(jnp.int32, (tm, tm), 0)
    c = lax.broadcasted_iota(jnp.int32, (tm, tm), 1)
    upper = jnp.where(r < c, 1.0, 0.0).astype(BF16)
    sel = jnp.where(chosen, 1.0, 0.0)
    rank_ref[...] = _dot(sel.astype(BF16), upper)
    cnt = jnp.sum(sel, axis=1, keepdims=True)
    cnt_ref[...] = jnp.broadcast_to(cnt, (N_EXPERTS, LANES)).astype(jnp.int32)


def _router_call(h32, router_w, router_b):
    t, d = h32.shape
    nt = t // TM_MOE
    bias = jnp.broadcast_to(router_b.reshape(N_EXPERTS, 1), (N_EXPERTS, TM_MOE))
    comb, rank, cnt = pl.pallas_call(
        _router_kernel,
        grid=(nt,),
        in_specs=[pl.BlockSpec((TM_MOE, d), lambda i: (i, 0)),
                  pl.BlockSpec((N_EXPERTS, d), lambda i: (0, 0)),
                  pl.BlockSpec((N_EXPERTS, TM_MOE), lambda i: (0, 0))],
        out_specs=[pl.BlockSpec((N_EXPERTS, TM_MOE), lambda i: (0, i)),
                   pl.BlockSpec((N_EXPERTS, TM_MOE), lambda i: (0, i)),
                   pl.BlockSpec((N_EXPERTS, LANES), lambda i: (i, 0))],
        out_shape=[jax.ShapeDtypeStruct((N_EXPERTS, t), F32),
                   jax.ShapeDtypeStruct((N_EXPERTS, t), F32),
                   jax.ShapeDtypeStruct((nt * N_EXPERTS, LANES), jnp.int32)],
        compiler_params=_cparams("parallel"),
        name="router",
    )(h32, router_w.T, bias)
    return comb, rank, cnt[:, 0].reshape(nt, N_EXPERTS)


def _moe_plan(cnt, t):
    nt = cnt.shape[0]
    pad = (cnt + SEG_ALIGN - 1) // SEG_ALIGN * SEG_ALIGN
    lseg = jnp.cumsum(pad, axis=1) - pad
    ltot = jnp.sum(pad, axis=1)
    etot = jnp.sum(pad, axis=0)
    region = (etot + ROW_BLK - 1) // ROW_BLK * ROW_BLK
    rend = jnp.cumsum(region)
    gpos = (rend - region)[None, :] + jnp.cumsum(pad, axis=0) - pad
    nblk = (rend[-1] // ROW_BLK).astype(jnp.int32).reshape(1)
    blk_expert = jnp.minimum(
        jnp.searchsorted(rend, jnp.arange(_moe_blocks(t), dtype=jnp.int32) * ROW_BLK, side="right"),
        N_EXPERTS - 1).astype(jnp.int32)
    i32 = lambda a: a.astype(jnp.int32)
    return i32(pad), i32(lseg), i32(gpos), i32(ltot), nblk, blk_expert


def _moe_blocks(t):
    nt = t // TM_MOE
    rows = t * TOP_K + nt * N_EXPERTS * (SEG_ALIGN - 1) + N_EXPERTS * (ROW_BLK - 1)
    return -(-rows // ROW_BLK)


def _build_slot_matrix(p_ref, comb_ref, rank_ref, pad_ref, lseg_ref, i, weighted):
    tm = p_ref.shape[1]
    p_ref[...] = jnp.zeros_like(p_ref)
    rowi = lax.broadcasted_iota(jnp.int32, (SLOT_CHUNK, tm), 0).astype(F32)

    def expert_body(e, carry):
        rrow = rank_ref[pl.ds(e, 1), :]
        wrow = comb_ref[pl.ds(e, 1), :]
        base = lseg_ref[i, e]
        nch = lax.shift_right_logical(pad_ref[i, e] + (SLOT_CHUNK - 1), SLOT_CHUNK_LOG2)

        def chunk_body(c, carry2):
            off = c * SLOT_CHUNK
            hit = (rrow == rowi + off.astype(F32)) & (wrow > 0.0)
            val = jnp.where(hit, wrow if weighted else 1.0, 0.0).astype(BF16)
            p_ref[pl.ds(pl.multiple_of(base + off, SEG_ALIGN), SLOT_CHUNK), :] = val
            return carry2

        lax.fori_loop(0, nch, chunk_body, 0)
        return carry

    lax.fori_loop(0, N_EXPERTS, expert_body, 0)


def _segment_copies(pad_ref, lseg_ref, gpos_ref, tile, make_copy):
    def expert_body(e, carry):
        n = lax.shift_right_logical(pad_ref[tile, e], SEG_ALIGN_LOG2)
        l0 = lseg_ref[tile, e]
        g0 = gpos_ref[tile, e]

        def piece(c, carry2):
            make_copy(pl.multiple_of(l0 + c * SEG_ALIGN, SEG_ALIGN),
                      pl.multiple_of(g0 + c * SEG_ALIGN, SEG_ALIGN)).start()
            return carry2

        lax.fori_loop(0, n, piece, 0)
        return carry

    lax.fori_loop(0, N_EXPERTS, expert_body, 0)


def _wait_copies(ltot_ref, tile, make_copy):
    def piece(c, carry):
        make_copy(0, 0).wait()
        return carry

    lax.fori_loop(0, lax.shift_right_logical(ltot_ref[tile], SEG_ALIGN_LOG2), piece, 0)


def _dispatch_kernel(pad_ref, lseg_ref, gpos_ref, ltot_ref, x_ref, comb_ref, rank_ref, xs_hbm,
                     p_ref, xs_ref, sem):
    i = pl.program_id(0)
    last = pl.num_programs(0) - 1
    buf = lax.rem(i, 2)
    _build_slot_matrix(p_ref, comb_ref, rank_ref, pad_ref, lseg_ref, i, weighted=False)
    ntot = ltot_ref[i]
    for cb in range(LOCAL_ROWS // SORT_BLK):
        @pl.when(cb * SORT_BLK < ntot)
        def _():
            rs = slice(cb * SORT_BLK, (cb + 1) * SORT_BLK)
            xs_ref[buf, rs, :] = _dot(p_ref[rs, :], x_ref[...]).astype(BF16)

    def copy_for(b):
        return lambda l0, g0: pltpu.make_async_copy(
            xs_ref.at[b, pl.ds(l0, SEG_ALIGN), :], xs_hbm.at[pl.ds(g0, SEG_ALIGN), :], sem.at[b])

    _segment_copies(pad_ref, lseg_ref, gpos_ref, i, copy_for(buf))

    @pl.when(i > 0)
    def _():
        _wait_copies(ltot_ref, i - 1, copy_for(1 - buf))

    @pl.when(i == last)
    def _():
        _wait_copies(ltot_ref, i, copy_for(buf))


def _dispatch_call(h16, comb, rank, plan):
    t, d = h16.shape
    nt = t // TM_MOE
    pad, lseg, gpos, ltot, _, _ = plan
    grid_spec = pltpu.PrefetchScalarGridSpec(
        num_scalar_prefetch=4,
        grid=(nt,),
        in_specs=[
            pl.BlockSpec((TM_MOE, d), lambda i, *_: (i, 0)),
            pl.BlockSpec((N_EXPERTS, TM_MOE), lambda i, *_: (0, i)),
            pl.BlockSpec((N_EXPERTS, TM_MOE), lambda i, *_: (0, i)),
        ],
        out_specs=pl.BlockSpec(memory_space=pl.ANY),
        scratch_shapes=[
            pltpu.VMEM((LOCAL_ROWS, TM_MOE), BF16),
            pltpu.VMEM((2, LOCAL_ROWS, d), BF16),
            pltpu.SemaphoreType.DMA((2,)),
        ],
    )
    return pl.pallas_call(
        _dispatch_kernel,
        grid_spec=grid_spec,
        out_shape=jax.ShapeDtypeStruct((_moe_blocks(t) * ROW_BLK, d), BF16),
        compiler_params=_cparams("arbitrary"),
        name="moe_dispatch",
    )(pad, lseg, gpos, ltot, h16, comb, rank)


def _expert_ffn_kernel(nblk_ref, be_ref, x_ref, wgu_ref, wd_ref, o_ref):
    @pl.when(pl.program_id(0) < nblk_ref[0])
    def _():
        gu = _dot(x_ref[...], wgu_ref[0])
        hmid = _silu(gu[:, :D_EXPERT]) * gu[:, D_EXPERT:]
        o_ref[...] = _dot(hmid.astype(BF16), wd_ref[0]).astype(o_ref.dtype)


def _expert_ffn_call(xs, plan, w_gate, w_up, w_down):
    rows, d = xs.shape
    _, _, _, _, nblk, blk_expert = plan
    wgu = jnp.concatenate([w_gate, w_up], axis=2).astype(BF16)
    wd = w_down.astype(BF16)
    live = lambda b, nblk, be: jnp.minimum(b, nblk[0] - 1)
    grid_spec = pltpu.PrefetchScalarGridSpec(
        num_scalar_prefetch=2,
        grid=(rows // ROW_BLK,),
        in_specs=[
            pl.BlockSpec((ROW_BLK, d), lambda b, nblk, be: (live(b, nblk, be), 0)),
            pl.BlockSpec((1, d, 2 * D_EXPERT), lambda b, nblk, be: (be[live(b, nblk, be)], 0, 0)),
            pl.BlockSpec((1, D_EXPERT, d), lambda b, nblk, be: (be[live(b, nblk, be)], 0, 0)),
        ],
        out_specs=pl.BlockSpec((ROW_BLK, d), lambda b, nblk, be: (live(b, nblk, be), 0)),
    )
    return pl.pallas_call(
        _expert_ffn_kernel,
        grid_spec=grid_spec,
        out_shape=jax.ShapeDtypeStruct((rows, d), BF16),
        compiler_params=_cparams("arbitrary"),
        name="moe_expert_ffn",
    )(nblk, blk_expert, xs, wgu, wd)


def _combine_kernel(pad_ref, lseg_ref, gpos_ref, ltot_ref, ys_hbm, comb_ref, rank_ref,
                    h16_ref, h32_ref, p_ref, swgu_ref, swd_ref, pwi_ref, pwg_ref, lg_ref, lb_ref,
                    o32_ref, o16_ref, pw_ref, ys_ref, sem):
    i = pl.program_id(0)
    last = pl.num_programs(0) - 1
    buf = lax.rem(i, 2)

    def copy_for(b):
        return lambda l0, g0: pltpu.make_async_copy(
            ys_hbm.at[pl.ds(g0, SEG_ALIGN), :], ys_ref.at[b, pl.ds(l0, SEG_ALIGN), :], sem.at[b])

    def fetch(tile, b):
        ys_ref[b] = jnp.zeros(ys_ref.shape[1:], ys_ref.dtype)
        _segment_copies(pad_ref, lseg_ref, gpos_ref, tile, copy_for(b))

    @pl.when(i == 0)
    def _():
        fetch(i, buf)

    @pl.when(i < last)
    def _():
        fetch(i + 1, 1 - buf)

    _build_slot_matrix(pw_ref, comb_ref, rank_ref, pad_ref, lseg_ref, i, weighted=True)
    x = h16_ref[...]
    gu = _dot(x, swgu_ref[...])
    shared = _dot((_silu(gu[:, :D_SHARED]) * gu[:, D_SHARED:]).astype(BF16), swd_ref[...])
    ple = _dot(p_ref[...].astype(BF16), pwi_ref[...]) * _sigmoid(_dot(x, pwg_ref[...]))
    z = ALPHA * h32_ref[...] + shared + ple

    _wait_copies(ltot_ref, i, copy_for(buf))
    z += _dot_tn(pw_ref[...], ys_ref[buf])
    y = _layer_norm(z, lg_ref[...], lb_ref[...])
    o32_ref[...] = y
    o16_ref[...] = y.astype(BF16)


def _combine_call(ys, comb, rank, plan, h16, h32, p, sw_gate, sw_up, sw_down, ple_w_in, ple_w_gate,
                  ln_g, ln_b):
    t, d = h32.shape
    nt = t // TM_MOE
    pad, lseg, gpos, ltot, _, _ = plan
    row = pl.BlockSpec((TM_MOE, d), lambda i, *_: (i, 0))
    vec = pl.BlockSpec((1, d), lambda i, *_: (0, 0))
    full = lambda a, b: pl.BlockSpec((a, b), lambda i, *_: (0, 0))
    route = pl.BlockSpec((N_EXPERTS, TM_MOE), lambda i, *_: (0, i))
    swgu = jnp.concatenate([sw_gate, sw_up], axis=1).astype(BF16)
    grid_spec = pltpu.PrefetchScalarGridSpec(
        num_scalar_prefetch=4,
        grid=(nt,),
        in_specs=[pl.BlockSpec(memory_space=pl.ANY), route, route, row, row,
                  pl.BlockSpec((TM_MOE, PLE_DIM), lambda i, *_: (i, 0)),
                  full(d, 2 * D_SHARED), full(D_SHARED, d), full(PLE_DIM, d), full(d, d), vec, vec],
        out_specs=[row, row],
        scratch_shapes=[
            pltpu.VMEM((LOCAL_ROWS, TM_MOE), BF16),
            pltpu.VMEM((2, LOCAL_ROWS, d), BF16),
            pltpu.SemaphoreType.DMA((2,)),
        ],
    )
    return pl.pallas_call(
        _combine_kernel,
        grid_spec=grid_spec,
        out_shape=[jax.ShapeDtypeStruct((t, d), F32), jax.ShapeDtypeStruct((t, d), BF16)],
        compiler_params=_cparams("arbitrary"),
        name="moe_combine_tail",
    )(pad, lseg, gpos, ltot, ys, comb, rank, h16, h32, p, swgu, sw_down.astype(BF16),
      ple_w_in.astype(BF16), ple_w_gate.astype(BF16), ln_g.reshape(1, d), ln_b.reshape(1, d))


def kernel(x, p, ln_in_g, ln_in_b, w_in, gla_fg_w, gla_fg_b, gla_norm_g, sg_norm_g, sg_norm_b, sg_w, sg_b, mla_qn_g, mla_w_uq, mla_kvn_g, mla_w_ukv, w_branch, w_out, ln1_g, ln1_b, router_w, router_b, exp_w_gate, exp_w_up, exp_w_down, sh_w_gate, sh_w_up, sh_w_down, ple_w_in, ple_w_gate, ln2_g, ln2_b):
    batch, seq, d = x.shape
    t = batch * seq
    depth = w_in.shape[0]
    h32, h16 = _ln_call(x.reshape(t, d), ln_in_g, ln_in_b)
    for i in range(depth):
        proj = _proj_call(h16, _prep_w_in(w_in[i]))
        o_a = _gla_call(proj, gla_fg_w[i], gla_fg_b[i], gla_norm_g[i], batch, seq)
        o_b = _sg_call(proj, sg_norm_g[i], sg_norm_b[i], sg_w[i], sg_b[i])
        q, k, vt = _mla_prep_call(proj, mla_qn_g[i], mla_w_uq[i], mla_kvn_g[i], mla_w_ukv[i], seq)
        o_c = _attn_call(q, k, vt, batch, seq)
        h32, h16 = _merge_call(o_a, o_b, o_c, proj, h32, w_branch[i], w_out[i], ln1_g[i], ln1_b[i])
        comb, rank, cnt = _router_call(h32, router_w[i], router_b[i])
        plan = _moe_plan(cnt, t)
        xs = _dispatch_call(h16, comb, rank, plan)
        ys = _expert_ffn_call(xs, plan, exp_w_gate[i], exp_w_up[i], exp_w_down[i])
        h32, h16 = _combine_call(ys, comb, rank, plan, h16, h32, p[i].reshape(t, -1), sh_w_gate[i],
                                 sh_w_up[i], sh_w_down[i], ple_w_in[i], ple_w_gate[i],
                                 ln2_g[i], ln2_b[i])
    return h32.reshape(batch, seq, d)
```

```python
import functools

import jax
import jax.numpy as jnp
from jax import lax
from jax.experimental import pallas as pl
from jax.experimental.pallas import tpu as pltpu

F32 = jnp.float32
BF16 = jnp.bfloat16

D_MODEL = 1024
DEPTH = 2
CHUNK = 64
GLA_HEADS, GLA_DK, GLA_DV, GLA_GATE_RANK, GLA_TAU = 4, 128, 256, 16, 16.0
SG_WIDTH, SG_GROUPS, SG_BLOCK = 1024, 4, 128
MLA_HEADS, MLA_Q_RANK, MLA_KV_RANK = 8, 384, 256
MLA_NOPE, MLA_ROPE, MLA_V = 128, 64, 128
MLA_QK = MLA_NOPE + MLA_ROPE
ROPE_THETA = 10000.0
N_BRANCHES = 3
N_EXPERTS, N_GROUPS, TOPK_GROUPS, TOP_K = 64, 8, 4, 8
GROUP_SIZE = N_EXPERTS // N_GROUPS
D_EXPERT, D_SHARED = 256, 256
ROUTED_SCALE = 2.5
PLE_DIM = 256
ALPHA = (2 * DEPTH) ** 0.25

LANES = 128
SUBLANES = 8
VMEM_LIMIT_BYTES = 56 * 1024 * 1024

P_SMALL = 1024
P_GFR = 384
P_CKV = 512
P_KROPE = 768
P_GATES = 1024
P_SG = 4096
P_GV = 6144
P_GR = 7168
P_GQ = 8192
P_GK = 8704
P_TOTAL = 9216
MLA_HEAD_PAD = 256

TM_LN = 512
TM_PROJ, TN_PROJ = 1024, 1024
R_GLA = 512
TM_SG = 256
TQ_ATT = 512
TM_MLA = TQ_ATT
HEADS_ATT = 2
LOG2_E = 1.4426950408889634
TM_MERGE = 512
TM_MOE = 512
SEG_ALIGN_LOG2 = 4
SEG_ALIGN = 1 << SEG_ALIGN_LOG2
SLOT_CHUNK_LOG2 = 6
SLOT_CHUNK = 1 << SLOT_CHUNK_LOG2
SORT_BLK = 512
ROW_BLK = 512
BIG_PIECE_LOG2 = 6
BIG_PIECE = 1 << BIG_PIECE_LOG2
LOCAL_ROWS = -(-(TM_MOE * TOP_K + N_EXPERTS * (SEG_ALIGN - 1) + SLOT_CHUNK) // SORT_BLK) * SORT_BLK


def _cparams(*sem):
    return pltpu.CompilerParams(dimension_semantics=sem, vmem_limit_bytes=VMEM_LIMIT_BYTES)


def _dot(a, b):
    return jnp.dot(a, b, preferred_element_type=F32)


def _dot_nt(a, b):
    return lax.dot_general(a, b, (((1,), (1,)), ((), ())), preferred_element_type=F32)


def _dot_tn(a, b):
    return lax.dot_general(a, b, (((0,), (0,)), ((), ())), preferred_element_type=F32)


def _layer_norm(x, g, b, eps=1e-5):
    mu = jnp.mean(x, axis=-1, keepdims=True)
    xc = x - mu
    var = jnp.mean(xc * xc, axis=-1, keepdims=True)
    return xc * lax.rsqrt(var + eps) * g + b


def _rms_norm(x, g, eps=1e-6):
    return x * lax.rsqrt(jnp.mean(x * x, axis=-1, keepdims=True) + eps) * g


def _sigmoid(x):
    return 1.0 / (1.0 + jnp.exp(-x))


def _silu(x):
    return x * _sigmoid(x)


def _split_bf16(x):
    hi = x.astype(BF16)
    lo = (x - hi.astype(F32)).astype(BF16)
    return hi, lo


def _ln_kernel(x_ref, g_ref, b_ref, o32_ref, o16_ref):
    y = _layer_norm(x_ref[...], g_ref[...], b_ref[...])
    o32_ref[...] = y
    o16_ref[...] = y.astype(BF16)


def _ln_call(x, g, b):
    t, d = x.shape
    row = pl.BlockSpec((TM_LN, d), lambda i: (i, 0))
    vec = pl.BlockSpec((1, d), lambda i: (0, 0))
    return pl.pallas_call(
        _ln_kernel,
        grid=(t // TM_LN,),
        in_specs=[row, vec, vec],
        out_specs=[row, row],
        out_shape=[jax.ShapeDtypeStruct((t, d), F32), jax.ShapeDtypeStruct((t, d), BF16)],
        compiler_params=_cparams("parallel"),
        name="ln_in",
    )(x, g.reshape(1, d), b.reshape(1, d))


def _proj_kernel(x_ref, w_ref, o_ref):
    o_ref[...] = _dot(x_ref[...], w_ref[...]).astype(o_ref.dtype)


def _proj_call(h16, w):
    t, k = h16.shape
    n = w.shape[1]
    return pl.pallas_call(
        _proj_kernel,
        grid=(t // TM_PROJ, n // TN_PROJ),
        in_specs=[pl.BlockSpec((TM_PROJ, k), lambda i, j: (i, 0)),
                  pl.BlockSpec((k, TN_PROJ), lambda i, j: (0, j))],
        out_specs=pl.BlockSpec((TM_PROJ, TN_PROJ), lambda i, j: (i, j)),
        out_shape=jax.ShapeDtypeStruct((t, n), BF16),
        compiler_params=_cparams("parallel", "arbitrary"),
        name="in_proj",
    )(h16, w)


def _prep_w_in(w_in):
    d = w_in.shape[0]
    gq, gk = w_in[:, 0:512], w_in[:, 512:1024]
    gv, gr = w_in[:, 1024:2048], w_in[:, 2048:3072]
    gfr = w_in[:, 3072:3088]
    sgz = w_in[:, 3088:5136]
    mq = w_in[:, 5136:5520]
    ckv = w_in[:, 5520:5776]
    kr = w_in[:, 5776:5840]
    gates = w_in[:, 5840:8912]
    half = MLA_ROPE // 2
    kr_rot = jnp.concatenate([-kr[:, half:], kr[:, :half]], axis=1)
    z = lambda n: jnp.zeros((d, n), w_in.dtype)
    small = jnp.concatenate([mq, gfr, z(112), ckv, kr, kr_rot, z(128)], axis=1)
    return jnp.concatenate([small, gates, sgz, gv, gr, gq, gk], axis=1).astype(BF16)


def _gla_kernel(q_ref, k_ref, v_ref, gr_ref, gf_ref, fgw_ref, fgb_ref, g_ref, o_ref,
                st_ref, qd_ref, ki_ref, ke_ref, dec_ref):
    rows = q_ref.shape[0]

    @pl.when(pl.program_id(1) == 0)
    def _():
        st_ref[...] = jnp.zeros_like(st_ref)

    pre = _dot(gf_ref[...], fgw_ref[...]) + fgb_ref[...]
    glog = -(jnp.maximum(-pre, 0.0) + jnp.log1p(jnp.exp(-jnp.abs(pre)))) * (1.0 / GLA_TAU)

    r = lax.broadcasted_iota(jnp.int32, (rows, rows), 0)
    c = lax.broadcasted_iota(jnp.int32, (rows, rows), 1)
    same = lax.shift_right_logical(r, 6) == lax.shift_right_logical(c, 6)
    tri = jnp.where(same & (c <= r), 1.0, 0.0).astype(BF16)
    blk = jnp.where(same, 1.0, 0.0).astype(BF16)
    hi, lo = _split_bf16(glog)
    bcum = _dot(tri, hi) + _dot(tri, lo)
    btot = _dot(blk, hi) + _dot(blk, lo)

    q = q_ref[...].astype(F32) * (GLA_DK ** -0.5)
    k = k_ref[...].astype(F32)
    qd_ref[...] = (q * jnp.exp(bcum)).astype(BF16)
    ki_ref[...] = (k * jnp.exp(-bcum)).astype(BF16)
    ke_ref[...] = (k * jnp.exp(btot - bcum)).astype(BF16)
    dec_ref[...] = jnp.exp(btot)

    tr = lax.broadcasted_iota(jnp.int32, (CHUNK, CHUNK), 0)
    tc = lax.broadcasted_iota(jnp.int32, (CHUNK, CHUNK), 1)
    causal = tc <= tr

    def chunk_body(ci, carry):
        r0 = pl.multiple_of(ci * CHUNK, CHUNK)
        rs = pl.ds(r0, CHUNK)
        for hd in range(GLA_HEADS):
            ks = slice(hd * GLA_DK, (hd + 1) * GLA_DK)
            vs = slice(hd * GLA_DV, (hd + 1) * GLA_DV)
            qd = qd_ref[rs, ks]
            ki = ki_ref[rs, ks]
            ke = ke_ref[rs, ks]
            v = v_ref[rs, vs]
            att = jnp.where(causal, _dot_nt(qd, ki), 0.0)
            st = st_ref[hd]
            o = _dot(att.astype(BF16), v) + _dot_nt(qd, st.astype(BF16))
            dec = dec_ref[pl.ds(r0, 1), ks]
            st_ref[hd] = st * dec + _dot_tn(v, ke)
            on = _rms_norm(o, g_ref[:, vs])
            gate = gr_ref[rs, vs].astype(F32)
            o_ref[rs, vs] = (on * _silu(gate)).astype(o_ref.dtype)
        return carry

    lax.fori_loop(0, rows // CHUNK, chunk_body, 0)


def _gla_call(proj, fg_w, fg_b, gla_g, batch, seq):
    t = proj.shape[0]
    nr = seq // R_GLA
    row = lambda i, j: i * nr + j
    fgw = jnp.zeros((LANES, GLA_HEADS * GLA_DK), F32).at[:GLA_GATE_RANK].set(fg_w).astype(BF16)
    hk = GLA_HEADS * GLA_DK
    hv = GLA_HEADS * GLA_DV
    return pl.pallas_call(
        _gla_kernel,
        grid=(batch, nr),
        in_specs=[
            pl.BlockSpec((R_GLA, hk), lambda i, j: (row(i, j), P_GQ // hk)),
            pl.BlockSpec((R_GLA, hk), lambda i, j: (row(i, j), P_GK // hk)),
            pl.BlockSpec((R_GLA, hv), lambda i, j: (row(i, j), P_GV // hv)),
            pl.BlockSpec((R_GLA, hv), lambda i, j: (row(i, j), P_GR // hv)),
            pl.BlockSpec((R_GLA, LANES), lambda i, j: (row(i, j), P_GFR // LANES)),
            pl.BlockSpec((LANES, hk), lambda i, j: (0, 0)),
            pl.BlockSpec((1, hk), lambda i, j: (0, 0)),
            pl.BlockSpec((1, hv), lambda i, j: (0, 0)),
        ],
        out_specs=pl.BlockSpec((R_GLA, hv), lambda i, j: (row(i, j), 0)),
        out_shape=jax.ShapeDtypeStruct((t, hv), BF16),
        scratch_shapes=[
            pltpu.VMEM((GLA_HEADS, GLA_DV, GLA_DK), F32),
            pltpu.VMEM((R_GLA, hk), BF16),
            pltpu.VMEM((R_GLA, hk), BF16),
            pltpu.VMEM((R_GLA, hk), BF16),
            pltpu.VMEM((R_GLA, hk), F32),
        ],
        compiler_params=_cparams("parallel", "arbitrary"),
        name="gla",
    )(proj, proj, proj, proj, proj, fgw, fg_b.reshape(1, hk), gla_g.reshape(1, hv))


def _gelu(x):
    return 0.5 * x * (1.0 + lax.erf(x * (2.0 ** -0.5)))


def _sg_kernel(u_ref, v_ref, g_ref, b_ref, ws_ref, bias_ref, o_ref):
    rows = u_ref.shape[0]
    gw = SG_WIDTH // SG_GROUPS
    u = _gelu(u_ref[...].astype(F32))
    v = _gelu(v_ref[...].astype(F32))
    vn = _layer_norm(v, g_ref[...], b_ref[...]).astype(BF16)
    tr = lax.broadcasted_iota(jnp.int32, (SG_BLOCK, SG_BLOCK), 0)
    tc = lax.broadcasted_iota(jnp.int32, (SG_BLOCK, SG_BLOCK), 1)
    tril = tc <= tr
    for g in range(SG_GROUPS):
        w = jnp.where(tril, ws_ref[g], 0.0).astype(BF16)
        cs = slice(g * gw, (g + 1) * gw)
        for blk in range(rows // SG_BLOCK):
            rs = slice(blk * SG_BLOCK, (blk + 1) * SG_BLOCK)
            sv = _dot(w, vn[rs, cs]) + bias_ref[:, cs]
            o_ref[rs, cs] = (u[rs, cs] * sv).astype(o_ref.dtype)


def _sg_call(proj, sg_g, sg_bn, sg_w, sg_b):
    t = proj.shape[0]
    gw = SG_WIDTH // SG_GROUPS
    bias = jnp.repeat(sg_b.T, gw, axis=1)
    return pl.pallas_call(
        _sg_kernel,
        grid=(t // TM_SG,),
        in_specs=[
            pl.BlockSpec((TM_SG, SG_WIDTH), lambda i: (i, P_SG // SG_WIDTH)),
            pl.BlockSpec((TM_SG, SG_WIDTH), lambda i: (i, P_SG // SG_WIDTH + 1)),
            pl.BlockSpec((1, SG_WIDTH), lambda i: (0, 0)),
            pl.BlockSpec((1, SG_WIDTH), lambda i: (0, 0)),
            pl.BlockSpec((SG_GROUPS, SG_BLOCK, SG_BLOCK), lambda i: (0, 0, 0)),
            pl.BlockSpec((SG_BLOCK, SG_WIDTH), lambda i: (0, 0)),
        ],
        out_specs=pl.BlockSpec((TM_SG, SG_WIDTH), lambda i: (i, 0)),
        out_shape=jax.ShapeDtypeStruct((t, SG_WIDTH), BF16),
        compiler_params=_cparams("parallel"),
        name="spatial_gating",
    )(proj, proj, sg_g.reshape(1, -1), sg_bn.reshape(1, -1), sg_w, bias)


def _mla_prep_kernel(sm_ref, cs_ref, qg_ref, kg_ref, wq_ref, wkv_ref, q_ref, k_ref, vt_ref):
    rows = sm_ref.shape[0]
    mq = sm_ref[:, 0:MLA_Q_RANK].astype(F32)
    qn = _rms_norm(mq, qg_ref[...]).astype(BF16)
    qf = _dot(qn, wq_ref[...])
    ckv = sm_ref[:, P_CKV:P_CKV + MLA_KV_RANK].astype(F32)
    cn = _rms_norm(ckv, kg_ref[...]).astype(BF16)
    kv = _dot(cn, wkv_ref[...])

    cs = cs_ref[...]
    lane = lax.broadcasted_iota(jnp.int32, (rows, LANES), 1)
    low = lane < MLA_ROPE

    def rope(pair):
        t = pair * cs
        return jnp.where(low, t + pltpu.roll(t, MLA_ROPE, 1), 0.0)

    scale = (MLA_QK ** -0.5) * LOG2_E
    kr = rope(sm_ref[:, P_KROPE:P_KROPE + LANES].astype(F32)).astype(BF16)
    for h in range(MLA_HEADS):
        c0 = h * MLA_HEAD_PAD
        q_ref[:, c0:c0 + MLA_NOPE] = (qf[:, c0:c0 + MLA_NOPE] * scale).astype(BF16)
        q_ref[:, c0 + MLA_NOPE:c0 + MLA_HEAD_PAD] = (
            rope(qf[:, c0 + MLA_NOPE:c0 + MLA_HEAD_PAD]) * scale).astype(BF16)
        k_ref[:, c0:c0 + MLA_NOPE] = kv[:, h * MLA_NOPE:(h + 1) * MLA_NOPE].astype(BF16)
        k_ref[:, c0 + MLA_NOPE:c0 + MLA_HEAD_PAD] = kr
    vt_ref[0] = kv[:, MLA_HEADS * MLA_NOPE:].T.astype(BF16)


def _mla_prep_call(proj, qn_g, w_uq, kvn_g, w_ukv, seq):
    t = proj.shape[0]
    half = MLA_ROPE // 2
    wq = w_uq.reshape(MLA_Q_RANK, MLA_HEADS, MLA_QK)
    wr = wq[:, :, MLA_NOPE:]
    wr_rot = jnp.concatenate([-wr[:, :, half:], wr[:, :, :half]], axis=2)
    wq_ext = jnp.concatenate([wq[:, :, :MLA_NOPE], wr, wr_rot], axis=2)
    wq_ext = wq_ext.reshape(MLA_Q_RANK, MLA_HEADS * MLA_HEAD_PAD).astype(BF16)
    wkv = w_ukv.reshape(MLA_KV_RANK, MLA_HEADS, MLA_NOPE + MLA_V)
    wkv = jnp.concatenate([wkv[:, :, :MLA_NOPE].reshape(MLA_KV_RANK, -1),
                           wkv[:, :, MLA_NOPE:].reshape(MLA_KV_RANK, -1)], axis=1).astype(BF16)
    inv = ROPE_THETA ** (-jnp.arange(half, dtype=F32) / half)
    ang = jnp.arange(seq, dtype=F32)[:, None] * inv
    cs = jnp.concatenate([jnp.cos(ang), jnp.cos(ang), jnp.sin(ang), jnp.sin(ang)], axis=1)
    ns = seq // TM_MLA
    hq = MLA_HEADS * MLA_HEAD_PAD
    hv = MLA_HEADS * MLA_V
    return pl.pallas_call(
        _mla_prep_kernel,
        grid=(t // TM_MLA,),
        in_specs=[
            pl.BlockSpec((TM_MLA, P_SMALL), lambda i: (i, 0)),
            pl.BlockSpec((TM_MLA, LANES), lambda i: (i % ns, 0)),
            pl.BlockSpec((1, MLA_Q_RANK), lambda i: (0, 0)),
            pl.BlockSpec((1, MLA_KV_RANK), lambda i: (0, 0)),
            pl.BlockSpec((MLA_Q_RANK, hq), lambda i: (0, 0)),
            pl.BlockSpec((MLA_KV_RANK, MLA_HEADS * (MLA_NOPE + MLA_V)), lambda i: (0, 0)),
        ],
        out_specs=[
            pl.BlockSpec((TM_MLA, hq), lambda i: (i, 0)),
            pl.BlockSpec((TM_MLA, hq), lambda i: (i, 0)),
            pl.BlockSpec((1, hv, TM_MLA), lambda i: (i, 0, 0)),
        ],
        out_shape=[jax.ShapeDtypeStruct((t, hq), BF16), jax.ShapeDtypeStruct((t, hq), BF16),
                   jax.ShapeDtypeStruct((t // TM_MLA, hv, TM_MLA), BF16)],
        compiler_params=_cparams("parallel"),
        name="mla_prep",
    )(proj, cs, qn_g.reshape(1, -1), kvn_g.reshape(1, -1), wq_ext, wkv)


def _attn_kernel(q_ref, k_ref, vt_ref, o_ref, acc_ref):
    tq = q_ref.shape[0]
    i = pl.program_id(2)
    acc_ref[...] = jnp.zeros_like(acc_ref)

    def step(j, ms, ls, mask):
        r0 = pl.multiple_of(j * tq, tq)
        new_m, new_l = [], []
        for hh in range(HEADS_ATT):
            qs = slice(hh * MLA_HEAD_PAD, (hh + 1) * MLA_HEAD_PAD)
            s = _dot_nt(k_ref[pl.ds(r0, tq), qs], q_ref[:, qs])
            if mask is not None:
                s = jnp.where(mask, s, -jnp.inf)
            m_new = jnp.maximum(ms[hh], jnp.max(s, axis=0, keepdims=True))
            alpha = jnp.exp2(ms[hh] - m_new)
            p = jnp.exp2(s - m_new)
            new_l.append(alpha * ls[hh] + jnp.sum(p, axis=0, keepdims=True))
            vt = vt_ref[j, hh * MLA_V:(hh + 1) * MLA_V, :]
            acc_ref[hh] = alpha * acc_ref[hh] + _dot(vt, p.astype(BF16))
            new_m.append(m_new)
        return tuple(new_m), tuple(new_l)

    m0 = tuple(jnp.full((1, tq), -jnp.inf, F32) for _ in range(HEADS_ATT))
    l0 = tuple(jnp.zeros((1, tq), F32) for _ in range(HEADS_ATT))
    ms, ls = lax.fori_loop(0, i, lambda j, c: step(j, c[0], c[1], None), (m0, l0))
    kk = lax.broadcasted_iota(jnp.int32, (tq, tq), 0)
    qq = lax.broadcasted_iota(jnp.int32, (tq, tq), 1)
    visible = lax.shift_right_logical(kk, 6) <= lax.shift_right_logical(qq, 6)
    ms, ls = step(i, ms, ls, visible)
    for hh in range(HEADS_ATT):
        o_ref[:, hh * MLA_V:(hh + 1) * MLA_V] = (acc_ref[hh] / ls[hh]).T.astype(o_ref.dtype)


def _attn_call(q, k, vt, batch, seq):
    t = q.shape[0]
    nq = seq // TQ_ATT
    qw = HEADS_ATT * MLA_HEAD_PAD
    vw = HEADS_ATT * MLA_V
    return pl.pallas_call(
        _attn_kernel,
        grid=(batch, MLA_HEADS // HEADS_ATT, nq),
        in_specs=[
            pl.BlockSpec((TQ_ATT, qw), lambda b, h, i: (b * nq + i, h)),
            pl.BlockSpec((seq, qw), lambda b, h, i: (b, h)),
            pl.BlockSpec((nq, vw, TQ_ATT), lambda b, h, i: (b, h, 0)),
        ],
        out_specs=pl.BlockSpec((TQ_ATT, vw), lambda b, h, i: (b * nq + i, h)),
        out_shape=jax.ShapeDtypeStruct((t, MLA_HEADS * MLA_V), BF16),
        scratch_shapes=[pltpu.VMEM((HEADS_ATT, MLA_V, TQ_ATT), F32)],
        compiler_params=_cparams("parallel", "parallel", "arbitrary"),
        name="mla_attention",
    )(q, k, vt)


def _merge_kernel(oa_ref, ob_ref, oc_ref, g0_ref, g1_ref, g2_ref, h_ref, wb_ref, wo_ref,
                  lg_ref, lb_ref, o32_ref, o16_ref):
    merged = _sigmoid(g0_ref[...].astype(F32)) * _dot(oa_ref[...], wb_ref[0])
    merged += _sigmoid(g1_ref[...].astype(F32)) * _dot(ob_ref[...], wb_ref[1])
    merged += _sigmoid(g2_ref[...].astype(F32)) * _dot(oc_ref[...], wb_ref[2])
    mix = _dot(merged.astype(BF16), wo_ref[...])
    y = _layer_norm(ALPHA * h_ref[...] + mix, lg_ref[...], lb_ref[...])
    o32_ref[...] = y
    o16_ref[...] = y.astype(BF16)


def _merge_call(o_a, o_b, o_c, proj, h32, w_branch, w_out, ln_g, ln_b):
    t, d = h32.shape
    row = pl.BlockSpec((TM_MERGE, d), lambda i: (i, 0))
    gate = lambda n: pl.BlockSpec((TM_MERGE, d), lambda i: (i, P_GATES // d + n))
    vec = pl.BlockSpec((1, d), lambda i: (0, 0))
    return pl.pallas_call(
        _merge_kernel,
        grid=(t // TM_MERGE,),
        in_specs=[row, row, row, gate(0), gate(1), gate(2), row,
                  pl.BlockSpec((N_BRANCHES, d, d), lambda i: (0, 0, 0)),
                  pl.BlockSpec((d, d), lambda i: (0, 0)), vec, vec],
        out_specs=[row, row],
        out_shape=[jax.ShapeDtypeStruct((t, d), F32), jax.ShapeDtypeStruct((t, d), BF16)],
        compiler_params=_cparams("parallel"),
        name="merge",
    )(o_a, o_b, o_c, proj, proj, proj, h32, w_branch.astype(BF16), w_out.astype(BF16),
      ln_g.reshape(1, d), ln_b.reshape(1, d))


def _first_argmax_mask(vals, iota, n):
    m = jnp.max(vals, axis=0, keepdims=True)
    idx = jnp.min(jnp.where(vals == m, iota, n), axis=0, keepdims=True)
    return iota == idx


def _router_kernel(h_ref, wt_ref, b_ref, comb_ref, rank_ref, cnt_ref):
    tm = h_ref.shape[0]
    h = h_ref[...]
    h_hi, h_lo = _split_bf16(h)
    w = wt_ref[...]
    w_hi, w_lo = _split_bf16(w)
    logits = _dot_nt(w_hi, h_hi) + _dot_nt(w_hi, h_lo) + _dot_nt(w_lo, h_hi)
    scores = _sigmoid(logits)
    biased = scores + b_ref[...]

    neg = -jnp.inf
    sub = lax.broadcasted_iota(jnp.int32, (GROUP_SIZE, tm), 0)
    grp_rows = []
    for g in range(N_GROUPS):
        blk = biased[g * GROUP_SIZE:(g + 1) * GROUP_SIZE, :]
        m1 = jnp.max(blk, axis=0, keepdims=True)
        first = _first_argmax_mask(blk, sub, GROUP_SIZE)
        m2 = jnp.max(jnp.where(first, neg, blk), axis=0, keepdims=True)
        grp_rows.append(m1 + m2)
    gs = jnp.concatenate(grp_rows, axis=0)
    gsel = jnp.zeros((N_GROUPS, tm), jnp.bool_)
    gi = lax.broadcasted_iota(jnp.int32, (N_GROUPS, tm), 0)
    for _ in range(TOPK_GROUPS):
        pick = _first_argmax_mask(gs, gi, N_GROUPS)
        gsel = gsel | pick
        gs = jnp.where(pick, neg, gs)
    emask = jnp.concatenate(
        [jnp.broadcast_to(gsel[g:g + 1, :], (GROUP_SIZE, tm)) for g in range(N_GROUPS)], axis=0)
    cand = jnp.where(emask, biased, neg)
    ei = lax.broadcasted_iota(jnp.int32, (N_EXPERTS, tm), 0)
    chosen = jnp.zeros((N_EXPERTS, tm), jnp.bool_)
    for _ in range(TOP_K):
        pick = _first_argmax_mask(cand, ei, N_EXPERTS)
        chosen = chosen | pick
        cand = jnp.where(pick, neg, cand)
    wsel = jnp.where(chosen, scores, 0.0)
    comb_ref[...] = wsel / jnp.sum(wsel, axis=0, keepdims=True) * ROUTED_SCALE

    r = lax.broadcasted_iota(jnp.int32, (tm, tm), 0)
    c = lax.broadcasted_iota(jnp.int32, (tm, tm), 1)
    upper = jnp.where(r < c, 1.0, 0.0).astype(BF16)
    sel = jnp.where(chosen, 1.0, 0.0)
    rank_ref[...] = _dot(sel.astype(BF16), upper)
    cnt = jnp.sum(sel, axis=1, keepdims=True)
    cnt_ref[...] = jnp.broadcast_to(cnt, (N_EXPERTS, LANES)).astype(jnp.int32)


def _router_call(h32, router_w, router_b):
    t, d = h32.shape
    nt = t // TM_MOE
    bias = jnp.broadcast_to(router_b.reshape(N_EXPERTS, 1), (N_EXPERTS, TM_MOE))
    comb, rank, cnt = pl.pallas_call(
        _router_kernel,
        grid=(nt,),
        in_specs=[pl.BlockSpec((TM_MOE, d), lambda i: (i, 0)),
                  pl.BlockSpec((N_EXPERTS, d), lambda i: (0, 0)),
                  pl.BlockSpec((N_EXPERTS, TM_MOE), lambda i: (0, 0))],
        out_specs=[pl.BlockSpec((N_EXPERTS, TM_MOE), lambda i: (0, i)),
                   pl.BlockSpec((N_EXPERTS, TM_MOE), lambda i: (0, i)),
                   pl.BlockSpec((N_EXPERTS, LANES), lambda i: (i, 0))],
        out_shape=[jax.ShapeDtypeStruct((N_EXPERTS, t), F32),
                   jax.ShapeDtypeStruct((N_EXPERTS, t), F32),
                   jax.ShapeDtypeStruct((nt * N_EXPERTS, LANES), jnp.int32)],
        compiler_params=_cparams("parallel"),
        name="router",
    )(h32, router_w.T, bias)
    return comb, rank, cnt[:, 0].reshape(nt, N_EXPERTS)


def _moe_plan(cnt, t):
    nt = cnt.shape[0]
    pad = (cnt + SEG_ALIGN - 1) // SEG_ALIGN * SEG_ALIGN
    lseg = jnp.cumsum(pad, axis=1) - pad
    ltot = jnp.sum(pad, axis=1)
    etot = jnp.sum(pad, axis=0)
    region = (etot + ROW_BLK - 1) // ROW_BLK * ROW_BLK
    rend = jnp.cumsum(region)
    gpos = (rend - region)[None, :] + jnp.cumsum(pad, axis=0) - pad
    nblk = (rend[-1] // ROW_BLK).astype(jnp.int32).reshape(1)
    blk_start = jnp.arange(_moe_blocks(t), dtype=jnp.int32) * ROW_BLK
    blk_expert = jnp.minimum(jnp.sum(rend[None, :] <= blk_start[:, None], axis=1), N_EXPERTS - 1)
    npiece = jnp.stack([jnp.sum(pad // BIG_PIECE, axis=1),
                        jnp.sum((pad % BIG_PIECE) // SEG_ALIGN, axis=1)], axis=1)
    i32 = lambda a: a.astype(jnp.int32)
    return i32(pad), i32(lseg), i32(gpos), i32(ltot), i32(npiece), nblk, i32(blk_expert)


def _moe_blocks(t):
    nt = t // TM_MOE
    rows = t * TOP_K + nt * N_EXPERTS * (SEG_ALIGN - 1) + N_EXPERTS * (ROW_BLK - 1)
    return -(-rows // ROW_BLK)


def _build_slot_matrix(p_ref, comb_ref, rank_ref, pad_ref, lseg_ref, i, weighted):
    tm = p_ref.shape[1]
    p_ref[...] = jnp.zeros_like(p_ref)
    rowi = lax.broadcasted_iota(jnp.int32, (SLOT_CHUNK, tm), 0).astype(F32)

    def expert_body(e, carry):
        rrow = rank_ref[pl.ds(e, 1), :]
        wrow = comb_ref[pl.ds(e, 1), :]
        base = lseg_ref[i, e]
        nch = lax.shift_right_logical(pad_ref[i, e] + (SLOT_CHUNK - 1), SLOT_CHUNK_LOG2)

        def chunk_body(c, carry2):
            off = c * SLOT_CHUNK
            hit = (rrow == rowi + off.astype(F32)) & (wrow > 0.0)
            val = jnp.where(hit, wrow if weighted else 1.0, 0.0).astype(BF16)
            p_ref[pl.ds(pl.multiple_of(base + off, SEG_ALIGN), SLOT_CHUNK), :] = val
            return carry2

        lax.fori_loop(0, nch, chunk_body, 0)
        return carry

    lax.fori_loop(0, N_EXPERTS, expert_body, 0)


def _segment_copies(pad_ref, lseg_ref, gpos_ref, tile, make_copy):
    def expert_body(e, carry):
        rows = pad_ref[tile, e]
        nbig = lax.shift_right_logical(rows, BIG_PIECE_LOG2)
        nsmall = lax.shift_right_logical(rows & (BIG_PIECE - 1), SEG_ALIGN_LOG2)
        l0 = lseg_ref[tile, e]
        g0 = gpos_ref[tile, e]

        def big(c, carry2):
            make_copy(pl.multiple_of(l0 + c * BIG_PIECE, SEG_ALIGN),
                      pl.multiple_of(g0 + c * BIG_PIECE, SEG_ALIGN), BIG_PIECE).start()
            return carry2

        lax.fori_loop(0, nbig, big, 0)
        l1 = l0 + nbig * BIG_PIECE
        g1 = g0 + nbig * BIG_PIECE

        def small(c, carry2):
            make_copy(pl.multiple_of(l1 + c * SEG_ALIGN, SEG_ALIGN),
                      pl.multiple_of(g1 + c * SEG_ALIGN, SEG_ALIGN), SEG_ALIGN).start()
            return carry2

        lax.fori_loop(0, nsmall, small, 0)
        return carry

    lax.fori_loop(0, N_EXPERTS, expert_body, 0)


def _wait_copies(npiece_ref, tile, make_copy):
    for col, rows in ((0, BIG_PIECE), (1, SEG_ALIGN)):
        def piece(c, carry, rows=rows):
            make_copy(0, 0, rows).wait()
            return carry

        lax.fori_loop(0, npiece_ref[tile, col], piece, 0)


def _dispatch_kernel(pad_ref, lseg_ref, gpos_ref, ltot_ref, npiece_ref, x_ref, comb_ref, rank_ref,
                     xs_hbm, p_ref, xs_ref, sem):
    i = pl.program_id(0)
    last = pl.num_programs(0) - 1
    buf = lax.rem(i, 2)
    _build_slot_matrix(p_ref, comb_ref, rank_ref, pad_ref, lseg_ref, i, weighted=False)
    ntot = ltot_ref[i]
    for cb in range(LOCAL_ROWS // SORT_BLK):
        @pl.when(cb * SORT_BLK < ntot)
        def _():
            rs = slice(cb * SORT_BLK, (cb + 1) * SORT_BLK)
            xs_ref[buf, rs, :] = _dot(p_ref[rs, :], x_ref[...]).astype(BF16)

    def copy_for(b):
        return lambda l0, g0, rows: pltpu.make_async_copy(
            xs_ref.at[b, pl.ds(l0, rows), :], xs_hbm.at[pl.ds(g0, rows), :], sem.at[b])

    _segment_copies(pad_ref, lseg_ref, gpos_ref, i, copy_for(buf))

    @pl.when(i > 0)
    def _():
        _wait_copies(npiece_ref, i - 1, copy_for(1 - buf))

    @pl.when(i == last)
    def _():
        _wait_copies(npiece_ref, i, copy_for(buf))


def _dispatch_call(h16, comb, rank, plan):
    t, d = h16.shape
    nt = t // TM_MOE
    pad, lseg, gpos, ltot, npiece, _, _ = plan
    grid_spec = pltpu.PrefetchScalarGridSpec(
        num_scalar_prefetch=5,
        grid=(nt,),
        in_specs=[
            pl.BlockSpec((TM_MOE, d), lambda i, *_: (i, 0)),
            pl.BlockSpec((N_EXPERTS, TM_MOE), lambda i, *_: (0, i)),
            pl.BlockSpec((N_EXPERTS, TM_MOE), lambda i, *_: (0, i)),
        ],
        out_specs=pl.BlockSpec(memory_space=pl.ANY),
        scratch_shapes=[
            pltpu.VMEM((LOCAL_ROWS, TM_MOE), BF16),
            pltpu.VMEM((2, LOCAL_ROWS, d), BF16),
            pltpu.SemaphoreType.DMA((2,)),
        ],
    )
    return pl.pallas_call(
        _dispatch_kernel,
        grid_spec=grid_spec,
        out_shape=jax.ShapeDtypeStruct((_moe_blocks(t) * ROW_BLK, d), BF16),
        compiler_params=_cparams("arbitrary"),
        name="moe_dispatch",
    )(pad, lseg, gpos, ltot, npiece, h16, comb, rank)


def _expert_ffn_kernel(nblk_ref, be_ref, x_ref, wgu_ref, wd_ref, o_ref):
    @pl.when(pl.program_id(0) < nblk_ref[0])
    def _():
        gu = _dot(x_ref[...], wgu_ref[0])
        hmid = _silu(gu[:, :D_EXPERT]) * gu[:, D_EXPERT:]
        o_ref[...] = _dot(hmid.astype(BF16), wd_ref[0]).astype(o_ref.dtype)


def _expert_ffn_call(xs, plan, w_gate, w_up, w_down):
    rows, d = xs.shape
    nblk, blk_expert = plan[-2:]
    wgu = jnp.concatenate([w_gate, w_up], axis=2).astype(BF16)
    wd = w_down.astype(BF16)
    live = lambda b, nblk, be: jnp.minimum(b, nblk[0] - 1)
    grid_spec = pltpu.PrefetchScalarGridSpec(
        num_scalar_prefetch=2,
        grid=(rows // ROW_BLK,),
        in_specs=[
            pl.BlockSpec((ROW_BLK, d), lambda b, nblk, be: (live(b, nblk, be), 0)),
            pl.BlockSpec((1, d, 2 * D_EXPERT), lambda b, nblk, be: (be[live(b, nblk, be)], 0, 0)),
            pl.BlockSpec((1, D_EXPERT, d), lambda b, nblk, be: (be[live(b, nblk, be)], 0, 0)),
        ],
        out_specs=pl.BlockSpec((ROW_BLK, d), lambda b, nblk, be: (live(b, nblk, be), 0)),
    )
    return pl.pallas_call(
        _expert_ffn_kernel,
        grid_spec=grid_spec,
        out_shape=jax.ShapeDtypeStruct((rows, d), BF16),
        compiler_params=_cparams("arbitrary"),
        name="moe_expert_ffn",
    )(nblk, blk_expert, xs, wgu, wd)


def _combine_kernel(pad_ref, lseg_ref, gpos_ref, ltot_ref, npiece_ref, ys_hbm, comb_ref, rank_ref,
                    h16_ref, h32_ref, p_ref, swgu_ref, swd_ref, pwi_ref, pwg_ref, lg_ref, lb_ref,
                    o32_ref, o16_ref, pw_ref, ys_ref, sem):
    i = pl.program_id(0)
    last = pl.num_programs(0) - 1
    buf = lax.rem(i, 2)

    def copy_for(b):
        return lambda l0, g0, rows: pltpu.make_async_copy(
            ys_hbm.at[pl.ds(g0, rows), :], ys_ref.at[b, pl.ds(l0, rows), :], sem.at[b])

    def fetch(tile, b):
        ys_ref[b] = jnp.zeros(ys_ref.shape[1:], ys_ref.dtype)
        _segment_copies(pad_ref, lseg_ref, gpos_ref, tile, copy_for(b))

    @pl.when(i == 0)
    def _():
        fetch(i, buf)

    @pl.when(i < last)
    def _():
        fetch(i + 1, 1 - buf)

    _build_slot_matrix(pw_ref, comb_ref, rank_ref, pad_ref, lseg_ref, i, weighted=True)
    x = h16_ref[...]
    gu = _dot(x, swgu_ref[...])
    shared = _dot((_silu(gu[:, :D_SHARED]) * gu[:, D_SHARED:]).astype(BF16), swd_ref[...])
    ple = _dot(p_ref[...].astype(BF16), pwi_ref[...]) * _sigmoid(_dot(x, pwg_ref[...]))
    z = ALPHA * h32_ref[...] + shared + ple

    _wait_copies(npiece_ref, i, copy_for(buf))
    z += _dot_tn(pw_ref[...], ys_ref[buf])
    y = _layer_norm(z, lg_ref[...], lb_ref[...])
    o32_ref[...] = y
    o16_ref[...] = y.astype(BF16)


def _combine_call(ys, comb, rank, plan, h16, h32, p, sw_gate, sw_up, sw_down, ple_w_in, ple_w_gate,
                  ln_g, ln_b):
    t, d = h32.shape
    nt = t // TM_MOE
    pad, lseg, gpos, ltot, npiece, _, _ = plan
    row = pl.BlockSpec((TM_MOE, d), lambda i, *_: (i, 0))
    vec = pl.BlockSpec((1, d), lambda i, *_: (0, 0))
    full = lambda a, b: pl.BlockSpec((a, b), lambda i, *_: (0, 0))
    route = pl.BlockSpec((N_EXPERTS, TM_MOE), lambda i, *_: (0, i))
    swgu = jnp.concatenate([sw_gate, sw_up], axis=1).astype(BF16)
    grid_spec = pltpu.PrefetchScalarGridSpec(
        num_scalar_prefetch=5,
        grid=(nt,),
        in_specs=[pl.BlockSpec(memory_space=pl.ANY), route, route, row, row,
                  pl.BlockSpec((TM_MOE, PLE_DIM), lambda i, *_: (i, 0)),
                  full(d, 2 * D_SHARED), full(D_SHARED, d), full(PLE_DIM, d), full(d, d), vec, vec],
        out_specs=[row, row],
        scratch_shapes=[
            pltpu.VMEM((LOCAL_ROWS, TM_MOE), BF16),
            pltpu.VMEM((2, LOCAL_ROWS, d), BF16),
            pltpu.SemaphoreType.DMA((2,)),
        ],
    )
    return pl.pallas_call(
        _combine_kernel,
        grid_spec=grid_spec,
        out_shape=[jax.ShapeDtypeStruct((t, d), F32), jax.ShapeDtypeStruct((t, d), BF16)],
        compiler_params=_cparams("arbitrary"),
        name="moe_combine_tail",
    )(pad, lseg, gpos, ltot, npiece, ys, comb, rank, h16, h32, p, swgu, sw_down.astype(BF16),
      ple_w_in.astype(BF16), ple_w_gate.astype(BF16), ln_g.reshape(1, d), ln_b.reshape(1, d))


def kernel(x, p, ln_in_g, ln_in_b, w_in, gla_fg_w, gla_fg_b, gla_norm_g, sg_norm_g, sg_norm_b, sg_w, sg_b, mla_qn_g, mla_w_uq, mla_kvn_g, mla_w_ukv, w_branch, w_out, ln1_g, ln1_b, router_w, router_b, exp_w_gate, exp_w_up, exp_w_down, sh_w_gate, sh_w_up, sh_w_down, ple_w_in, ple_w_gate, ln2_g, ln2_b):
    batch, seq, d = x.shape
    t = batch * seq
    depth = w_in.shape[0]
    h32, h16 = _ln_call(x.reshape(t, d), ln_in_g, ln_in_b)
    for i in range(depth):
        proj = _proj_call(h16, _prep_w_in(w_in[i]))
        o_a = _gla_call(proj, gla_fg_w[i], gla_fg_b[i], gla_norm_g[i], batch, seq)
        o_b = _sg_call(proj, sg_norm_g[i], sg_norm_b[i], sg_w[i], sg_b[i])
        q, k, vt = _mla_prep_call(proj, mla_qn_g[i], mla_w_uq[i], mla_kvn_g[i], mla_w_ukv[i], seq)
        o_c = _attn_call(q, k, vt, batch, seq)
        h32, h16 = _merge_call(o_a, o_b, o_c, proj, h32, w_branch[i], w_out[i], ln1_g[i], ln1_b[i])
        comb, rank, cnt = _router_call(h32, router_w[i], router_b[i])
        plan = _moe_plan(cnt, t)
        xs = _dispatch_call(h16, comb, rank, plan)
        ys = _expert_ffn_call(xs, plan, exp_w_gate[i], exp_w_up[i], exp_w_down[i])
        h32, h16 = _combine_call(ys, comb, rank, plan, h16, h32, p[i].reshape(t, -1), sh_w_gate[i],
                                 sh_w_up[i], sh_w_down[i], ple_w_in[i], ple_w_gate[i],
                                 ln2_g[i], ln2_b[i])
    return h32.reshape(batch, seq, d)
```

```python
import functools

import jax
import jax.numpy as jnp
from jax import lax
from jax.experimental import pallas as pl
from jax.experimental.pallas import tpu as pltpu

F32 = jnp.float32
BF16 = jnp.bfloat16

D_MODEL = 1024
DEPTH = 2
CHUNK = 64
GLA_HEADS, GLA_DK, GLA_DV, GLA_GATE_RANK, GLA_TAU = 4, 128, 256, 16, 16.0
SG_WIDTH, SG_GROUPS, SG_BLOCK = 1024, 4, 128
MLA_HEADS, MLA_Q_RANK, MLA_KV_RANK = 8, 384, 256
MLA_NOPE, MLA_ROPE, MLA_V = 128, 64, 128
MLA_QK = MLA_NOPE + MLA_ROPE
ROPE_THETA = 10000.0
N_BRANCHES = 3
N_EXPERTS, N_GROUPS, TOPK_GROUPS, TOP_K = 64, 8, 4, 8
GROUP_SIZE = N_EXPERTS // N_GROUPS
D_EXPERT, D_SHARED = 256, 256
ROUTED_SCALE = 2.5
PLE_DIM = 256
ALPHA = (2 * DEPTH) ** 0.25

LANES = 128
SUBLANES = 8
VMEM_LIMIT_BYTES = 56 * 1024 * 1024

P_SMALL = 1024
P_GFR = 384
P_CKV = 512
P_KROPE = 768
P_GATES = 1024
P_SG = 4096
P_GV = 6144
P_GR = 7168
P_GQ = 8192
P_GK = 8704
P_TOTAL = 9216
MLA_HEAD_PAD = 256

TM_LN = 512
TM_PROJ, TN_PROJ = 1024, 1024
R_GLA = 512
TM_SG = 256
TQ_ATT = 512
TM_MLA = TQ_ATT
HEADS_ATT = 2
LOG2_E = 1.4426950408889634
TM_MERGE = 512
TM_MOE = 512
SEG_ALIGN_LOG2 = 4
SEG_ALIGN = 1 << SEG_ALIGN_LOG2
SLOT_CHUNK_LOG2 = 6
SLOT_CHUNK = 1 << SLOT_CHUNK_LOG2
SORT_BLK = 512
ROW_BLK = 512
BIG_PIECE_LOG2 = 6
BIG_PIECE = 1 << BIG_PIECE_LOG2
LOCAL_ROWS = -(-(TM_MOE * TOP_K + N_EXPERTS * (SEG_ALIGN - 1) + SLOT_CHUNK) // SORT_BLK) * SORT_BLK


def _cparams(*sem):
    return pltpu.CompilerParams(dimension_semantics=sem, vmem_limit_bytes=VMEM_LIMIT_BYTES)


def _dot(a, b):
    return jnp.dot(a, b, preferred_element_type=F32)


def _dot_nt(a, b):
    return lax.dot_general(a, b, (((1,), (1,)), ((), ())), preferred_element_type=F32)


def _dot_tn(a, b):
    return lax.dot_general(a, b, (((0,), (0,)), ((), ())), preferred_element_type=F32)


def _layer_norm(x, g, b, eps=1e-5):
    mu = jnp.mean(x, axis=-1, keepdims=True)
    xc = x - mu
    var = jnp.mean(xc * xc, axis=-1, keepdims=True)
    return xc * lax.rsqrt(var + eps) * g + b


def _rms_norm(x, g, eps=1e-6):
    return x * lax.rsqrt(jnp.mean(x * x, axis=-1, keepdims=True) + eps) * g


def _sigmoid(x):
    return 1.0 / (1.0 + jnp.exp(-x))


def _silu(x):
    return x * _sigmoid(x)


def _split_bf16(x):
    hi = x.astype(BF16)
    lo = (x - hi.astype(F32)).astype(BF16)
    return hi, lo


def _ln_kernel(x_ref, g_ref, b_ref, o32_ref, o16_ref):
    y = _layer_norm(x_ref[...], g_ref[...], b_ref[...])
    o32_ref[...] = y
    o16_ref[...] = y.astype(BF16)


def _ln_call(x, g, b):
    t, d = x.shape
    row = pl.BlockSpec((TM_LN, d), lambda i: (i, 0))
    vec = pl.BlockSpec((1, d), lambda i: (0, 0))
    return pl.pallas_call(
        _ln_kernel,
        grid=(t // TM_LN,),
        in_specs=[row, vec, vec],
        out_specs=[row, row],
        out_shape=[jax.ShapeDtypeStruct((t, d), F32), jax.ShapeDtypeStruct((t, d), BF16)],
        compiler_params=_cparams("parallel"),
        name="ln_in",
    )(x, g.reshape(1, d), b.reshape(1, d))


def _proj_kernel(x_ref, w_ref, o_ref):
    o_ref[...] = _dot(x_ref[...], w_ref[...]).astype(o_ref.dtype)


def _proj_call(h16, w):
    t, k = h16.shape
    n = w.shape[1]
    return pl.pallas_call(
        _proj_kernel,
        grid=(t // TM_PROJ, n // TN_PROJ),
        in_specs=[pl.BlockSpec((TM_PROJ, k), lambda i, j: (i, 0)),
                  pl.BlockSpec((k, TN_PROJ), lambda i, j: (0, j))],
        out_specs=pl.BlockSpec((TM_PROJ, TN_PROJ), lambda i, j: (i, j)),
        out_shape=jax.ShapeDtypeStruct((t, n), BF16),
        compiler_params=_cparams("parallel", "arbitrary"),
        name="in_proj",
    )(h16, w)


def _prep_w_in(w_in):
    d = w_in.shape[0]
    gq, gk = w_in[:, 0:512], w_in[:, 512:1024]
    gv, gr = w_in[:, 1024:2048], w_in[:, 2048:3072]
    gfr = w_in[:, 3072:3088]
    sgz = w_in[:, 3088:5136]
    mq = w_in[:, 5136:5520]
    ckv = w_in[:, 5520:5776]
    kr = w_in[:, 5776:5840]
    gates = w_in[:, 5840:8912]
    half = MLA_ROPE // 2
    kr_rot = jnp.concatenate([-kr[:, half:], kr[:, :half]], axis=1)
    z = lambda n: jnp.zeros((d, n), w_in.dtype)
    small = jnp.concatenate([mq, gfr, z(112), ckv, kr, kr_rot, z(128)], axis=1)
    return jnp.concatenate([small, gates, sgz, gv, gr, gq, gk], axis=1).astype(BF16)


def _gla_kernel(q_ref, k_ref, v_ref, gr_ref, gf_ref, fgw_ref, fgb_ref, g_ref, o_ref,
                st_ref, qd_ref, ki_ref, ke_ref, dec_ref):
    rows = q_ref.shape[0]

    @pl.when(pl.program_id(1) == 0)
    def _():
        st_ref[...] = jnp.zeros_like(st_ref)

    pre = _dot(gf_ref[...], fgw_ref[...]) + fgb_ref[...]
    glog = -(jnp.maximum(-pre, 0.0) + jnp.log1p(jnp.exp(-jnp.abs(pre)))) * (1.0 / GLA_TAU)

    r = lax.broadcasted_iota(jnp.int32, (rows, rows), 0)
    c = lax.broadcasted_iota(jnp.int32, (rows, rows), 1)
    same = lax.shift_right_logical(r, 6) == lax.shift_right_logical(c, 6)
    tri = jnp.where(same & (c <= r), 1.0, 0.0).astype(BF16)
    blk = jnp.where(same, 1.0, 0.0).astype(BF16)
    hi, lo = _split_bf16(glog)
    bcum = _dot(tri, hi) + _dot(tri, lo)
    btot = _dot(blk, hi) + _dot(blk, lo)

    q = q_ref[...].astype(F32) * (GLA_DK ** -0.5)
    k = k_ref[...].astype(F32)
    qd_ref[...] = (q * jnp.exp(bcum)).astype(BF16)
    ki_ref[...] = (k * jnp.exp(-bcum)).astype(BF16)
    ke_ref[...] = (k * jnp.exp(btot - bcum)).astype(BF16)
    dec_ref[...] = jnp.exp(btot)

    tr = lax.broadcasted_iota(jnp.int32, (CHUNK, CHUNK), 0)
    tc = lax.broadcasted_iota(jnp.int32, (CHUNK, CHUNK), 1)
    causal = tc <= tr

    def chunk_body(ci, carry):
        r0 = pl.multiple_of(ci * CHUNK, CHUNK)
        rs = pl.ds(r0, CHUNK)
        for hd in range(GLA_HEADS):
            ks = slice(hd * GLA_DK, (hd + 1) * GLA_DK)
            vs = slice(hd * GLA_DV, (hd + 1) * GLA_DV)
            qd = qd_ref[rs, ks]
            ki = ki_ref[rs, ks]
            ke = ke_ref[rs, ks]
            v = v_ref[rs, vs]
            att = jnp.where(causal, _dot_nt(qd, ki), 0.0)
            st = st_ref[hd]
            o = _dot(att.astype(BF16), v) + _dot_nt(qd, st.astype(BF16))
            dec = dec_ref[pl.ds(r0, 1), ks]
            st_ref[hd] = st * dec + _dot_tn(v, ke)
            on = _rms_norm(o, g_ref[:, vs])
            gate = gr_ref[rs, vs].astype(F32)
            o_ref[rs, vs] = (on * _silu(gate)).astype(o_ref.dtype)
        return carry

    lax.fori_loop(0, rows // CHUNK, chunk_body, 0)


def _gla_call(proj, fg_w, fg_b, gla_g, batch, seq):
    t = proj.shape[0]
    nr = seq // R_GLA
    row = lambda i, j: i * nr + j
    fgw = jnp.zeros((LANES, GLA_HEADS * GLA_DK), F32).at[:GLA_GATE_RANK].set(fg_w).astype(BF16)
    hk = GLA_HEADS * GLA_DK
    hv = GLA_HEADS * GLA_DV
    return pl.pallas_call(
        _gla_kernel,
        grid=(batch, nr),
        in_specs=[
            pl.BlockSpec((R_GLA, hk), lambda i, j: (row(i, j), P_GQ // hk)),
            pl.BlockSpec((R_GLA, hk), lambda i, j: (row(i, j), P_GK // hk)),
            pl.BlockSpec((R_GLA, hv), lambda i, j: (row(i, j), P_GV // hv)),
            pl.BlockSpec((R_GLA, hv), lambda i, j: (row(i, j), P_GR // hv)),
            pl.BlockSpec((R_GLA, LANES), lambda i, j: (row(i, j), P_GFR // LANES)),
            pl.BlockSpec((LANES, hk), lambda i, j: (0, 0)),
            pl.BlockSpec((1, hk), lambda i, j: (0, 0)),
            pl.BlockSpec((1, hv), lambda i, j: (0, 0)),
        ],
        out_specs=pl.BlockSpec((R_GLA, hv), lambda i, j: (row(i, j), 0)),
        out_shape=jax.ShapeDtypeStruct((t, hv), BF16),
        scratch_shapes=[
            pltpu.VMEM((GLA_HEADS, GLA_DV, GLA_DK), F32),
            pltpu.VMEM((R_GLA, hk), BF16),
            pltpu.VMEM((R_GLA, hk), BF16),
            pltpu.VMEM((R_GLA, hk), BF16),
            pltpu.VMEM((R_GLA, hk), F32),
        ],
        compiler_params=_cparams("parallel", "arbitrary"),
        name="gla",
    )(proj, proj, proj, proj, proj, fgw, fg_b.reshape(1, hk), gla_g.reshape(1, hv))


def _gelu(x):
    return 0.5 * x * (1.0 + lax.erf(x * (2.0 ** -0.5)))


def _sg_kernel(u_ref, v_ref, g_ref, b_ref, ws_ref, bias_ref, o_ref):
    rows = u_ref.shape[0]
    gw = SG_WIDTH // SG_GROUPS
    u = _gelu(u_ref[...].astype(F32))
    v = _gelu(v_ref[...].astype(F32))
    vn = _layer_norm(v, g_ref[...], b_ref[...]).astype(BF16)
    tr = lax.broadcasted_iota(jnp.int32, (SG_BLOCK, SG_BLOCK), 0)
    tc = lax.broadcasted_iota(jnp.int32, (SG_BLOCK, SG_BLOCK), 1)
    tril = tc <= tr
    for g in range(SG_GROUPS):
        w = jnp.where(tril, ws_ref[g], 0.0).astype(BF16)
        cs = slice(g * gw, (g + 1) * gw)
        for blk in range(rows // SG_BLOCK):
            rs = slice(blk * SG_BLOCK, (blk + 1) * SG_BLOCK)
            sv = _dot(w, vn[rs, cs]) + bias_ref[:, cs]
            o_ref[rs, cs] = (u[rs, cs] * sv).astype(o_ref.dtype)


def _sg_call(proj, sg_g, sg_bn, sg_w, sg_b):
    t = proj.shape[0]
    gw = SG_WIDTH // SG_GROUPS
    bias = jnp.repeat(sg_b.T, gw, axis=1)
    return pl.pallas_call(
        _sg_kernel,
        grid=(t // TM_SG,),
        in_specs=[
            pl.BlockSpec((TM_SG, SG_WIDTH), lambda i: (i, P_SG // SG_WIDTH)),
            pl.BlockSpec((TM_SG, SG_WIDTH), lambda i: (i, P_SG // SG_WIDTH + 1)),
            pl.BlockSpec((1, SG_WIDTH), lambda i: (0, 0)),
            pl.BlockSpec((1, SG_WIDTH), lambda i: (0, 0)),
            pl.BlockSpec((SG_GROUPS, SG_BLOCK, SG_BLOCK), lambda i: (0, 0, 0)),
            pl.BlockSpec((SG_BLOCK, SG_WIDTH), lambda i: (0, 0)),
        ],
        out_specs=pl.BlockSpec((TM_SG, SG_WIDTH), lambda i: (i, 0)),
        out_shape=jax.ShapeDtypeStruct((t, SG_WIDTH), BF16),
        compiler_params=_cparams("parallel"),
        name="spatial_gating",
    )(proj, proj, sg_g.reshape(1, -1), sg_bn.reshape(1, -1), sg_w, bias)


def _mla_prep_kernel(sm_ref, cs_ref, qg_ref, kg_ref, wq_ref, wkv_ref, q_ref, k_ref, vt_ref):
    rows = sm_ref.shape[0]
    mq = sm_ref[:, 0:MLA_Q_RANK].astype(F32)
    qn = _rms_norm(mq, qg_ref[...]).astype(BF16)
    qf = _dot(qn, wq_ref[...])
    ckv = sm_ref[:, P_CKV:P_CKV + MLA_KV_RANK].astype(F32)
    cn = _rms_norm(ckv, kg_ref[...]).astype(BF16)
    kv = _dot(cn, wkv_ref[...])

    cs = cs_ref[...]
    lane = lax.broadcasted_iota(jnp.int32, (rows, LANES), 1)
    low = lane < MLA_ROPE

    def rope(pair):
        t = pair * cs
        return jnp.where(low, t + pltpu.roll(t, MLA_ROPE, 1), 0.0)

    scale = (MLA_QK ** -0.5) * LOG2_E
    kr = rope(sm_ref[:, P_KROPE:P_KROPE + LANES].astype(F32)).astype(BF16)
    for h in range(MLA_HEADS):
        c0 = h * MLA_HEAD_PAD
        q_ref[:, c0:c0 + MLA_NOPE] = (qf[:, c0:c0 + MLA_NOPE] * scale).astype(BF16)
        q_ref[:, c0 + MLA_NOPE:c0 + MLA_HEAD_PAD] = (
            rope(qf[:, c0 + MLA_NOPE:c0 + MLA_HEAD_PAD]) * scale).astype(BF16)
        k_ref[:, c0:c0 + MLA_NOPE] = kv[:, h * MLA_NOPE:(h + 1) * MLA_NOPE].astype(BF16)
        k_ref[:, c0 + MLA_NOPE:c0 + MLA_HEAD_PAD] = kr
    vt_ref[0] = kv[:, MLA_HEADS * MLA_NOPE:].T.astype(BF16)


def _mla_prep_call(proj, qn_g, w_uq, kvn_g, w_ukv, seq):
    t = proj.shape[0]
    half = MLA_ROPE // 2
    wq = w_uq.reshape(MLA_Q_RANK, MLA_HEADS, MLA_QK)
    wr = wq[:, :, MLA_NOPE:]
    wr_rot = jnp.concatenate([-wr[:, :, half:], wr[:, :, :half]], axis=2)
    wq_ext = jnp.concatenate([wq[:, :, :MLA_NOPE], wr, wr_rot], axis=2)
    wq_ext = wq_ext.reshape(MLA_Q_RANK, MLA_HEADS * MLA_HEAD_PAD).astype(BF16)
    wkv = w_ukv.reshape(MLA_KV_RANK, MLA_HEADS, MLA_NOPE + MLA_V)
    wkv = jnp.concatenate([wkv[:, :, :MLA_NOPE].reshape(MLA_KV_RANK, -1),
                           wkv[:, :, MLA_NOPE:].reshape(MLA_KV_RANK, -1)], axis=1).astype(BF16)
    inv = ROPE_THETA ** (-jnp.arange(half, dtype=F32) / half)
    ang = jnp.arange(seq, dtype=F32)[:, None] * inv
    cs = jnp.concatenate([jnp.cos(ang), jnp.cos(ang), jnp.sin(ang), jnp.sin(ang)], axis=1)
    ns = seq // TM_MLA
    hq = MLA_HEADS * MLA_HEAD_PAD
    hv = MLA_HEADS * MLA_V
    return pl.pallas_call(
        _mla_prep_kernel,
        grid=(t // TM_MLA,),
        in_specs=[
            pl.BlockSpec((TM_MLA, P_SMALL), lambda i: (i, 0)),
            pl.BlockSpec((TM_MLA, LANES), lambda i: (i % ns, 0)),
            pl.BlockSpec((1, MLA_Q_RANK), lambda i: (0, 0)),
            pl.BlockSpec((1, MLA_KV_RANK), lambda i: (0, 0)),
            pl.BlockSpec((MLA_Q_RANK, hq), lambda i: (0, 0)),
            pl.BlockSpec((MLA_KV_RANK, MLA_HEADS * (MLA_NOPE + MLA_V)), lambda i: (0, 0)),
        ],
        out_specs=[
            pl.BlockSpec((TM_MLA, hq), lambda i: (i, 0)),
            pl.BlockSpec((TM_MLA, hq), lambda i: (i, 0)),
            pl.BlockSpec((1, hv, TM_MLA), lambda i: (i, 0, 0)),
        ],
        out_shape=[jax.ShapeDtypeStruct((t, hq), BF16), jax.ShapeDtypeStruct((t, hq), BF16),
                   jax.ShapeDtypeStruct((t // TM_MLA, hv, TM_MLA), BF16)],
        compiler_params=_cparams("parallel"),
        name="mla_prep",
    )(proj, cs, qn_g.reshape(1, -1), kvn_g.reshape(1, -1), wq_ext, wkv)


def _attn_kernel(q_ref, k_ref, vt_ref, o_ref, acc_ref):
    tq = q_ref.shape[0]
    i = pl.program_id(2)
    acc_ref[...] = jnp.zeros_like(acc_ref)

    def step(j, ms, ls, mask):
        r0 = pl.multiple_of(j * tq, tq)
        new_m, new_l = [], []
        for hh in range(HEADS_ATT):
            qs = slice(hh * MLA_HEAD_PAD, (hh + 1) * MLA_HEAD_PAD)
            s = _dot_nt(k_ref[pl.ds(r0, tq), qs], q_ref[:, qs])
            if mask is not None:
                s = jnp.where(mask, s, -jnp.inf)
            m_new = jnp.maximum(ms[hh], jnp.max(s, axis=0, keepdims=True))
            alpha = jnp.exp2(ms[hh] - m_new)
            p = jnp.exp2(s - m_new)
            new_l.append(alpha * ls[hh] + jnp.sum(p, axis=0, keepdims=True))
            vt = vt_ref[j, hh * MLA_V:(hh + 1) * MLA_V, :]
            acc_ref[hh] = alpha * acc_ref[hh] + _dot(vt, p.astype(BF16))
            new_m.append(m_new)
        return tuple(new_m), tuple(new_l)

    m0 = tuple(jnp.full((1, tq), -jnp.inf, F32) for _ in range(HEADS_ATT))
    l0 = tuple(jnp.zeros((1, tq), F32) for _ in range(HEADS_ATT))
    ms, ls = lax.fori_loop(0, i, lambda j, c: step(j, c[0], c[1], None), (m0, l0))
    kk = lax.broadcasted_iota(jnp.int32, (tq, tq), 0)
    qq = lax.broadcasted_iota(jnp.int32, (tq, tq), 1)
    visible = lax.shift_right_logical(kk, 6) <= lax.shift_right_logical(qq, 6)
    ms, ls = step(i, ms, ls, visible)
    for hh in range(HEADS_ATT):
        o_ref[:, hh * MLA_V:(hh + 1) * MLA_V] = (acc_ref[hh] / ls[hh]).T.astype(o_ref.dtype)


def _attn_call(q, k, vt, batch, seq):
    t = q.shape[0]
    nq = seq // TQ_ATT
    qw = HEADS_ATT * MLA_HEAD_PAD
    vw = HEADS_ATT * MLA_V
    return pl.pallas_call(
        _attn_kernel,
        grid=(batch, MLA_HEADS // HEADS_ATT, nq),
        in_specs=[
            pl.BlockSpec((TQ_ATT, qw), lambda b, h, i: (b * nq + i, h)),
            pl.BlockSpec((seq, qw), lambda b, h, i: (b, h)),
            pl.BlockSpec((nq, vw, TQ_ATT), lambda b, h, i: (b, h, 0)),
        ],
        out_specs=pl.BlockSpec((TQ_ATT, vw), lambda b, h, i: (b * nq + i, h)),
        out_shape=jax.ShapeDtypeStruct((t, MLA_HEADS * MLA_V), BF16),
        scratch_shapes=[pltpu.VMEM((HEADS_ATT, MLA_V, TQ_ATT), F32)],
        compiler_params=_cparams("parallel", "parallel", "arbitrary"),
        name="mla_attention",
    )(q, k, vt)


def _merge_kernel(oa_ref, ob_ref, oc_ref, g0_ref, g1_ref, g2_ref, h_ref, wb_ref, wo_ref,
                  lg_ref, lb_ref, o32_ref, o16_ref):
    merged = _sigmoid(g0_ref[...].astype(F32)) * _dot(oa_ref[...], wb_ref[0])
    merged += _sigmoid(g1_ref[...].astype(F32)) * _dot(ob_ref[...], wb_ref[1])
    merged += _sigmoid(g2_ref[...].astype(F32)) * _dot(oc_ref[...], wb_ref[2])
    mix = _dot(merged.astype(BF16), wo_ref[...])
    y = _layer_norm(ALPHA * h_ref[...] + mix, lg_ref[...], lb_ref[...])
    o32_ref[...] = y
    o16_ref[...] = y.astype(BF16)


def _merge_call(o_a, o_b, o_c, proj, h32, w_branch, w_out, ln_g, ln_b):
    t, d = h32.shape
    row = pl.BlockSpec((TM_MERGE, d), lambda i: (i, 0))
    gate = lambda n: pl.BlockSpec((TM_MERGE, d), lambda i: (i, P_GATES // d + n))
    vec = pl.BlockSpec((1, d), lambda i: (0, 0))
    return pl.pallas_call(
        _merge_kernel,
        grid=(t // TM_MERGE,),
        in_specs=[row, row, row, gate(0), gate(1), gate(2), row,
                  pl.BlockSpec((N_BRANCHES, d, d), lambda i: (0, 0, 0)),
                  pl.BlockSpec((d, d), lambda i: (0, 0)), vec, vec],
        out_specs=[row, row],
        out_shape=[jax.ShapeDtypeStruct((t, d), F32), jax.ShapeDtypeStruct((t, d), BF16)],
        compiler_params=_cparams("parallel"),
        name="merge",
    )(o_a, o_b, o_c, proj, proj, proj, h32, w_branch.astype(BF16), w_out.astype(BF16),
      ln_g.reshape(1, d), ln_b.reshape(1, d))


def _first_argmax_mask(vals, iota, n):
    m = jnp.max(vals, axis=0, keepdims=True)
    idx = jnp.min(jnp.where(vals == m, iota, n), axis=0, keepdims=True)
    return iota == idx


def _router_kernel(h_ref, wt_ref, b_ref, comb_ref, rank_ref, cnt_ref):
    tm = h_ref.shape[0]
    h = h_ref[...]
    h_hi, h_lo = _split_bf16(h)
    w = wt_ref[...]
    w_hi, w_lo = _split_bf16(w)
    logits = _dot_nt(w_hi, h_hi) + _dot_nt(w_hi, h_lo) + _dot_nt(w_lo, h_hi)
    scores = _sigmoid(logits)
    biased = scores + b_ref[...]

    neg = -jnp.inf
    sub = lax.broadcasted_iota(jnp.int32, (GROUP_SIZE, tm), 0)
    grp_rows = []
    for g in range(N_GROUPS):
        blk = biased[g * GROUP_SIZE:(g + 1) * GROUP_SIZE, :]
        m1 = jnp.max(blk, axis=0, keepdims=True)
        first = _first_argmax_mask(blk, sub, GROUP_SIZE)
        m2 = jnp.max(jnp.where(first, neg, blk), axis=0, keepdims=True)
        grp_rows.append(m1 + m2)
    gs = jnp.concatenate(grp_rows, axis=0)
    gsel = jnp.zeros((N_GROUPS, tm), jnp.bool_)
    gi = lax.broadcasted_iota(jnp.int32, (N_GROUPS, tm), 0)
    for _ in range(TOPK_GROUPS):
        pick = _first_argmax_mask(gs, gi, N_GROUPS)
        gsel = gsel | pick
        gs = jnp.where(pick, neg, gs)
    emask = jnp.concatenate(
        [jnp.broadcast_to(gsel[g:g + 1, :], (GROUP_SIZE, tm)) for g in range(N_GROUPS)], axis=0)
    cand = jnp.where(emask, biased, neg)
    ei = lax.broadcasted_iota(jnp.int32, (N_EXPERTS, tm), 0)
    chosen = jnp.zeros((N_EXPERTS, tm), jnp.bool_)
    for _ in range(TOP_K):
        pick = _first_argmax_mask(cand, ei, N_EXPERTS)
        chosen = chosen | pick
        cand = jnp.where(pick, neg, cand)
    wsel = jnp.where(chosen, scores, 0.0)
    comb_ref[...] = wsel / jnp.sum(wsel, axis=0, keepdims=True) * ROUTED_SCALE

    r = lax.broadcasted_iota(jnp.int32, (tm, tm), 0)
    c = lax.broadcasted_iota(jnp.int32, (tm, tm), 1)
    upper = jnp.where(r < c, 1.0, 0.0).astype(BF16)
    sel = jnp.where(chosen, 1.0, 0.0)
    rank_ref[...] = _dot(sel.astype(BF16), upper)
    cnt = jnp.sum(sel, axis=1, keepdims=True)
    cnt_ref[...] = jnp.broadcast_to(cnt, (N_EXPERTS, LANES)).astype(jnp.int32)


def _router_call(h32, router_w, router_b):
    t, d = h32.shape
    nt = t // TM_MOE
    bias = jnp.broadcast_to(router_b.reshape(N_EXPERTS, 1), (N_EXPERTS, TM_MOE))
    comb, rank, cnt = pl.pallas_call(
        _router_kernel,
        grid=(nt,),
        in_specs=[pl.BlockSpec((TM_MOE, d), lambda i: (i, 0)),
                  pl.BlockSpec((N_EXPERTS, d), lambda i: (0, 0)),
                  pl.BlockSpec((N_EXPERTS, TM_MOE), lambda i: (0, 0))],
        out_specs=[pl.BlockSpec((N_EXPERTS, TM_MOE), lambda i: (0, i)),
                   pl.BlockSpec((N_EXPERTS, TM_MOE), lambda i: (0, i)),
                   pl.BlockSpec((N_EXPERTS, LANES), lambda i: (i, 0))],
        out_shape=[jax.ShapeDtypeStruct((N_EXPERTS, t), F32),
                   jax.ShapeDtypeStruct((N_EXPERTS, t), F32),
                   jax.ShapeDtypeStruct((nt * N_EXPERTS, LANES), jnp.int32)],
        compiler_params=_cparams("parallel"),
        name="router",
    )(h32, router_w.T, bias)
    return comb, rank, cnt[:, 0].reshape(nt, N_EXPERTS)


def _moe_plan(cnt, t):
    nt = cnt.shape[0]
    pad = (cnt + SEG_ALIGN - 1) // SEG_ALIGN * SEG_ALIGN
    lseg = jnp.cumsum(pad, axis=1) - pad
    ltot = jnp.sum(pad, axis=1)
    etot = jnp.sum(pad, axis=0)
    region = (etot + ROW_BLK - 1) // ROW_BLK * ROW_BLK
    rend = jnp.cumsum(region)
    gpos = (rend - region)[None, :] + jnp.cumsum(pad, axis=0) - pad
    nblk = (rend[-1] // ROW_BLK).astype(jnp.int32).reshape(1)
    blk_start = jnp.arange(_moe_blocks(t), dtype=jnp.int32) * ROW_BLK
    blk_expert = jnp.minimum(jnp.sum(rend[None, :] <= blk_start[:, None], axis=1), N_EXPERTS - 1)
    npiece = jnp.stack([jnp.sum(pad // BIG_PIECE, axis=1),
                        jnp.sum((pad % BIG_PIECE) // SEG_ALIGN, axis=1)], axis=1)
    i32 = lambda a: a.astype(jnp.int32)
    return i32(pad), i32(lseg), i32(gpos), i32(ltot), i32(npiece), nblk, i32(blk_expert)


def _moe_blocks(t):
    nt = t // TM_MOE
    rows = t * TOP_K + nt * N_EXPERTS * (SEG_ALIGN - 1) + N_EXPERTS * (ROW_BLK - 1)
    return -(-rows // ROW_BLK)


def _zero_uncovered_blocks(ref2d, ntot):
    for cb in range(LOCAL_ROWS // SORT_BLK):
        @pl.when((cb + 1) * SORT_BLK > ntot)
        def _():
            ref2d[cb * SORT_BLK:(cb + 1) * SORT_BLK, :] = jnp.zeros((SORT_BLK, ref2d.shape[1]),
                                                                   ref2d.dtype)


def _build_slot_matrix(p_ref, comb_ref, rank_ref, pad_ref, lseg_ref, ntot, i, weighted):
    tm = p_ref.shape[1]
    _zero_uncovered_blocks(p_ref, ntot)
    rowi = lax.broadcasted_iota(jnp.int32, (SLOT_CHUNK, tm), 0).astype(F32)

    def expert_body(e, carry):
        rrow = rank_ref[pl.ds(e, 1), :]
        wrow = comb_ref[pl.ds(e, 1), :]
        base = lseg_ref[i, e]
        nch = lax.shift_right_logical(pad_ref[i, e] + (SLOT_CHUNK - 1), SLOT_CHUNK_LOG2)

        def chunk_body(c, carry2):
            off = c * SLOT_CHUNK
            hit = (rrow == rowi + off.astype(F32)) & (wrow > 0.0)
            val = jnp.where(hit, wrow if weighted else 1.0, 0.0).astype(BF16)
            p_ref[pl.ds(pl.multiple_of(base + off, SEG_ALIGN), SLOT_CHUNK), :] = val
            return carry2

        lax.fori_loop(0, nch, chunk_body, 0)
        return carry

    lax.fori_loop(0, N_EXPERTS, expert_body, 0)


def _segment_copies(pad_ref, lseg_ref, gpos_ref, tile, make_copy):
    def expert_body(e, carry):
        rows = pad_ref[tile, e]
        nbig = lax.shift_right_logical(rows, BIG_PIECE_LOG2)
        nsmall = lax.shift_right_logical(rows & (BIG_PIECE - 1), SEG_ALIGN_LOG2)
        l0 = lseg_ref[tile, e]
        g0 = gpos_ref[tile, e]

        def big(c, carry2):
            make_copy(pl.multiple_of(l0 + c * BIG_PIECE, SEG_ALIGN),
                      pl.multiple_of(g0 + c * BIG_PIECE, SEG_ALIGN), BIG_PIECE).start()
            return carry2

        lax.fori_loop(0, nbig, big, 0)
        l1 = l0 + nbig * BIG_PIECE
        g1 = g0 + nbig * BIG_PIECE

        def small(c, carry2):
            make_copy(pl.multiple_of(l1 + c * SEG_ALIGN, SEG_ALIGN),
                      pl.multiple_of(g1 + c * SEG_ALIGN, SEG_ALIGN), SEG_ALIGN).start()
            return carry2

        lax.fori_loop(0, nsmall, small, 0)
        return carry

    lax.fori_loop(0, N_EXPERTS, expert_body, 0)


def _wait_copies(npiece_ref, tile, make_copy):
    for col, rows in ((0, BIG_PIECE), (1, SEG_ALIGN)):
        def piece(c, carry, rows=rows):
            make_copy(0, 0, rows).wait()
            return carry

        lax.fori_loop(0, npiece_ref[tile, col], piece, 0)


def _dispatch_kernel(pad_ref, lseg_ref, gpos_ref, ltot_ref, npiece_ref, x_ref, comb_ref, rank_ref,
                     xs_hbm, p_ref, xs_ref, sem):
    i = pl.program_id(0)
    last = pl.num_programs(0) - 1
    buf = lax.rem(i, 2)
    ntot = ltot_ref[i]
    _build_slot_matrix(p_ref, comb_ref, rank_ref, pad_ref, lseg_ref, ntot, i, weighted=False)
    for cb in range(LOCAL_ROWS // SORT_BLK):
        @pl.when(cb * SORT_BLK < ntot)
        def _():
            rs = slice(cb * SORT_BLK, (cb + 1) * SORT_BLK)
            xs_ref[buf, rs, :] = _dot(p_ref[rs, :], x_ref[...]).astype(BF16)

    def copy_for(b):
        return lambda l0, g0, rows: pltpu.make_async_copy(
            xs_ref.at[b, pl.ds(l0, rows), :], xs_hbm.at[pl.ds(g0, rows), :], sem.at[b])

    _segment_copies(pad_ref, lseg_ref, gpos_ref, i, copy_for(buf))

    @pl.when(i > 0)
    def _():
        _wait_copies(npiece_ref, i - 1, copy_for(1 - buf))

    @pl.when(i == last)
    def _():
        _wait_copies(npiece_ref, i, copy_for(buf))


def _dispatch_call(h16, comb, rank, plan):
    t, d = h16.shape
    nt = t // TM_MOE
    pad, lseg, gpos, ltot, npiece, _, _ = plan
    grid_spec = pltpu.PrefetchScalarGridSpec(
        num_scalar_prefetch=5,
        grid=(nt,),
        in_specs=[
            pl.BlockSpec((TM_MOE, d), lambda i, *_: (i, 0)),
            pl.BlockSpec((N_EXPERTS, TM_MOE), lambda i, *_: (0, i)),
            pl.BlockSpec((N_EXPERTS, TM_MOE), lambda i, *_: (0, i)),
        ],
        out_specs=pl.BlockSpec(memory_space=pl.ANY),
        scratch_shapes=[
            pltpu.VMEM((LOCAL_ROWS, TM_MOE), BF16),
            pltpu.VMEM((2, LOCAL_ROWS, d), BF16),
            pltpu.SemaphoreType.DMA((2,)),
        ],
    )
    return pl.pallas_call(
        _dispatch_kernel,
        grid_spec=grid_spec,
        out_shape=jax.ShapeDtypeStruct((_moe_blocks(t) * ROW_BLK, d), BF16),
        compiler_params=_cparams("arbitrary"),
        name="moe_dispatch",
    )(pad, lseg, gpos, ltot, npiece, h16, comb, rank)


def _expert_ffn_kernel(nblk_ref, be_ref, x_ref, wg_ref, wu_ref, wd_ref, o_ref):
    @pl.when(pl.program_id(0) < nblk_ref[0])
    def _():
        x = x_ref[...]
        gate = _dot(x, wg_ref[0, 0].astype(BF16))
        up = _dot(x, wu_ref[0, 0].astype(BF16))
        hmid = (_silu(gate) * up).astype(BF16)
        o_ref[...] = _dot(hmid, wd_ref[0, 0].astype(BF16)).astype(o_ref.dtype)


def _expert_ffn_call(xs, plan, w_gate, w_up, w_down, layer):
    rows, d = xs.shape
    nblk, blk_expert = plan[-2:]
    live = lambda b, nblk, be: jnp.minimum(b, nblk[0] - 1)
    wmap = lambda b, nblk, be: (layer, be[live(b, nblk, be)], 0, 0)
    grid_spec = pltpu.PrefetchScalarGridSpec(
        num_scalar_prefetch=2,
        grid=(rows // ROW_BLK,),
        in_specs=[
            pl.BlockSpec((ROW_BLK, d), lambda b, nblk, be: (live(b, nblk, be), 0)),
            pl.BlockSpec((1, 1, d, D_EXPERT), wmap),
            pl.BlockSpec((1, 1, d, D_EXPERT), wmap),
            pl.BlockSpec((1, 1, D_EXPERT, d), wmap),
        ],
        out_specs=pl.BlockSpec((ROW_BLK, d), lambda b, nblk, be: (live(b, nblk, be), 0)),
    )
    return pl.pallas_call(
        _expert_ffn_kernel,
        grid_spec=grid_spec,
        out_shape=jax.ShapeDtypeStruct((rows, d), BF16),
        compiler_params=_cparams("arbitrary"),
        name="moe_expert_ffn",
    )(nblk, blk_expert, xs, w_gate, w_up, w_down)


def _combine_kernel(pad_ref, lseg_ref, gpos_ref, ltot_ref, npiece_ref, ys_hbm, comb_ref, rank_ref,
                    h16_ref, h32_ref, p_ref, swgu_ref, swd_ref, pwi_ref, pwg_ref, lg_ref, lb_ref,
                    o32_ref, o16_ref, pw_ref, ys_ref, sem):
    i = pl.program_id(0)
    last = pl.num_programs(0) - 1
    buf = lax.rem(i, 2)

    def copy_for(b):
        return lambda l0, g0, rows: pltpu.make_async_copy(
            ys_hbm.at[pl.ds(g0, rows), :], ys_ref.at[b, pl.ds(l0, rows), :], sem.at[b])

    def fetch(tile, b):
        _zero_uncovered_blocks(ys_ref.at[b], ltot_ref[tile])
        _segment_copies(pad_ref, lseg_ref, gpos_ref, tile, copy_for(b))

    @pl.when(i == 0)
    def _():
        fetch(i, buf)

    @pl.when(i < last)
    def _():
        fetch(i + 1, 1 - buf)

    _build_slot_matrix(pw_ref, comb_ref, rank_ref, pad_ref, lseg_ref, ltot_ref[i], i, weighted=True)
    x = h16_ref[...]
    gu = _dot(x, swgu_ref[...])
    shared = _dot((_silu(gu[:, :D_SHARED]) * gu[:, D_SHARED:]).astype(BF16), swd_ref[...])
    ple = _dot(p_ref[...].astype(BF16), pwi_ref[...]) * _sigmoid(_dot(x, pwg_ref[...]))
    z = ALPHA * h32_ref[...] + shared + ple

    _wait_copies(npiece_ref, i, copy_for(buf))
    z += _dot_tn(pw_ref[...], ys_ref[buf])
    y = _layer_norm(z, lg_ref[...], lb_ref[...])
    o32_ref[...] = y
    o16_ref[...] = y.astype(BF16)


def _combine_call(ys, comb, rank, plan, h16, h32, p, sw_gate, sw_up, sw_down, ple_w_in, ple_w_gate,
                  ln_g, ln_b):
    t, d = h32.shape
    nt = t // TM_MOE
    pad, lseg, gpos, ltot, npiece, _, _ = plan
    row = pl.BlockSpec((TM_MOE, d), lambda i, *_: (i, 0))
    vec = pl.BlockSpec((1, d), lambda i, *_: (0, 0))
    full = lambda a, b: pl.BlockSpec((a, b), lambda i, *_: (0, 0))
    route = pl.BlockSpec((N_EXPERTS, TM_MOE), lambda i, *_: (0, i))
    swgu = jnp.concatenate([sw_gate, sw_up], axis=1).astype(BF16)
    grid_spec = pltpu.PrefetchScalarGridSpec(
        num_scalar_prefetch=5,
        grid=(nt,),
        in_specs=[pl.BlockSpec(memory_space=pl.ANY), route, route, row, row,
                  pl.BlockSpec((TM_MOE, PLE_DIM), lambda i, *_: (i, 0)),
                  full(d, 2 * D_SHARED), full(D_SHARED, d), full(PLE_DIM, d), full(d, d), vec, vec],
        out_specs=[row, row],
        scratch_shapes=[
            pltpu.VMEM((LOCAL_ROWS, TM_MOE), BF16),
            pltpu.VMEM((2, LOCAL_ROWS, d), BF16),
            pltpu.SemaphoreType.DMA((2,)),
        ],
    )
    return pl.pallas_call(
        _combine_kernel,
        grid_spec=grid_spec,
        out_shape=[jax.ShapeDtypeStruct((t, d), F32), jax.ShapeDtypeStruct((t, d), BF16)],
        compiler_params=_cparams("arbitrary"),
        name="moe_combine_tail",
    )(pad, lseg, gpos, ltot, npiece, ys, comb, rank, h16, h32, p, swgu, sw_down.astype(BF16),
      ple_w_in.astype(BF16), ple_w_gate.astype(BF16), ln_g.reshape(1, d), ln_b.reshape(1, d))


def kernel(x, p, ln_in_g, ln_in_b, w_in, gla_fg_w, gla_fg_b, gla_norm_g, sg_norm_g, sg_norm_b, sg_w, sg_b, mla_qn_g, mla_w_uq, mla_kvn_g, mla_w_ukv, w_branch, w_out, ln1_g, ln1_b, router_w, router_b, exp_w_gate, exp_w_up, exp_w_down, sh_w_gate, sh_w_up, sh_w_down, ple_w_in, ple_w_gate, ln2_g, ln2_b):
    batch, seq, d = x.shape
    t = batch * seq
    depth = w_in.shape[0]
    h32, h16 = _ln_call(x.reshape(t, d), ln_in_g, ln_in_b)
    for i in range(depth):
        proj = _proj_call(h16, _prep_w_in(w_in[i]))
        o_a = _gla_call(proj, gla_fg_w[i], gla_fg_b[i], gla_norm_g[i], batch, seq)
        o_b = _sg_call(proj, sg_norm_g[i], sg_norm_b[i], sg_w[i], sg_b[i])
        q, k, vt = _mla_prep_call(proj, mla_qn_g[i], mla_w_uq[i], mla_kvn_g[i], mla_w_ukv[i], seq)
        o_c = _attn_call(q, k, vt, batch, seq)
        h32, h16 = _merge_call(o_a, o_b, o_c, proj, h32, w_branch[i], w_out[i], ln1_g[i], ln1_b[i])
        comb, rank, cnt = _router_call(h32, router_w[i], router_b[i])
        plan = _moe_plan(cnt, t)
        xs = _dispatch_call(h16, comb, rank, plan)
        ys = _expert_ffn_call(xs, plan, exp_w_gate, exp_w_up, exp_w_down, i)
        h32, h16 = _combine_call(ys, comb, rank, plan, h16, h32, p[i].reshape(t, -1), sh_w_gate[i],
                                 sh_w_up[i], sh_w_down[i], ple_w_in[i], ple_w_gate[i],
                                 ln2_g[i], ln2_b[i])
    return h32.reshape(batch, seq, d)
```

```python
import functools

import jax
import jax.numpy as jnp
from jax import lax
from jax.experimental import pallas as pl
from jax.experimental.pallas import tpu as pltpu

F32 = jnp.float32
BF16 = jnp.bfloat16

D_MODEL = 1024
DEPTH = 2
CHUNK = 64
GLA_HEADS, GLA_DK, GLA_DV, GLA_GATE_RANK, GLA_TAU = 4, 128, 256, 16, 16.0
SG_WIDTH, SG_GROUPS, SG_BLOCK = 1024, 4, 128
MLA_HEADS, MLA_Q_RANK, MLA_KV_RANK = 8, 384, 256
MLA_NOPE, MLA_ROPE, MLA_V = 128, 64, 128
MLA_QK = MLA_NOPE + MLA_ROPE
ROPE_THETA = 10000.0
N_BRANCHES = 3
N_EXPERTS, N_GROUPS, TOPK_GROUPS, TOP_K = 64, 8, 4, 8
GROUP_SIZE = N_EXPERTS // N_GROUPS
D_EXPERT, D_SHARED = 256, 256
ROUTED_SCALE = 2.5
PLE_DIM = 256
ALPHA = (2 * DEPTH) ** 0.25

LANES = 128
SUBLANES = 8
VMEM_LIMIT_BYTES = 56 * 1024 * 1024

P_SMALL = 1024
P_GFR = 384
P_CKV = 512
P_KROPE = 768
P_GATES = 1024
P_SG = 4096
P_GV = 6144
P_GR = 7168
P_GQ = 8192
P_GK = 8704
P_TOTAL = 9216
MLA_HEAD_PAD = 256

TM_LN = 512
TM_PROJ, TN_PROJ = 1024, 1024
R_GLA = 512
TM_SG = 256
TQ_ATT = 512
TK_ATT = 256
TM_MLA = TQ_ATT
MLA_V_EXT = MLA_V + 16
HEADS_ATT = 2
LOG2_E = 1.4426950408889634
TM_MERGE = 512
TM_MOE = 512
SEG_ALIGN_LOG2 = 4
SEG_ALIGN = 1 << SEG_ALIGN_LOG2
SLOT_CHUNK_LOG2 = 6
SLOT_CHUNK = 1 << SLOT_CHUNK_LOG2
SORT_BLK = 512
ROW_BLK = 512
BIG_PIECE_LOG2 = 6
BIG_PIECE = 1 << BIG_PIECE_LOG2
LOCAL_ROWS = -(-(TM_MOE * TOP_K + N_EXPERTS * (SEG_ALIGN - 1) + SLOT_CHUNK) // SORT_BLK) * SORT_BLK


def _cparams(*sem):
    return pltpu.CompilerParams(dimension_semantics=sem, vmem_limit_bytes=VMEM_LIMIT_BYTES)


def _dot(a, b):
    return jnp.dot(a, b, preferred_element_type=F32)


def _dot_nt(a, b):
    return lax.dot_general(a, b, (((1,), (1,)), ((), ())), preferred_element_type=F32)


def _dot_tn(a, b):
    return lax.dot_general(a, b, (((0,), (0,)), ((), ())), preferred_element_type=F32)


def _layer_norm(x, g, b, eps=1e-5):
    mu = jnp.mean(x, axis=-1, keepdims=True)
    xc = x - mu
    var = jnp.mean(xc * xc, axis=-1, keepdims=True)
    return xc * lax.rsqrt(var + eps) * g + b


def _rms_norm(x, g, eps=1e-6):
    return x * lax.rsqrt(jnp.mean(x * x, axis=-1, keepdims=True) + eps) * g


def _sigmoid(x):
    return 1.0 / (1.0 + jnp.exp(-x))


def _silu(x):
    return x * _sigmoid(x)


def _split_bf16(x):
    hi = x.astype(BF16)
    lo = (x - hi.astype(F32)).astype(BF16)
    return hi, lo


def _ln_kernel(x_ref, g_ref, b_ref, o32_ref, o16_ref):
    y = _layer_norm(x_ref[...], g_ref[...], b_ref[...])
    o32_ref[...] = y
    o16_ref[...] = y.astype(BF16)


def _ln_call(x, g, b):
    t, d = x.shape
    row = pl.BlockSpec((TM_LN, d), lambda i: (i, 0))
    vec = pl.BlockSpec((1, d), lambda i: (0, 0))
    return pl.pallas_call(
        _ln_kernel,
        grid=(t // TM_LN,),
        in_specs=[row, vec, vec],
        out_specs=[row, row],
        out_shape=[jax.ShapeDtypeStruct((t, d), F32), jax.ShapeDtypeStruct((t, d), BF16)],
        compiler_params=_cparams("parallel"),
        name="ln_in",
    )(x, g.reshape(1, d), b.reshape(1, d))


def _proj_kernel(x_ref, w_ref, o_ref):
    o_ref[...] = _dot(x_ref[...], w_ref[...]).astype(o_ref.dtype)


def _proj_call(h16, w):
    t, k = h16.shape
    n = w.shape[1]
    return pl.pallas_call(
        _proj_kernel,
        grid=(t // TM_PROJ, n // TN_PROJ),
        in_specs=[pl.BlockSpec((TM_PROJ, k), lambda i, j: (i, 0)),
                  pl.BlockSpec((k, TN_PROJ), lambda i, j: (0, j))],
        out_specs=pl.BlockSpec((TM_PROJ, TN_PROJ), lambda i, j: (i, j)),
        out_shape=jax.ShapeDtypeStruct((t, n), BF16),
        compiler_params=_cparams("parallel", "arbitrary"),
        name="in_proj",
    )(h16, w)


def _prep_w_in(w_in):
    d = w_in.shape[0]
    gq, gk = w_in[:, 0:512], w_in[:, 512:1024]
    gv, gr = w_in[:, 1024:2048], w_in[:, 2048:3072]
    gfr = w_in[:, 3072:3088]
    sgz = w_in[:, 3088:5136]
    mq = w_in[:, 5136:5520]
    ckv = w_in[:, 5520:5776]
    kr = w_in[:, 5776:5840]
    gates = w_in[:, 5840:8912]
    half = MLA_ROPE // 2
    kr_rot = jnp.concatenate([-kr[:, half:], kr[:, :half]], axis=1)
    z = lambda n: jnp.zeros((d, n), w_in.dtype)
    small = jnp.concatenate([mq, gfr, z(112), ckv, kr, kr_rot, z(128)], axis=1)
    return jnp.concatenate([small, gates, sgz, gv, gr, gq, gk], axis=1).astype(BF16)


def _gla_kernel(q_ref, k_ref, v_ref, gr_ref, gf_ref, fgw_ref, fgb_ref, g_ref, o_ref,
                st_ref, qd_ref, ki_ref, ke_ref, dec_ref):
    rows = q_ref.shape[0]

    @pl.when(pl.program_id(1) == 0)
    def _():
        st_ref[...] = jnp.zeros_like(st_ref)

    pre = _dot(gf_ref[...], fgw_ref[...]) + fgb_ref[...]
    glog = -(jnp.maximum(-pre, 0.0) + jnp.log1p(jnp.exp(-jnp.abs(pre)))) * (1.0 / GLA_TAU)

    r = lax.broadcasted_iota(jnp.int32, (rows, rows), 0)
    c = lax.broadcasted_iota(jnp.int32, (rows, rows), 1)
    same = lax.shift_right_logical(r, 6) == lax.shift_right_logical(c, 6)
    tri = jnp.where(same & (c <= r), 1.0, 0.0).astype(BF16)
    blk = jnp.where(same, 1.0, 0.0).astype(BF16)
    hi, lo = _split_bf16(glog)
    bcum = _dot(tri, hi) + _dot(tri, lo)
    btot = _dot(blk, hi) + _dot(blk, lo)

    q = q_ref[...].astype(F32) * (GLA_DK ** -0.5)
    k = k_ref[...].astype(F32)
    qd_ref[...] = (q * jnp.exp(bcum)).astype(BF16)
    ki_ref[...] = (k * jnp.exp(-bcum)).astype(BF16)
    ke_ref[...] = (k * jnp.exp(btot - bcum)).astype(BF16)
    dec_ref[...] = jnp.exp(btot)

    tr = lax.broadcasted_iota(jnp.int32, (CHUNK, CHUNK), 0)
    tc = lax.broadcasted_iota(jnp.int32, (CHUNK, CHUNK), 1)
    causal = tc <= tr

    def chunk_body(ci, carry):
        r0 = pl.multiple_of(ci * CHUNK, CHUNK)
        rs = pl.ds(r0, CHUNK)
        for hd in range(GLA_HEADS):
            ks = slice(hd * GLA_DK, (hd + 1) * GLA_DK)
            vs = slice(hd * GLA_DV, (hd + 1) * GLA_DV)
            qd = qd_ref[rs, ks]
            ki = ki_ref[rs, ks]
            ke = ke_ref[rs, ks]
            v = v_ref[rs, vs]
            att = jnp.where(causal, _dot_nt(qd, ki), 0.0)
            st = st_ref[hd]
            o = _dot(att.astype(BF16), v) + _dot_nt(qd, st.astype(BF16))
            dec = dec_ref[pl.ds(r0, 1), ks]
            st_ref[hd] = st * dec + _dot_tn(v, ke)
            on = _rms_norm(o, g_ref[:, vs])
            gate = gr_ref[rs, vs].astype(F32)
            o_ref[rs, vs] = (on * _silu(gate)).astype(o_ref.dtype)
        return carry

    lax.fori_loop(0, rows // CHUNK, chunk_body, 0)


def _gla_call(proj, fg_w, fg_b, gla_g, batch, seq):
    t = proj.shape[0]
    nr = seq // R_GLA
    row = lambda i, j: i * nr + j
    fgw = jnp.zeros((LANES, GLA_HEADS * GLA_DK), F32).at[:GLA_GATE_RANK].set(fg_w).astype(BF16)
    hk = GLA_HEADS * GLA_DK
    hv = GLA_HEADS * GLA_DV
    return pl.pallas_call(
        _gla_kernel,
        grid=(batch, nr),
        in_specs=[
            pl.BlockSpec((R_GLA, hk), lambda i, j: (row(i, j), P_GQ // hk)),
            pl.BlockSpec((R_GLA, hk), lambda i, j: (row(i, j), P_GK // hk)),
            pl.BlockSpec((R_GLA, hv), lambda i, j: (row(i, j), P_GV // hv)),
            pl.BlockSpec((R_GLA, hv), lambda i, j: (row(i, j), P_GR // hv)),
            pl.BlockSpec((R_GLA, LANES), lambda i, j: (row(i, j), P_GFR // LANES)),
            pl.BlockSpec((LANES, hk), lambda i, j: (0, 0)),
            pl.BlockSpec((1, hk), lambda i, j: (0, 0)),
            pl.BlockSpec((1, hv), lambda i, j: (0, 0)),
        ],
        out_specs=pl.BlockSpec((R_GLA, hv), lambda i, j: (row(i, j), 0)),
        out_shape=jax.ShapeDtypeStruct((t, hv), BF16),
        scratch_shapes=[
            pltpu.VMEM((GLA_HEADS, GLA_DV, GLA_DK), F32),
            pltpu.VMEM((R_GLA, hk), BF16),
            pltpu.VMEM((R_GLA, hk), BF16),
            pltpu.VMEM((R_GLA, hk), BF16),
            pltpu.VMEM((R_GLA, hk), F32),
        ],
        compiler_params=_cparams("parallel", "arbitrary"),
        name="gla",
    )(proj, proj, proj, proj, proj, fgw, fg_b.reshape(1, hk), gla_g.reshape(1, hv))


def _gelu(x):
    return 0.5 * x * (1.0 + lax.erf(x * (2.0 ** -0.5)))


def _sg_kernel(u_ref, v_ref, g_ref, b_ref, ws_ref, bias_ref, o_ref):
    rows = u_ref.shape[0]
    gw = SG_WIDTH // SG_GROUPS
    u = _gelu(u_ref[...].astype(F32))
    v = _gelu(v_ref[...].astype(F32))
    vn = _layer_norm(v, g_ref[...], b_ref[...]).astype(BF16)
    tr = lax.broadcasted_iota(jnp.int32, (SG_BLOCK, SG_BLOCK), 0)
    tc = lax.broadcasted_iota(jnp.int32, (SG_BLOCK, SG_BLOCK), 1)
    tril = tc <= tr
    for g in range(SG_GROUPS):
        w = jnp.where(tril, ws_ref[g], 0.0).astype(BF16)
        cs = slice(g * gw, (g + 1) * gw)
        for blk in range(rows // SG_BLOCK):
            rs = slice(blk * SG_BLOCK, (blk + 1) * SG_BLOCK)
            sv = _dot(w, vn[rs, cs]) + bias_ref[:, cs]
            o_ref[rs, cs] = (u[rs, cs] * sv).astype(o_ref.dtype)


def _sg_call(proj, sg_g, sg_bn, sg_w, sg_b):
    t = proj.shape[0]
    gw = SG_WIDTH // SG_GROUPS
    bias = jnp.repeat(sg_b.T, gw, axis=1)
    return pl.pallas_call(
        _sg_kernel,
        grid=(t // TM_SG,),
        in_specs=[
            pl.BlockSpec((TM_SG, SG_WIDTH), lambda i: (i, P_SG // SG_WIDTH)),
            pl.BlockSpec((TM_SG, SG_WIDTH), lambda i: (i, P_SG // SG_WIDTH + 1)),
            pl.BlockSpec((1, SG_WIDTH), lambda i: (0, 0)),
            pl.BlockSpec((1, SG_WIDTH), lambda i: (0, 0)),
            pl.BlockSpec((SG_GROUPS, SG_BLOCK, SG_BLOCK), lambda i: (0, 0, 0)),
            pl.BlockSpec((SG_BLOCK, SG_WIDTH), lambda i: (0, 0)),
        ],
        out_specs=pl.BlockSpec((TM_SG, SG_WIDTH), lambda i: (i, 0)),
        out_shape=jax.ShapeDtypeStruct((t, SG_WIDTH), BF16),
        compiler_params=_cparams("parallel"),
        name="spatial_gating",
    )(proj, proj, sg_g.reshape(1, -1), sg_bn.reshape(1, -1), sg_w, bias)


def _mla_prep_kernel(sm_ref, cs_ref, qg_ref, kg_ref, wq_ref, wkv_ref, q_ref, k_ref, vt_ref):
    rows = sm_ref.shape[0]
    mq = sm_ref[:, 0:MLA_Q_RANK].astype(F32)
    qn = _rms_norm(mq, qg_ref[...]).astype(BF16)
    qf = _dot(qn, wq_ref[...])
    ckv = sm_ref[:, P_CKV:P_CKV + MLA_KV_RANK].astype(F32)
    cn = _rms_norm(ckv, kg_ref[...]).astype(BF16)
    kv = _dot(cn, wkv_ref[...])

    cs = cs_ref[...]
    lane = lax.broadcasted_iota(jnp.int32, (rows, LANES), 1)
    low = lane < MLA_ROPE

    def rope(pair):
        t = pair * cs
        return jnp.where(low, t + pltpu.roll(t, MLA_ROPE, 1), 0.0)

    scale = (MLA_QK ** -0.5) * LOG2_E
    kr = rope(sm_ref[:, P_KROPE:P_KROPE + LANES].astype(F32)).astype(BF16)
    for h in range(MLA_HEADS):
        c0 = h * MLA_HEAD_PAD
        q_ref[:, c0:c0 + MLA_NOPE] = (qf[:, c0:c0 + MLA_NOPE] * scale).astype(BF16)
        q_ref[:, c0 + MLA_NOPE:c0 + MLA_HEAD_PAD] = (
            rope(qf[:, c0 + MLA_NOPE:c0 + MLA_HEAD_PAD]) * scale).astype(BF16)
        k_ref[:, c0:c0 + MLA_NOPE] = kv[:, h * MLA_NOPE:(h + 1) * MLA_NOPE].astype(BF16)
        k_ref[:, c0 + MLA_NOPE:c0 + MLA_HEAD_PAD] = kr
    vt = kv[:, MLA_HEADS * MLA_NOPE:].T.astype(BF16)
    ones = jnp.ones((MLA_V_EXT - MLA_V, TK_ATT), BF16)
    for blk in range(rows // TK_ATT):
        ks = slice(blk * TK_ATT, (blk + 1) * TK_ATT)
        for h in range(MLA_HEADS):
            r0 = h * MLA_V_EXT
            vt_ref[blk, r0:r0 + MLA_V, :] = vt[h * MLA_V:(h + 1) * MLA_V, ks]
            vt_ref[blk, r0 + MLA_V:r0 + MLA_V_EXT, :] = ones


def _mla_prep_call(proj, qn_g, w_uq, kvn_g, w_ukv, seq):
    t = proj.shape[0]
    half = MLA_ROPE // 2
    wq = w_uq.reshape(MLA_Q_RANK, MLA_HEADS, MLA_QK)
    wr = wq[:, :, MLA_NOPE:]
    wr_rot = jnp.concatenate([-wr[:, :, half:], wr[:, :, :half]], axis=2)
    wq_ext = jnp.concatenate([wq[:, :, :MLA_NOPE], wr, wr_rot], axis=2)
    wq_ext = wq_ext.reshape(MLA_Q_RANK, MLA_HEADS * MLA_HEAD_PAD).astype(BF16)
    wkv = w_ukv.reshape(MLA_KV_RANK, MLA_HEADS, MLA_NOPE + MLA_V)
    wkv = jnp.concatenate([wkv[:, :, :MLA_NOPE].reshape(MLA_KV_RANK, -1),
                           wkv[:, :, MLA_NOPE:].reshape(MLA_KV_RANK, -1)], axis=1).astype(BF16)
    inv = ROPE_THETA ** (-jnp.arange(half, dtype=F32) / half)
    ang = jnp.arange(seq, dtype=F32)[:, None] * inv
    cs = jnp.concatenate([jnp.cos(ang), jnp.cos(ang), jnp.sin(ang), jnp.sin(ang)], axis=1)
    ns = seq // TM_MLA
    hq = MLA_HEADS * MLA_HEAD_PAD
    hv = MLA_HEADS * MLA_V_EXT
    return pl.pallas_call(
        _mla_prep_kernel,
        grid=(t // TM_MLA,),
        in_specs=[
            pl.BlockSpec((TM_MLA, P_SMALL), lambda i: (i, 0)),
            pl.BlockSpec((TM_MLA, LANES), lambda i: (i % ns, 0)),
            pl.BlockSpec((1, MLA_Q_RANK), lambda i: (0, 0)),
            pl.BlockSpec((1, MLA_KV_RANK), lambda i: (0, 0)),
            pl.BlockSpec((MLA_Q_RANK, hq), lambda i: (0, 0)),
            pl.BlockSpec((MLA_KV_RANK, MLA_HEADS * (MLA_NOPE + MLA_V)), lambda i: (0, 0)),
        ],
        out_specs=[
            pl.BlockSpec((TM_MLA, hq), lambda i: (i, 0)),
            pl.BlockSpec((TM_MLA, hq), lambda i: (i, 0)),
            pl.BlockSpec((TM_MLA // TK_ATT, hv, TK_ATT), lambda i: (i, 0, 0)),
        ],
        out_shape=[jax.ShapeDtypeStruct((t, hq), BF16), jax.ShapeDtypeStruct((t, hq), BF16),
                   jax.ShapeDtypeStruct((t // TK_ATT, hv, TK_ATT), BF16)],
        compiler_params=_cparams("parallel"),
        name="mla_prep",
    )(proj, cs, qn_g.reshape(1, -1), kvn_g.reshape(1, -1), wq_ext, wkv)


def _attn_kernel(q_ref, k_ref, vt_ref, o_ref, acc_ref, sa_ref, sb_ref):
    tq = q_ref.shape[0]
    i = pl.program_id(2)
    acc_ref[...] = jnp.zeros_like(acc_ref)

    def scores(j, s_ref):
        r0 = pl.multiple_of(j * TK_ATT, TK_ATT)
        for hh in range(HEADS_ATT):
            qs = slice(hh * MLA_HEAD_PAD, (hh + 1) * MLA_HEAD_PAD)
            s_ref[hh] = _dot_nt(k_ref[pl.ds(r0, TK_ATT), qs], q_ref[:, qs])

    def consume(j, s_ref, ms, mask):
        new_m = []
        for hh in range(HEADS_ATT):
            s = s_ref[hh]
            if mask is not None:
                s = jnp.where(mask, s, -jnp.inf)
            m_new = jnp.maximum(ms[hh], jnp.max(s, axis=0, keepdims=True))
            alpha = jnp.exp2(ms[hh] - m_new)
            p = jnp.exp2(s - m_new).astype(BF16)
            vt = vt_ref[j, hh * MLA_V_EXT:(hh + 1) * MLA_V_EXT, :]
            acc_ref[hh] = alpha * acc_ref[hh] + _dot(vt, p)
            new_m.append(m_new)
        return tuple(new_m)

    def pair(jj, ms):
        j = 2 * jj
        scores(j + 1, sb_ref)
        ms = consume(j, sa_ref, ms, None)
        scores(j + 2, sa_ref)
        return consume(j + 1, sb_ref, ms, None)

    scores(0, sa_ref)
    m0 = tuple(jnp.full((1, tq), -jnp.inf, F32) for _ in range(HEADS_ATT))
    ms = lax.fori_loop(0, i * (tq // (2 * TK_ATT)), pair, m0)
    jd = i * (tq // TK_ATT)
    scores(jd + 1, sb_ref)
    kk = lax.shift_right_logical(lax.broadcasted_iota(jnp.int32, (TK_ATT, tq), 0), 6)
    qq = lax.shift_right_logical(lax.broadcasted_iota(jnp.int32, (TK_ATT, tq), 1), 6)
    ms = consume(jd, sa_ref, ms, kk <= qq)
    ms = consume(jd + 1, sb_ref, ms, kk + (TK_ATT // CHUNK) <= qq)
    for hh in range(HEADS_ATT):
        acc = acc_ref[hh]
        o = acc[:MLA_V, :] / acc[MLA_V:MLA_V + 1, :]
        o_ref[:, hh * MLA_V:(hh + 1) * MLA_V] = o.T.astype(o_ref.dtype)


def _attn_call(q, k, vt, batch, seq):
    t = q.shape[0]
    nq = seq // TQ_ATT
    qw = HEADS_ATT * MLA_HEAD_PAD
    vw = HEADS_ATT * MLA_V_EXT
    s_buf = pltpu.VMEM((HEADS_ATT, TK_ATT, TQ_ATT), F32)
    return pl.pallas_call(
        _attn_kernel,
        grid=(batch, MLA_HEADS // HEADS_ATT, nq),
        in_specs=[
            pl.BlockSpec((TQ_ATT, qw), lambda b, h, i: (b * nq + i, h)),
            pl.BlockSpec((seq, qw), lambda b, h, i: (b, h)),
            pl.BlockSpec((seq // TK_ATT, vw, TK_ATT), lambda b, h, i: (b, h, 0)),
        ],
        out_specs=pl.BlockSpec((TQ_ATT, HEADS_ATT * MLA_V), lambda b, h, i: (b * nq + i, h)),
        out_shape=jax.ShapeDtypeStruct((t, MLA_HEADS * MLA_V), BF16),
        scratch_shapes=[pltpu.VMEM((HEADS_ATT, MLA_V_EXT, TQ_ATT), F32), s_buf, s_buf],
        compiler_params=_cparams("parallel", "parallel", "arbitrary"),
        name="mla_attention",
    )(q, k, vt)


def _merge_kernel(oa_ref, ob_ref, oc_ref, g0_ref, g1_ref, g2_ref, h_ref, wb_ref, wo_ref,
                  lg_ref, lb_ref, o32_ref, o16_ref):
    merged = _sigmoid(g0_ref[...].astype(F32)) * _dot(oa_ref[...], wb_ref[0])
    merged += _sigmoid(g1_ref[...].astype(F32)) * _dot(ob_ref[...], wb_ref[1])
    merged += _sigmoid(g2_ref[...].astype(F32)) * _dot(oc_ref[...], wb_ref[2])
    mix = _dot(merged.astype(BF16), wo_ref[...])
    y = _layer_norm(ALPHA * h_ref[...] + mix, lg_ref[...], lb_ref[...])
    o32_ref[...] = y
    o16_ref[...] = y.astype(BF16)


def _merge_call(o_a, o_b, o_c, proj, h32, w_branch, w_out, ln_g, ln_b):
    t, d = h32.shape
    row = pl.BlockSpec((TM_MERGE, d), lambda i: (i, 0))
    gate = lambda n: pl.BlockSpec((TM_MERGE, d), lambda i: (i, P_GATES // d + n))
    vec = pl.BlockSpec((1, d), lambda i: (0, 0))
    return pl.pallas_call(
        _merge_kernel,
        grid=(t // TM_MERGE,),
        in_specs=[row, row, row, gate(0), gate(1), gate(2), row,
                  pl.BlockSpec((N_BRANCHES, d, d), lambda i: (0, 0, 0)),
                  pl.BlockSpec((d, d), lambda i: (0, 0)), vec, vec],
        out_specs=[row, row],
        out_shape=[jax.ShapeDtypeStruct((t, d), F32), jax.ShapeDtypeStruct((t, d), BF16)],
        compiler_params=_cparams("parallel"),
        name="merge",
    )(o_a, o_b, o_c, proj, proj, proj, h32, w_branch.astype(BF16), w_out.astype(BF16),
      ln_g.reshape(1, d), ln_b.reshape(1, d))


def _first_argmax_mask(vals, iota, n):
    m = jnp.max(vals, axis=0, keepdims=True)
    idx = jnp.min(jnp.where(vals == m, iota, n), axis=0, keepdims=True)
    return iota == idx


def _router_kernel(h_ref, wt_ref, b_ref, comb_ref, rank_ref, cnt_ref):
    tm = h_ref.shape[0]
    h = h_ref[...]
    h_hi, h_lo = _split_bf16(h)
    w = wt_ref[...]
    w_hi, w_lo = _split_bf16(w)
    logits = _dot_nt(w_hi, h_hi) + _dot_nt(w_hi, h_lo) + _dot_nt(w_lo, h_hi)
    scores = _sigmoid(logits)
    biased = scores + b_ref[...]

    neg = -jnp.inf
    sub = lax.broadcasted_iota(jnp.int32, (GROUP_SIZE, tm), 0)
    grp_rows = []
    for g in range(N_GROUPS):
        blk = biased[g * GROUP_SIZE:(g + 1) * GROUP_SIZE, :]
        m1 = jnp.max(blk, axis=0, keepdims=True)
        first = _first_argmax_mask(blk, sub, GROUP_SIZE)
        m2 = jnp.max(jnp.where(first, neg, blk), axis=0, keepdims=True)
        grp_rows.append(m1 + m2)
    gs = jnp.concatenate(grp_rows, axis=0)
    gsel = jnp.zeros((N_GROUPS, tm), jnp.bool_)
    gi = lax.broadcasted_iota(jnp.int32, (N_GROUPS, tm), 0)
    for _ in range(TOPK_GROUPS):
        pick = _first_argmax_mask(gs, gi, N_GROUPS)
        gsel = gsel | pick
        gs = jnp.where(pick, neg, gs)
    emask = jnp.concatenate(
        [jnp.broadcast_to(gsel[g:g + 1, :], (GROUP_SIZE, tm)) for g in range(N_GROUPS)], axis=0)
    cand = jnp.where(emask, biased, neg)
    ei = lax.broadcasted_iota(jnp.int32, (N_EXPERTS, tm), 0)
    chosen = jnp.zeros((N_EXPERTS, tm), jnp.bool_)
    for _ in range(TOP_K):
        pick = _first_argmax_mask(cand, ei, N_EXPERTS)
        chosen = chosen | pick
        cand = jnp.where(pick, neg, cand)
    wsel = jnp.where(chosen, scores, 0.0)
    comb_ref[...] = wsel / jnp.sum(wsel, axis=0, keepdims=True) * ROUTED_SCALE

    r = lax.broadcasted_iota(jnp.int32, (tm, tm), 0)
    c = lax.broadcasted_iota(jnp.int32, (tm, tm), 1)
    upper = jnp.where(r < c, 1.0, 0.0).astype(BF16)
    sel = jnp.where(chosen, 1.0, 0.0)
    rank_ref[...] = _dot(sel.astype(BF16), upper)
    cnt = jnp.sum(sel, axis=1, keepdims=True)
    cnt_ref[...] = jnp.broadcast_to(cnt, (N_EXPERTS, LANES)).astype(jnp.int32)


def _router_call(h32, router_w, router_b):
    t, d = h32.shape
    nt = t // TM_MOE
    bias = jnp.broadcast_to(router_b.reshape(N_EXPERTS, 1), (N_EXPERTS, TM_MOE))
    comb, rank, cnt = pl.pallas_call(
        _router_kernel,
        grid=(nt,),
        in_specs=[pl.BlockSpec((TM_MOE, d), lambda i: (i, 0)),
                  pl.BlockSpec((N_EXPERTS, d), lambda i: (0, 0)),
                  pl.BlockSpec((N_EXPERTS, TM_MOE), lambda i: (0, 0))],
        out_specs=[pl.BlockSpec((N_EXPERTS, TM_MOE), lambda i: (0, i)),
                   pl.BlockSpec((N_EXPERTS, TM_MOE), lambda i: (0, i)),
                   pl.BlockSpec((N_EXPERTS, LANES), lambda i: (i, 0))],
        out_shape=[jax.ShapeDtypeStruct((N_EXPERTS, t), F32),
                   jax.ShapeDtypeStruct((N_EXPERTS, t), F32),
                   jax.ShapeDtypeStruct((nt * N_EXPERTS, LANES), jnp.int32)],
        compiler_params=_cparams("parallel"),
        name="router",
    )(h32, router_w.T, bias)
    return comb, rank, cnt[:, 0].reshape(nt, N_EXPERTS)


def _moe_plan(cnt, t):
    nt = cnt.shape[0]
    pad = (cnt + SEG_ALIGN - 1) // SEG_ALIGN * SEG_ALIGN
    lseg = jnp.cumsum(pad, axis=1) - pad
    ltot = jnp.sum(pad, axis=1)
    etot = jnp.sum(pad, axis=0)
    region = (etot + ROW_BLK - 1) // ROW_BLK * ROW_BLK
    rend = jnp.cumsum(region)
    gpos = (rend - region)[None, :] + jnp.cumsum(pad, axis=0) - pad
    nblk = (rend[-1] // ROW_BLK).astype(jnp.int32).reshape(1)
    blk_start = jnp.arange(_moe_blocks(t), dtype=jnp.int32) * ROW_BLK
    blk_expert = jnp.minimum(jnp.sum(rend[None, :] <= blk_start[:, None], axis=1), N_EXPERTS - 1)
    npiece = jnp.stack([jnp.sum(pad // BIG_PIECE, axis=1),
                        jnp.sum((pad % BIG_PIECE) // SEG_ALIGN, axis=1)], axis=1)
    i32 = lambda a: a.astype(jnp.int32)
    return i32(pad), i32(lseg), i32(gpos), i32(ltot), i32(npiece), nblk, i32(blk_expert)


def _moe_blocks(t):
    nt = t // TM_MOE
    rows = t * TOP_K + nt * N_EXPERTS * (SEG_ALIGN - 1) + N_EXPERTS * (ROW_BLK - 1)
    return -(-rows // ROW_BLK)


def _zero_uncovered_blocks(ref2d, ntot):
    for cb in range(LOCAL_ROWS // SORT_BLK):
        @pl.when((cb + 1) * SORT_BLK > ntot)
        def _():
            ref2d[cb * SORT_BLK:(cb + 1) * SORT_BLK, :] = jnp.zeros((SORT_BLK, ref2d.shape[1]),
                                                                   ref2d.dtype)


def _build_slot_matrix(p_ref, comb_ref, rank_ref, pad_ref, lseg_ref, ntot, i, weighted):
    tm = p_ref.shape[1]
    _zero_uncovered_blocks(p_ref, ntot)
    rowi = lax.broadcasted_iota(jnp.int32, (SLOT_CHUNK, tm), 0).astype(F32)

    def expert_body(e, carry):
        rrow = rank_ref[pl.ds(e, 1), :]
        wrow = comb_ref[pl.ds(e, 1), :]
        base = lseg_ref[i, e]
        nch = lax.shift_right_logical(pad_ref[i, e] + (SLOT_CHUNK - 1), SLOT_CHUNK_LOG2)

        def chunk_body(c, carry2):
            off = c * SLOT_CHUNK
            hit = (rrow == rowi + off.astype(F32)) & (wrow > 0.0)
            val = jnp.where(hit, wrow if weighted else 1.0, 0.0).astype(BF16)
            p_ref[pl.ds(pl.multiple_of(base + off, SEG_ALIGN), SLOT_CHUNK), :] = val
            return carry2

        lax.fori_loop(0, nch, chunk_body, 0)
        return carry

    lax.fori_loop(0, N_EXPERTS, expert_body, 0)


def _segment_copies(pad_ref, lseg_ref, gpos_ref, tile, make_copy):
    def expert_body(e, carry):
        rows = pad_ref[tile, e]
        nbig = lax.shift_right_logical(rows, BIG_PIECE_LOG2)
        nsmall = lax.shift_right_logical(rows & (BIG_PIECE - 1), SEG_ALIGN_LOG2)
        l0 = lseg_ref[tile, e]
        g0 = gpos_ref[tile, e]

        def big(c, carry2):
            make_copy(pl.multiple_of(l0 + c * BIG_PIECE, SEG_ALIGN),
                      pl.multiple_of(g0 + c * BIG_PIECE, SEG_ALIGN), BIG_PIECE).start()
            return carry2

        lax.fori_loop(0, nbig, big, 0)
        l1 = l0 + nbig * BIG_PIECE
        g1 = g0 + nbig * BIG_PIECE

        def small(c, carry2):
            make_copy(pl.multiple_of(l1 + c * SEG_ALIGN, SEG_ALIGN),
                      pl.multiple_of(g1 + c * SEG_ALIGN, SEG_ALIGN), SEG_ALIGN).start()
            return carry2

        lax.fori_loop(0, nsmall, small, 0)
        return carry

    lax.fori_loop(0, N_EXPERTS, expert_body, 0)


def _wait_copies(npiece_ref, tile, make_copy):
    for col, rows in ((0, BIG_PIECE), (1, SEG_ALIGN)):
        def piece(c, carry, rows=rows):
            make_copy(0, 0, rows).wait()
            return carry

        lax.fori_loop(0, npiece_ref[tile, col], piece, 0)


def _dispatch_kernel(pad_ref, lseg_ref, gpos_ref, ltot_ref, npiece_ref, x_ref, comb_ref, rank_ref,
                     xs_hbm, p_ref, xs_ref, sem):
    i = pl.program_id(0)
    last = pl.num_programs(0) - 1
    buf = lax.rem(i, 2)
    ntot = ltot_ref[i]
    _build_slot_matrix(p_ref, comb_ref, rank_ref, pad_ref, lseg_ref, ntot, i, weighted=False)
    for cb in range(LOCAL_ROWS // SORT_BLK):
        @pl.when(cb * SORT_BLK < ntot)
        def _():
            rs = slice(cb * SORT_BLK, (cb + 1) * SORT_BLK)
            xs_ref[buf, rs, :] = _dot(p_ref[rs, :], x_ref[...]).astype(BF16)

    def copy_for(b):
        return lambda l0, g0, rows: pltpu.make_async_copy(
            xs_ref.at[b, pl.ds(l0, rows), :], xs_hbm.at[pl.ds(g0, rows), :], sem.at[b])

    _segment_copies(pad_ref, lseg_ref, gpos_ref, i, copy_for(buf))

    @pl.when(i > 0)
    def _():
        _wait_copies(npiece_ref, i - 1, copy_for(1 - buf))

    @pl.when(i == last)
    def _():
        _wait_copies(npiece_ref, i, copy_for(buf))


def _dispatch_call(h16, comb, rank, plan):
    t, d = h16.shape
    nt = t // TM_MOE
    pad, lseg, gpos, ltot, npiece, _, _ = plan
    grid_spec = pltpu.PrefetchScalarGridSpec(
        num_scalar_prefetch=5,
        grid=(nt,),
        in_specs=[
            pl.BlockSpec((TM_MOE, d), lambda i, *_: (i, 0)),
            pl.BlockSpec((N_EXPERTS, TM_MOE), lambda i, *_: (0, i)),
            pl.BlockSpec((N_EXPERTS, TM_MOE), lambda i, *_: (0, i)),
        ],
        out_specs=pl.BlockSpec(memory_space=pl.ANY),
        scratch_shapes=[
            pltpu.VMEM((LOCAL_ROWS, TM_MOE), BF16),
            pltpu.VMEM((2, LOCAL_ROWS, d), BF16),
            pltpu.SemaphoreType.DMA((2,)),
        ],
    )
    return pl.pallas_call(
        _dispatch_kernel,
        grid_spec=grid_spec,
        out_shape=jax.ShapeDtypeStruct((_moe_blocks(t) * ROW_BLK, d), BF16),
        compiler_params=_cparams("arbitrary"),
        name="moe_dispatch",
    )(pad, lseg, gpos, ltot, npiece, h16, comb, rank)


def _expert_ffn_kernel(nblk_ref, be_ref, x_ref, wg_ref, wu_ref, wd_ref, o_ref):
    @pl.when(pl.program_id(0) < nblk_ref[0])
    def _():
        x = x_ref[...]
        gate = _dot(x, wg_ref[0, 0].astype(BF16))
        up = _dot(x, wu_ref[0, 0].astype(BF16))
        hmid = (_silu(gate) * up).astype(BF16)
        o_ref[...] = _dot(hmid, wd_ref[0, 0].astype(BF16)).astype(o_ref.dtype)


def _expert_ffn_call(xs, plan, w_gate, w_up, w_down, layer):
    rows, d = xs.shape
    nblk, blk_expert = plan[-2:]
    live = lambda b, nblk, be: jnp.minimum(b, nblk[0] - 1)
    wmap = lambda b, nblk, be: (layer, be[live(b, nblk, be)], 0, 0)
    grid_spec = pltpu.PrefetchScalarGridSpec(
        num_scalar_prefetch=2,
        grid=(rows // ROW_BLK,),
        in_specs=[
            pl.BlockSpec((ROW_BLK, d), lambda b, nblk, be: (live(b, nblk, be), 0)),
            pl.BlockSpec((1, 1, d, D_EXPERT), wmap),
            pl.BlockSpec((1, 1, d, D_EXPERT), wmap),
            pl.BlockSpec((1, 1, D_EXPERT, d), wmap),
        ],
        out_specs=pl.BlockSpec((ROW_BLK, d), lambda b, nblk, be: (live(b, nblk, be), 0)),
    )
    return pl.pallas_call(
        _expert_ffn_kernel,
        grid_spec=grid_spec,
        out_shape=jax.ShapeDtypeStruct((rows, d), BF16),
        compiler_params=_cparams("arbitrary"),
        name="moe_expert_ffn",
    )(nblk, blk_expert, xs, w_gate, w_up, w_down)


def _combine_kernel(pad_ref, lseg_ref, gpos_ref, ltot_ref, npiece_ref, ys_hbm, comb_ref, rank_ref,
                    h16_ref, h32_ref, p_ref, swgu_ref, swd_ref, pwi_ref, pwg_ref, lg_ref, lb_ref,
                    o32_ref, o16_ref, pw_ref, ys_ref, sem):
    i = pl.program_id(0)
    last = pl.num_programs(0) - 1
    buf = lax.rem(i, 2)

    def copy_for(b):
        return lambda l0, g0, rows: pltpu.make_async_copy(
            ys_hbm.at[pl.ds(g0, rows), :], ys_ref.at[b, pl.ds(l0, rows), :], sem.at[b])

    def fetch(tile, b):
        _zero_uncovered_blocks(ys_ref.at[b], ltot_ref[tile])
        _segment_copies(pad_ref, lseg_ref, gpos_ref, tile, copy_for(b))

    @pl.when(i == 0)
    def _():
        fetch(i, buf)

    @pl.when(i < last)
    def _():
        fetch(i + 1, 1 - buf)

    _build_slot_matrix(pw_ref, comb_ref, rank_ref, pad_ref, lseg_ref, ltot_ref[i], i, weighted=True)
    x = h16_ref[...]
    gu = _dot(x, swgu_ref[...])
    shared = _dot((_silu(gu[:, :D_SHARED]) * gu[:, D_SHARED:]).astype(BF16), swd_ref[...])
    ple = _dot(p_ref[...].astype(BF16), pwi_ref[...]) * _sigmoid(_dot(x, pwg_ref[...]))
    z = ALPHA * h32_ref[...] + shared + ple

    _wait_copies(npiece_ref, i, copy_for(buf))
    z += _dot_tn(pw_ref[...], ys_ref[buf])
    y = _layer_norm(z, lg_ref[...], lb_ref[...])
    o32_ref[...] = y
    o16_ref[...] = y.astype(BF16)


def _combine_call(ys, comb, rank, plan, h16, h32, p, sw_gate, sw_up, sw_down, ple_w_in, ple_w_gate,
                  ln_g, ln_b):
    t, d = h32.shape
    nt = t // TM_MOE
    pad, lseg, gpos, ltot, npiece, _, _ = plan
    row = pl.BlockSpec((TM_MOE, d), lambda i, *_: (i, 0))
    vec = pl.BlockSpec((1, d), lambda i, *_: (0, 0))
    full = lambda a, b: pl.BlockSpec((a, b), lambda i, *_: (0, 0))
    route = pl.BlockSpec((N_EXPERTS, TM_MOE), lambda i, *_: (0, i))
    swgu = jnp.concatenate([sw_gate, sw_up], axis=1).astype(BF16)
    grid_spec = pltpu.PrefetchScalarGridSpec(
        num_scalar_prefetch=5,
        grid=(nt,),
        in_specs=[pl.BlockSpec(memory_space=pl.ANY), route, route, row, row,
                  pl.BlockSpec((TM_MOE, PLE_DIM), lambda i, *_: (i, 0)),
                  full(d, 2 * D_SHARED), full(D_SHARED, d), full(PLE_DIM, d), full(d, d), vec, vec],
        out_specs=[row, row],
        scratch_shapes=[
            pltpu.VMEM((LOCAL_ROWS, TM_MOE), BF16),
            pltpu.VMEM((2, LOCAL_ROWS, d), BF16),
            pltpu.SemaphoreType.DMA((2,)),
        ],
    )
    return pl.pallas_call(
        _combine_kernel,
        grid_spec=grid_spec,
        out_shape=[jax.ShapeDtypeStruct((t, d), F32), jax.ShapeDtypeStruct((t, d), BF16)],
        compiler_params=_cparams("arbitrary"),
        name="moe_combine_tail",
    )(pad, lseg, gpos, ltot, npiece, ys, comb, rank, h16, h32, p, swgu, sw_down.astype(BF16),
      ple_w_in.astype(BF16), ple_w_gate.astype(BF16), ln_g.reshape(1, d), ln_b.reshape(1, d))


def kernel(x, p, ln_in_g, ln_in_b, w_in, gla_fg_w, gla_fg_b, gla_norm_g, sg_norm_g, sg_norm_b, sg_w, sg_b, mla_qn_g, mla_w_uq, mla_kvn_g, mla_w_ukv, w_branch, w_out, ln1_g, ln1_b, router_w, router_b, exp_w_gate, exp_w_up, exp_w_down, sh_w_gate, sh_w_up, sh_w_down, ple_w_in, ple_w_gate, ln2_g, ln2_b):
    batch, seq, d = x.shape
    t = batch * seq
    depth = w_in.shape[0]
    h32, h16 = _ln_call(x.reshape(t, d), ln_in_g, ln_in_b)
    for i in range(depth):
        proj = _proj_call(h16, _prep_w_in(w_in[i]))
        o_a = _gla_call(proj, gla_fg_w[i], gla_fg_b[i], gla_norm_g[i], batch, seq)
        o_b = _sg_call(proj, sg_norm_g[i], sg_norm_b[i], sg_w[i], sg_b[i])
        q, k, vt = _mla_prep_call(proj, mla_qn_g[i], mla_w_uq[i], mla_kvn_g[i], mla_w_ukv[i], seq)
        o_c = _attn_call(q, k, vt, batch, seq)
        h32, h16 = _merge_call(o_a, o_b, o_c, proj, h32, w_branch[i], w_out[i], ln1_g[i], ln1_b[i])
        comb, rank, cnt = _router_call(h32, router_w[i], router_b[i])
        plan = _moe_plan(cnt, t)
        xs = _dispatch_call(h16, comb, rank, plan)
        ys = _expert_ffn_call(xs, plan, exp_w_gate, exp_w_up, exp_w_down, i)
        h32, h16 = _combine_call(ys, comb, rank, plan, h16, h32, p[i].reshape(t, -1), sh_w_gate[i],
                                 sh_w_up[i], sh_w_down[i], ple_w_in[i], ple_w_gate[i],
                                 ln2_g[i], ln2_b[i])
    return h32.reshape(batch, seq, d)
```

```python
import functools

import jax
import jax.numpy as jnp
from jax import lax
from jax.experimental import pallas as pl
from jax.experimental.pallas import tpu as pltpu

F32 = jnp.float32
BF16 = jnp.bfloat16

D_MODEL = 1024
DEPTH = 2
CHUNK = 64
GLA_HEADS, GLA_DK, GLA_DV, GLA_GATE_RANK, GLA_TAU = 4, 128, 256, 16, 16.0
SG_WIDTH, SG_GROUPS, SG_BLOCK = 1024, 4, 128
MLA_HEADS, MLA_Q_RANK, MLA_KV_RANK = 8, 384, 256
MLA_NOPE, MLA_ROPE, MLA_V = 128, 64, 128
MLA_QK = MLA_NOPE + MLA_ROPE
ROPE_THETA = 10000.0
N_BRANCHES = 3
N_EXPERTS, N_GROUPS, TOPK_GROUPS, TOP_K = 64, 8, 4, 8
GROUP_SIZE = N_EXPERTS // N_GROUPS
D_EXPERT, D_SHARED = 256, 256
ROUTED_SCALE = 2.5
PLE_DIM = 256
ALPHA = (2 * DEPTH) ** 0.25

LANES = 128
SUBLANES = 8
VMEM_LIMIT_BYTES = 56 * 1024 * 1024

P_SMALL = 1024
P_GFR = 384
P_CKV = 512
P_KROPE = 768
P_GATES = 1024
P_SG = 4096
P_GV = 6144
P_GR = 7168
P_GQ = 8192
P_GK = 8704
P_TOTAL = 9216
MLA_HEAD_PAD = 256

TM_LN = 512
TM_PROJ, TN_PROJ = 1024, 1024
R_GLA = 512
TM_SG = 256
TQ_ATT = 512
TK_ATT = 256
TM_MLA = TQ_ATT
MLA_V_EXT = MLA_V + 16
HEADS_ATT = 2
LOG2_E = 1.4426950408889634
TM_MERGE = 512
TM_MOE = 512
SEG_ALIGN_LOG2 = 4
SEG_ALIGN = 1 << SEG_ALIGN_LOG2
SLOT_CHUNK_LOG2 = 6
SLOT_CHUNK = 1 << SLOT_CHUNK_LOG2
SORT_BLK = 512
ROW_BLK = 512
BIG_PIECE_LOG2 = 6
BIG_PIECE = 1 << BIG_PIECE_LOG2
LOCAL_ROWS = -(-(TM_MOE * TOP_K + N_EXPERTS * (SEG_ALIGN - 1) + SLOT_CHUNK) // SORT_BLK) * SORT_BLK


def _cparams(*sem):
    return pltpu.CompilerParams(dimension_semantics=sem, vmem_limit_bytes=VMEM_LIMIT_BYTES)


def _dot(a, b):
    return jnp.dot(a, b, preferred_element_type=F32)


def _dot_nt(a, b):
    return lax.dot_general(a, b, (((1,), (1,)), ((), ())), preferred_element_type=F32)


def _dot_tn(a, b):
    return lax.dot_general(a, b, (((0,), (0,)), ((), ())), preferred_element_type=F32)


def _layer_norm(x, g, b, eps=1e-5):
    mu = jnp.mean(x, axis=-1, keepdims=True)
    xc = x - mu
    var = jnp.mean(xc * xc, axis=-1, keepdims=True)
    return xc * lax.rsqrt(var + eps) * g + b


def _rms_norm(x, g, eps=1e-6):
    return x * lax.rsqrt(jnp.mean(x * x, axis=-1, keepdims=True) + eps) * g


def _sigmoid(x):
    return 1.0 / (1.0 + jnp.exp(-x))


def _silu(x):
    return x * _sigmoid(x)


def _split_bf16(x):
    hi = x.astype(BF16)
    lo = (x - hi.astype(F32)).astype(BF16)
    return hi, lo


def _ln_kernel(x_ref, g_ref, b_ref, o32_ref, o16_ref):
    y = _layer_norm(x_ref[...], g_ref[...], b_ref[...])
    o32_ref[...] = y
    o16_ref[...] = y.astype(BF16)


def _ln_call(x, g, b):
    t, d = x.shape
    row = pl.BlockSpec((TM_LN, d), lambda i: (i, 0))
    vec = pl.BlockSpec((1, d), lambda i: (0, 0))
    return pl.pallas_call(
        _ln_kernel,
        grid=(t // TM_LN,),
        in_specs=[row, vec, vec],
        out_specs=[row, row],
        out_shape=[jax.ShapeDtypeStruct((t, d), F32), jax.ShapeDtypeStruct((t, d), BF16)],
        compiler_params=_cparams("parallel"),
        name="ln_in",
    )(x, g.reshape(1, d), b.reshape(1, d))


def _proj_kernel(x_ref, w_ref, o_ref):
    o_ref[...] = _dot(x_ref[...], w_ref[...]).astype(o_ref.dtype)


def _proj_call(h16, w):
    t, k = h16.shape
    n = w.shape[1]
    return pl.pallas_call(
        _proj_kernel,
        grid=(t // TM_PROJ, n // TN_PROJ),
        in_specs=[pl.BlockSpec((TM_PROJ, k), lambda i, j: (i, 0)),
                  pl.BlockSpec((k, TN_PROJ), lambda i, j: (0, j))],
        out_specs=pl.BlockSpec((TM_PROJ, TN_PROJ), lambda i, j: (i, j)),
        out_shape=jax.ShapeDtypeStruct((t, n), BF16),
        compiler_params=_cparams("parallel", "arbitrary"),
        name="in_proj",
    )(h16, w)


def _prep_w_in(w_in):
    d = w_in.shape[0]
    gq, gk = w_in[:, 0:512], w_in[:, 512:1024]
    gv, gr = w_in[:, 1024:2048], w_in[:, 2048:3072]
    gfr = w_in[:, 3072:3088]
    sgz = w_in[:, 3088:5136]
    mq = w_in[:, 5136:5520]
    ckv = w_in[:, 5520:5776]
    kr = w_in[:, 5776:5840]
    gates = w_in[:, 5840:8912]
    half = MLA_ROPE // 2
    kr_rot = jnp.concatenate([-kr[:, half:], kr[:, :half]], axis=1)
    z = lambda n: jnp.zeros((d, n), w_in.dtype)
    small = jnp.concatenate([mq, gfr, z(112), ckv, kr, kr_rot, z(128)], axis=1)
    return jnp.concatenate([small, gates, sgz, gv, gr, gq, gk], axis=1).astype(BF16)


def _gla_kernel(q_ref, k_ref, v_ref, gr_ref, gf_ref, fgw_ref, fgb_ref, g_ref, tri_ref, blk_ref,
                o_ref, st_ref, qd_ref, ki_ref, ke_ref, dec_ref):
    rows = q_ref.shape[0]

    @pl.when(pl.program_id(1) == 0)
    def _():
        st_ref[...] = jnp.zeros_like(st_ref)

    pre = _dot(gf_ref[...], fgw_ref[...]) + fgb_ref[...]
    glog = -(jnp.maximum(-pre, 0.0) + jnp.log1p(jnp.exp(-jnp.abs(pre)))) * (1.0 / GLA_TAU)

    hi, lo = _split_bf16(glog)
    bcum = _dot(tri_ref[...], hi) + _dot(tri_ref[...], lo)
    btot = _dot(blk_ref[...], hi) + _dot(blk_ref[...], lo)

    q = q_ref[...].astype(F32) * (GLA_DK ** -0.5)
    k = k_ref[...].astype(F32)
    qd_ref[...] = (q * jnp.exp(bcum)).astype(BF16)
    ki_ref[...] = (k * jnp.exp(-bcum)).astype(BF16)
    ke_ref[...] = (k * jnp.exp(btot - bcum)).astype(BF16)
    dec_ref[...] = jnp.exp(btot)

    tr = lax.broadcasted_iota(jnp.int32, (CHUNK, CHUNK), 0)
    tc = lax.broadcasted_iota(jnp.int32, (CHUNK, CHUNK), 1)
    causal = tc <= tr

    def chunk_body(ci, carry):
        r0 = pl.multiple_of(ci * CHUNK, CHUNK)
        rs = pl.ds(r0, CHUNK)
        for hd in range(GLA_HEADS):
            ks = slice(hd * GLA_DK, (hd + 1) * GLA_DK)
            vs = slice(hd * GLA_DV, (hd + 1) * GLA_DV)
            qd = qd_ref[rs, ks]
            ki = ki_ref[rs, ks]
            ke = ke_ref[rs, ks]
            v = v_ref[rs, vs]
            att = jnp.where(causal, _dot_nt(qd, ki), 0.0)
            st = st_ref[hd]
            o = _dot(att.astype(BF16), v) + _dot_nt(qd, st.astype(BF16))
            dec = dec_ref[pl.ds(r0, 1), ks]
            st_ref[hd] = st * dec + _dot_tn(v, ke)
            on = _rms_norm(o, g_ref[:, vs])
            gate = gr_ref[rs, vs].astype(F32)
            o_ref[rs, vs] = (on * _silu(gate)).astype(o_ref.dtype)
        return carry

    lax.fori_loop(0, rows // CHUNK, chunk_body, 0)


def _gla_call(proj, fg_w, fg_b, gla_g, batch, seq):
    t = proj.shape[0]
    nr = seq // R_GLA
    row = lambda i, j: i * nr + j
    fgw = jnp.zeros((LANES, GLA_HEADS * GLA_DK), F32).at[:GLA_GATE_RANK].set(fg_w).astype(BF16)
    hk = GLA_HEADS * GLA_DK
    hv = GLA_HEADS * GLA_DV
    ridx = jnp.arange(R_GLA, dtype=jnp.int32)
    same = (ridx[:, None] // CHUNK) == (ridx[None, :] // CHUNK)
    blk = same.astype(BF16)
    tri = (same & (ridx[None, :] <= ridx[:, None])).astype(BF16)
    return pl.pallas_call(
        _gla_kernel,
        grid=(batch, nr),
        in_specs=[
            pl.BlockSpec((R_GLA, hk), lambda i, j: (row(i, j), P_GQ // hk)),
            pl.BlockSpec((R_GLA, hk), lambda i, j: (row(i, j), P_GK // hk)),
            pl.BlockSpec((R_GLA, hv), lambda i, j: (row(i, j), P_GV // hv)),
            pl.BlockSpec((R_GLA, hv), lambda i, j: (row(i, j), P_GR // hv)),
            pl.BlockSpec((R_GLA, LANES), lambda i, j: (row(i, j), P_GFR // LANES)),
            pl.BlockSpec((LANES, hk), lambda i, j: (0, 0)),
            pl.BlockSpec((1, hk), lambda i, j: (0, 0)),
            pl.BlockSpec((1, hv), lambda i, j: (0, 0)),
            pl.BlockSpec((R_GLA, R_GLA), lambda i, j: (0, 0)),
            pl.BlockSpec((R_GLA, R_GLA), lambda i, j: (0, 0)),
        ],
        out_specs=pl.BlockSpec((R_GLA, hv), lambda i, j: (row(i, j), 0)),
        out_shape=jax.ShapeDtypeStruct((t, hv), BF16),
        scratch_shapes=[
            pltpu.VMEM((GLA_HEADS, GLA_DV, GLA_DK), F32),
            pltpu.VMEM((R_GLA, hk), BF16),
            pltpu.VMEM((R_GLA, hk), BF16),
            pltpu.VMEM((R_GLA, hk), BF16),
            pltpu.VMEM((R_GLA, hk), F32),
        ],
        compiler_params=_cparams("parallel", "arbitrary"),
        name="gla",
    )(proj, proj, proj, proj, proj, fgw, fg_b.reshape(1, hk), gla_g.reshape(1, hv), tri, blk)


def _gelu(x):
    return 0.5 * x * (1.0 + lax.erf(x * (2.0 ** -0.5)))


def _sg_kernel(u_ref, v_ref, g_ref, b_ref, ws_ref, bias_ref, o_ref):
    rows = u_ref.shape[0]
    gw = SG_WIDTH // SG_GROUPS
    u = _gelu(u_ref[...].astype(F32))
    v = _gelu(v_ref[...].astype(F32))
    vn = _layer_norm(v, g_ref[...], b_ref[...]).astype(BF16)
    tr = lax.broadcasted_iota(jnp.int32, (SG_BLOCK, SG_BLOCK), 0)
    tc = lax.broadcasted_iota(jnp.int32, (SG_BLOCK, SG_BLOCK), 1)
    tril = tc <= tr
    for g in range(SG_GROUPS):
        w = jnp.where(tril, ws_ref[g], 0.0).astype(BF16)
        cs = slice(g * gw, (g + 1) * gw)
        for blk in range(rows // SG_BLOCK):
            rs = slice(blk * SG_BLOCK, (blk + 1) * SG_BLOCK)
            sv = _dot(w, vn[rs, cs]) + bias_ref[:, cs]
            o_ref[rs, cs] = (u[rs, cs] * sv).astype(o_ref.dtype)


def _sg_call(proj, sg_g, sg_bn, sg_w, sg_b):
    t = proj.shape[0]
    gw = SG_WIDTH // SG_GROUPS
    bias = jnp.repeat(sg_b.T, gw, axis=1)
    return pl.pallas_call(
        _sg_kernel,
        grid=(t // TM_SG,),
        in_specs=[
            pl.BlockSpec((TM_SG, SG_WIDTH), lambda i: (i, P_SG // SG_WIDTH)),
            pl.BlockSpec((TM_SG, SG_WIDTH), lambda i: (i, P_SG // SG_WIDTH + 1)),
            pl.BlockSpec((1, SG_WIDTH), lambda i: (0, 0)),
            pl.BlockSpec((1, SG_WIDTH), lambda i: (0, 0)),
            pl.BlockSpec((SG_GROUPS, SG_BLOCK, SG_BLOCK), lambda i: (0, 0, 0)),
            pl.BlockSpec((SG_BLOCK, SG_WIDTH), lambda i: (0, 0)),
        ],
        out_specs=pl.BlockSpec((TM_SG, SG_WIDTH), lambda i: (i, 0)),
        out_shape=jax.ShapeDtypeStruct((t, SG_WIDTH), BF16),
        compiler_params=_cparams("parallel"),
        name="spatial_gating",
    )(proj, proj, sg_g.reshape(1, -1), sg_bn.reshape(1, -1), sg_w, bias)


def _mla_prep_kernel(sm_ref, cs_ref, qg_ref, kg_ref, wq_ref, wkv_ref, q_ref, k_ref, vt_ref):
    rows = sm_ref.shape[0]
    mq = sm_ref[:, 0:MLA_Q_RANK].astype(F32)
    qn = _rms_norm(mq, qg_ref[...]).astype(BF16)
    qf = _dot(qn, wq_ref[...])
    ckv = sm_ref[:, P_CKV:P_CKV + MLA_KV_RANK].astype(F32)
    cn = _rms_norm(ckv, kg_ref[...]).astype(BF16)
    kv = _dot(cn, wkv_ref[...])

    cs = cs_ref[...]
    lane = lax.broadcasted_iota(jnp.int32, (rows, LANES), 1)
    low = lane < MLA_ROPE

    def rope(pair):
        t = pair * cs
        return jnp.where(low, t + pltpu.roll(t, MLA_ROPE, 1), 0.0)

    scale = (MLA_QK ** -0.5) * LOG2_E
    kr = rope(sm_ref[:, P_KROPE:P_KROPE + LANES].astype(F32)).astype(BF16)
    for h in range(MLA_HEADS):
        c0 = h * MLA_HEAD_PAD
        q_ref[:, c0:c0 + MLA_NOPE] = (qf[:, c0:c0 + MLA_NOPE] * scale).astype(BF16)
        q_ref[:, c0 + MLA_NOPE:c0 + MLA_HEAD_PAD] = (
            rope(qf[:, c0 + MLA_NOPE:c0 + MLA_HEAD_PAD]) * scale).astype(BF16)
        k_ref[:, c0:c0 + MLA_NOPE] = kv[:, h * MLA_NOPE:(h + 1) * MLA_NOPE].astype(BF16)
        k_ref[:, c0 + MLA_NOPE:c0 + MLA_HEAD_PAD] = kr
    vt = kv[:, MLA_HEADS * MLA_NOPE:].T.astype(BF16)
    ones = jnp.ones((MLA_V_EXT - MLA_V, TK_ATT), BF16)
    for blk in range(rows // TK_ATT):
        ks = slice(blk * TK_ATT, (blk + 1) * TK_ATT)
        for h in range(MLA_HEADS):
            r0 = h * MLA_V_EXT
            vt_ref[blk, r0:r0 + MLA_V, :] = vt[h * MLA_V:(h + 1) * MLA_V, ks]
            vt_ref[blk, r0 + MLA_V:r0 + MLA_V_EXT, :] = ones


def _mla_prep_call(proj, qn_g, w_uq, kvn_g, w_ukv, seq):
    t = proj.shape[0]
    half = MLA_ROPE // 2
    wq = w_uq.reshape(MLA_Q_RANK, MLA_HEADS, MLA_QK)
    wr = wq[:, :, MLA_NOPE:]
    wr_rot = jnp.concatenate([-wr[:, :, half:], wr[:, :, :half]], axis=2)
    wq_ext = jnp.concatenate([wq[:, :, :MLA_NOPE], wr, wr_rot], axis=2)
    wq_ext = wq_ext.reshape(MLA_Q_RANK, MLA_HEADS * MLA_HEAD_PAD).astype(BF16)
    wkv = w_ukv.reshape(MLA_KV_RANK, MLA_HEADS, MLA_NOPE + MLA_V)
    wkv = jnp.concatenate([wkv[:, :, :MLA_NOPE].reshape(MLA_KV_RANK, -1),
                           wkv[:, :, MLA_NOPE:].reshape(MLA_KV_RANK, -1)], axis=1).astype(BF16)
    inv = ROPE_THETA ** (-jnp.arange(half, dtype=F32) / half)
    ang = jnp.arange(seq, dtype=F32)[:, None] * inv
    cs = jnp.concatenate([jnp.cos(ang), jnp.cos(ang), jnp.sin(ang), jnp.sin(ang)], axis=1)
    ns = seq // TM_MLA
    hq = MLA_HEADS * MLA_HEAD_PAD
    hv = MLA_HEADS * MLA_V_EXT
    return pl.pallas_call(
        _mla_prep_kernel,
        grid=(t // TM_MLA,),
        in_specs=[
            pl.BlockSpec((TM_MLA, P_SMALL), lambda i: (i, 0)),
            pl.BlockSpec((TM_MLA, LANES), lambda i: (i % ns, 0)),
            pl.BlockSpec((1, MLA_Q_RANK), lambda i: (0, 0)),
            pl.BlockSpec((1, MLA_KV_RANK), lambda i: (0, 0)),
            pl.BlockSpec((MLA_Q_RANK, hq), lambda i: (0, 0)),
            pl.BlockSpec((MLA_KV_RANK, MLA_HEADS * (MLA_NOPE + MLA_V)), lambda i: (0, 0)),
        ],
        out_specs=[
            pl.BlockSpec((TM_MLA, hq), lambda i: (i, 0)),
            pl.BlockSpec((TM_MLA, hq), lambda i: (i, 0)),
            pl.BlockSpec((TM_MLA // TK_ATT, hv, TK_ATT), lambda i: (i, 0, 0)),
        ],
        out_shape=[jax.ShapeDtypeStruct((t, hq), BF16), jax.ShapeDtypeStruct((t, hq), BF16),
                   jax.ShapeDtypeStruct((t // TK_ATT, hv, TK_ATT), BF16)],
        compiler_params=_cparams("parallel"),
        name="mla_prep",
    )(proj, cs, qn_g.reshape(1, -1), kvn_g.reshape(1, -1), wq_ext, wkv)


def _attn_kernel(q_ref, k_ref, vt_ref, o_ref, acc_ref, sa_ref, sb_ref):
    tq = q_ref.shape[0]
    i = pl.program_id(2)
    acc_ref[...] = jnp.zeros_like(acc_ref)

    def scores(j, s_ref):
        r0 = pl.multiple_of(j * TK_ATT, TK_ATT)
        for hh in range(HEADS_ATT):
            qs = slice(hh * MLA_HEAD_PAD, (hh + 1) * MLA_HEAD_PAD)
            s_ref[hh] = _dot_nt(k_ref[pl.ds(r0, TK_ATT), qs], q_ref[:, qs])

    def consume(j, s_ref, ms, mask):
        new_m = []
        for hh in range(HEADS_ATT):
            s = s_ref[hh]
            if mask is not None:
                s = jnp.where(mask, s, -jnp.inf)
            m_new = jnp.maximum(ms[hh], jnp.max(s, axis=0, keepdims=True))
            alpha = jnp.exp2(ms[hh] - m_new)
            p = jnp.exp2(s - m_new).astype(BF16)
            vt = vt_ref[j, hh * MLA_V_EXT:(hh + 1) * MLA_V_EXT, :]
            acc_ref[hh] = alpha * acc_ref[hh] + _dot(vt, p)
            new_m.append(m_new)
        return tuple(new_m)

    def pair(jj, ms):
        j = 2 * jj
        scores(j + 1, sb_ref)
        ms = consume(j, sa_ref, ms, None)
        scores(j + 2, sa_ref)
        return consume(j + 1, sb_ref, ms, None)

    scores(0, sa_ref)
    m0 = tuple(jnp.full((1, tq), -jnp.inf, F32) for _ in range(HEADS_ATT))
    ms = lax.fori_loop(0, i * (tq // (2 * TK_ATT)), pair, m0)
    jd = i * (tq // TK_ATT)
    scores(jd + 1, sb_ref)
    kk = lax.shift_right_logical(lax.broadcasted_iota(jnp.int32, (TK_ATT, tq), 0), 6)
    qq = lax.shift_right_logical(lax.broadcasted_iota(jnp.int32, (TK_ATT, tq), 1), 6)
    ms = consume(jd, sa_ref, ms, kk <= qq)
    ms = consume(jd + 1, sb_ref, ms, kk + (TK_ATT // CHUNK) <= qq)
    for hh in range(HEADS_ATT):
        acc = acc_ref[hh]
        o = acc[:MLA_V, :] / acc[MLA_V:MLA_V + 1, :]
        o_ref[:, hh * MLA_V:(hh + 1) * MLA_V] = o.T.astype(o_ref.dtype)


def _attn_call(q, k, vt, batch, seq):
    t = q.shape[0]
    nq = seq // TQ_ATT
    qw = HEADS_ATT * MLA_HEAD_PAD
    vw = HEADS_ATT * MLA_V_EXT
    s_buf = pltpu.VMEM((HEADS_ATT, TK_ATT, TQ_ATT), F32)
    return pl.pallas_call(
        _attn_kernel,
        grid=(batch, MLA_HEADS // HEADS_ATT, nq),
        in_specs=[
            pl.BlockSpec((TQ_ATT, qw), lambda b, h, i: (b * nq + i, h)),
            pl.BlockSpec((seq, qw), lambda b, h, i: (b, h)),
            pl.BlockSpec((seq // TK_ATT, vw, TK_ATT), lambda b, h, i: (b, h, 0)),
        ],
        out_specs=pl.BlockSpec((TQ_ATT, HEADS_ATT * MLA_V), lambda b, h, i: (b * nq + i, h)),
        out_shape=jax.ShapeDtypeStruct((t, MLA_HEADS * MLA_V), BF16),
        scratch_shapes=[pltpu.VMEM((HEADS_ATT, MLA_V_EXT, TQ_ATT), F32), s_buf, s_buf],
        compiler_params=_cparams("parallel", "parallel", "arbitrary"),
        name="mla_attention",
    )(q, k, vt)


def _merge_kernel(oa_ref, ob_ref, oc_ref, g0_ref, g1_ref, g2_ref, h_ref, wb_ref, wo_ref,
                  lg_ref, lb_ref, o32_ref, o16_ref):
    merged = _sigmoid(g0_ref[...].astype(F32)) * _dot(oa_ref[...], wb_ref[0])
    merged += _sigmoid(g1_ref[...].astype(F32)) * _dot(ob_ref[...], wb_ref[1])
    merged += _sigmoid(g2_ref[...].astype(F32)) * _dot(oc_ref[...], wb_ref[2])
    mix = _dot(merged.astype(BF16), wo_ref[...])
    y = _layer_norm(ALPHA * h_ref[...] + mix, lg_ref[...], lb_ref[...])
    o32_ref[...] = y
    o16_ref[...] = y.astype(BF16)


def _merge_call(o_a, o_b, o_c, proj, h32, w_branch, w_out, ln_g, ln_b):
    t, d = h32.shape
    row = pl.BlockSpec((TM_MERGE, d), lambda i: (i, 0))
    gate = lambda n: pl.BlockSpec((TM_MERGE, d), lambda i: (i, P_GATES // d + n))
    vec = pl.BlockSpec((1, d), lambda i: (0, 0))
    return pl.pallas_call(
        _merge_kernel,
        grid=(t // TM_MERGE,),
        in_specs=[row, row, row, gate(0), gate(1), gate(2), row,
                  pl.BlockSpec((N_BRANCHES, d, d), lambda i: (0, 0, 0)),
                  pl.BlockSpec((d, d), lambda i: (0, 0)), vec, vec],
        out_specs=[row, row],
        out_shape=[jax.ShapeDtypeStruct((t, d), F32), jax.ShapeDtypeStruct((t, d), BF16)],
        compiler_params=_cparams("parallel"),
        name="merge",
    )(o_a, o_b, o_c, proj, proj, proj, h32, w_branch.astype(BF16), w_out.astype(BF16),
      ln_g.reshape(1, d), ln_b.reshape(1, d))


def _first_argmax_mask(vals, iota, n):
    m = jnp.max(vals, axis=0, keepdims=True)
    idx = jnp.min(jnp.where(vals == m, iota, n), axis=0, keepdims=True)
    return iota == idx


def _router_kernel(h_ref, wt_ref, b_ref, comb_ref, rank_ref, cnt_ref):
    tm = h_ref.shape[0]
    h = h_ref[...]
    h_hi, h_lo = _split_bf16(h)
    w = wt_ref[...]
    w_hi, w_lo = _split_bf16(w)
    logits = _dot_nt(w_hi, h_hi) + _dot_nt(w_hi, h_lo) + _dot_nt(w_lo, h_hi)
    scores = _sigmoid(logits)
    biased = scores + b_ref[...]

    neg = -jnp.inf
    sub = lax.broadcasted_iota(jnp.int32, (GROUP_SIZE, tm), 0)
    grp_rows = []
    for g in range(N_GROUPS):
        blk = biased[g * GROUP_SIZE:(g + 1) * GROUP_SIZE, :]
        m1 = jnp.max(blk, axis=0, keepdims=True)
        first = _first_argmax_mask(blk, sub, GROUP_SIZE)
        m2 = jnp.max(jnp.where(first, neg, blk), axis=0, keepdims=True)
        grp_rows.append(m1 + m2)
    gs = jnp.concatenate(grp_rows, axis=0)
    gsel = jnp.zeros((N_GROUPS, tm), jnp.bool_)
    gi = lax.broadcasted_iota(jnp.int32, (N_GROUPS, tm), 0)
    for _ in range(TOPK_GROUPS):
        pick = _first_argmax_mask(gs, gi, N_GROUPS)
        gsel = gsel | pick
        gs = jnp.where(pick, neg, gs)
    emask = jnp.concatenate(
        [jnp.broadcast_to(gsel[g:g + 1, :], (GROUP_SIZE, tm)) for g in range(N_GROUPS)], axis=0)
    cand = jnp.where(emask, biased, neg)
    ei = lax.broadcasted_iota(jnp.int32, (N_EXPERTS, tm), 0)
    chosen = jnp.zeros((N_EXPERTS, tm), jnp.bool_)
    for _ in range(TOP_K):
        pick = _first_argmax_mask(cand, ei, N_EXPERTS)
        chosen = chosen | pick
        cand = jnp.where(pick, neg, cand)
    wsel = jnp.where(chosen, scores, 0.0)
    comb_ref[...] = wsel / jnp.sum(wsel, axis=0, keepdims=True) * ROUTED_SCALE

    r = lax.broadcasted_iota(jnp.int32, (tm, tm), 0)
    c = lax.broadcasted_iota(jnp.int32, (tm, tm), 1)
    upper = jnp.where(r < c, 1.0, 0.0).astype(BF16)
    sel = jnp.where(chosen, 1.0, 0.0)
    rank_ref[...] = _dot(sel.astype(BF16), upper)
    cnt = jnp.sum(sel, axis=1, keepdims=True)
    cnt_ref[...] = jnp.broadcast_to(cnt, (N_EXPERTS, LANES)).astype(jnp.int32)


def _router_call(h32, router_w, router_b):
    t, d = h32.shape
    nt = t // TM_MOE
    bias = jnp.broadcast_to(router_b.reshape(N_EXPERTS, 1), (N_EXPERTS, TM_MOE))
    comb, rank, cnt = pl.pallas_call(
        _router_kernel,
        grid=(nt,),
        in_specs=[pl.BlockSpec((TM_MOE, d), lambda i: (i, 0)),
                  pl.BlockSpec((N_EXPERTS, d), lambda i: (0, 0)),
                  pl.BlockSpec((N_EXPERTS, TM_MOE), lambda i: (0, 0))],
        out_specs=[pl.BlockSpec((N_EXPERTS, TM_MOE), lambda i: (0, i)),
                   pl.BlockSpec((N_EXPERTS, TM_MOE), lambda i: (0, i)),
                   pl.BlockSpec((N_EXPERTS, LANES), lambda i: (i, 0))],
        out_shape=[jax.ShapeDtypeStruct((N_EXPERTS, t), F32),
                   jax.ShapeDtypeStruct((N_EXPERTS, t), F32),
                   jax.ShapeDtypeStruct((nt * N_EXPERTS, LANES), jnp.int32)],
        compiler_params=_cparams("parallel"),
        name="router",
    )(h32, router_w.T, bias)
    return comb, rank, cnt[:, 0].reshape(nt, N_EXPERTS)


def _moe_plan(cnt, t):
    nt = cnt.shape[0]
    pad = (cnt + SEG_ALIGN - 1) // SEG_ALIGN * SEG_ALIGN
    lseg = jnp.cumsum(pad, axis=1) - pad
    ltot = jnp.sum(pad, axis=1)
    etot = jnp.sum(pad, axis=0)
    region = (etot + ROW_BLK - 1) // ROW_BLK * ROW_BLK
    rend = jnp.cumsum(region)
    gpos = (rend - region)[None, :] + jnp.cumsum(pad, axis=0) - pad
    nblk = (rend[-1] // ROW_BLK).astype(jnp.int32).reshape(1)
    blk_start = jnp.arange(_moe_blocks(t), dtype=jnp.int32) * ROW_BLK
    blk_expert = jnp.minimum(jnp.sum(rend[None, :] <= blk_start[:, None], axis=1), N_EXPERTS - 1)
    npiece = jnp.stack([jnp.sum(pad // BIG_PIECE, axis=1),
                        jnp.sum((pad % BIG_PIECE) // SEG_ALIGN, axis=1)], axis=1)
    i32 = lambda a: a.astype(jnp.int32)
    return i32(pad), i32(lseg), i32(gpos), i32(ltot), i32(npiece), nblk, i32(blk_expert)


def _moe_blocks(t):
    nt = t // TM_MOE
    rows = t * TOP_K + nt * N_EXPERTS * (SEG_ALIGN - 1) + N_EXPERTS * (ROW_BLK - 1)
    return -(-rows // ROW_BLK)


def _zero_uncovered_blocks(ref2d, ntot):
    for cb in range(LOCAL_ROWS // SORT_BLK):
        @pl.when((cb + 1) * SORT_BLK > ntot)
        def _():
            ref2d[cb * SORT_BLK:(cb + 1) * SORT_BLK, :] = jnp.zeros((SORT_BLK, ref2d.shape[1]),
                                                                   ref2d.dtype)


def _build_slot_matrix(p_ref, comb_ref, rank_ref, pad_ref, lseg_ref, ntot, i, weighted):
    tm = p_ref.shape[1]
    _zero_uncovered_blocks(p_ref, ntot)
    rowi = lax.broadcasted_iota(jnp.int32, (SLOT_CHUNK, tm), 0).astype(F32)

    def expert_body(e, carry):
        rrow = rank_ref[pl.ds(e, 1), :]
        wrow = comb_ref[pl.ds(e, 1), :]
        base = lseg_ref[i, e]
        nch = lax.shift_right_logical(pad_ref[i, e] + (SLOT_CHUNK - 1), SLOT_CHUNK_LOG2)

        def chunk_body(c, carry2):
            off = c * SLOT_CHUNK
            hit = (rrow == rowi + off.astype(F32)) & (wrow > 0.0)
            val = jnp.where(hit, wrow if weighted else 1.0, 0.0).astype(BF16)
            p_ref[pl.ds(pl.multiple_of(base + off, SEG_ALIGN), SLOT_CHUNK), :] = val
            return carry2

        lax.fori_loop(0, nch, chunk_body, 0)
        return carry

    lax.fori_loop(0, N_EXPERTS, expert_body, 0, unroll=2)


def _segment_copies(pad_ref, lseg_ref, gpos_ref, tile, make_copy):
    def expert_body(e, carry):
        rows = pad_ref[tile, e]
        nbig = lax.shift_right_logical(rows, BIG_PIECE_LOG2)
        nsmall = lax.shift_right_logical(rows & (BIG_PIECE - 1), SEG_ALIGN_LOG2)
        l0 = lseg_ref[tile, e]
        g0 = gpos_ref[tile, e]

        def big(c, carry2):
            make_copy(pl.multiple_of(l0 + c * BIG_PIECE, SEG_ALIGN),
                      pl.multiple_of(g0 + c * BIG_PIECE, SEG_ALIGN), BIG_PIECE).start()
            return carry2

        lax.fori_loop(0, nbig, big, 0)
        l1 = l0 + nbig * BIG_PIECE
        g1 = g0 + nbig * BIG_PIECE

        def small(c, carry2):
            make_copy(pl.multiple_of(l1 + c * SEG_ALIGN, SEG_ALIGN),
                      pl.multiple_of(g1 + c * SEG_ALIGN, SEG_ALIGN), SEG_ALIGN).start()
            return carry2

        lax.fori_loop(0, nsmall, small, 0)
        return carry

    lax.fori_loop(0, N_EXPERTS, expert_body, 0, unroll=2)


def _wait_copies(npiece_ref, tile, make_copy):
    for col, rows in ((0, BIG_PIECE), (1, SEG_ALIGN)):
        def piece(c, carry, rows=rows):
            make_copy(0, 0, rows).wait()
            return carry

        lax.fori_loop(0, npiece_ref[tile, col], piece, 0)


def _dispatch_kernel(pad_ref, lseg_ref, gpos_ref, ltot_ref, npiece_ref, x_ref, comb_ref, rank_ref,
                     xs_hbm, p_ref, xs_ref, sem):
    i = pl.program_id(0)
    last = pl.num_programs(0) - 1
    buf = lax.rem(i, 2)
    ntot = ltot_ref[i]
    _build_slot_matrix(p_ref, comb_ref, rank_ref, pad_ref, lseg_ref, ntot, i, weighted=False)
    for cb in range(LOCAL_ROWS // SORT_BLK):
        @pl.when(cb * SORT_BLK < ntot)
        def _():
            rs = slice(cb * SORT_BLK, (cb + 1) * SORT_BLK)
            xs_ref[buf, rs, :] = _dot(p_ref[rs, :], x_ref[...]).astype(BF16)

    def copy_for(b):
        return lambda l0, g0, rows: pltpu.make_async_copy(
            xs_ref.at[b, pl.ds(l0, rows), :], xs_hbm.at[pl.ds(g0, rows), :], sem.at[b])

    _segment_copies(pad_ref, lseg_ref, gpos_ref, i, copy_for(buf))

    @pl.when(i > 0)
    def _():
        _wait_copies(npiece_ref, i - 1, copy_for(1 - buf))

    @pl.when(i == last)
    def _():
        _wait_copies(npiece_ref, i, copy_for(buf))


def _dispatch_call(h16, comb, rank, plan):
    t, d = h16.shape
    nt = t // TM_MOE
    pad, lseg, gpos, ltot, npiece, _, _ = plan
    grid_spec = pltpu.PrefetchScalarGridSpec(
        num_scalar_prefetch=5,
        grid=(nt,),
        in_specs=[
            pl.BlockSpec((TM_MOE, d), lambda i, *_: (i, 0)),
            pl.BlockSpec((N_EXPERTS, TM_MOE), lambda i, *_: (0, i)),
            pl.BlockSpec((N_EXPERTS, TM_MOE), lambda i, *_: (0, i)),
        ],
        out_specs=pl.BlockSpec(memory_space=pl.ANY),
        scratch_shapes=[
            pltpu.VMEM((LOCAL_ROWS, TM_MOE), BF16),
            pltpu.VMEM((2, LOCAL_ROWS, d), BF16),
            pltpu.SemaphoreType.DMA((2,)),
        ],
    )
    return pl.pallas_call(
        _dispatch_kernel,
        grid_spec=grid_spec,
        out_shape=jax.ShapeDtypeStruct((_moe_blocks(t) * ROW_BLK, d), BF16),
        compiler_params=_cparams("arbitrary"),
        name="moe_dispatch",
    )(pad, lseg, gpos, ltot, npiece, h16, comb, rank)


def _expert_ffn_kernel(nblk_ref, be_ref, x_ref, wg_ref, wu_ref, wd_ref, o_ref):
    @pl.when(pl.program_id(0) < nblk_ref[0])
    def _():
        x = x_ref[...]
        gate = _dot(x, wg_ref[0, 0].astype(BF16))
        up = _dot(x, wu_ref[0, 0].astype(BF16))
        hmid = (_silu(gate) * up).astype(BF16)
        o_ref[...] = _dot(hmid, wd_ref[0, 0].astype(BF16)).astype(o_ref.dtype)


def _expert_ffn_call(xs, plan, w_gate, w_up, w_down, layer):
    rows, d = xs.shape
    nblk, blk_expert = plan[-2:]
    live = lambda b, nblk, be: jnp.minimum(b, nblk[0] - 1)
    wmap = lambda b, nblk, be: (layer, be[live(b, nblk, be)], 0, 0)
    grid_spec = pltpu.PrefetchScalarGridSpec(
        num_scalar_prefetch=2,
        grid=(rows // ROW_BLK,),
        in_specs=[
            pl.BlockSpec((ROW_BLK, d), lambda b, nblk, be: (live(b, nblk, be), 0)),
            pl.BlockSpec((1, 1, d, D_EXPERT), wmap),
            pl.BlockSpec((1, 1, d, D_EXPERT), wmap),
            pl.BlockSpec((1, 1, D_EXPERT, d), wmap),
        ],
        out_specs=pl.BlockSpec((ROW_BLK, d), lambda b, nblk, be: (live(b, nblk, be), 0)),
    )
    return pl.pallas_call(
        _expert_ffn_kernel,
        grid_spec=grid_spec,
        out_shape=jax.ShapeDtypeStruct((rows, d), BF16),
        compiler_params=_cparams("arbitrary"),
        name="moe_expert_ffn",
    )(nblk, blk_expert, xs, w_gate, w_up, w_down)


def _combine_kernel(pad_ref, lseg_ref, gpos_ref, ltot_ref, npiece_ref, ys_hbm, comb_ref, rank_ref,
                    h16_ref, h32_ref, p_ref, swgu_ref, swd_ref, pwi_ref, pwg_ref, lg_ref, lb_ref,
                    o32_ref, o16_ref, pw_ref, ys_ref, sem):
    i = pl.program_id(0)
    last = pl.num_programs(0) - 1
    buf = lax.rem(i, 2)

    def copy_for(b):
        return lambda l0, g0, rows: pltpu.make_async_copy(
            ys_hbm.at[pl.ds(g0, rows), :], ys_ref.at[b, pl.ds(l0, rows), :], sem.at[b])

    def fetch(tile, b):
        _zero_uncovered_blocks(ys_ref.at[b], ltot_ref[tile])
        _segment_copies(pad_ref, lseg_ref, gpos_ref, tile, copy_for(b))

    @pl.when(i == 0)
    def _():
        fetch(i, buf)

    @pl.when(i < last)
    def _():
        fetch(i + 1, 1 - buf)

    _build_slot_matrix(pw_ref, comb_ref, rank_ref, pad_ref, lseg_ref, ltot_ref[i], i, weighted=True)
    x = h16_ref[...]
    gu = _dot(x, swgu_ref[...])
    shared = _dot((_silu(gu[:, :D_SHARED]) * gu[:, D_SHARED:]).astype(BF16), swd_ref[...])
    ple = _dot(p_ref[...].astype(BF16), pwi_ref[...]) * _sigmoid(_dot(x, pwg_ref[...]))
    z = ALPHA * h32_ref[...] + shared + ple

    _wait_copies(npiece_ref, i, copy_for(buf))
    z += _dot_tn(pw_ref[...], ys_ref[buf])
    y = _layer_norm(z, lg_ref[...], lb_ref[...])
    o32_ref[...] = y
    o16_ref[...] = y.astype(BF16)


def _combine_call(ys, comb, rank, plan, h16, h32, p, sw_gate, sw_up, sw_down, ple_w_in, ple_w_gate,
                  ln_g, ln_b):
    t, d = h32.shape
    nt = t // TM_MOE
    pad, lseg, gpos, ltot, npiece, _, _ = plan
    row = pl.BlockSpec((TM_MOE, d), lambda i, *_: (i, 0))
    vec = pl.BlockSpec((1, d), lambda i, *_: (0, 0))
    full = lambda a, b: pl.BlockSpec((a, b), lambda i, *_: (0, 0))
    route = pl.BlockSpec((N_EXPERTS, TM_MOE), lambda i, *_: (0, i))
    swgu = jnp.concatenate([sw_gate, sw_up], axis=1).astype(BF16)
    grid_spec = pltpu.PrefetchScalarGridSpec(
        num_scalar_prefetch=5,
        grid=(nt,),
        in_specs=[pl.BlockSpec(memory_space=pl.ANY), route, route, row, row,
                  pl.BlockSpec((TM_MOE, PLE_DIM), lambda i, *_: (i, 0)),
                  full(d, 2 * D_SHARED), full(D_SHARED, d), full(PLE_DIM, d), full(d, d), vec, vec],
        out_specs=[row, row],
        scratch_shapes=[
            pltpu.VMEM((LOCAL_ROWS, TM_MOE), BF16),
            pltpu.VMEM((2, LOCAL_ROWS, d), BF16),
            pltpu.SemaphoreType.DMA((2,)),
        ],
    )
    return pl.pallas_call(
        _combine_kernel,
        grid_spec=grid_spec,
        out_shape=[jax.ShapeDtypeStruct((t, d), F32), jax.ShapeDtypeStruct((t, d), BF16)],
        compiler_params=_cparams("arbitrary"),
        name="moe_combine_tail",
    )(pad, lseg, gpos, ltot, npiece, ys, comb, rank, h16, h32, p, swgu, sw_down.astype(BF16),
      ple_w_in.astype(BF16), ple_w_gate.astype(BF16), ln_g.reshape(1, d), ln_b.reshape(1, d))


def kernel(x, p, ln_in_g, ln_in_b, w_in, gla_fg_w, gla_fg_b, gla_norm_g, sg_norm_g, sg_norm_b, sg_w, sg_b, mla_qn_g, mla_w_uq, mla_kvn_g, mla_w_ukv, w_branch, w_out, ln1_g, ln1_b, router_w, router_b, exp_w_gate, exp_w_up, exp_w_down, sh_w_gate, sh_w_up, sh_w_down, ple_w_in, ple_w_gate, ln2_g, ln2_b):
    batch, seq, d = x.shape
    t = batch * seq
    depth = w_in.shape[0]
    h32, h16 = _ln_call(x.reshape(t, d), ln_in_g, ln_in_b)
    for i in range(depth):
        proj = _proj_call(h16, _prep_w_in(w_in[i]))
        o_a = _gla_call(proj, gla_fg_w[i], gla_fg_b[i], gla_norm_g[i], batch, seq)
        o_b = _sg_call(proj, sg_norm_g[i], sg_norm_b[i], sg_w[i], sg_b[i])
        q, k, vt = _mla_prep_call(proj, mla_qn_g[i], mla_w_uq[i], mla_kvn_g[i], mla_w_ukv[i], seq)
        o_c = _attn_call(q, k, vt, batch, seq)
        h32, h16 = _merge_call(o_a, o_b, o_c, proj, h32, w_branch[i], w_out[i], ln1_g[i], ln1_b[i])
        comb, rank, cnt = _router_call(h32, router_w[i], router_b[i])
        plan = _moe_plan(cnt, t)
        xs = _dispatch_call(h16, comb, rank, plan)
        ys = _expert_ffn_call(xs, plan, exp_w_gate, exp_w_up, exp_w_down, i)
        h32, h16 = _combine_call(ys, comb, rank, plan, h16, h32, p[i].reshape(t, -1), sh_w_gate[i],
                                 sh_w_up[i], sh_w_down[i], ple_w_in[i], ple_w_gate[i],
                                 ln2_g[i], ln2_b[i])
    return h32.reshape(batch, seq, d)
```

```python
import functools

import jax
import jax.numpy as jnp
from jax import lax
from jax.experimental import pallas as pl
from jax.experimental.pallas import tpu as pltpu

F32 = jnp.float32
BF16 = jnp.bfloat16

D_MODEL = 1024
DEPTH = 2
CHUNK = 64
GLA_HEADS, GLA_DK, GLA_DV, GLA_GATE_RANK, GLA_TAU = 4, 128, 256, 16, 16.0
SG_WIDTH, SG_GROUPS, SG_BLOCK = 1024, 4, 128
MLA_HEADS, MLA_Q_RANK, MLA_KV_RANK = 8, 384, 256
MLA_NOPE, MLA_ROPE, MLA_V = 128, 64, 128
MLA_QK = MLA_NOPE + MLA_ROPE
ROPE_THETA = 10000.0
N_BRANCHES = 3
N_EXPERTS, N_GROUPS, TOPK_GROUPS, TOP_K = 64, 8, 4, 8
GROUP_SIZE = N_EXPERTS // N_GROUPS
D_EXPERT, D_SHARED = 256, 256
ROUTED_SCALE = 2.5
PLE_DIM = 256
ALPHA = (2 * DEPTH) ** 0.25

LANES = 128
SUBLANES = 8
VMEM_LIMIT_BYTES = 56 * 1024 * 1024

P_SMALL = 1024
P_GFR = 384
P_CKV = 512
P_KROPE = 768
P_GATES = 1024
P_SG = 4096
P_GV = 6144
P_GR = 7168
P_GQ = 8192
P_GK = 8704
P_TOTAL = 9216
MLA_HEAD_PAD = 256

TM_LN = 512
TM_PROJ, TN_PROJ = 1024, 1024
R_GLA = 512
TM_SG = 256
TQ_ATT = 512
TK_ATT = 256
TM_MLA = TQ_ATT
MLA_V_EXT = MLA_V + 16
HEADS_ATT = 2
LOG2_E = 1.4426950408889634
TM_MERGE = 512
TM_MOE = 512
SEG_ALIGN_LOG2 = 4
SEG_ALIGN = 1 << SEG_ALIGN_LOG2
SLOT_CHUNK_LOG2 = 6
SLOT_CHUNK = 1 << SLOT_CHUNK_LOG2
SORT_BLK = 512
ROW_BLK = 512
BIG_PIECE_LOG2 = 6
BIG_PIECE = 1 << BIG_PIECE_LOG2
LOCAL_ROWS = -(-(TM_MOE * TOP_K + N_EXPERTS * (SEG_ALIGN - 1) + SLOT_CHUNK) // SORT_BLK) * SORT_BLK


def _cparams(*sem):
    return pltpu.CompilerParams(dimension_semantics=sem, vmem_limit_bytes=VMEM_LIMIT_BYTES)


def _dot(a, b):
    return jnp.dot(a, b, preferred_element_type=F32)


def _dot_nt(a, b):
    return lax.dot_general(a, b, (((1,), (1,)), ((), ())), preferred_element_type=F32)


def _dot_tn(a, b):
    return lax.dot_general(a, b, (((0,), (0,)), ((), ())), preferred_element_type=F32)


def _layer_norm(x, g, b, eps=1e-5):
    mu = jnp.mean(x, axis=-1, keepdims=True)
    xc = x - mu
    var = jnp.mean(xc * xc, axis=-1, keepdims=True)
    return xc * lax.rsqrt(var + eps) * g + b


def _rms_norm(x, g, eps=1e-6):
    return x * lax.rsqrt(jnp.mean(x * x, axis=-1, keepdims=True) + eps) * g


def _sigmoid(x):
    return 1.0 / (1.0 + jnp.exp(-x))


def _silu(x):
    return x * _sigmoid(x)


def _split_bf16(x):
    hi = x.astype(BF16)
    lo = (x - hi.astype(F32)).astype(BF16)
    return hi, lo


def _ln_kernel(x_ref, g_ref, b_ref, o32_ref, o16_ref):
    y = _layer_norm(x_ref[...], g_ref[...], b_ref[...])
    o32_ref[...] = y
    o16_ref[...] = y.astype(BF16)


def _ln_call(x, g, b):
    t, d = x.shape
    row = pl.BlockSpec((TM_LN, d), lambda i: (i, 0))
    vec = pl.BlockSpec((1, d), lambda i: (0, 0))
    return pl.pallas_call(
        _ln_kernel,
        grid=(t // TM_LN,),
        in_specs=[row, vec, vec],
        out_specs=[row, row],
        out_shape=[jax.ShapeDtypeStruct((t, d), F32), jax.ShapeDtypeStruct((t, d), BF16)],
        compiler_params=_cparams("parallel"),
        name="ln_in",
    )(x, g.reshape(1, d), b.reshape(1, d))


def _proj_kernel(x_ref, w_ref, o_ref):
    o_ref[...] = _dot(x_ref[...], w_ref[...]).astype(o_ref.dtype)


def _proj_call(h16, w):
    t, k = h16.shape
    n = w.shape[1]
    return pl.pallas_call(
        _proj_kernel,
        grid=(t // TM_PROJ, n // TN_PROJ),
        in_specs=[pl.BlockSpec((TM_PROJ, k), lambda i, j: (i, 0)),
                  pl.BlockSpec((k, TN_PROJ), lambda i, j: (0, j))],
        out_specs=pl.BlockSpec((TM_PROJ, TN_PROJ), lambda i, j: (i, j)),
        out_shape=jax.ShapeDtypeStruct((t, n), BF16),
        compiler_params=_cparams("parallel", "arbitrary"),
        name="in_proj",
    )(h16, w)


def _prep_w_in(w_in):
    d = w_in.shape[0]
    gq, gk = w_in[:, 0:512], w_in[:, 512:1024]
    gv, gr = w_in[:, 1024:2048], w_in[:, 2048:3072]
    gfr = w_in[:, 3072:3088]
    sgz = w_in[:, 3088:5136]
    mq = w_in[:, 5136:5520]
    ckv = w_in[:, 5520:5776]
    kr = w_in[:, 5776:5840]
    gates = w_in[:, 5840:8912]
    half = MLA_ROPE // 2
    kr_rot = jnp.concatenate([-kr[:, half:], kr[:, :half]], axis=1)
    z = lambda n: jnp.zeros((d, n), w_in.dtype)
    small = jnp.concatenate([mq, gfr, z(112), ckv, kr, kr_rot, z(128)], axis=1)
    return jnp.concatenate([small, gates, sgz, gv, gr, gq, gk], axis=1).astype(BF16)


def _gla_kernel(q_ref, k_ref, v_ref, gr_ref, gf_ref, fgw_ref, fgb_ref, g_ref, tri_ref, blk_ref,
                o_ref, st_ref, qd_ref, ki_ref, ke_ref, dec_ref):
    rows = q_ref.shape[0]

    @pl.when(pl.program_id(1) == 0)
    def _():
        st_ref[...] = jnp.zeros_like(st_ref)

    pre = _dot(gf_ref[...], fgw_ref[...]) + fgb_ref[...]
    glog = -(jnp.maximum(-pre, 0.0) + jnp.log1p(jnp.exp(-jnp.abs(pre)))) * (1.0 / GLA_TAU)

    hi, lo = _split_bf16(glog)
    bcum = _dot(tri_ref[...], hi) + _dot(tri_ref[...], lo)
    btot = _dot(blk_ref[...], hi) + _dot(blk_ref[...], lo)

    q = q_ref[...].astype(F32) * (GLA_DK ** -0.5)
    k = k_ref[...].astype(F32)
    qd_ref[...] = (q * jnp.exp(bcum)).astype(BF16)
    ki_ref[...] = (k * jnp.exp(-bcum)).astype(BF16)
    ke_ref[...] = (k * jnp.exp(btot - bcum)).astype(BF16)
    dec_ref[...] = jnp.exp(btot)

    tr = lax.broadcasted_iota(jnp.int32, (CHUNK, CHUNK), 0)
    tc = lax.broadcasted_iota(jnp.int32, (CHUNK, CHUNK), 1)
    causal = tc <= tr

    def chunk_body(ci, carry):
        r0 = pl.multiple_of(ci * CHUNK, CHUNK)
        rs = pl.ds(r0, CHUNK)
        for hd in range(GLA_HEADS):
            ks = slice(hd * GLA_DK, (hd + 1) * GLA_DK)
            vs = slice(hd * GLA_DV, (hd + 1) * GLA_DV)
            qd = qd_ref[rs, ks]
            ki = ki_ref[rs, ks]
            ke = ke_ref[rs, ks]
            v = v_ref[rs, vs]
            att = jnp.where(causal, _dot_nt(qd, ki), 0.0)
            st = st_ref[hd]
            o = _dot(att.astype(BF16), v) + _dot_nt(qd, st.astype(BF16))
            dec = dec_ref[pl.ds(r0, 1), ks]
            st_ref[hd] = st * dec + _dot_tn(v, ke)
            on = _rms_norm(o, g_ref[:, vs])
            gate = gr_ref[rs, vs].astype(F32)
            o_ref[rs, vs] = (on * _silu(gate)).astype(o_ref.dtype)
        return carry

    lax.fori_loop(0, rows // CHUNK, chunk_body, 0)


def _gla_call(proj, fg_w, fg_b, gla_g, batch, seq):
    t = proj.shape[0]
    nr = seq // R_GLA
    row = lambda i, j: i * nr + j
    fgw = jnp.zeros((LANES, GLA_HEADS * GLA_DK), F32).at[:GLA_GATE_RANK].set(fg_w).astype(BF16)
    hk = GLA_HEADS * GLA_DK
    hv = GLA_HEADS * GLA_DV
    ridx = jnp.arange(R_GLA, dtype=jnp.int32)
    same = (ridx[:, None] // CHUNK) == (ridx[None, :] // CHUNK)
    blk = same.astype(BF16)
    tri = (same & (ridx[None, :] <= ridx[:, None])).astype(BF16)
    return pl.pallas_call(
        _gla_kernel,
        grid=(batch, nr),
        in_specs=[
            pl.BlockSpec((R_GLA, hk), lambda i, j: (row(i, j), P_GQ // hk)),
            pl.BlockSpec((R_GLA, hk), lambda i, j: (row(i, j), P_GK // hk)),
            pl.BlockSpec((R_GLA, hv), lambda i, j: (row(i, j), P_GV // hv)),
            pl.BlockSpec((R_GLA, hv), lambda i, j: (row(i, j), P_GR // hv)),
            pl.BlockSpec((R_GLA, LANES), lambda i, j: (row(i, j), P_GFR // LANES)),
            pl.BlockSpec((LANES, hk), lambda i, j: (0, 0)),
            pl.BlockSpec((1, hk), lambda i, j: (0, 0)),
            pl.BlockSpec((1, hv), lambda i, j: (0, 0)),
            pl.BlockSpec((R_GLA, R_GLA), lambda i, j: (0, 0)),
            pl.BlockSpec((R_GLA, R_GLA), lambda i, j: (0, 0)),
        ],
        out_specs=pl.BlockSpec((R_GLA, hv), lambda i, j: (row(i, j), 0)),
        out_shape=jax.ShapeDtypeStruct((t, hv), BF16),
        scratch_shapes=[
            pltpu.VMEM((GLA_HEADS, GLA_DV, GLA_DK), F32),
            pltpu.VMEM((R_GLA, hk), BF16),
            pltpu.VMEM((R_GLA, hk), BF16),
            pltpu.VMEM((R_GLA, hk), BF16),
            pltpu.VMEM((R_GLA, hk), F32),
        ],
        compiler_params=_cparams("parallel", "arbitrary"),
        name="gla",
    )(proj, proj, proj, proj, proj, fgw, fg_b.reshape(1, hk), gla_g.reshape(1, hv), tri, blk)


def _gelu(x):
    return 0.5 * x * (1.0 + lax.erf(x * (2.0 ** -0.5)))


def _sg_kernel(u_ref, v_ref, g_ref, b_ref, ws_ref, bias_ref, o_ref):
    rows = u_ref.shape[0]
    gw = SG_WIDTH // SG_GROUPS
    u = _gelu(u_ref[...].astype(F32))
    v = _gelu(v_ref[...].astype(F32))
    vn = _layer_norm(v, g_ref[...], b_ref[...]).astype(BF16)
    tr = lax.broadcasted_iota(jnp.int32, (SG_BLOCK, SG_BLOCK), 0)
    tc = lax.broadcasted_iota(jnp.int32, (SG_BLOCK, SG_BLOCK), 1)
    tril = tc <= tr
    for g in range(SG_GROUPS):
        w = jnp.where(tril, ws_ref[g], 0.0).astype(BF16)
        cs = slice(g * gw, (g + 1) * gw)
        for blk in range(rows // SG_BLOCK):
            rs = slice(blk * SG_BLOCK, (blk + 1) * SG_BLOCK)
            sv = _dot(w, vn[rs, cs]) + bias_ref[:, cs]
            o_ref[rs, cs] = (u[rs, cs] * sv).astype(o_ref.dtype)


def _sg_call(proj, sg_g, sg_bn, sg_w, sg_b):
    t = proj.shape[0]
    gw = SG_WIDTH // SG_GROUPS
    bias = jnp.repeat(sg_b.T, gw, axis=1)
    return pl.pallas_call(
        _sg_kernel,
        grid=(t // TM_SG,),
        in_specs=[
            pl.BlockSpec((TM_SG, SG_WIDTH), lambda i: (i, P_SG // SG_WIDTH)),
            pl.BlockSpec((TM_SG, SG_WIDTH), lambda i: (i, P_SG // SG_WIDTH + 1)),
            pl.BlockSpec((1, SG_WIDTH), lambda i: (0, 0)),
            pl.BlockSpec((1, SG_WIDTH), lambda i: (0, 0)),
            pl.BlockSpec((SG_GROUPS, SG_BLOCK, SG_BLOCK), lambda i: (0, 0, 0)),
            pl.BlockSpec((SG_BLOCK, SG_WIDTH), lambda i: (0, 0)),
        ],
        out_specs=pl.BlockSpec((TM_SG, SG_WIDTH), lambda i: (i, 0)),
        out_shape=jax.ShapeDtypeStruct((t, SG_WIDTH), BF16),
        compiler_params=_cparams("parallel"),
        name="spatial_gating",
    )(proj, proj, sg_g.reshape(1, -1), sg_bn.reshape(1, -1), sg_w, bias)


def _mla_prep_kernel(sm_ref, cs_ref, qg_ref, kg_ref, wq_ref, wkv_ref, q_ref, k_ref, vt_ref):
    rows = sm_ref.shape[0]
    mq = sm_ref[:, 0:MLA_Q_RANK].astype(F32)
    qn = _rms_norm(mq, qg_ref[...]).astype(BF16)
    qf = _dot(qn, wq_ref[...])
    ckv = sm_ref[:, P_CKV:P_CKV + MLA_KV_RANK].astype(F32)
    cn = _rms_norm(ckv, kg_ref[...]).astype(BF16)
    kv = _dot(cn, wkv_ref[...])

    cs = cs_ref[...]
    lane = lax.broadcasted_iota(jnp.int32, (rows, LANES), 1)
    low = lane < MLA_ROPE

    def rope(pair):
        t = pair * cs
        return jnp.where(low, t + pltpu.roll(t, MLA_ROPE, 1), 0.0)

    scale = (MLA_QK ** -0.5) * LOG2_E
    kr = rope(sm_ref[:, P_KROPE:P_KROPE + LANES].astype(F32)).astype(BF16)
    for h in range(MLA_HEADS):
        c0 = h * MLA_HEAD_PAD
        q_ref[:, c0:c0 + MLA_NOPE] = (qf[:, c0:c0 + MLA_NOPE] * scale).astype(BF16)
        q_ref[:, c0 + MLA_NOPE:c0 + MLA_HEAD_PAD] = (
            rope(qf[:, c0 + MLA_NOPE:c0 + MLA_HEAD_PAD]) * scale).astype(BF16)
        k_ref[:, c0:c0 + MLA_NOPE] = kv[:, h * MLA_NOPE:(h + 1) * MLA_NOPE].astype(BF16)
        k_ref[:, c0 + MLA_NOPE:c0 + MLA_HEAD_PAD] = kr
    vt = kv[:, MLA_HEADS * MLA_NOPE:].T.astype(BF16)
    ones = jnp.ones((MLA_V_EXT - MLA_V, TK_ATT), BF16)
    for blk in range(rows // TK_ATT):
        ks = slice(blk * TK_ATT, (blk + 1) * TK_ATT)
        for h in range(MLA_HEADS):
            r0 = h * MLA_V_EXT
            vt_ref[blk, r0:r0 + MLA_V, :] = vt[h * MLA_V:(h + 1) * MLA_V, ks]
            vt_ref[blk, r0 + MLA_V:r0 + MLA_V_EXT, :] = ones


def _mla_prep_call(proj, qn_g, w_uq, kvn_g, w_ukv, seq):
    t = proj.shape[0]
    half = MLA_ROPE // 2
    wq = w_uq.reshape(MLA_Q_RANK, MLA_HEADS, MLA_QK)
    wr = wq[:, :, MLA_NOPE:]
    wr_rot = jnp.concatenate([-wr[:, :, half:], wr[:, :, :half]], axis=2)
    wq_ext = jnp.concatenate([wq[:, :, :MLA_NOPE], wr, wr_rot], axis=2)
    wq_ext = wq_ext.reshape(MLA_Q_RANK, MLA_HEADS * MLA_HEAD_PAD).astype(BF16)
    wkv = w_ukv.reshape(MLA_KV_RANK, MLA_HEADS, MLA_NOPE + MLA_V)
    wkv = jnp.concatenate([wkv[:, :, :MLA_NOPE].reshape(MLA_KV_RANK, -1),
                           wkv[:, :, MLA_NOPE:].reshape(MLA_KV_RANK, -1)], axis=1).astype(BF16)
    inv = ROPE_THETA ** (-jnp.arange(half, dtype=F32) / half)
    ang = jnp.arange(seq, dtype=F32)[:, None] * inv
    cs = jnp.concatenate([jnp.cos(ang), jnp.cos(ang), jnp.sin(ang), jnp.sin(ang)], axis=1)
    ns = seq // TM_MLA
    hq = MLA_HEADS * MLA_HEAD_PAD
    hv = MLA_HEADS * MLA_V_EXT
    return pl.pallas_call(
        _mla_prep_kernel,
        grid=(t // TM_MLA,),
        in_specs=[
            pl.BlockSpec((TM_MLA, P_SMALL), lambda i: (i, 0)),
            pl.BlockSpec((TM_MLA, LANES), lambda i: (i % ns, 0)),
            pl.BlockSpec((1, MLA_Q_RANK), lambda i: (0, 0)),
            pl.BlockSpec((1, MLA_KV_RANK), lambda i: (0, 0)),
            pl.BlockSpec((MLA_Q_RANK, hq), lambda i: (0, 0)),
            pl.BlockSpec((MLA_KV_RANK, MLA_HEADS * (MLA_NOPE + MLA_V)), lambda i: (0, 0)),
        ],
        out_specs=[
            pl.BlockSpec((TM_MLA, hq), lambda i: (i, 0)),
            pl.BlockSpec((TM_MLA, hq), lambda i: (i, 0)),
            pl.BlockSpec((TM_MLA // TK_ATT, hv, TK_ATT), lambda i: (i, 0, 0)),
        ],
        out_shape=[jax.ShapeDtypeStruct((t, hq), BF16), jax.ShapeDtypeStruct((t, hq), BF16),
                   jax.ShapeDtypeStruct((t // TK_ATT, hv, TK_ATT), BF16)],
        compiler_params=_cparams("parallel"),
        name="mla_prep",
    )(proj, cs, qn_g.reshape(1, -1), kvn_g.reshape(1, -1), wq_ext, wkv)


def _attn_kernel(q_ref, k_ref, vt_ref, o_ref, acc_ref, sa_ref, sb_ref):
    tq = q_ref.shape[0]
    i = pl.program_id(2)
    acc_ref[...] = jnp.zeros_like(acc_ref)

    def scores(j, s_ref):
        r0 = pl.multiple_of(j * TK_ATT, TK_ATT)
        for hh in range(HEADS_ATT):
            qs = slice(hh * MLA_HEAD_PAD, (hh + 1) * MLA_HEAD_PAD)
            s_ref[hh] = _dot_nt(k_ref[pl.ds(r0, TK_ATT), qs], q_ref[:, qs])

    def consume(j, s_ref, ms, mask):
        new_m = []
        for hh in range(HEADS_ATT):
            s = s_ref[hh]
            if mask is not None:
                s = jnp.where(mask, s, -jnp.inf)
            m_new = jnp.maximum(ms[hh], jnp.max(s, axis=0, keepdims=True))
            alpha = jnp.exp2(ms[hh] - m_new)
            p = jnp.exp2(s - m_new).astype(BF16)
            vt = vt_ref[j, hh * MLA_V_EXT:(hh + 1) * MLA_V_EXT, :]
            acc_ref[hh] = alpha * acc_ref[hh] + _dot(vt, p)
            new_m.append(m_new)
        return tuple(new_m)

    def pair(jj, ms):
        j = 2 * jj
        scores(j + 1, sb_ref)
        ms = consume(j, sa_ref, ms, None)
        scores(j + 2, sa_ref)
        return consume(j + 1, sb_ref, ms, None)

    scores(0, sa_ref)
    m0 = tuple(jnp.full((1, tq), -jnp.inf, F32) for _ in range(HEADS_ATT))
    ms = lax.fori_loop(0, i * (tq // (2 * TK_ATT)), pair, m0)
    jd = i * (tq // TK_ATT)
    scores(jd + 1, sb_ref)
    kk = lax.shift_right_logical(lax.broadcasted_iota(jnp.int32, (TK_ATT, tq), 0), 6)
    qq = lax.shift_right_logical(lax.broadcasted_iota(jnp.int32, (TK_ATT, tq), 1), 6)
    ms = consume(jd, sa_ref, ms, kk <= qq)
    ms = consume(jd + 1, sb_ref, ms, kk + (TK_ATT // CHUNK) <= qq)
    for hh in range(HEADS_ATT):
        acc = acc_ref[hh]
        o = acc[:MLA_V, :] / acc[MLA_V:MLA_V + 1, :]
        o_ref[:, hh * MLA_V:(hh + 1) * MLA_V] = o.T.astype(o_ref.dtype)


def _attn_call(q, k, vt, batch, seq):
    t = q.shape[0]
    nq = seq // TQ_ATT
    qw = HEADS_ATT * MLA_HEAD_PAD
    vw = HEADS_ATT * MLA_V_EXT
    s_buf = pltpu.VMEM((HEADS_ATT, TK_ATT, TQ_ATT), F32)
    return pl.pallas_call(
        _attn_kernel,
        grid=(batch, MLA_HEADS // HEADS_ATT, nq),
        in_specs=[
            pl.BlockSpec((TQ_ATT, qw), lambda b, h, i: (b * nq + i, h)),
            pl.BlockSpec((seq, qw), lambda b, h, i: (b, h)),
            pl.BlockSpec((seq // TK_ATT, vw, TK_ATT), lambda b, h, i: (b, h, 0)),
        ],
        out_specs=pl.BlockSpec((TQ_ATT, HEADS_ATT * MLA_V), lambda b, h, i: (b * nq + i, h)),
        out_shape=jax.ShapeDtypeStruct((t, MLA_HEADS * MLA_V), BF16),
        scratch_shapes=[pltpu.VMEM((HEADS_ATT, MLA_V_EXT, TQ_ATT), F32), s_buf, s_buf],
        compiler_params=_cparams("parallel", "parallel", "arbitrary"),
        name="mla_attention",
    )(q, k, vt)


def _merge_kernel(oa_ref, ob_ref, oc_ref, g0_ref, g1_ref, g2_ref, h_ref, wb_ref, wo_ref,
                  lg_ref, lb_ref, o32_ref, o16_ref):
    merged = _sigmoid(g0_ref[...].astype(F32)) * _dot(oa_ref[...], wb_ref[0])
    merged += _sigmoid(g1_ref[...].astype(F32)) * _dot(ob_ref[...], wb_ref[1])
    merged += _sigmoid(g2_ref[...].astype(F32)) * _dot(oc_ref[...], wb_ref[2])
    mix = _dot(merged.astype(BF16), wo_ref[...])
    y = _layer_norm(ALPHA * h_ref[...] + mix, lg_ref[...], lb_ref[...])
    o32_ref[...] = y
    o16_ref[...] = y.astype(BF16)


def _merge_call(o_a, o_b, o_c, proj, h32, w_branch, w_out, ln_g, ln_b):
    t, d = h32.shape
    row = pl.BlockSpec((TM_MERGE, d), lambda i: (i, 0))
    gate = lambda n: pl.BlockSpec((TM_MERGE, d), lambda i: (i, P_GATES // d + n))
    vec = pl.BlockSpec((1, d), lambda i: (0, 0))
    return pl.pallas_call(
        _merge_kernel,
        grid=(t // TM_MERGE,),
        in_specs=[row, row, row, gate(0), gate(1), gate(2), row,
                  pl.BlockSpec((N_BRANCHES, d, d), lambda i: (0, 0, 0)),
                  pl.BlockSpec((d, d), lambda i: (0, 0)), vec, vec],
        out_specs=[row, row],
        out_shape=[jax.ShapeDtypeStruct((t, d), F32), jax.ShapeDtypeStruct((t, d), BF16)],
        compiler_params=_cparams("parallel"),
        name="merge",
    )(o_a, o_b, o_c, proj, proj, proj, h32, w_branch.astype(BF16), w_out.astype(BF16),
      ln_g.reshape(1, d), ln_b.reshape(1, d))


def _first_argmax_mask(vals, iota, n):
    m = jnp.max(vals, axis=0, keepdims=True)
    idx = jnp.min(jnp.where(vals == m, iota, n), axis=0, keepdims=True)
    return iota == idx


def _router_kernel(h_ref, wt_ref, b_ref, comb_ref, rank_ref, cnt_ref):
    tm = h_ref.shape[0]
    h = h_ref[...]
    h_hi, h_lo = _split_bf16(h)
    w = wt_ref[...]
    w_hi, w_lo = _split_bf16(w)
    logits = _dot_nt(w_hi, h_hi) + _dot_nt(w_hi, h_lo) + _dot_nt(w_lo, h_hi)
    scores = _sigmoid(logits)
    biased = scores + b_ref[...]

    neg = -jnp.inf
    sub = lax.broadcasted_iota(jnp.int32, (GROUP_SIZE, tm), 0)
    grp_rows = []
    for g in range(N_GROUPS):
        blk = biased[g * GROUP_SIZE:(g + 1) * GROUP_SIZE, :]
        m1 = jnp.max(blk, axis=0, keepdims=True)
        first = _first_argmax_mask(blk, sub, GROUP_SIZE)
        m2 = jnp.max(jnp.where(first, neg, blk), axis=0, keepdims=True)
        grp_rows.append(m1 + m2)
    gs = jnp.concatenate(grp_rows, axis=0)
    gsel = jnp.zeros((N_GROUPS, tm), jnp.bool_)
    gi = lax.broadcasted_iota(jnp.int32, (N_GROUPS, tm), 0)
    for _ in range(TOPK_GROUPS):
        pick = _first_argmax_mask(gs, gi, N_GROUPS)
        gsel = gsel | pick
        gs = jnp.where(pick, neg, gs)
    emask = jnp.concatenate(
        [jnp.broadcast_to(gsel[g:g + 1, :], (GROUP_SIZE, tm)) for g in range(N_GROUPS)], axis=0)
    cand = jnp.where(emask, biased, neg)
    ei = lax.broadcasted_iota(jnp.int32, (N_EXPERTS, tm), 0)
    chosen = jnp.zeros((N_EXPERTS, tm), jnp.bool_)
    for _ in range(TOP_K):
        pick = _first_argmax_mask(cand, ei, N_EXPERTS)
        chosen = chosen | pick
        cand = jnp.where(pick, neg, cand)
    wsel = jnp.where(chosen, scores, 0.0)
    comb_ref[...] = wsel / jnp.sum(wsel, axis=0, keepdims=True) * ROUTED_SCALE

    r = lax.broadcasted_iota(jnp.int32, (tm, tm), 0)
    c = lax.broadcasted_iota(jnp.int32, (tm, tm), 1)
    upper = jnp.where(r < c, 1.0, 0.0).astype(BF16)
    sel = jnp.where(chosen, 1.0, 0.0)
    rank_ref[...] = _dot(sel.astype(BF16), upper)
    cnt = jnp.sum(sel, axis=1, keepdims=True)
    cnt_ref[...] = jnp.broadcast_to(cnt, (N_EXPERTS, LANES)).astype(jnp.int32)


def _router_call(h32, router_w, router_b):
    t, d = h32.shape
    nt = t // TM_MOE
    bias = jnp.broadcast_to(router_b.reshape(N_EXPERTS, 1), (N_EXPERTS, TM_MOE))
    comb, rank, cnt = pl.pallas_call(
        _router_kernel,
        grid=(nt,),
        in_specs=[pl.BlockSpec((TM_MOE, d), lambda i: (i, 0)),
                  pl.BlockSpec((N_EXPERTS, d), lambda i: (0, 0)),
                  pl.BlockSpec((N_EXPERTS, TM_MOE), lambda i: (0, 0))],
        out_specs=[pl.BlockSpec((N_EXPERTS, TM_MOE), lambda i: (0, i)),
                   pl.BlockSpec((N_EXPERTS, TM_MOE), lambda i: (0, i)),
                   pl.BlockSpec((N_EXPERTS, LANES), lambda i: (i, 0))],
        out_shape=[jax.ShapeDtypeStruct((N_EXPERTS, t), F32),
                   jax.ShapeDtypeStruct((N_EXPERTS, t), F32),
                   jax.ShapeDtypeStruct((nt * N_EXPERTS, LANES), jnp.int32)],
        compiler_params=_cparams("parallel"),
        name="router",
    )(h32, router_w.T, bias)
    return comb, rank, cnt[:, 0].reshape(nt, N_EXPERTS)


def _moe_plan(cnt, t):
    nt = cnt.shape[0]
    pad = (cnt + SEG_ALIGN - 1) // SEG_ALIGN * SEG_ALIGN
    lseg = jnp.cumsum(pad, axis=1) - pad
    ltot = jnp.sum(pad, axis=1)
    etot = jnp.sum(pad, axis=0)
    region = (etot + ROW_BLK - 1) // ROW_BLK * ROW_BLK
    rend = jnp.cumsum(region)
    gpos = (rend - region)[None, :] + jnp.cumsum(pad, axis=0) - pad
    nblk = (rend[-1] // ROW_BLK).astype(jnp.int32).reshape(1)
    blk_start = jnp.arange(_moe_blocks(t), dtype=jnp.int32) * ROW_BLK
    blk_expert = jnp.minimum(jnp.sum(rend[None, :] <= blk_start[:, None], axis=1), N_EXPERTS - 1)
    npiece = jnp.stack([jnp.sum(pad // BIG_PIECE, axis=1),
                        jnp.sum((pad % BIG_PIECE) // SEG_ALIGN, axis=1)], axis=1)
    i32 = lambda a: a.astype(jnp.int32)
    return i32(pad), i32(lseg), i32(gpos), i32(ltot), i32(npiece), nblk, i32(blk_expert)


def _moe_blocks(t):
    nt = t // TM_MOE
    rows = t * TOP_K + nt * N_EXPERTS * (SEG_ALIGN - 1) + N_EXPERTS * (ROW_BLK - 1)
    return -(-rows // (2 * ROW_BLK)) * 2


def _zero_uncovered_blocks(ref2d, ntot):
    for cb in range(LOCAL_ROWS // SORT_BLK):
        @pl.when((cb + 1) * SORT_BLK > ntot)
        def _():
            ref2d[cb * SORT_BLK:(cb + 1) * SORT_BLK, :] = jnp.zeros((SORT_BLK, ref2d.shape[1]),
                                                                   ref2d.dtype)


def _build_slot_matrix(p_ref, comb_ref, rank_ref, pad_ref, lseg_ref, ntot, i, weighted):
    tm = p_ref.shape[1]
    _zero_uncovered_blocks(p_ref, ntot)
    rowi = lax.broadcasted_iota(jnp.int32, (SLOT_CHUNK, tm), 0).astype(F32)

    def expert_body(e, carry):
        rrow = rank_ref[pl.ds(e, 1), :]
        wrow = comb_ref[pl.ds(e, 1), :]
        base = lseg_ref[i, e]
        nch = lax.shift_right_logical(pad_ref[i, e] + (SLOT_CHUNK - 1), SLOT_CHUNK_LOG2)

        def chunk_body(c, carry2):
            off = c * SLOT_CHUNK
            hit = (rrow == rowi + off.astype(F32)) & (wrow > 0.0)
            val = jnp.where(hit, wrow if weighted else 1.0, 0.0).astype(BF16)
            p_ref[pl.ds(pl.multiple_of(base + off, SEG_ALIGN), SLOT_CHUNK), :] = val
            return carry2

        lax.fori_loop(0, nch, chunk_body, 0)
        return carry

    lax.fori_loop(0, N_EXPERTS, expert_body, 0, unroll=2)


def _segment_copies(pad_ref, lseg_ref, gpos_ref, tile, make_copy):
    def expert_body(e, carry):
        rows = pad_ref[tile, e]
        nbig = lax.shift_right_logical(rows, BIG_PIECE_LOG2)
        nsmall = lax.shift_right_logical(rows & (BIG_PIECE - 1), SEG_ALIGN_LOG2)
        l0 = lseg_ref[tile, e]
        g0 = gpos_ref[tile, e]

        def big(c, carry2):
            make_copy(pl.multiple_of(l0 + c * BIG_PIECE, SEG_ALIGN),
                      pl.multiple_of(g0 + c * BIG_PIECE, SEG_ALIGN), BIG_PIECE).start()
            return carry2

        lax.fori_loop(0, nbig, big, 0)
        l1 = l0 + nbig * BIG_PIECE
        g1 = g0 + nbig * BIG_PIECE

        def small(c, carry2):
            make_copy(pl.multiple_of(l1 + c * SEG_ALIGN, SEG_ALIGN),
                      pl.multiple_of(g1 + c * SEG_ALIGN, SEG_ALIGN), SEG_ALIGN).start()
            return carry2

        lax.fori_loop(0, nsmall, small, 0)
        return carry

    lax.fori_loop(0, N_EXPERTS, expert_body, 0, unroll=2)


def _wait_copies(npiece_ref, tile, make_copy):
    for col, rows in ((0, BIG_PIECE), (1, SEG_ALIGN)):
        def piece(c, carry, rows=rows):
            make_copy(0, 0, rows).wait()
            return carry

        lax.fori_loop(0, npiece_ref[tile, col], piece, 0)


def _dispatch_kernel(pad_ref, lseg_ref, gpos_ref, ltot_ref, npiece_ref, x_ref, comb_ref, rank_ref,
                     xs_hbm, p_ref, xs_ref, sem):
    i = pl.program_id(0)
    last = pl.num_programs(0) - 1
    buf = lax.rem(i, 2)
    ntot = ltot_ref[i]
    _build_slot_matrix(p_ref, comb_ref, rank_ref, pad_ref, lseg_ref, ntot, i, weighted=False)
    for cb in range(LOCAL_ROWS // SORT_BLK):
        @pl.when(cb * SORT_BLK < ntot)
        def _():
            rs = slice(cb * SORT_BLK, (cb + 1) * SORT_BLK)
            xs_ref[buf, rs, :] = _dot(p_ref[rs, :], x_ref[...]).astype(BF16)

    def copy_for(b):
        return lambda l0, g0, rows: pltpu.make_async_copy(
            xs_ref.at[b, pl.ds(l0, rows), :], xs_hbm.at[pl.ds(g0, rows), :], sem.at[b])

    _segment_copies(pad_ref, lseg_ref, gpos_ref, i, copy_for(buf))

    @pl.when(i > 0)
    def _():
        _wait_copies(npiece_ref, i - 1, copy_for(1 - buf))

    @pl.when(i == last)
    def _():
        _wait_copies(npiece_ref, i, copy_for(buf))


def _dispatch_call(h16, comb, rank, plan):
    t, d = h16.shape
    nt = t // TM_MOE
    pad, lseg, gpos, ltot, npiece, _, _ = plan
    grid_spec = pltpu.PrefetchScalarGridSpec(
        num_scalar_prefetch=5,
        grid=(nt,),
        in_specs=[
            pl.BlockSpec((TM_MOE, d), lambda i, *_: (i, 0)),
            pl.BlockSpec((N_EXPERTS, TM_MOE), lambda i, *_: (0, i)),
            pl.BlockSpec((N_EXPERTS, TM_MOE), lambda i, *_: (0, i)),
        ],
        out_specs=pl.BlockSpec(memory_space=pl.ANY),
        scratch_shapes=[
            pltpu.VMEM((LOCAL_ROWS, TM_MOE), BF16),
            pltpu.VMEM((2, LOCAL_ROWS, d), BF16),
            pltpu.SemaphoreType.DMA((2,)),
        ],
    )
    return pl.pallas_call(
        _dispatch_kernel,
        grid_spec=grid_spec,
        out_shape=jax.ShapeDtypeStruct((_moe_blocks(t) * ROW_BLK, d), BF16),
        compiler_params=_cparams("arbitrary"),
        name="moe_dispatch",
    )(pad, lseg, gpos, ltot, npiece, h16, comb, rank)


def _expert_ffn_kernel(nblk_ref, be_ref, x_ref, wg0_ref, wu0_ref, wd0_ref, wg1_ref, wu1_ref, wd1_ref,
                       o_ref):
    second = 2 * pl.program_id(0) + 1
    weights = ((wg0_ref, wu0_ref, wd0_ref), (wg1_ref, wu1_ref, wd1_ref))

    def chain(half):
        wg_ref, wu_ref, wd_ref = weights[half]
        rs = slice(half * ROW_BLK, (half + 1) * ROW_BLK)
        x = x_ref[rs, :]
        gate = _dot(x, wg_ref[0, 0].astype(BF16))
        up = _dot(x, wu_ref[0, 0].astype(BF16))
        hmid = (_silu(gate) * up).astype(BF16)
        o_ref[rs, :] = _dot(hmid, wd_ref[0, 0].astype(BF16)).astype(o_ref.dtype)

    @pl.when(second < nblk_ref[0])
    def _():
        chain(0)
        chain(1)

    @pl.when(second == nblk_ref[0])
    def _():
        chain(0)


def _expert_ffn_call(xs, plan, w_gate, w_up, w_down, layer):
    rows, d = xs.shape
    nblk, blk_expert = plan[-2:]
    live = lambda s, nblk: jnp.minimum(s, lax.shift_right_logical(nblk[0] - 1, 1))
    wmap = lambda half: (lambda s, nblk, be: (
        layer, be[jnp.minimum(2 * live(s, nblk) + half, nblk[0] - 1)], 0, 0))
    wspecs = []
    for half in range(2):
        wspecs += [pl.BlockSpec((1, 1, d, D_EXPERT), wmap(half)),
                   pl.BlockSpec((1, 1, d, D_EXPERT), wmap(half)),
                   pl.BlockSpec((1, 1, D_EXPERT, d), wmap(half))]
    grid_spec = pltpu.PrefetchScalarGridSpec(
        num_scalar_prefetch=2,
        grid=(rows // (2 * ROW_BLK),),
        in_specs=[pl.BlockSpec((2 * ROW_BLK, d), lambda s, nblk, be: (live(s, nblk), 0))] + wspecs,
        out_specs=pl.BlockSpec((2 * ROW_BLK, d), lambda s, nblk, be: (live(s, nblk), 0)),
    )
    return pl.pallas_call(
        _expert_ffn_kernel,
        grid_spec=grid_spec,
        out_shape=jax.ShapeDtypeStruct((rows, d), BF16),
        compiler_params=_cparams("arbitrary"),
        name="moe_expert_ffn",
    )(nblk, blk_expert, xs, w_gate, w_up, w_down, w_gate, w_up, w_down)


def _combine_kernel(pad_ref, lseg_ref, gpos_ref, ltot_ref, npiece_ref, ys_hbm, comb_ref, rank_ref,
                    h16_ref, h32_ref, p_ref, swgu_ref, swd_ref, pwi_ref, pwg_ref, lg_ref, lb_ref,
                    o32_ref, o16_ref, pw_ref, ys_ref, sem):
    i = pl.program_id(0)
    last = pl.num_programs(0) - 1
    buf = lax.rem(i, 2)

    def copy_for(b):
        return lambda l0, g0, rows: pltpu.make_async_copy(
            ys_hbm.at[pl.ds(g0, rows), :], ys_ref.at[b, pl.ds(l0, rows), :], sem.at[b])

    def fetch(tile, b):
        _zero_uncovered_blocks(ys_ref.at[b], ltot_ref[tile])
        _segment_copies(pad_ref, lseg_ref, gpos_ref, tile, copy_for(b))

    @pl.when(i == 0)
    def _():
        fetch(i, buf)

    @pl.when(i < last)
    def _():
        fetch(i + 1, 1 - buf)

    _build_slot_matrix(pw_ref, comb_ref, rank_ref, pad_ref, lseg_ref, ltot_ref[i], i, weighted=True)
    x = h16_ref[...]
    gu = _dot(x, swgu_ref[...])
    shared = _dot((_silu(gu[:, :D_SHARED]) * gu[:, D_SHARED:]).astype(BF16), swd_ref[...])
    ple = _dot(p_ref[...].astype(BF16), pwi_ref[...]) * _sigmoid(_dot(x, pwg_ref[...]))
    z = ALPHA * h32_ref[...] + shared + ple

    _wait_copies(npiece_ref, i, copy_for(buf))
    z += _dot_tn(pw_ref[...], ys_ref[buf])
    y = _layer_norm(z, lg_ref[...], lb_ref[...])
    o32_ref[...] = y
    o16_ref[...] = y.astype(BF16)


def _combine_call(ys, comb, rank, plan, h16, h32, p, sw_gate, sw_up, sw_down, ple_w_in, ple_w_gate,
                  ln_g, ln_b):
    t, d = h32.shape
    nt = t // TM_MOE
    pad, lseg, gpos, ltot, npiece, _, _ = plan
    row = pl.BlockSpec((TM_MOE, d), lambda i, *_: (i, 0))
    vec = pl.BlockSpec((1, d), lambda i, *_: (0, 0))
    full = lambda a, b: pl.BlockSpec((a, b), lambda i, *_: (0, 0))
    route = pl.BlockSpec((N_EXPERTS, TM_MOE), lambda i, *_: (0, i))
    swgu = jnp.concatenate([sw_gate, sw_up], axis=1).astype(BF16)
    grid_spec = pltpu.PrefetchScalarGridSpec(
        num_scalar_prefetch=5,
        grid=(nt,),
        in_specs=[pl.BlockSpec(memory_space=pl.ANY), route, route, row, row,
                  pl.BlockSpec((TM_MOE, PLE_DIM), lambda i, *_: (i, 0)),
                  full(d, 2 * D_SHARED), full(D_SHARED, d), full(PLE_DIM, d), full(d, d), vec, vec],
        out_specs=[row, row],
        scratch_shapes=[
            pltpu.VMEM((LOCAL_ROWS, TM_MOE), BF16),
            pltpu.VMEM((2, LOCAL_ROWS, d), BF16),
            pltpu.SemaphoreType.DMA((2,)),
        ],
    )
    return pl.pallas_call(
        _combine_kernel,
        grid_spec=grid_spec,
        out_shape=[jax.ShapeDtypeStruct((t, d), F32), jax.ShapeDtypeStruct((t, d), BF16)],
        compiler_params=_cparams("arbitrary"),
        name="moe_combine_tail",
    )(pad, lseg, gpos, ltot, npiece, ys, comb, rank, h16, h32, p, swgu, sw_down.astype(BF16),
      ple_w_in.astype(BF16), ple_w_gate.astype(BF16), ln_g.reshape(1, d), ln_b.reshape(1, d))


def kernel(x, p, ln_in_g, ln_in_b, w_in, gla_fg_w, gla_fg_b, gla_norm_g, sg_norm_g, sg_norm_b, sg_w, sg_b, mla_qn_g, mla_w_uq, mla_kvn_g, mla_w_ukv, w_branch, w_out, ln1_g, ln1_b, router_w, router_b, exp_w_gate, exp_w_up, exp_w_down, sh_w_gate, sh_w_up, sh_w_down, ple_w_in, ple_w_gate, ln2_g, ln2_b):
    batch, seq, d = x.shape
    t = batch * seq
    depth = w_in.shape[0]
    h32, h16 = _ln_call(x.reshape(t, d), ln_in_g, ln_in_b)
    for i in range(depth):
        proj = _proj_call(h16, _prep_w_in(w_in[i]))
        o_a = _gla_call(proj, gla_fg_w[i], gla_fg_b[i], gla_norm_g[i], batch, seq)
        o_b = _sg_call(proj, sg_norm_g[i], sg_norm_b[i], sg_w[i], sg_b[i])
        q, k, vt = _mla_prep_call(proj, mla_qn_g[i], mla_w_uq[i], mla_kvn_g[i], mla_w_ukv[i], seq)
        o_c = _attn_call(q, k, vt, batch, seq)
        h32, h16 = _merge_call(o_a, o_b, o_c, proj, h32, w_branch[i], w_out[i], ln1_g[i], ln1_b[i])
        comb, rank, cnt = _router_call(h32, router_w[i], router_b[i])
        plan = _moe_plan(cnt, t)
        xs = _dispatch_call(h16, comb, rank, plan)
        ys = _expert_ffn_call(xs, plan, exp_w_gate, exp_w_up, exp_w_down, i)
        h32, h16 = _combine_call(ys, comb, rank, plan, h16, h32, p[i].reshape(t, -1), sh_w_gate[i],
                                 sh_w_up[i], sh_w_down[i], ple_w_in[i], ple_w_gate[i],
                                 ln2_g[i], ln2_b[i])
    return h32.reshape(batch, seq, d)
```

```python
import functools

import jax
import jax.numpy as jnp
from jax import lax
from jax.experimental import pallas as pl
from jax.experimental.pallas import tpu as pltpu

F32 = jnp.float32
BF16 = jnp.bfloat16

D_MODEL = 1024
DEPTH = 2
CHUNK = 64
GLA_HEADS, GLA_DK, GLA_DV, GLA_GATE_RANK, GLA_TAU = 4, 128, 256, 16, 16.0
SG_WIDTH, SG_GROUPS, SG_BLOCK = 1024, 4, 128
MLA_HEADS, MLA_Q_RANK, MLA_KV_RANK = 8, 384, 256
MLA_NOPE, MLA_ROPE, MLA_V = 128, 64, 128
MLA_QK = MLA_NOPE + MLA_ROPE
ROPE_THETA = 10000.0
N_BRANCHES = 3
N_EXPERTS, N_GROUPS, TOPK_GROUPS, TOP_K = 64, 8, 4, 8
GROUP_SIZE = N_EXPERTS // N_GROUPS
D_EXPERT, D_SHARED = 256, 256
ROUTED_SCALE = 2.5
PLE_DIM = 256
ALPHA = (2 * DEPTH) ** 0.25

LANES = 128
SUBLANES = 8
VMEM_PHYSICAL_BYTES = 64 * 1024 * 1024
VMEM_LIMIT_BYTES = VMEM_PHYSICAL_BYTES - 4 * 1024 * 1024

P_SMALL = 1024
P_GFR = 384
P_CKV = 512
P_KROPE = 768
P_GATES = 1024
P_SG = 4096
P_GV = 6144
P_GR = 7168
P_GQ = 8192
P_GK = 8704
P_TOTAL = 9216
MLA_HEAD_PAD = 256

TM_LN = 512
TM_PROJ, TN_PROJ = 1024, 1024
R_GLA = 512
TM_SG = 256
TQ_ATT = 512
TK_ATT = 256
TM_MLA = TQ_ATT
MLA_V_EXT = MLA_V + 16
HEADS_ATT = 4
LOG2_E = 1.4426950408889634
TM_MERGE = 512
TM_MOE = 512
SEG_ALIGN_LOG2 = 4
SEG_ALIGN = 1 << SEG_ALIGN_LOG2
SLOT_CHUNK_LOG2 = 6
SLOT_CHUNK = 1 << SLOT_CHUNK_LOG2
SORT_BLK = 512
ROW_BLK = 512
BIG_PIECE_LOG2 = 6
BIG_PIECE = 1 << BIG_PIECE_LOG2
LOCAL_ROWS = -(-(TM_MOE * TOP_K + N_EXPERTS * (SEG_ALIGN - 1) + SLOT_CHUNK) // SORT_BLK) * SORT_BLK


def _cparams(*sem):
    return pltpu.CompilerParams(dimension_semantics=sem, vmem_limit_bytes=VMEM_LIMIT_BYTES)


def _dot(a, b):
    return jnp.dot(a, b, preferred_element_type=F32)


def _dot_nt(a, b):
    return lax.dot_general(a, b, (((1,), (1,)), ((), ())), preferred_element_type=F32)


def _dot_tn(a, b):
    return lax.dot_general(a, b, (((0,), (0,)), ((), ())), preferred_element_type=F32)


def _layer_norm(x, g, b, eps=1e-5):
    mu = jnp.mean(x, axis=-1, keepdims=True)
    xc = x - mu
    var = jnp.mean(xc * xc, axis=-1, keepdims=True)
    return xc * lax.rsqrt(var + eps) * g + b


def _rms_norm(x, g, eps=1e-6):
    return x * lax.rsqrt(jnp.mean(x * x, axis=-1, keepdims=True) + eps) * g


def _sigmoid(x):
    return 1.0 / (1.0 + jnp.exp(-x))


def _silu(x):
    return x * _sigmoid(x)


def _split_bf16(x):
    hi = x.astype(BF16)
    lo = (x - hi.astype(F32)).astype(BF16)
    return hi, lo


def _ln_kernel(x_ref, g_ref, b_ref, o32_ref, o16_ref):
    y = _layer_norm(x_ref[...], g_ref[...], b_ref[...])
    o32_ref[...] = y
    o16_ref[...] = y.astype(BF16)


def _ln_call(x, g, b):
    t, d = x.shape
    row = pl.BlockSpec((TM_LN, d), lambda i: (i, 0))
    vec = pl.BlockSpec((1, d), lambda i: (0, 0))
    return pl.pallas_call(
        _ln_kernel,
        grid=(t // TM_LN,),
        in_specs=[row, vec, vec],
        out_specs=[row, row],
        out_shape=[jax.ShapeDtypeStruct((t, d), F32), jax.ShapeDtypeStruct((t, d), BF16)],
        compiler_params=_cparams("parallel"),
        name="ln_in",
    )(x, g.reshape(1, d), b.reshape(1, d))


def _proj_kernel(x_ref, w_ref, o_ref):
    o_ref[...] = _dot(x_ref[...], w_ref[...]).astype(o_ref.dtype)


def _proj_call(h16, w):
    t, k = h16.shape
    n = w.shape[1]
    return pl.pallas_call(
        _proj_kernel,
        grid=(t // TM_PROJ, n // TN_PROJ),
        in_specs=[pl.BlockSpec((TM_PROJ, k), lambda i, j: (i, 0)),
                  pl.BlockSpec((k, TN_PROJ), lambda i, j: (0, j))],
        out_specs=pl.BlockSpec((TM_PROJ, TN_PROJ), lambda i, j: (i, j)),
        out_shape=jax.ShapeDtypeStruct((t, n), BF16),
        compiler_params=_cparams("parallel", "arbitrary"),
        name="in_proj",
    )(h16, w)


def _prep_w_in(w_in):
    d = w_in.shape[0]
    gq, gk = w_in[:, 0:512], w_in[:, 512:1024]
    gv, gr = w_in[:, 1024:2048], w_in[:, 2048:3072]
    gfr = w_in[:, 3072:3088]
    sgz = w_in[:, 3088:5136]
    mq = w_in[:, 5136:5520]
    ckv = w_in[:, 5520:5776]
    kr = w_in[:, 5776:5840]
    gates = w_in[:, 5840:8912]
    half = MLA_ROPE // 2
    kr_rot = jnp.concatenate([-kr[:, half:], kr[:, :half]], axis=1)
    z = lambda n: jnp.zeros((d, n), w_in.dtype)
    small = jnp.concatenate([mq, gfr, z(112), ckv, kr, kr_rot, z(128)], axis=1)
    return jnp.concatenate([small, gates, sgz, gv, gr, gq, gk], axis=1).astype(BF16)


def _gla_kernel(q_ref, k_ref, v_ref, gr_ref, gf_ref, fgw_ref, fgb_ref, g_ref, tri_ref, blk_ref,
                o_ref, st_ref, qd_ref, ki_ref, ke_ref, dec_ref):
    rows = q_ref.shape[0]

    @pl.when(pl.program_id(1) == 0)
    def _():
        st_ref[...] = jnp.zeros_like(st_ref)

    pre = _dot(gf_ref[...], fgw_ref[...]) + fgb_ref[...]
    glog = -(jnp.maximum(-pre, 0.0) + jnp.log1p(jnp.exp(-jnp.abs(pre)))) * (1.0 / GLA_TAU)

    hi, lo = _split_bf16(glog)
    bcum = _dot(tri_ref[...], hi) + _dot(tri_ref[...], lo)
    btot = _dot(blk_ref[...], hi) + _dot(blk_ref[...], lo)

    q = q_ref[...].astype(F32) * (GLA_DK ** -0.5)
    k = k_ref[...].astype(F32)
    qd_ref[...] = (q * jnp.exp(bcum)).astype(BF16)
    ki_ref[...] = (k * jnp.exp(-bcum)).astype(BF16)
    ke_ref[...] = (k * jnp.exp(btot - bcum)).astype(BF16)
    dec_ref[...] = jnp.exp(btot)

    tr = lax.broadcasted_iota(jnp.int32, (CHUNK, CHUNK), 0)
    tc = lax.broadcasted_iota(jnp.int32, (CHUNK, CHUNK), 1)
    causal = tc <= tr

    def chunk_body(ci, carry):
        r0 = pl.multiple_of(ci * CHUNK, CHUNK)
        rs = pl.ds(r0, CHUNK)
        for hd in range(GLA_HEADS):
            ks = slice(hd * GLA_DK, (hd + 1) * GLA_DK)
            vs = slice(hd * GLA_DV, (hd + 1) * GLA_DV)
            qd = qd_ref[rs, ks]
            ki = ki_ref[rs, ks]
            ke = ke_ref[rs, ks]
            v = v_ref[rs, vs]
            att = jnp.where(causal, _dot_nt(qd, ki), 0.0)
            st = st_ref[hd]
            o = _dot(att.astype(BF16), v) + _dot_nt(qd, st.astype(BF16))
            dec = dec_ref[pl.ds(r0, 1), ks]
            st_ref[hd] = st * dec + _dot_tn(v, ke)
            on = _rms_norm(o, g_ref[:, vs])
            gate = gr_ref[rs, vs].astype(F32)
            o_ref[rs, vs] = (on * _silu(gate)).astype(o_ref.dtype)
        return carry

    lax.fori_loop(0, rows // CHUNK, chunk_body, 0)


def _gla_call(proj, fg_w, fg_b, gla_g, batch, seq):
    t = proj.shape[0]
    nr = seq // R_GLA
    row = lambda i, j: i * nr + j
    fgw = jnp.zeros((LANES, GLA_HEADS * GLA_DK), F32).at[:GLA_GATE_RANK].set(fg_w).astype(BF16)
    hk = GLA_HEADS * GLA_DK
    hv = GLA_HEADS * GLA_DV
    ridx = jnp.arange(R_GLA, dtype=jnp.int32)
    same = (ridx[:, None] // CHUNK) == (ridx[None, :] // CHUNK)
    blk = same.astype(BF16)
    tri = (same & (ridx[None, :] <= ridx[:, None])).astype(BF16)
    return pl.pallas_call(
        _gla_kernel,
        grid=(batch, nr),
        in_specs=[
            pl.BlockSpec((R_GLA, hk), lambda i, j: (row(i, j), P_GQ // hk)),
            pl.BlockSpec((R_GLA, hk), lambda i, j: (row(i, j), P_GK // hk)),
            pl.BlockSpec((R_GLA, hv), lambda i, j: (row(i, j), P_GV // hv)),
            pl.BlockSpec((R_GLA, hv), lambda i, j: (row(i, j), P_GR // hv)),
            pl.BlockSpec((R_GLA, LANES), lambda i, j: (row(i, j), P_GFR // LANES)),
            pl.BlockSpec((LANES, hk), lambda i, j: (0, 0)),
            pl.BlockSpec((1, hk), lambda i, j: (0, 0)),
            pl.BlockSpec((1, hv), lambda i, j: (0, 0)),
            pl.BlockSpec((R_GLA, R_GLA), lambda i, j: (0, 0)),
            pl.BlockSpec((R_GLA, R_GLA), lambda i, j: (0, 0)),
        ],
        out_specs=pl.BlockSpec((R_GLA, hv), lambda i, j: (row(i, j), 0)),
        out_shape=jax.ShapeDtypeStruct((t, hv), BF16),
        scratch_shapes=[
            pltpu.VMEM((GLA_HEADS, GLA_DV, GLA_DK), F32),
            pltpu.VMEM((R_GLA, hk), BF16),
            pltpu.VMEM((R_GLA, hk), BF16),
            pltpu.VMEM((R_GLA, hk), BF16),
            pltpu.VMEM((R_GLA, hk), F32),
        ],
        compiler_params=_cparams("parallel", "arbitrary"),
        name="gla",
    )(proj, proj, proj, proj, proj, fgw, fg_b.reshape(1, hk), gla_g.reshape(1, hv), tri, blk)


def _gelu(x):
    return 0.5 * x * (1.0 + lax.erf(x * (2.0 ** -0.5)))


def _sg_kernel(u_ref, v_ref, g_ref, b_ref, ws_ref, bias_ref, o_ref):
    rows = u_ref.shape[0]
    gw = SG_WIDTH // SG_GROUPS
    u = _gelu(u_ref[...].astype(F32))
    v = _gelu(v_ref[...].astype(F32))
    vn = _layer_norm(v, g_ref[...], b_ref[...]).astype(BF16)
    tr = lax.broadcasted_iota(jnp.int32, (SG_BLOCK, SG_BLOCK), 0)
    tc = lax.broadcasted_iota(jnp.int32, (SG_BLOCK, SG_BLOCK), 1)
    tril = tc <= tr
    for g in range(SG_GROUPS):
        w = jnp.where(tril, ws_ref[g], 0.0).astype(BF16)
        cs = slice(g * gw, (g + 1) * gw)
        for blk in range(rows // SG_BLOCK):
            rs = slice(blk * SG_BLOCK, (blk + 1) * SG_BLOCK)
            sv = _dot(w, vn[rs, cs]) + bias_ref[:, cs]
            o_ref[rs, cs] = (u[rs, cs] * sv).astype(o_ref.dtype)


def _sg_call(proj, sg_g, sg_bn, sg_w, sg_b):
    t = proj.shape[0]
    gw = SG_WIDTH // SG_GROUPS
    bias = jnp.repeat(sg_b.T, gw, axis=1)
    return pl.pallas_call(
        _sg_kernel,
        grid=(t // TM_SG,),
        in_specs=[
            pl.BlockSpec((TM_SG, SG_WIDTH), lambda i: (i, P_SG // SG_WIDTH)),
            pl.BlockSpec((TM_SG, SG_WIDTH), lambda i: (i, P_SG // SG_WIDTH + 1)),
            pl.BlockSpec((1, SG_WIDTH), lambda i: (0, 0)),
            pl.BlockSpec((1, SG_WIDTH), lambda i: (0, 0)),
            pl.BlockSpec((SG_GROUPS, SG_BLOCK, SG_BLOCK), lambda i: (0, 0, 0)),
            pl.BlockSpec((SG_BLOCK, SG_WIDTH), lambda i: (0, 0)),
        ],
        out_specs=pl.BlockSpec((TM_SG, SG_WIDTH), lambda i: (i, 0)),
        out_shape=jax.ShapeDtypeStruct((t, SG_WIDTH), BF16),
        compiler_params=_cparams("parallel"),
        name="spatial_gating",
    )(proj, proj, sg_g.reshape(1, -1), sg_bn.reshape(1, -1), sg_w, bias)


def _mla_prep_kernel(sm_ref, cs_ref, qg_ref, kg_ref, wq_ref, wkv_ref, q_ref, k_ref, vt_ref):
    rows = sm_ref.shape[0]
    mq = sm_ref[:, 0:MLA_Q_RANK].astype(F32)
    qn = _rms_norm(mq, qg_ref[...]).astype(BF16)
    qf = _dot(qn, wq_ref[...])
    ckv = sm_ref[:, P_CKV:P_CKV + MLA_KV_RANK].astype(F32)
    cn = _rms_norm(ckv, kg_ref[...]).astype(BF16)
    kv = _dot(cn, wkv_ref[...])

    cs = cs_ref[...]
    lane = lax.broadcasted_iota(jnp.int32, (rows, LANES), 1)
    low = lane < MLA_ROPE

    def rope(pair):
        t = pair * cs
        return jnp.where(low, t + pltpu.roll(t, MLA_ROPE, 1), 0.0)

    scale = (MLA_QK ** -0.5) * LOG2_E
    kr = rope(sm_ref[:, P_KROPE:P_KROPE + LANES].astype(F32)).astype(BF16)
    for h in range(MLA_HEADS):
        c0 = h * MLA_HEAD_PAD
        q_ref[:, c0:c0 + MLA_NOPE] = (qf[:, c0:c0 + MLA_NOPE] * scale).astype(BF16)
        q_ref[:, c0 + MLA_NOPE:c0 + MLA_HEAD_PAD] = (
            rope(qf[:, c0 + MLA_NOPE:c0 + MLA_HEAD_PAD]) * scale).astype(BF16)
        k_ref[:, c0:c0 + MLA_NOPE] = kv[:, h * MLA_NOPE:(h + 1) * MLA_NOPE].astype(BF16)
        k_ref[:, c0 + MLA_NOPE:c0 + MLA_HEAD_PAD] = kr
    vt = kv[:, MLA_HEADS * MLA_NOPE:].T.astype(BF16)
    ones = jnp.ones((MLA_V_EXT - MLA_V, TK_ATT), BF16)
    for blk in range(rows // TK_ATT):
        ks = slice(blk * TK_ATT, (blk + 1) * TK_ATT)
        for h in range(MLA_HEADS):
            r0 = h * MLA_V_EXT
            vt_ref[blk, r0:r0 + MLA_V, :] = vt[h * MLA_V:(h + 1) * MLA_V, ks]
            vt_ref[blk, r0 + MLA_V:r0 + MLA_V_EXT, :] = ones


def _mla_prep_call(proj, qn_g, w_uq, kvn_g, w_ukv, seq):
    t = proj.shape[0]
    half = MLA_ROPE // 2
    wq = w_uq.reshape(MLA_Q_RANK, MLA_HEADS, MLA_QK)
    wr = wq[:, :, MLA_NOPE:]
    wr_rot = jnp.concatenate([-wr[:, :, half:], wr[:, :, :half]], axis=2)
    wq_ext = jnp.concatenate([wq[:, :, :MLA_NOPE], wr, wr_rot], axis=2)
    wq_ext = wq_ext.reshape(MLA_Q_RANK, MLA_HEADS * MLA_HEAD_PAD).astype(BF16)
    wkv = w_ukv.reshape(MLA_KV_RANK, MLA_HEADS, MLA_NOPE + MLA_V)
    wkv = jnp.concatenate([wkv[:, :, :MLA_NOPE].reshape(MLA_KV_RANK, -1),
                           wkv[:, :, MLA_NOPE:].reshape(MLA_KV_RANK, -1)], axis=1).astype(BF16)
    inv = ROPE_THETA ** (-jnp.arange(half, dtype=F32) / half)
    ang = jnp.arange(seq, dtype=F32)[:, None] * inv
    cs = jnp.concatenate([jnp.cos(ang), jnp.cos(ang), jnp.sin(ang), jnp.sin(ang)], axis=1)
    ns = seq // TM_MLA
    hq = MLA_HEADS * MLA_HEAD_PAD
    hv = MLA_HEADS * MLA_V_EXT
    return pl.pallas_call(
        _mla_prep_kernel,
        grid=(t // TM_MLA,),
        in_specs=[
            pl.BlockSpec((TM_MLA, P_SMALL), lambda i: (i, 0)),
            pl.BlockSpec((TM_MLA, LANES), lambda i: (i % ns, 0)),
            pl.BlockSpec((1, MLA_Q_RANK), lambda i: (0, 0)),
            pl.BlockSpec((1, MLA_KV_RANK), lambda i: (0, 0)),
            pl.BlockSpec((MLA_Q_RANK, hq), lambda i: (0, 0)),
            pl.BlockSpec((MLA_KV_RANK, MLA_HEADS * (MLA_NOPE + MLA_V)), lambda i: (0, 0)),
        ],
        out_specs=[
            pl.BlockSpec((TM_MLA, hq), lambda i: (i, 0)),
            pl.BlockSpec((TM_MLA, hq), lambda i: (i, 0)),
            pl.BlockSpec((TM_MLA // TK_ATT, hv, TK_ATT), lambda i: (i, 0, 0)),
        ],
        out_shape=[jax.ShapeDtypeStruct((t, hq), BF16), jax.ShapeDtypeStruct((t, hq), BF16),
                   jax.ShapeDtypeStruct((t // TK_ATT, hv, TK_ATT), BF16)],
        compiler_params=_cparams("parallel"),
        name="mla_prep",
    )(proj, cs, qn_g.reshape(1, -1), kvn_g.reshape(1, -1), wq_ext, wkv)


def _attn_kernel(q_ref, k_ref, vt_ref, o_ref, acc_ref, sa_ref, sb_ref):
    tq = q_ref.shape[0]
    i = pl.program_id(2)
    acc_ref[...] = jnp.zeros_like(acc_ref)

    def scores(j, s_ref):
        r0 = pl.multiple_of(j * TK_ATT, TK_ATT)
        for hh in range(HEADS_ATT):
            qs = slice(hh * MLA_HEAD_PAD, (hh + 1) * MLA_HEAD_PAD)
            s_ref[hh] = _dot_nt(k_ref[pl.ds(r0, TK_ATT), qs], q_ref[:, qs])

    def consume(j, s_ref, ms, mask):
        new_m = []
        for hh in range(HEADS_ATT):
            s = s_ref[hh]
            if mask is not None:
                s = jnp.where(mask, s, -jnp.inf)
            m_new = jnp.maximum(ms[hh], jnp.max(s, axis=0, keepdims=True))
            alpha = jnp.exp2(ms[hh] - m_new)
            p = jnp.exp2(s - m_new).astype(BF16)
            vt = vt_ref[j, hh * MLA_V_EXT:(hh + 1) * MLA_V_EXT, :]
            acc_ref[hh] = alpha * acc_ref[hh] + _dot(vt, p)
            new_m.append(m_new)
        return tuple(new_m)

    def pair(jj, ms):
        j = 2 * jj
        scores(j + 1, sb_ref)
        ms = consume(j, sa_ref, ms, None)
        scores(j + 2, sa_ref)
        return consume(j + 1, sb_ref, ms, None)

    scores(0, sa_ref)
    m0 = tuple(jnp.full((1, tq), -jnp.inf, F32) for _ in range(HEADS_ATT))
    ms = lax.fori_loop(0, i * (tq // (2 * TK_ATT)), pair, m0)
    jd = i * (tq // TK_ATT)
    scores(jd + 1, sb_ref)
    kk = lax.shift_right_logical(lax.broadcasted_iota(jnp.int32, (TK_ATT, tq), 0), 6)
    qq = lax.shift_right_logical(lax.broadcasted_iota(jnp.int32, (TK_ATT, tq), 1), 6)
    ms = consume(jd, sa_ref, ms, kk <= qq)
    ms = consume(jd + 1, sb_ref, ms, kk + (TK_ATT // CHUNK) <= qq)
    for hh in range(HEADS_ATT):
        acc = acc_ref[hh]
        o = acc[:MLA_V, :] / acc[MLA_V:MLA_V + 1, :]
        o_ref[:, hh * MLA_V:(hh + 1) * MLA_V] = o.T.astype(o_ref.dtype)


def _attn_call(q, k, vt, batch, seq):
    t = q.shape[0]
    nq = seq // TQ_ATT
    qw = HEADS_ATT * MLA_HEAD_PAD
    vw = HEADS_ATT * MLA_V_EXT
    s_buf = pltpu.VMEM((HEADS_ATT, TK_ATT, TQ_ATT), F32)
    return pl.pallas_call(
        _attn_kernel,
        grid=(batch, MLA_HEADS // HEADS_ATT, nq),
        in_specs=[
            pl.BlockSpec((TQ_ATT, qw), lambda b, h, i: (b * nq + i, h)),
            pl.BlockSpec((seq, qw), lambda b, h, i: (b, h)),
            pl.BlockSpec((seq // TK_ATT, vw, TK_ATT), lambda b, h, i: (b, h, 0)),
        ],
        out_specs=pl.BlockSpec((TQ_ATT, HEADS_ATT * MLA_V), lambda b, h, i: (b * nq + i, h)),
        out_shape=jax.ShapeDtypeStruct((t, MLA_HEADS * MLA_V), BF16),
        scratch_shapes=[pltpu.VMEM((HEADS_ATT, MLA_V_EXT, TQ_ATT), F32), s_buf, s_buf],
        compiler_params=_cparams("parallel", "parallel", "arbitrary"),
        name="mla_attention",
    )(q, k, vt)


def _merge_kernel(oa_ref, ob_ref, oc_ref, g0_ref, g1_ref, g2_ref, h_ref, wb_ref, wo_ref,
                  lg_ref, lb_ref, o32_ref, o16_ref):
    merged = _sigmoid(g0_ref[...].astype(F32)) * _dot(oa_ref[...], wb_ref[0])
    merged += _sigmoid(g1_ref[...].astype(F32)) * _dot(ob_ref[...], wb_ref[1])
    merged += _sigmoid(g2_ref[...].astype(F32)) * _dot(oc_ref[...], wb_ref[2])
    mix = _dot(merged.astype(BF16), wo_ref[...])
    y = _layer_norm(ALPHA * h_ref[...] + mix, lg_ref[...], lb_ref[...])
    o32_ref[...] = y
    o16_ref[...] = y.astype(BF16)


def _merge_call(o_a, o_b, o_c, proj, h32, w_branch, w_out, ln_g, ln_b):
    t, d = h32.shape
    row = pl.BlockSpec((TM_MERGE, d), lambda i: (i, 0))
    gate = lambda n: pl.BlockSpec((TM_MERGE, d), lambda i: (i, P_GATES // d + n))
    vec = pl.BlockSpec((1, d), lambda i: (0, 0))
    return pl.pallas_call(
        _merge_kernel,
        grid=(t // TM_MERGE,),
        in_specs=[row, row, row, gate(0), gate(1), gate(2), row,
                  pl.BlockSpec((N_BRANCHES, d, d), lambda i: (0, 0, 0)),
                  pl.BlockSpec((d, d), lambda i: (0, 0)), vec, vec],
        out_specs=[row, row],
        out_shape=[jax.ShapeDtypeStruct((t, d), F32), jax.ShapeDtypeStruct((t, d), BF16)],
        compiler_params=_cparams("parallel"),
        name="merge",
    )(o_a, o_b, o_c, proj, proj, proj, h32, w_branch.astype(BF16), w_out.astype(BF16),
      ln_g.reshape(1, d), ln_b.reshape(1, d))


def _first_argmax_mask(vals, iota, n):
    m = jnp.max(vals, axis=0, keepdims=True)
    idx = jnp.min(jnp.where(vals == m, iota, n), axis=0, keepdims=True)
    return iota == idx


def _router_kernel(h_ref, wt_ref, b_ref, comb_ref, rank_ref, cnt_ref):
    tm = h_ref.shape[0]
    h = h_ref[...]
    h_hi, h_lo = _split_bf16(h)
    w = wt_ref[...]
    w_hi, w_lo = _split_bf16(w)
    logits = _dot_nt(w_hi, h_hi) + _dot_nt(w_hi, h_lo) + _dot_nt(w_lo, h_hi)
    scores = _sigmoid(logits)
    biased = scores + b_ref[...]

    neg = -jnp.inf
    sub = lax.broadcasted_iota(jnp.int32, (GROUP_SIZE, tm), 0)
    grp_rows = []
    for g in range(N_GROUPS):
        blk = biased[g * GROUP_SIZE:(g + 1) * GROUP_SIZE, :]
        m1 = jnp.max(blk, axis=0, keepdims=True)
        first = _first_argmax_mask(blk, sub, GROUP_SIZE)
        m2 = jnp.max(jnp.where(first, neg, blk), axis=0, keepdims=True)
        grp_rows.append(m1 + m2)
    gs = jnp.concatenate(grp_rows, axis=0)
    gsel = jnp.zeros((N_GROUPS, tm), jnp.bool_)
    gi = lax.broadcasted_iota(jnp.int32, (N_GROUPS, tm), 0)
    for _ in range(TOPK_GROUPS):
        pick = _first_argmax_mask(gs, gi, N_GROUPS)
        gsel = gsel | pick
        gs = jnp.where(pick, neg, gs)
    emask = jnp.concatenate(
        [jnp.broadcast_to(gsel[g:g + 1, :], (GROUP_SIZE, tm)) for g in range(N_GROUPS)], axis=0)
    cand = jnp.where(emask, biased, neg)
    ei = lax.broadcasted_iota(jnp.int32, (N_EXPERTS, tm), 0)
    chosen = jnp.zeros((N_EXPERTS, tm), jnp.bool_)
    for _ in range(TOP_K):
        pick = _first_argmax_mask(cand, ei, N_EXPERTS)
        chosen = chosen | pick
        cand = jnp.where(pick, neg, cand)
    wsel = jnp.where(chosen, scores, 0.0)
    comb_ref[...] = wsel / jnp.sum(wsel, axis=0, keepdims=True) * ROUTED_SCALE

    r = lax.broadcasted_iota(jnp.int32, (tm, tm), 0)
    c = lax.broadcasted_iota(jnp.int32, (tm, tm), 1)
    upper = jnp.where(r < c, 1.0, 0.0).astype(BF16)
    sel = jnp.where(chosen, 1.0, 0.0)
    rank_ref[...] = _dot(sel.astype(BF16), upper)
    cnt = jnp.sum(sel, axis=1, keepdims=True)
    cnt_ref[...] = jnp.broadcast_to(cnt, (N_EXPERTS, LANES)).astype(jnp.int32)


def _router_call(h32, router_w, router_b):
    t, d = h32.shape
    nt = t // TM_MOE
    bias = jnp.broadcast_to(router_b.reshape(N_EXPERTS, 1), (N_EXPERTS, TM_MOE))
    comb, rank, cnt = pl.pallas_call(
        _router_kernel,
        grid=(nt,),
        in_specs=[pl.BlockSpec((TM_MOE, d), lambda i: (i, 0)),
                  pl.BlockSpec((N_EXPERTS, d), lambda i: (0, 0)),
                  pl.BlockSpec((N_EXPERTS, TM_MOE), lambda i: (0, 0))],
        out_specs=[pl.BlockSpec((N_EXPERTS, TM_MOE), lambda i: (0, i)),
                   pl.BlockSpec((N_EXPERTS, TM_MOE), lambda i: (0, i)),
                   pl.BlockSpec((N_EXPERTS, LANES), lambda i: (i, 0))],
        out_shape=[jax.ShapeDtypeStruct((N_EXPERTS, t), F32),
                   jax.ShapeDtypeStruct((N_EXPERTS, t), F32),
                   jax.ShapeDtypeStruct((nt * N_EXPERTS, LANES), jnp.int32)],
        compiler_params=_cparams("parallel"),
        name="router",
    )(h32, router_w.T, bias)
    return comb, rank, cnt[:, 0].reshape(nt, N_EXPERTS)


def _moe_plan(cnt, t):
    nt = cnt.shape[0]
    pad = (cnt + SEG_ALIGN - 1) // SEG_ALIGN * SEG_ALIGN
    lseg = jnp.cumsum(pad, axis=1) - pad
    ltot = jnp.sum(pad, axis=1)
    etot = jnp.sum(pad, axis=0)
    region = (etot + ROW_BLK - 1) // ROW_BLK * ROW_BLK
    rend = jnp.cumsum(region)
    gpos = (rend - region)[None, :] + jnp.cumsum(pad, axis=0) - pad
    nblk = (rend[-1] // ROW_BLK).astype(jnp.int32).reshape(1)
    blk_start = jnp.arange(_moe_blocks(t), dtype=jnp.int32) * ROW_BLK
    blk_expert = jnp.minimum(jnp.sum(rend[None, :] <= blk_start[:, None], axis=1), N_EXPERTS - 1)
    npiece = jnp.stack([jnp.sum(pad // BIG_PIECE, axis=1),
                        jnp.sum((pad % BIG_PIECE) // SEG_ALIGN, axis=1)], axis=1)
    i32 = lambda a: a.astype(jnp.int32)
    return i32(pad), i32(lseg), i32(gpos), i32(ltot), i32(npiece), nblk, i32(blk_expert)


def _moe_blocks(t):
    nt = t // TM_MOE
    rows = t * TOP_K + nt * N_EXPERTS * (SEG_ALIGN - 1) + N_EXPERTS * (ROW_BLK - 1)
    return -(-rows // (2 * ROW_BLK)) * 2


def _zero_uncovered_blocks(ref2d, ntot):
    for cb in range(LOCAL_ROWS // SORT_BLK):
        @pl.when((cb + 1) * SORT_BLK > ntot)
        def _():
            ref2d[cb * SORT_BLK:(cb + 1) * SORT_BLK, :] = jnp.zeros((SORT_BLK, ref2d.shape[1]),
                                                                   ref2d.dtype)


def _build_slot_matrices(p_ref, pw_ref, comb_ref, rank_ref, pad_ref, lseg_ref, ntot, i):
    tm = p_ref.shape[1]
    _zero_uncovered_blocks(p_ref, ntot)
    _zero_uncovered_blocks(pw_ref, ntot)
    rowi = lax.broadcasted_iota(jnp.int32, (SLOT_CHUNK, tm), 0).astype(F32)

    def expert_body(e, carry):
        rrow = rank_ref[pl.ds(e, 1), :]
        wrow = comb_ref[pl.ds(e, 1), :]
        base = lseg_ref[i, e]
        nch = lax.shift_right_logical(pad_ref[i, e] + (SLOT_CHUNK - 1), SLOT_CHUNK_LOG2)

        def chunk_body(c, carry2):
            off = c * SLOT_CHUNK
            hit = (rrow == rowi + off.astype(F32)) & (wrow > 0.0)
            rs = pl.ds(pl.multiple_of(base + off, SEG_ALIGN), SLOT_CHUNK)
            p_ref[rs, :] = jnp.where(hit, 1.0, 0.0).astype(BF16)
            pw_ref[rs, :] = jnp.where(hit, wrow, 0.0).astype(BF16)
            return carry2

        lax.fori_loop(0, nch, chunk_body, 0)
        return carry

    lax.fori_loop(0, N_EXPERTS, expert_body, 0, unroll=2)


def _segment_copies(pad_ref, lseg_ref, gpos_ref, tile, make_copy):
    def expert_body(e, carry):
        rows = pad_ref[tile, e]
        nbig = lax.shift_right_logical(rows, BIG_PIECE_LOG2)
        nsmall = lax.shift_right_logical(rows & (BIG_PIECE - 1), SEG_ALIGN_LOG2)
        l0 = lseg_ref[tile, e]
        g0 = gpos_ref[tile, e]

        def big(c, carry2):
            make_copy(pl.multiple_of(l0 + c * BIG_PIECE, SEG_ALIGN),
                      pl.multiple_of(g0 + c * BIG_PIECE, SEG_ALIGN), BIG_PIECE).start()
            return carry2

        lax.fori_loop(0, nbig, big, 0)
        l1 = l0 + nbig * BIG_PIECE
        g1 = g0 + nbig * BIG_PIECE

        def small(c, carry2):
            make_copy(pl.multiple_of(l1 + c * SEG_ALIGN, SEG_ALIGN),
                      pl.multiple_of(g1 + c * SEG_ALIGN, SEG_ALIGN), SEG_ALIGN).start()
            return carry2

        lax.fori_loop(0, nsmall, small, 0)
        return carry

    lax.fori_loop(0, N_EXPERTS, expert_body, 0, unroll=2)


def _wait_copies(npiece_ref, tile, make_copy):
    for col, rows in ((0, BIG_PIECE), (1, SEG_ALIGN)):
        def piece(c, carry, rows=rows):
            make_copy(0, 0, rows).wait()
            return carry

        lax.fori_loop(0, npiece_ref[tile, col], piece, 0)


def _dispatch_kernel(pad_ref, lseg_ref, gpos_ref, ltot_ref, npiece_ref, x_ref, comb_ref, rank_ref,
                     xs_hbm, pw_ref, p_ref, xs_ref, sem):
    i = pl.program_id(0)
    last = pl.num_programs(0) - 1
    buf = lax.rem(i, 2)
    ntot = ltot_ref[i]
    _build_slot_matrices(p_ref, pw_ref, comb_ref, rank_ref, pad_ref, lseg_ref, ntot, i)
    for cb in range(LOCAL_ROWS // SORT_BLK):
        @pl.when(cb * SORT_BLK < ntot)
        def _():
            rs = slice(cb * SORT_BLK, (cb + 1) * SORT_BLK)
            xs_ref[buf, rs, :] = _dot(p_ref[rs, :], x_ref[...]).astype(BF16)

    def copy_for(b):
        return lambda l0, g0, rows: pltpu.make_async_copy(
            xs_ref.at[b, pl.ds(l0, rows), :], xs_hbm.at[pl.ds(g0, rows), :], sem.at[b])

    _segment_copies(pad_ref, lseg_ref, gpos_ref, i, copy_for(buf))

    @pl.when(i > 0)
    def _():
        _wait_copies(npiece_ref, i - 1, copy_for(1 - buf))

    @pl.when(i == last)
    def _():
        _wait_copies(npiece_ref, i, copy_for(buf))


def _dispatch_call(h16, comb, rank, plan):
    t, d = h16.shape
    nt = t // TM_MOE
    pad, lseg, gpos, ltot, npiece, _, _ = plan
    grid_spec = pltpu.PrefetchScalarGridSpec(
        num_scalar_prefetch=5,
        grid=(nt,),
        in_specs=[
            pl.BlockSpec((TM_MOE, d), lambda i, *_: (i, 0)),
            pl.BlockSpec((N_EXPERTS, TM_MOE), lambda i, *_: (0, i)),
            pl.BlockSpec((N_EXPERTS, TM_MOE), lambda i, *_: (0, i)),
        ],
        out_specs=[pl.BlockSpec(memory_space=pl.ANY),
                   pl.BlockSpec((LOCAL_ROWS, TM_MOE), lambda i, *_: (i, 0))],
        scratch_shapes=[
            pltpu.VMEM((LOCAL_ROWS, TM_MOE), BF16),
            pltpu.VMEM((2, LOCAL_ROWS, d), BF16),
            pltpu.SemaphoreType.DMA((2,)),
        ],
    )
    return pl.pallas_call(
        _dispatch_kernel,
        grid_spec=grid_spec,
        out_shape=[jax.ShapeDtypeStruct((_moe_blocks(t) * ROW_BLK, d), BF16),
                   jax.ShapeDtypeStruct((nt * LOCAL_ROWS, TM_MOE), BF16)],
        compiler_params=_cparams("arbitrary"),
        name="moe_dispatch",
    )(pad, lseg, gpos, ltot, npiece, h16, comb, rank)


def _expert_ffn_kernel(nblk_ref, be_ref, x_ref, wg0_ref, wu0_ref, wd0_ref, wg1_ref, wu1_ref, wd1_ref,
                       o_ref):
    second = 2 * pl.program_id(0) + 1
    weights = ((wg0_ref, wu0_ref, wd0_ref), (wg1_ref, wu1_ref, wd1_ref))

    def chain(half):
        wg_ref, wu_ref, wd_ref = weights[half]
        rs = slice(half * ROW_BLK, (half + 1) * ROW_BLK)
        x = x_ref[rs, :]
        gate = _dot(x, wg_ref[0, 0].astype(BF16))
        up = _dot(x, wu_ref[0, 0].astype(BF16))
        hmid = (_silu(gate) * up).astype(BF16)
        o_ref[rs, :] = _dot(hmid, wd_ref[0, 0].astype(BF16)).astype(o_ref.dtype)

    @pl.when(second < nblk_ref[0])
    def _():
        chain(0)
        chain(1)

    @pl.when(second == nblk_ref[0])
    def _():
        chain(0)


def _expert_ffn_call(xs, plan, w_gate, w_up, w_down, layer):
    rows, d = xs.shape
    nblk, blk_expert = plan[-2:]
    live = lambda s, nblk: jnp.minimum(s, lax.shift_right_logical(nblk[0] - 1, 1))
    wmap = lambda half: (lambda s, nblk, be: (
        layer, be[jnp.minimum(2 * live(s, nblk) + half, nblk[0] - 1)], 0, 0))
    wspecs = []
    for half in range(2):
        wspecs += [pl.BlockSpec((1, 1, d, D_EXPERT), wmap(half)),
                   pl.BlockSpec((1, 1, d, D_EXPERT), wmap(half)),
                   pl.BlockSpec((1, 1, D_EXPERT, d), wmap(half))]
    grid_spec = pltpu.PrefetchScalarGridSpec(
        num_scalar_prefetch=2,
        grid=(rows // (2 * ROW_BLK),),
        in_specs=[pl.BlockSpec((2 * ROW_BLK, d), lambda s, nblk, be: (live(s, nblk), 0))] + wspecs,
        out_specs=pl.BlockSpec((2 * ROW_BLK, d), lambda s, nblk, be: (live(s, nblk), 0)),
    )
    return pl.pallas_call(
        _expert_ffn_kernel,
        grid_spec=grid_spec,
        out_shape=jax.ShapeDtypeStruct((rows, d), BF16),
        compiler_params=_cparams("arbitrary"),
        name="moe_expert_ffn",
    )(nblk, blk_expert, xs, w_gate, w_up, w_down, w_gate, w_up, w_down)


def _combine_kernel(pad_ref, lseg_ref, gpos_ref, ltot_ref, npiece_ref, ys_hbm, pw_ref,
                    h16_ref, h32_ref, p_ref, swgu_ref, swd_ref, pwi_ref, pwg_ref, lg_ref, lb_ref,
                    o32_ref, o16_ref, ys_ref, sem):
    i = pl.program_id(0)
    last = pl.num_programs(0) - 1
    buf = lax.rem(i, 2)

    def copy_for(b):
        return lambda l0, g0, rows: pltpu.make_async_copy(
            ys_hbm.at[pl.ds(g0, rows), :], ys_ref.at[b, pl.ds(l0, rows), :], sem.at[b])

    def fetch(tile, b):
        _zero_uncovered_blocks(ys_ref.at[b], ltot_ref[tile])
        _segment_copies(pad_ref, lseg_ref, gpos_ref, tile, copy_for(b))

    @pl.when(i == 0)
    def _():
        fetch(i, buf)

    @pl.when(i < last)
    def _():
        fetch(i + 1, 1 - buf)

    x = h16_ref[...]
    gu = _dot(x, swgu_ref[...])
    shared = _dot((_silu(gu[:, :D_SHARED]) * gu[:, D_SHARED:]).astype(BF16), swd_ref[...])
    ple = _dot(p_ref[...].astype(BF16), pwi_ref[...]) * _sigmoid(_dot(x, pwg_ref[...]))
    z = ALPHA * h32_ref[...] + shared + ple

    _wait_copies(npiece_ref, i, copy_for(buf))
    z += _dot_tn(pw_ref[...], ys_ref[buf])
    y = _layer_norm(z, lg_ref[...], lb_ref[...])
    o32_ref[...] = y
    o16_ref[...] = y.astype(BF16)


def _combine_call(ys, pw, plan, h16, h32, p, sw_gate, sw_up, sw_down, ple_w_in, ple_w_gate,
                  ln_g, ln_b):
    t, d = h32.shape
    nt = t // TM_MOE
    pad, lseg, gpos, ltot, npiece, _, _ = plan
    row = pl.BlockSpec((TM_MOE, d), lambda i, *_: (i, 0))
    vec = pl.BlockSpec((1, d), lambda i, *_: (0, 0))
    full = lambda a, b: pl.BlockSpec((a, b), lambda i, *_: (0, 0), pipeline_mode=pl.Buffered(1))
    swgu = jnp.concatenate([sw_gate, sw_up], axis=1).astype(BF16)
    grid_spec = pltpu.PrefetchScalarGridSpec(
        num_scalar_prefetch=5,
        grid=(nt,),
        in_specs=[pl.BlockSpec(memory_space=pl.ANY),
                  pl.BlockSpec((LOCAL_ROWS, TM_MOE), lambda i, *_: (i, 0)), row, row,
                  pl.BlockSpec((TM_MOE, PLE_DIM), lambda i, *_: (i, 0)),
                  full(d, 2 * D_SHARED), full(D_SHARED, d), full(PLE_DIM, d), full(d, d), vec, vec],
        out_specs=[row, row],
        scratch_shapes=[
            pltpu.VMEM((2, LOCAL_ROWS, d), BF16),
            pltpu.SemaphoreType.DMA((2,)),
        ],
    )
    return pl.pallas_call(
        _combine_kernel,
        grid_spec=grid_spec,
        out_shape=[jax.ShapeDtypeStruct((t, d), F32), jax.ShapeDtypeStruct((t, d), BF16)],
        compiler_params=_cparams("arbitrary"),
        name="moe_combine_tail",
    )(pad, lseg, gpos, ltot, npiece, ys, pw, h16, h32, p, swgu, sw_down.astype(BF16),
      ple_w_in.astype(BF16), ple_w_gate.astype(BF16), ln_g.reshape(1, d), ln_b.reshape(1, d))


def kernel(x, p, ln_in_g, ln_in_b, w_in, gla_fg_w, gla_fg_b, gla_norm_g, sg_norm_g, sg_norm_b, sg_w, sg_b, mla_qn_g, mla_w_uq, mla_kvn_g, mla_w_ukv, w_branch, w_out, ln1_g, ln1_b, router_w, router_b, exp_w_gate, exp_w_up, exp_w_down, sh_w_gate, sh_w_up, sh_w_down, ple_w_in, ple_w_gate, ln2_g, ln2_b):
    batch, seq, d = x.shape
    t = batch * seq
    depth = w_in.shape[0]
    h32, h16 = _ln_call(x.reshape(t, d), ln_in_g, ln_in_b)
    for i in range(depth):
        proj = _proj_call(h16, _prep_w_in(w_in[i]))
        o_a = _gla_call(proj, gla_fg_w[i], gla_fg_b[i], gla_norm_g[i], batch, seq)
        o_b = _sg_call(proj, sg_norm_g[i], sg_norm_b[i], sg_w[i], sg_b[i])
        q, k, vt = _mla_prep_call(proj, mla_qn_g[i], mla_w_uq[i], mla_kvn_g[i], mla_w_ukv[i], seq)
        o_c = _attn_call(q, k, vt, batch, seq)
        h32, h16 = _merge_call(o_a, o_b, o_c, proj, h32, w_branch[i], w_out[i], ln1_g[i], ln1_b[i])
        comb, rank, cnt = _router_call(h32, router_w[i], router_b[i])
        plan = _moe_plan(cnt, t)
        xs, pw = _dispatch_call(h16, comb, rank, plan)
        ys = _expert_ffn_call(xs, plan, exp_w_gate, exp_w_up, exp_w_down, i)
        h32, h16 = _combine_call(ys, pw, plan, h16, h32, p[i].reshape(t, -1), sh_w_gate[i],
                                 sh_w_up[i], sh_w_down[i], ple_w_in[i], ple_w_gate[i],
                                 ln2_g[i], ln2_b[i])
    return h32.reshape(batch, seq, d)
```

```python
import functools

import jax
import jax.numpy as jnp
from jax import lax
from jax.experimental import pallas as pl
from jax.experimental.pallas import tpu as pltpu

F32 = jnp.float32
BF16 = jnp.bfloat16

D_MODEL = 1024
DEPTH = 2
CHUNK = 64
GLA_HEADS, GLA_DK, GLA_DV, GLA_GATE_RANK, GLA_TAU = 4, 128, 256, 16, 16.0
SG_WIDTH, SG_GROUPS, SG_BLOCK = 1024, 4, 128
MLA_HEADS, MLA_Q_RANK, MLA_KV_RANK = 8, 384, 256
MLA_NOPE, MLA_ROPE, MLA_V = 128, 64, 128
MLA_QK = MLA_NOPE + MLA_ROPE
ROPE_THETA = 10000.0
N_BRANCHES = 3
N_EXPERTS, N_GROUPS, TOPK_GROUPS, TOP_K = 64, 8, 4, 8
GROUP_SIZE = N_EXPERTS // N_GROUPS
D_EXPERT, D_SHARED = 256, 256
ROUTED_SCALE = 2.5
PLE_DIM = 256
ALPHA = (2 * DEPTH) ** 0.25

LANES = 128
SUBLANES = 8
VMEM_PHYSICAL_BYTES = 64 * 1024 * 1024
VMEM_LIMIT_BYTES = VMEM_PHYSICAL_BYTES - 4 * 1024 * 1024

P_SMALL = 1024
P_GFR = 384
P_CKV = 512
P_KROPE = 768
P_GATES = 1024
P_SG = 4096
P_GV = 6144
P_GR = 7168
P_GQ = 8192
P_GK = 8704
P_TOTAL = 9216
MLA_HEAD_PAD = 256

TM_LN = 512
TM_PROJ, TN_PROJ = 1024, 1024
R_GLA = 512
TM_SG = 256
TQ_ATT = 512
TK_ATT = 256
TM_MLA = TQ_ATT
MLA_V_EXT = MLA_V + 16
HEADS_ATT = 4
LOG2_E = 1.4426950408889634
TM_MERGE = 512
TM_MOE = 512
SEG_ALIGN_LOG2 = 4
SEG_ALIGN = 1 << SEG_ALIGN_LOG2
SLOT_CHUNK_LOG2 = 6
SLOT_CHUNK = 1 << SLOT_CHUNK_LOG2
SORT_BLK = 512
ROW_BLK = 512
BIG_PIECE_LOG2 = 6
BIG_PIECE = 1 << BIG_PIECE_LOG2
LOCAL_ROWS = -(-(TM_MOE * TOP_K + N_EXPERTS * (SEG_ALIGN - 1) + SLOT_CHUNK) // SORT_BLK) * SORT_BLK


def _cparams(*sem):
    return pltpu.CompilerParams(dimension_semantics=sem, vmem_limit_bytes=VMEM_LIMIT_BYTES)


def _dot(a, b):
    return jnp.dot(a, b, preferred_element_type=F32)


def _dot_nt(a, b):
    return lax.dot_general(a, b, (((1,), (1,)), ((), ())), preferred_element_type=F32)


def _dot_tn(a, b):
    return lax.dot_general(a, b, (((0,), (0,)), ((), ())), preferred_element_type=F32)


def _layer_norm(x, g, b, eps=1e-5):
    mu = jnp.mean(x, axis=-1, keepdims=True)
    xc = x - mu
    var = jnp.mean(xc * xc, axis=-1, keepdims=True)
    return xc * lax.rsqrt(var + eps) * g + b


def _rms_norm(x, g, eps=1e-6):
    return x * lax.rsqrt(jnp.mean(x * x, axis=-1, keepdims=True) + eps) * g


def _sigmoid(x):
    return 1.0 / (1.0 + jnp.exp(-x))


def _silu(x):
    return x * _sigmoid(x)


def _split_bf16(x):
    hi = x.astype(BF16)
    lo = (x - hi.astype(F32)).astype(BF16)
    return hi, lo


def _ln_kernel(x_ref, g_ref, b_ref, o32_ref, o16_ref):
    y = _layer_norm(x_ref[...], g_ref[...], b_ref[...])
    o32_ref[...] = y
    o16_ref[...] = y.astype(BF16)


def _ln_call(x, g, b):
    t, d = x.shape
    row = pl.BlockSpec((TM_LN, d), lambda i: (i, 0))
    vec = pl.BlockSpec((1, d), lambda i: (0, 0))
    return pl.pallas_call(
        _ln_kernel,
        grid=(t // TM_LN,),
        in_specs=[row, vec, vec],
        out_specs=[row, row],
        out_shape=[jax.ShapeDtypeStruct((t, d), F32), jax.ShapeDtypeStruct((t, d), BF16)],
        compiler_params=_cparams("parallel"),
        name="ln_in",
    )(x, g.reshape(1, d), b.reshape(1, d))


def _proj_kernel(x_ref, w_ref, o_ref):
    o_ref[...] = _dot(x_ref[...], w_ref[...]).astype(o_ref.dtype)


def _proj_call(h16, w):
    t, k = h16.shape
    n = w.shape[1]
    return pl.pallas_call(
        _proj_kernel,
        grid=(t // TM_PROJ, n // TN_PROJ),
        in_specs=[pl.BlockSpec((TM_PROJ, k), lambda i, j: (i, 0)),
                  pl.BlockSpec((k, TN_PROJ), lambda i, j: (0, j))],
        out_specs=pl.BlockSpec((TM_PROJ, TN_PROJ), lambda i, j: (i, j)),
        out_shape=jax.ShapeDtypeStruct((t, n), BF16),
        compiler_params=_cparams("parallel", "arbitrary"),
        name="in_proj",
    )(h16, w)


def _prep_w_in(w_in):
    d = w_in.shape[0]
    gq, gk = w_in[:, 0:512], w_in[:, 512:1024]
    gv, gr = w_in[:, 1024:2048], w_in[:, 2048:3072]
    gfr = w_in[:, 3072:3088]
    sgz = w_in[:, 3088:5136]
    mq = w_in[:, 5136:5520]
    ckv = w_in[:, 5520:5776]
    kr = w_in[:, 5776:5840]
    gates = w_in[:, 5840:8912]
    half = MLA_ROPE // 2
    kr_rot = jnp.concatenate([-kr[:, half:], kr[:, :half]], axis=1)
    z = lambda n: jnp.zeros((d, n), w_in.dtype)
    small = jnp.concatenate([mq, gfr, z(112), ckv, kr, kr_rot, z(128)], axis=1)
    return jnp.concatenate([small, gates, sgz, gv, gr, gq, gk], axis=1).astype(BF16)


def _gla_kernel(q_ref, k_ref, v_ref, gr_ref, gf_ref, fgw_ref, fgb_ref, g_ref, tri_ref, blk_ref,
                o_ref, st_ref, qd_ref, ki_ref, ke_ref, dec_ref):
    rows = q_ref.shape[0]

    @pl.when(pl.program_id(1) == 0)
    def _():
        st_ref[...] = jnp.zeros_like(st_ref)

    pre = _dot(gf_ref[...], fgw_ref[...]) + fgb_ref[...]
    glog = -(jnp.maximum(-pre, 0.0) + jnp.log1p(jnp.exp(-jnp.abs(pre)))) * (1.0 / GLA_TAU)

    hi, lo = _split_bf16(glog)
    bcum = _dot(tri_ref[...], hi) + _dot(tri_ref[...], lo)
    btot = _dot(blk_ref[...], hi) + _dot(blk_ref[...], lo)

    q = q_ref[...].astype(F32) * (GLA_DK ** -0.5)
    k = k_ref[...].astype(F32)
    qd_ref[...] = (q * jnp.exp(bcum)).astype(BF16)
    ki_ref[...] = (k * jnp.exp(-bcum)).astype(BF16)
    ke_ref[...] = (k * jnp.exp(btot - bcum)).astype(BF16)
    dec_ref[...] = jnp.exp(btot)

    tr = lax.broadcasted_iota(jnp.int32, (CHUNK, CHUNK), 0)
    tc = lax.broadcasted_iota(jnp.int32, (CHUNK, CHUNK), 1)
    causal = tc <= tr

    def chunk_body(ci, carry):
        r0 = pl.multiple_of(ci * CHUNK, CHUNK)
        rs = pl.ds(r0, CHUNK)
        for hd in range(GLA_HEADS):
            ks = slice(hd * GLA_DK, (hd + 1) * GLA_DK)
            vs = slice(hd * GLA_DV, (hd + 1) * GLA_DV)
            qd = qd_ref[rs, ks]
            ki = ki_ref[rs, ks]
            ke = ke_ref[rs, ks]
            v = v_ref[rs, vs]
            att = jnp.where(causal, _dot_nt(qd, ki), 0.0)
            st = st_ref[hd]
            o = _dot(att.astype(BF16), v) + _dot_nt(qd, st.astype(BF16))
            dec = dec_ref[pl.ds(r0, 1), ks]
            st_ref[hd] = st * dec + _dot_tn(v, ke)
            on = _rms_norm(o, g_ref[:, vs])
            gate = gr_ref[rs, vs].astype(F32)
            o_ref[rs, vs] = (on * _silu(gate)).astype(o_ref.dtype)
        return carry

    lax.fori_loop(0, rows // CHUNK, chunk_body, 0)


def _gla_call(proj, fg_w, fg_b, gla_g, batch, seq):
    t = proj.shape[0]
    nr = seq // R_GLA
    row = lambda i, j: i * nr + j
    fgw = jnp.zeros((LANES, GLA_HEADS * GLA_DK), F32).at[:GLA_GATE_RANK].set(fg_w).astype(BF16)
    hk = GLA_HEADS * GLA_DK
    hv = GLA_HEADS * GLA_DV
    ridx = jnp.arange(R_GLA, dtype=jnp.int32)
    same = (ridx[:, None] // CHUNK) == (ridx[None, :] // CHUNK)
    blk = same.astype(BF16)
    tri = (same & (ridx[None, :] <= ridx[:, None])).astype(BF16)
    return pl.pallas_call(
        _gla_kernel,
        grid=(batch, nr),
        in_specs=[
            pl.BlockSpec((R_GLA, hk), lambda i, j: (row(i, j), P_GQ // hk)),
            pl.BlockSpec((R_GLA, hk), lambda i, j: (row(i, j), P_GK // hk)),
            pl.BlockSpec((R_GLA, hv), lambda i, j: (row(i, j), P_GV // hv)),
            pl.BlockSpec((R_GLA, hv), lambda i, j: (row(i, j), P_GR // hv)),
            pl.BlockSpec((R_GLA, LANES), lambda i, j: (row(i, j), P_GFR // LANES)),
            pl.BlockSpec((LANES, hk), lambda i, j: (0, 0)),
            pl.BlockSpec((1, hk), lambda i, j: (0, 0)),
            pl.BlockSpec((1, hv), lambda i, j: (0, 0)),
            pl.BlockSpec((R_GLA, R_GLA), lambda i, j: (0, 0)),
            pl.BlockSpec((R_GLA, R_GLA), lambda i, j: (0, 0)),
        ],
        out_specs=pl.BlockSpec((R_GLA, hv), lambda i, j: (row(i, j), 0)),
        out_shape=jax.ShapeDtypeStruct((t, hv), BF16),
        scratch_shapes=[
            pltpu.VMEM((GLA_HEADS, GLA_DV, GLA_DK), F32),
            pltpu.VMEM((R_GLA, hk), BF16),
            pltpu.VMEM((R_GLA, hk), BF16),
            pltpu.VMEM((R_GLA, hk), BF16),
            pltpu.VMEM((R_GLA, hk), F32),
        ],
        compiler_params=_cparams("parallel", "arbitrary"),
        name="gla",
    )(proj, proj, proj, proj, proj, fgw, fg_b.reshape(1, hk), gla_g.reshape(1, hv), tri, blk)


def _gelu(x):
    return 0.5 * x * (1.0 + lax.erf(x * (2.0 ** -0.5)))


def _sg_kernel(u_ref, v_ref, g_ref, b_ref, ws_ref, bias_ref, o_ref):
    rows = u_ref.shape[0]
    gw = SG_WIDTH // SG_GROUPS
    u = _gelu(u_ref[...].astype(F32))
    v = _gelu(v_ref[...].astype(F32))
    vn = _layer_norm(v, g_ref[...], b_ref[...]).astype(BF16)
    tr = lax.broadcasted_iota(jnp.int32, (SG_BLOCK, SG_BLOCK), 0)
    tc = lax.broadcasted_iota(jnp.int32, (SG_BLOCK, SG_BLOCK), 1)
    tril = tc <= tr
    for g in range(SG_GROUPS):
        w = jnp.where(tril, ws_ref[g], 0.0).astype(BF16)
        cs = slice(g * gw, (g + 1) * gw)
        for blk in range(rows // SG_BLOCK):
            rs = slice(blk * SG_BLOCK, (blk + 1) * SG_BLOCK)
            sv = _dot(w, vn[rs, cs]) + bias_ref[:, cs]
            o_ref[rs, cs] = (u[rs, cs] * sv).astype(o_ref.dtype)


def _sg_call(proj, sg_g, sg_bn, sg_w, sg_b):
    t = proj.shape[0]
    gw = SG_WIDTH // SG_GROUPS
    bias = jnp.repeat(sg_b.T, gw, axis=1)
    return pl.pallas_call(
        _sg_kernel,
        grid=(t // TM_SG,),
        in_specs=[
            pl.BlockSpec((TM_SG, SG_WIDTH), lambda i: (i, P_SG // SG_WIDTH)),
            pl.BlockSpec((TM_SG, SG_WIDTH), lambda i: (i, P_SG // SG_WIDTH + 1)),
            pl.BlockSpec((1, SG_WIDTH), lambda i: (0, 0)),
            pl.BlockSpec((1, SG_WIDTH), lambda i: (0, 0)),
            pl.BlockSpec((SG_GROUPS, SG_BLOCK, SG_BLOCK), lambda i: (0, 0, 0)),
            pl.BlockSpec((SG_BLOCK, SG_WIDTH), lambda i: (0, 0)),
        ],
        out_specs=pl.BlockSpec((TM_SG, SG_WIDTH), lambda i: (i, 0)),
        out_shape=jax.ShapeDtypeStruct((t, SG_WIDTH), BF16),
        compiler_params=_cparams("parallel"),
        name="spatial_gating",
    )(proj, proj, sg_g.reshape(1, -1), sg_bn.reshape(1, -1), sg_w, bias)


def _mla_prep_kernel(sm_ref, cs_ref, qg_ref, kg_ref, wq_ref, wkv_ref, q_ref, k_ref, vt_ref):
    rows = sm_ref.shape[0]
    mq = sm_ref[:, 0:MLA_Q_RANK].astype(F32)
    qn = _rms_norm(mq, qg_ref[...]).astype(BF16)
    qf = _dot(qn, wq_ref[...])
    ckv = sm_ref[:, P_CKV:P_CKV + MLA_KV_RANK].astype(F32)
    cn = _rms_norm(ckv, kg_ref[...]).astype(BF16)
    kv = _dot(cn, wkv_ref[...])

    cs = cs_ref[...]
    lane = lax.broadcasted_iota(jnp.int32, (rows, LANES), 1)
    low = lane < MLA_ROPE

    def rope(pair):
        t = pair * cs
        return jnp.where(low, t + pltpu.roll(t, MLA_ROPE, 1), 0.0)

    scale = (MLA_QK ** -0.5) * LOG2_E
    kr = rope(sm_ref[:, P_KROPE:P_KROPE + LANES].astype(F32)).astype(BF16)
    for h in range(MLA_HEADS):
        c0 = h * MLA_HEAD_PAD
        q_ref[:, c0:c0 + MLA_NOPE] = (qf[:, c0:c0 + MLA_NOPE] * scale).astype(BF16)
        q_ref[:, c0 + MLA_NOPE:c0 + MLA_HEAD_PAD] = (
            rope(qf[:, c0 + MLA_NOPE:c0 + MLA_HEAD_PAD]) * scale).astype(BF16)
        k_ref[:, c0:c0 + MLA_NOPE] = kv[:, h * MLA_NOPE:(h + 1) * MLA_NOPE].astype(BF16)
        k_ref[:, c0 + MLA_NOPE:c0 + MLA_HEAD_PAD] = kr
    vt = kv[:, MLA_HEADS * MLA_NOPE:].T.astype(BF16)
    ones = jnp.ones((MLA_V_EXT - MLA_V, TK_ATT), BF16)
    for blk in range(rows // TK_ATT):
        ks = slice(blk * TK_ATT, (blk + 1) * TK_ATT)
        for h in range(MLA_HEADS):
            r0 = h * MLA_V_EXT
            vt_ref[blk, r0:r0 + MLA_V, :] = vt[h * MLA_V:(h + 1) * MLA_V, ks]
            vt_ref[blk, r0 + MLA_V:r0 + MLA_V_EXT, :] = ones


def _mla_prep_call(proj, qn_g, w_uq, kvn_g, w_ukv, seq):
    t = proj.shape[0]
    half = MLA_ROPE // 2
    wq = w_uq.reshape(MLA_Q_RANK, MLA_HEADS, MLA_QK)
    wr = wq[:, :, MLA_NOPE:]
    wr_rot = jnp.concatenate([-wr[:, :, half:], wr[:, :, :half]], axis=2)
    wq_ext = jnp.concatenate([wq[:, :, :MLA_NOPE], wr, wr_rot], axis=2)
    wq_ext = wq_ext.reshape(MLA_Q_RANK, MLA_HEADS * MLA_HEAD_PAD).astype(BF16)
    wkv = w_ukv.reshape(MLA_KV_RANK, MLA_HEADS, MLA_NOPE + MLA_V)
    wkv = jnp.concatenate([wkv[:, :, :MLA_NOPE].reshape(MLA_KV_RANK, -1),
                           wkv[:, :, MLA_NOPE:].reshape(MLA_KV_RANK, -1)], axis=1).astype(BF16)
    inv = ROPE_THETA ** (-jnp.arange(half, dtype=F32) / half)
    ang = jnp.arange(seq, dtype=F32)[:, None] * inv
    cs = jnp.concatenate([jnp.cos(ang), jnp.cos(ang), jnp.sin(ang), jnp.sin(ang)], axis=1)
    ns = seq // TM_MLA
    hq = MLA_HEADS * MLA_HEAD_PAD
    hv = MLA_HEADS * MLA_V_EXT
    return pl.pallas_call(
        _mla_prep_kernel,
        grid=(t // TM_MLA,),
        in_specs=[
            pl.BlockSpec((TM_MLA, P_SMALL), lambda i: (i, 0)),
            pl.BlockSpec((TM_MLA, LANES), lambda i: (i % ns, 0)),
            pl.BlockSpec((1, MLA_Q_RANK), lambda i: (0, 0)),
            pl.BlockSpec((1, MLA_KV_RANK), lambda i: (0, 0)),
            pl.BlockSpec((MLA_Q_RANK, hq), lambda i: (0, 0)),
            pl.BlockSpec((MLA_KV_RANK, MLA_HEADS * (MLA_NOPE + MLA_V)), lambda i: (0, 0)),
        ],
        out_specs=[
            pl.BlockSpec((TM_MLA, hq), lambda i: (i, 0)),
            pl.BlockSpec((TM_MLA, hq), lambda i: (i, 0)),
            pl.BlockSpec((TM_MLA // TK_ATT, hv, TK_ATT), lambda i: (i, 0, 0)),
        ],
        out_shape=[jax.ShapeDtypeStruct((t, hq), BF16), jax.ShapeDtypeStruct((t, hq), BF16),
                   jax.ShapeDtypeStruct((t // TK_ATT, hv, TK_ATT), BF16)],
        compiler_params=_cparams("parallel"),
        name="mla_prep",
    )(proj, cs, qn_g.reshape(1, -1), kvn_g.reshape(1, -1), wq_ext, wkv)


def _attn_kernel(q_ref, k_ref, vt_ref, o_ref, acc_ref, sa_ref, sb_ref):
    tq = q_ref.shape[0]
    i = pl.program_id(2)
    acc_ref[...] = jnp.zeros_like(acc_ref)

    def scores(j, s_ref):
        r0 = pl.multiple_of(j * TK_ATT, TK_ATT)
        for hh in range(HEADS_ATT):
            qs = slice(hh * MLA_HEAD_PAD, (hh + 1) * MLA_HEAD_PAD)
            s_ref[hh] = _dot_nt(k_ref[pl.ds(r0, TK_ATT), qs], q_ref[:, qs])

    def consume(j, s_ref, ms, mask):
        new_m = []
        for hh in range(HEADS_ATT):
            s = s_ref[hh]
            if mask is not None:
                s = jnp.where(mask, s, -jnp.inf)
            m_new = jnp.maximum(ms[hh], jnp.max(s, axis=0, keepdims=True))
            alpha = jnp.exp2(ms[hh] - m_new)
            p = jnp.exp2(s - m_new).astype(BF16)
            vt = vt_ref[j, hh * MLA_V_EXT:(hh + 1) * MLA_V_EXT, :]
            acc_ref[hh] = alpha * acc_ref[hh] + _dot(vt, p)
            new_m.append(m_new)
        return tuple(new_m)

    def pair(jj, ms):
        j = 2 * jj
        scores(j + 1, sb_ref)
        ms = consume(j, sa_ref, ms, None)
        scores(j + 2, sa_ref)
        return consume(j + 1, sb_ref, ms, None)

    scores(0, sa_ref)
    m0 = tuple(jnp.full((1, tq), -jnp.inf, F32) for _ in range(HEADS_ATT))
    ms = lax.fori_loop(0, i * (tq // (2 * TK_ATT)), pair, m0)
    jd = i * (tq // TK_ATT)
    scores(jd + 1, sb_ref)
    kk = lax.shift_right_logical(lax.broadcasted_iota(jnp.int32, (TK_ATT, tq), 0), 6)
    qq = lax.shift_right_logical(lax.broadcasted_iota(jnp.int32, (TK_ATT, tq), 1), 6)
    ms = consume(jd, sa_ref, ms, kk <= qq)
    ms = consume(jd + 1, sb_ref, ms, kk + (TK_ATT // CHUNK) <= qq)
    for hh in range(HEADS_ATT):
        acc = acc_ref[hh]
        o = acc[:MLA_V, :] / acc[MLA_V:MLA_V + 1, :]
        o_ref[:, hh * MLA_V:(hh + 1) * MLA_V] = o.T.astype(o_ref.dtype)


def _attn_call(q, k, vt, batch, seq):
    t = q.shape[0]
    nq = seq // TQ_ATT
    qw = HEADS_ATT * MLA_HEAD_PAD
    vw = HEADS_ATT * MLA_V_EXT
    s_buf = pltpu.VMEM((HEADS_ATT, TK_ATT, TQ_ATT), F32)
    return pl.pallas_call(
        _attn_kernel,
        grid=(batch, MLA_HEADS // HEADS_ATT, nq),
        in_specs=[
            pl.BlockSpec((TQ_ATT, qw), lambda b, h, i: (b * nq + i, h)),
            pl.BlockSpec((seq, qw), lambda b, h, i: (b, h)),
            pl.BlockSpec((seq // TK_ATT, vw, TK_ATT), lambda b, h, i: (b, h, 0)),
        ],
        out_specs=pl.BlockSpec((TQ_ATT, HEADS_ATT * MLA_V), lambda b, h, i: (b * nq + i, h)),
        out_shape=jax.ShapeDtypeStruct((t, MLA_HEADS * MLA_V), BF16),
        scratch_shapes=[pltpu.VMEM((HEADS_ATT, MLA_V_EXT, TQ_ATT), F32), s_buf, s_buf],
        compiler_params=_cparams("parallel", "parallel", "arbitrary"),
        name="mla_attention",
    )(q, k, vt)


def _merge_kernel(oa_ref, ob_ref, oc_ref, g0_ref, g1_ref, g2_ref, h_ref, wb_ref, wo_ref,
                  lg_ref, lb_ref, o32_ref, o16_ref):
    merged = _sigmoid(g0_ref[...].astype(F32)) * _dot(oa_ref[...], wb_ref[0])
    merged += _sigmoid(g1_ref[...].astype(F32)) * _dot(ob_ref[...], wb_ref[1])
    merged += _sigmoid(g2_ref[...].astype(F32)) * _dot(oc_ref[...], wb_ref[2])
    mix = _dot(merged.astype(BF16), wo_ref[...])
    y = _layer_norm(ALPHA * h_ref[...] + mix, lg_ref[...], lb_ref[...])
    o32_ref[...] = y
    o16_ref[...] = y.astype(BF16)


def _merge_call(o_a, o_b, o_c, proj, h32, w_branch, w_out, ln_g, ln_b):
    t, d = h32.shape
    row = pl.BlockSpec((TM_MERGE, d), lambda i: (i, 0))
    gate = lambda n: pl.BlockSpec((TM_MERGE, d), lambda i: (i, P_GATES // d + n))
    vec = pl.BlockSpec((1, d), lambda i: (0, 0))
    return pl.pallas_call(
        _merge_kernel,
        grid=(t // TM_MERGE,),
        in_specs=[row, row, row, gate(0), gate(1), gate(2), row,
                  pl.BlockSpec((N_BRANCHES, d, d), lambda i: (0, 0, 0)),
                  pl.BlockSpec((d, d), lambda i: (0, 0)), vec, vec],
        out_specs=[row, row],
        out_shape=[jax.ShapeDtypeStruct((t, d), F32), jax.ShapeDtypeStruct((t, d), BF16)],
        compiler_params=_cparams("parallel"),
        name="merge",
    )(o_a, o_b, o_c, proj, proj, proj, h32, w_branch.astype(BF16), w_out.astype(BF16),
      ln_g.reshape(1, d), ln_b.reshape(1, d))


def _first_argmax_mask(vals, iota, n):
    m = jnp.max(vals, axis=0, keepdims=True)
    idx = jnp.min(jnp.where(vals == m, iota, n), axis=0, keepdims=True)
    return iota == idx


def _router_kernel(h_ref, wt_ref, b_ref, comb_ref, rank_ref, cnt_ref):
    tm = h_ref.shape[0]
    h = h_ref[...]
    h_hi, h_lo = _split_bf16(h)
    w = wt_ref[...]
    w_hi, w_lo = _split_bf16(w)
    logits = _dot_nt(w_hi, h_hi) + _dot_nt(w_hi, h_lo) + _dot_nt(w_lo, h_hi)
    scores = _sigmoid(logits)
    biased = scores + b_ref[...]

    neg = -jnp.inf
    sub = lax.broadcasted_iota(jnp.int32, (GROUP_SIZE, tm), 0)
    grp_rows = []
    for g in range(N_GROUPS):
        blk = biased[g * GROUP_SIZE:(g + 1) * GROUP_SIZE, :]
        m1 = jnp.max(blk, axis=0, keepdims=True)
        first = _first_argmax_mask(blk, sub, GROUP_SIZE)
        m2 = jnp.max(jnp.where(first, neg, blk), axis=0, keepdims=True)
        grp_rows.append(m1 + m2)
    gs = jnp.concatenate(grp_rows, axis=0)
    gsel = jnp.zeros((N_GROUPS, tm), jnp.bool_)
    gi = lax.broadcasted_iota(jnp.int32, (N_GROUPS, tm), 0)
    for _ in range(TOPK_GROUPS):
        pick = _first_argmax_mask(gs, gi, N_GROUPS)
        gsel = gsel | pick
        gs = jnp.where(pick, neg, gs)
    emask = jnp.concatenate(
        [jnp.broadcast_to(gsel[g:g + 1, :], (GROUP_SIZE, tm)) for g in range(N_GROUPS)], axis=0)
    cand = jnp.where(emask, biased, neg)
    ei = lax.broadcasted_iota(jnp.int32, (N_EXPERTS, tm), 0)
    chosen = jnp.zeros((N_EXPERTS, tm), jnp.bool_)
    for _ in range(TOP_K):
        pick = _first_argmax_mask(cand, ei, N_EXPERTS)
        chosen = chosen | pick
        cand = jnp.where(pick, neg, cand)
    wsel = jnp.where(chosen, scores, 0.0)
    comb_ref[...] = wsel / jnp.sum(wsel, axis=0, keepdims=True) * ROUTED_SCALE

    r = lax.broadcasted_iota(jnp.int32, (tm, tm), 0)
    c = lax.broadcasted_iota(jnp.int32, (tm, tm), 1)
    upper = jnp.where(r < c, 1.0, 0.0).astype(BF16)
    sel = jnp.where(chosen, 1.0, 0.0)
    rank_ref[...] = _dot(sel.astype(BF16), upper)
    cnt = jnp.sum(sel, axis=1, keepdims=True)
    cnt_ref[...] = jnp.broadcast_to(cnt, (N_EXPERTS, LANES)).astype(jnp.int32)


def _router_call(h32, router_w, router_b):
    t, d = h32.shape
    nt = t // TM_MOE
    bias = jnp.broadcast_to(router_b.reshape(N_EXPERTS, 1), (N_EXPERTS, TM_MOE))
    comb, rank, cnt = pl.pallas_call(
        _router_kernel,
        grid=(nt,),
        in_specs=[pl.BlockSpec((TM_MOE, d), lambda i: (i, 0)),
                  pl.BlockSpec((N_EXPERTS, d), lambda i: (0, 0)),
                  pl.BlockSpec((N_EXPERTS, TM_MOE), lambda i: (0, 0))],
        out_specs=[pl.BlockSpec((N_EXPERTS, TM_MOE), lambda i: (0, i)),
                   pl.BlockSpec((N_EXPERTS, TM_MOE), lambda i: (0, i)),
                   pl.BlockSpec((N_EXPERTS, LANES), lambda i: (i, 0))],
        out_shape=[jax.ShapeDtypeStruct((N_EXPERTS, t), F32),
                   jax.ShapeDtypeStruct((N_EXPERTS, t), F32),
                   jax.ShapeDtypeStruct((nt * N_EXPERTS, LANES), jnp.int32)],
        compiler_params=_cparams("parallel"),
        name="router",
    )(h32, router_w.T, bias)
    return comb, rank, cnt[:, 0].reshape(nt, N_EXPERTS)


def _moe_plan(cnt, t):
    nt = cnt.shape[0]
    pad = (cnt + SEG_ALIGN - 1) // SEG_ALIGN * SEG_ALIGN
    lseg = jnp.cumsum(pad, axis=1) - pad
    ltot = jnp.sum(pad, axis=1)
    etot = jnp.sum(pad, axis=0)
    region = (etot + ROW_BLK - 1) // ROW_BLK * ROW_BLK
    rend = jnp.cumsum(region)
    gpos = (rend - region)[None, :] + jnp.cumsum(pad, axis=0) - pad
    nblk = (rend[-1] // ROW_BLK).astype(jnp.int32).reshape(1)
    blk_start = jnp.arange(_moe_blocks(t), dtype=jnp.int32) * ROW_BLK
    blk_expert = jnp.minimum(jnp.sum(rend[None, :] <= blk_start[:, None], axis=1), N_EXPERTS - 1)
    nbig = pad // BIG_PIECE
    nsmall = (pad % BIG_PIECE) // SEG_ALIGN
    npiece = jnp.stack([jnp.sum(nbig, axis=1), jnp.sum(nsmall, axis=1)], axis=1)

    def piece_list(count, max_pieces, piece_rows, first_row):
        run = jnp.cumsum(count, axis=1)
        j = jnp.arange(max_pieces, dtype=jnp.int32)[None, :, None]
        owner = jnp.sum(run[:, None, :] <= j, axis=2)
        onehot = owner[:, :, None] == jnp.arange(N_EXPERTS, dtype=jnp.int32)[None, None, :]
        pick = lambda a: jnp.sum(jnp.where(onehot, a[:, None, :], 0), axis=2)
        within = (j[:, :, 0] - pick(run - count)) * piece_rows
        return pick(lseg + first_row) + within, pick(gpos + first_row) + within

    max_big = (TM_MOE * TOP_K + N_EXPERTS * (SEG_ALIGN - 1)) // BIG_PIECE
    max_small = N_EXPERTS * (BIG_PIECE // SEG_ALIGN - 1)
    bsrc, bdst = piece_list(nbig, max_big, BIG_PIECE, jnp.zeros_like(pad))
    ssrc, sdst = piece_list(nsmall, max_small, SEG_ALIGN, nbig * BIG_PIECE)
    i32 = lambda a: a.astype(jnp.int32)
    pieces = (i32(npiece), i32(bsrc), i32(bdst), i32(ssrc), i32(sdst))
    return i32(pad), i32(lseg), i32(ltot), pieces, nblk, i32(blk_expert)


def _moe_blocks(t):
    nt = t // TM_MOE
    rows = t * TOP_K + nt * N_EXPERTS * (SEG_ALIGN - 1) + N_EXPERTS * (ROW_BLK - 1)
    return -(-rows // (2 * ROW_BLK)) * 2


def _zero_uncovered_blocks(ref2d, ntot):
    for cb in range(LOCAL_ROWS // SORT_BLK):
        @pl.when((cb + 1) * SORT_BLK > ntot)
        def _():
            ref2d[cb * SORT_BLK:(cb + 1) * SORT_BLK, :] = jnp.zeros((SORT_BLK, ref2d.shape[1]),
                                                                   ref2d.dtype)


def _build_slot_matrices(p_ref, pw_ref, comb_ref, rank_ref, pad_ref, lseg_ref, ntot, i):
    tm = p_ref.shape[1]
    _zero_uncovered_blocks(p_ref, ntot)
    _zero_uncovered_blocks(pw_ref, ntot)
    rowi = lax.broadcasted_iota(jnp.int32, (SLOT_CHUNK, tm), 0).astype(F32)

    def expert_body(e, carry):
        rrow = rank_ref[pl.ds(e, 1), :]
        wrow = comb_ref[pl.ds(e, 1), :]
        base = lseg_ref[i, e]
        nch = lax.shift_right_logical(pad_ref[i, e] + (SLOT_CHUNK - 1), SLOT_CHUNK_LOG2)

        def chunk_body(c, carry2):
            off = c * SLOT_CHUNK
            hit = (rrow == rowi + off.astype(F32)) & (wrow > 0.0)
            rs = pl.ds(pl.multiple_of(base + off, SEG_ALIGN), SLOT_CHUNK)
            p_ref[rs, :] = jnp.where(hit, 1.0, 0.0).astype(BF16)
            pw_ref[rs, :] = jnp.where(hit, wrow, 0.0).astype(BF16)
            return carry2

        lax.fori_loop(0, nch, chunk_body, 0)
        return carry

    lax.fori_loop(0, N_EXPERTS, expert_body, 0, unroll=2)


def _segment_copies(piece_refs, tile, make_copy):
    npiece_ref, bsrc_ref, bdst_ref, ssrc_ref, sdst_ref = piece_refs
    for col, rows, src_ref, dst_ref in ((0, BIG_PIECE, bsrc_ref, bdst_ref),
                                        (1, SEG_ALIGN, ssrc_ref, sdst_ref)):
        def piece(j, carry, rows=rows, src_ref=src_ref, dst_ref=dst_ref):
            make_copy(pl.multiple_of(src_ref[tile, j], SEG_ALIGN),
                      pl.multiple_of(dst_ref[tile, j], SEG_ALIGN), rows).start()
            return carry

        lax.fori_loop(0, npiece_ref[tile, col], piece, 0)


def _wait_copies(piece_refs, tile, make_copy):
    npiece_ref = piece_refs[0]
    for col, rows in ((0, BIG_PIECE), (1, SEG_ALIGN)):
        def piece(c, carry, rows=rows):
            make_copy(0, 0, rows).wait()
            return carry

        lax.fori_loop(0, npiece_ref[tile, col], piece, 0)


def _dispatch_kernel(pad_ref, lseg_ref, ltot_ref, npiece_ref, bsrc_ref, bdst_ref, ssrc_ref, sdst_ref,
                     x_ref, comb_ref, rank_ref, xs_hbm, pw_ref, p_ref, xs_ref, sem):
    piece_refs = (npiece_ref, bsrc_ref, bdst_ref, ssrc_ref, sdst_ref)
    i = pl.program_id(0)
    last = pl.num_programs(0) - 1
    buf = lax.rem(i, 2)
    ntot = ltot_ref[i]
    _build_slot_matrices(p_ref, pw_ref, comb_ref, rank_ref, pad_ref, lseg_ref, ntot, i)
    for cb in range(LOCAL_ROWS // SORT_BLK):
        @pl.when(cb * SORT_BLK < ntot)
        def _():
            rs = slice(cb * SORT_BLK, (cb + 1) * SORT_BLK)
            xs_ref[buf, rs, :] = _dot(p_ref[rs, :], x_ref[...]).astype(BF16)

    def copy_for(b):
        return lambda l0, g0, rows: pltpu.make_async_copy(
            xs_ref.at[b, pl.ds(l0, rows), :], xs_hbm.at[pl.ds(g0, rows), :], sem.at[b])

    _segment_copies(piece_refs, i, copy_for(buf))

    @pl.when(i > 0)
    def _():
        _wait_copies(piece_refs, i - 1, copy_for(1 - buf))

    @pl.when(i == last)
    def _():
        _wait_copies(piece_refs, i, copy_for(buf))


def _dispatch_call(h16, comb, rank, plan):
    t, d = h16.shape
    nt = t // TM_MOE
    pad, lseg, ltot, pieces, _, _ = plan
    grid_spec = pltpu.PrefetchScalarGridSpec(
        num_scalar_prefetch=3 + len(pieces),
        grid=(nt,),
        in_specs=[
            pl.BlockSpec((TM_MOE, d), lambda i, *_: (i, 0)),
            pl.BlockSpec((N_EXPERTS, TM_MOE), lambda i, *_: (0, i)),
            pl.BlockSpec((N_EXPERTS, TM_MOE), lambda i, *_: (0, i)),
        ],
        out_specs=[pl.BlockSpec(memory_space=pl.ANY),
                   pl.BlockSpec((LOCAL_ROWS, TM_MOE), lambda i, *_: (i, 0))],
        scratch_shapes=[
            pltpu.VMEM((LOCAL_ROWS, TM_MOE), BF16),
            pltpu.VMEM((2, LOCAL_ROWS, d), BF16),
            pltpu.SemaphoreType.DMA((2,)),
        ],
    )
    return pl.pallas_call(
        _dispatch_kernel,
        grid_spec=grid_spec,
        out_shape=[jax.ShapeDtypeStruct((_moe_blocks(t) * ROW_BLK, d), BF16),
                   jax.ShapeDtypeStruct((nt * LOCAL_ROWS, TM_MOE), BF16)],
        compiler_params=_cparams("arbitrary"),
        name="moe_dispatch",
    )(pad, lseg, ltot, *pieces, h16, comb, rank)


def _expert_ffn_kernel(nblk_ref, be_ref, x_ref, wg0_ref, wu0_ref, wd0_ref, wg1_ref, wu1_ref, wd1_ref,
                       o_ref):
    second = 2 * pl.program_id(0) + 1
    weights = ((wg0_ref, wu0_ref, wd0_ref), (wg1_ref, wu1_ref, wd1_ref))

    def chain(half):
        wg_ref, wu_ref, wd_ref = weights[half]
        rs = slice(half * ROW_BLK, (half + 1) * ROW_BLK)
        x = x_ref[rs, :]
        gate = _dot(x, wg_ref[0, 0].astype(BF16))
        up = _dot(x, wu_ref[0, 0].astype(BF16))
        hmid = (_silu(gate) * up).astype(BF16)
        o_ref[rs, :] = _dot(hmid, wd_ref[0, 0].astype(BF16)).astype(o_ref.dtype)

    @pl.when(second < nblk_ref[0])
    def _():
        chain(0)
        chain(1)

    @pl.when(second == nblk_ref[0])
    def _():
        chain(0)


def _expert_ffn_call(xs, plan, w_gate, w_up, w_down, layer):
    rows, d = xs.shape
    nblk, blk_expert = plan[-2:]
    live = lambda s, nblk: jnp.minimum(s, lax.shift_right_logical(nblk[0] - 1, 1))
    wmap = lambda half: (lambda s, nblk, be: (
        layer, be[jnp.minimum(2 * live(s, nblk) + half, nblk[0] - 1)], 0, 0))
    wspecs = []
    for half in range(2):
        wspecs += [pl.BlockSpec((1, 1, d, D_EXPERT), wmap(half)),
                   pl.BlockSpec((1, 1, d, D_EXPERT), wmap(half)),
                   pl.BlockSpec((1, 1, D_EXPERT, d), wmap(half))]
    grid_spec = pltpu.PrefetchScalarGridSpec(
        num_scalar_prefetch=2,
        grid=(rows // (2 * ROW_BLK),),
        in_specs=[pl.BlockSpec((2 * ROW_BLK, d), lambda s, nblk, be: (live(s, nblk), 0))] + wspecs,
        out_specs=pl.BlockSpec((2 * ROW_BLK, d), lambda s, nblk, be: (live(s, nblk), 0)),
    )
    return pl.pallas_call(
        _expert_ffn_kernel,
        grid_spec=grid_spec,
        out_shape=jax.ShapeDtypeStruct((rows, d), BF16),
        compiler_params=_cparams("arbitrary"),
        name="moe_expert_ffn",
    )(nblk, blk_expert, xs, w_gate, w_up, w_down, w_gate, w_up, w_down)


def _combine_kernel(ltot_ref, npiece_ref, bsrc_ref, bdst_ref, ssrc_ref, sdst_ref, ys_hbm, pw_ref,
                    h16_ref, h32_ref, p_ref, swgu_ref, swd_ref, pwi_ref, pwg_ref, lg_ref, lb_ref,
                    o32_ref, o16_ref, ys_ref, sem):
    piece_refs = (npiece_ref, bsrc_ref, bdst_ref, ssrc_ref, sdst_ref)
    i = pl.program_id(0)
    last = pl.num_programs(0) - 1
    buf = lax.rem(i, 2)

    def copy_for(b):
        return lambda l0, g0, rows: pltpu.make_async_copy(
            ys_hbm.at[pl.ds(g0, rows), :], ys_ref.at[b, pl.ds(l0, rows), :], sem.at[b])

    def fetch(tile, b):
        _zero_uncovered_blocks(ys_ref.at[b], ltot_ref[tile])
        _segment_copies(piece_refs, tile, copy_for(b))

    @pl.when(i == 0)
    def _():
        fetch(i, buf)

    @pl.when(i < last)
    def _():
        fetch(i + 1, 1 - buf)

    x = h16_ref[...]
    gu = _dot(x, swgu_ref[...])
    shared = _dot((_silu(gu[:, :D_SHARED]) * gu[:, D_SHARED:]).astype(BF16), swd_ref[...])
    ple = _dot(p_ref[...].astype(BF16), pwi_ref[...]) * _sigmoid(_dot(x, pwg_ref[...]))
    z = ALPHA * h32_ref[...] + shared + ple

    _wait_copies(piece_refs, i, copy_for(buf))
    z += _dot_tn(pw_ref[...], ys_ref[buf])
    y = _layer_norm(z, lg_ref[...], lb_ref[...])
    o32_ref[...] = y
    o16_ref[...] = y.astype(BF16)


def _combine_call(ys, pw, plan, h16, h32, p, sw_gate, sw_up, sw_down, ple_w_in, ple_w_gate,
                  ln_g, ln_b):
    t, d = h32.shape
    nt = t // TM_MOE
    _, _, ltot, pieces, _, _ = plan
    row = pl.BlockSpec((TM_MOE, d), lambda i, *_: (i, 0))
    vec = pl.BlockSpec((1, d), lambda i, *_: (0, 0))
    full = lambda a, b: pl.BlockSpec((a, b), lambda i, *_: (0, 0), pipeline_mode=pl.Buffered(1))
    swgu = jnp.concatenate([sw_gate, sw_up], axis=1).astype(BF16)
    grid_spec = pltpu.PrefetchScalarGridSpec(
        num_scalar_prefetch=1 + len(pieces),
        grid=(nt,),
        in_specs=[pl.BlockSpec(memory_space=pl.ANY),
                  pl.BlockSpec((LOCAL_ROWS, TM_MOE), lambda i, *_: (i, 0)), row, row,
                  pl.BlockSpec((TM_MOE, PLE_DIM), lambda i, *_: (i, 0)),
                  full(d, 2 * D_SHARED), full(D_SHARED, d), full(PLE_DIM, d), full(d, d), vec, vec],
        out_specs=[row, row],
        scratch_shapes=[
            pltpu.VMEM((2, LOCAL_ROWS, d), BF16),
            pltpu.SemaphoreType.DMA((2,)),
        ],
    )
    return pl.pallas_call(
        _combine_kernel,
        grid_spec=grid_spec,
        out_shape=[jax.ShapeDtypeStruct((t, d), F32), jax.ShapeDtypeStruct((t, d), BF16)],
        compiler_params=_cparams("arbitrary"),
        name="moe_combine_tail",
    )(ltot, *pieces, ys, pw, h16, h32, p, swgu, sw_down.astype(BF16),
      ple_w_in.astype(BF16), ple_w_gate.astype(BF16), ln_g.reshape(1, d), ln_b.reshape(1, d))


def kernel(x, p, ln_in_g, ln_in_b, w_in, gla_fg_w, gla_fg_b, gla_norm_g, sg_norm_g, sg_norm_b, sg_w, sg_b, mla_qn_g, mla_w_uq, mla_kvn_g, mla_w_ukv, w_branch, w_out, ln1_g, ln1_b, router_w, router_b, exp_w_gate, exp_w_up, exp_w_down, sh_w_gate, sh_w_up, sh_w_down, ple_w_in, ple_w_gate, ln2_g, ln2_b):
    batch, seq, d = x.shape
    t = batch * seq
    depth = w_in.shape[0]
    h32, h16 = _ln_call(x.reshape(t, d), ln_in_g, ln_in_b)
    for i in range(depth):
        proj = _proj_call(h16, _prep_w_in(w_in[i]))
        o_a = _gla_call(proj, gla_fg_w[i], gla_fg_b[i], gla_norm_g[i], batch, seq)
        o_b = _sg_call(proj, sg_norm_g[i], sg_norm_b[i], sg_w[i], sg_b[i])
        q, k, vt = _mla_prep_call(proj, mla_qn_g[i], mla_w_uq[i], mla_kvn_g[i], mla_w_ukv[i], seq)
        o_c = _attn_call(q, k, vt, batch, seq)
        h32, h16 = _merge_call(o_a, o_b, o_c, proj, h32, w_branch[i], w_out[i], ln1_g[i], ln1_b[i])
        comb, rank, cnt = _router_call(h32, router_w[i], router_b[i])
        plan = _moe_plan(cnt, t)
        xs, pw = _dispatch_call(h16, comb, rank, plan)
        ys = _expert_ffn_call(xs, plan, exp_w_gate, exp_w_up, exp_w_down, i)
        h32, h16 = _combine_call(ys, pw, plan, h16, h32, p[i].reshape(t, -1), sh_w_gate[i],
                                 sh_w_up[i], sh_w_down[i], ple_w_in[i], ple_w_gate[i],
                                 ln2_g[i], ln2_b[i])
    return h32.reshape(batch, seq, d)
```

```python
import functools

import jax
import jax.numpy as jnp
from jax import lax
from jax.experimental import pallas as pl
from jax.experimental.pallas import tpu as pltpu

F32 = jnp.float32
BF16 = jnp.bfloat16

D_MODEL = 1024
DEPTH = 2
CHUNK = 64
GLA_HEADS, GLA_DK, GLA_DV, GLA_GATE_RANK, GLA_TAU = 4, 128, 256, 16, 16.0
SG_WIDTH, SG_GROUPS, SG_BLOCK = 1024, 4, 128
MLA_HEADS, MLA_Q_RANK, MLA_KV_RANK = 8, 384, 256
MLA_NOPE, MLA_ROPE, MLA_V = 128, 64, 128
MLA_QK = MLA_NOPE + MLA_ROPE
ROPE_THETA = 10000.0
N_BRANCHES = 3
N_EXPERTS, N_GROUPS, TOPK_GROUPS, TOP_K = 64, 8, 4, 8
GROUP_SIZE = N_EXPERTS // N_GROUPS
D_EXPERT, D_SHARED = 256, 256
ROUTED_SCALE = 2.5
PLE_DIM = 256
ALPHA = (2 * DEPTH) ** 0.25

LANES = 128
SUBLANES = 8
VMEM_PHYSICAL_BYTES = 64 * 1024 * 1024
VMEM_LIMIT_BYTES = VMEM_PHYSICAL_BYTES - 4 * 1024 * 1024

P_SMALL = 1024
P_GFR = 384
P_CKV = 512
P_KROPE = 768
P_GATES = 1024
P_SG = 4096
P_GV = 6144
P_GR = 7168
P_GQ = 8192
P_GK = 8704
P_TOTAL = 9216
MLA_HEAD_PAD = 256

TM_LN = 512
TM_PROJ, TN_PROJ = 1024, 1536
R_GLA = 512
TM_SG = 256
TQ_ATT = 512
TK_ATT = 256
TM_MLA = TQ_ATT
MLA_V_EXT = MLA_V + 16
HEADS_ATT = 4
LOG2_E = 1.4426950408889634
TM_MERGE = 512
TM_MOE = 512
SEG_ALIGN_LOG2 = 4
SEG_ALIGN = 1 << SEG_ALIGN_LOG2
SLOT_CHUNK_LOG2 = 6
SLOT_CHUNK = 1 << SLOT_CHUNK_LOG2
SORT_BLK = 512
ROW_BLK = 512
BIG_PIECE_LOG2 = 6
BIG_PIECE = 1 << BIG_PIECE_LOG2
LOCAL_ROWS = -(-(TM_MOE * TOP_K + N_EXPERTS * (SEG_ALIGN - 1) + SLOT_CHUNK) // SORT_BLK) * SORT_BLK


def _cparams(*sem):
    return pltpu.CompilerParams(dimension_semantics=sem, vmem_limit_bytes=VMEM_LIMIT_BYTES)


def _dot(a, b):
    return jnp.dot(a, b, preferred_element_type=F32)


def _dot_nt(a, b):
    return lax.dot_general(a, b, (((1,), (1,)), ((), ())), preferred_element_type=F32)


def _dot_tn(a, b):
    return lax.dot_general(a, b, (((0,), (0,)), ((), ())), preferred_element_type=F32)


def _layer_norm(x, g, b, eps=1e-5):
    mu = jnp.mean(x, axis=-1, keepdims=True)
    xc = x - mu
    var = jnp.mean(xc * xc, axis=-1, keepdims=True)
    return xc * lax.rsqrt(var + eps) * g + b


def _rms_norm(x, g, eps=1e-6):
    return x * lax.rsqrt(jnp.mean(x * x, axis=-1, keepdims=True) + eps) * g


def _sigmoid(x):
    return 1.0 / (1.0 + jnp.exp(-x))


def _silu(x):
    return x * _sigmoid(x)


def _split_bf16(x):
    hi = x.astype(BF16)
    lo = (x - hi.astype(F32)).astype(BF16)
    return hi, lo


def _ln_kernel(x_ref, g_ref, b_ref, o32_ref, o16_ref):
    y = _layer_norm(x_ref[...], g_ref[...], b_ref[...])
    o32_ref[...] = y
    o16_ref[...] = y.astype(BF16)


def _ln_call(x, g, b):
    t, d = x.shape
    row = pl.BlockSpec((TM_LN, d), lambda i: (i, 0))
    vec = pl.BlockSpec((1, d), lambda i: (0, 0))
    return pl.pallas_call(
        _ln_kernel,
        grid=(t // TM_LN,),
        in_specs=[row, vec, vec],
        out_specs=[row, row],
        out_shape=[jax.ShapeDtypeStruct((t, d), F32), jax.ShapeDtypeStruct((t, d), BF16)],
        compiler_params=_cparams("parallel"),
        name="ln_in",
    )(x, g.reshape(1, d), b.reshape(1, d))


def _proj_kernel(x_ref, w_ref, o_ref):
    o_ref[...] = _dot(x_ref[...], w_ref[...]).astype(o_ref.dtype)


def _proj_call(h16, w):
    t, k = h16.shape
    n = w.shape[1]
    return pl.pallas_call(
        _proj_kernel,
        grid=(t // TM_PROJ, n // TN_PROJ),
        in_specs=[pl.BlockSpec((TM_PROJ, k), lambda i, j: (i, 0)),
                  pl.BlockSpec((k, TN_PROJ), lambda i, j: (0, j))],
        out_specs=pl.BlockSpec((TM_PROJ, TN_PROJ), lambda i, j: (i, j)),
        out_shape=jax.ShapeDtypeStruct((t, n), BF16),
        compiler_params=_cparams("parallel", "arbitrary"),
        name="in_proj",
    )(h16, w)


def _prep_w_in(w_in):
    d = w_in.shape[0]
    gq, gk = w_in[:, 0:512], w_in[:, 512:1024]
    gv, gr = w_in[:, 1024:2048], w_in[:, 2048:3072]
    gfr = w_in[:, 3072:3088]
    sgz = w_in[:, 3088:5136]
    mq = w_in[:, 5136:5520]
    ckv = w_in[:, 5520:5776]
    kr = w_in[:, 5776:5840]
    gates = w_in[:, 5840:8912]
    half = MLA_ROPE // 2
    kr_rot = jnp.concatenate([-kr[:, half:], kr[:, :half]], axis=1)
    z = lambda n: jnp.zeros((d, n), w_in.dtype)
    small = jnp.concatenate([mq, gfr, z(112), ckv, kr, kr_rot, z(128)], axis=1)
    return jnp.concatenate([small, gates, sgz, gv, gr, gq, gk], axis=1).astype(BF16)


def _gla_kernel(q_ref, k_ref, v_ref, gr_ref, gf_ref, fgw_ref, fgb_ref, g_ref, tri_ref, blk_ref,
                o_ref, st_ref, qd_ref, ki_ref, ke_ref, dec_ref):
    rows = q_ref.shape[0]

    @pl.when(pl.program_id(1) == 0)
    def _():
        st_ref[...] = jnp.zeros_like(st_ref)

    pre = _dot(gf_ref[...], fgw_ref[...]) + fgb_ref[...]
    glog = -(jnp.maximum(-pre, 0.0) + jnp.log1p(jnp.exp(-jnp.abs(pre)))) * (1.0 / GLA_TAU)

    hi, lo = _split_bf16(glog)
    bcum = _dot(tri_ref[...], hi) + _dot(tri_ref[...], lo)
    btot = _dot(blk_ref[...], hi) + _dot(blk_ref[...], lo)

    q = q_ref[...].astype(F32) * (GLA_DK ** -0.5)
    k = k_ref[...].astype(F32)
    qd_ref[...] = (q * jnp.exp(bcum)).astype(BF16)
    ki_ref[...] = (k * jnp.exp(-bcum)).astype(BF16)
    ke_ref[...] = (k * jnp.exp(btot - bcum)).astype(BF16)
    dec_ref[...] = jnp.exp(btot)

    tr = lax.broadcasted_iota(jnp.int32, (CHUNK, CHUNK), 0)
    tc = lax.broadcasted_iota(jnp.int32, (CHUNK, CHUNK), 1)
    causal = tc <= tr

    def chunk_body(ci, carry):
        r0 = pl.multiple_of(ci * CHUNK, CHUNK)
        rs = pl.ds(r0, CHUNK)
        for hd in range(GLA_HEADS):
            ks = slice(hd * GLA_DK, (hd + 1) * GLA_DK)
            vs = slice(hd * GLA_DV, (hd + 1) * GLA_DV)
            qd = qd_ref[rs, ks]
            ki = ki_ref[rs, ks]
            ke = ke_ref[rs, ks]
            v = v_ref[rs, vs]
            att = jnp.where(causal, _dot_nt(qd, ki), 0.0)
            st = st_ref[hd]
            o = _dot(att.astype(BF16), v) + _dot_nt(qd, st.astype(BF16))
            dec = dec_ref[pl.ds(r0, 1), ks]
            st_ref[hd] = st * dec + _dot_tn(v, ke)
            on = _rms_norm(o, g_ref[:, vs])
            gate = gr_ref[rs, vs].astype(F32)
            o_ref[rs, vs] = (on * _silu(gate)).astype(o_ref.dtype)
        return carry

    lax.fori_loop(0, rows // CHUNK, chunk_body, 0, unroll=True)


def _gla_call(proj, fg_w, fg_b, gla_g, batch, seq):
    t = proj.shape[0]
    nr = seq // R_GLA
    row = lambda i, j: i * nr + j
    fgw = jnp.zeros((LANES, GLA_HEADS * GLA_DK), F32).at[:GLA_GATE_RANK].set(fg_w).astype(BF16)
    hk = GLA_HEADS * GLA_DK
    hv = GLA_HEADS * GLA_DV
    ridx = jnp.arange(R_GLA, dtype=jnp.int32)
    same = (ridx[:, None] // CHUNK) == (ridx[None, :] // CHUNK)
    blk = same.astype(BF16)
    tri = (same & (ridx[None, :] <= ridx[:, None])).astype(BF16)
    return pl.pallas_call(
        _gla_kernel,
        grid=(batch, nr),
        in_specs=[
            pl.BlockSpec((R_GLA, hk), lambda i, j: (row(i, j), P_GQ // hk)),
            pl.BlockSpec((R_GLA, hk), lambda i, j: (row(i, j), P_GK // hk)),
            pl.BlockSpec((R_GLA, hv), lambda i, j: (row(i, j), P_GV // hv)),
            pl.BlockSpec((R_GLA, hv), lambda i, j: (row(i, j), P_GR // hv)),
            pl.BlockSpec((R_GLA, LANES), lambda i, j: (row(i, j), P_GFR // LANES)),
            pl.BlockSpec((LANES, hk), lambda i, j: (0, 0)),
            pl.BlockSpec((1, hk), lambda i, j: (0, 0)),
            pl.BlockSpec((1, hv), lambda i, j: (0, 0)),
            pl.BlockSpec((R_GLA, R_GLA), lambda i, j: (0, 0)),
            pl.BlockSpec((R_GLA, R_GLA), lambda i, j: (0, 0)),
        ],
        out_specs=pl.BlockSpec((R_GLA, hv), lambda i, j: (row(i, j), 0)),
        out_shape=jax.ShapeDtypeStruct((t, hv), BF16),
        scratch_shapes=[
            pltpu.VMEM((GLA_HEADS, GLA_DV, GLA_DK), F32),
            pltpu.VMEM((R_GLA, hk), BF16),
            pltpu.VMEM((R_GLA, hk), BF16),
            pltpu.VMEM((R_GLA, hk), BF16),
            pltpu.VMEM((R_GLA, hk), F32),
        ],
        compiler_params=_cparams("parallel", "arbitrary"),
        name="gla",
    )(proj, proj, proj, proj, proj, fgw, fg_b.reshape(1, hk), gla_g.reshape(1, hv), tri, blk)


def _gelu(x):
    return 0.5 * x * (1.0 + lax.erf(x * (2.0 ** -0.5)))


def _sg_kernel(u_ref, v_ref, g_ref, b_ref, ws_ref, bias_ref, o_ref):
    rows = u_ref.shape[0]
    gw = SG_WIDTH // SG_GROUPS
    u = _gelu(u_ref[...].astype(F32))
    v = _gelu(v_ref[...].astype(F32))
    vn = _layer_norm(v, g_ref[...], b_ref[...]).astype(BF16)
    tr = lax.broadcasted_iota(jnp.int32, (SG_BLOCK, SG_BLOCK), 0)
    tc = lax.broadcasted_iota(jnp.int32, (SG_BLOCK, SG_BLOCK), 1)
    tril = tc <= tr
    for g in range(SG_GROUPS):
        w = jnp.where(tril, ws_ref[g], 0.0).astype(BF16)
        cs = slice(g * gw, (g + 1) * gw)
        for blk in range(rows // SG_BLOCK):
            rs = slice(blk * SG_BLOCK, (blk + 1) * SG_BLOCK)
            sv = _dot(w, vn[rs, cs]) + bias_ref[:, cs]
            o_ref[rs, cs] = (u[rs, cs] * sv).astype(o_ref.dtype)


def _sg_call(proj, sg_g, sg_bn, sg_w, sg_b):
    t = proj.shape[0]
    gw = SG_WIDTH // SG_GROUPS
    bias = jnp.repeat(sg_b.T, gw, axis=1)
    return pl.pallas_call(
        _sg_kernel,
        grid=(t // TM_SG,),
        in_specs=[
            pl.BlockSpec((TM_SG, SG_WIDTH), lambda i: (i, P_SG // SG_WIDTH)),
            pl.BlockSpec((TM_SG, SG_WIDTH), lambda i: (i, P_SG // SG_WIDTH + 1)),
            pl.BlockSpec((1, SG_WIDTH), lambda i: (0, 0)),
            pl.BlockSpec((1, SG_WIDTH), lambda i: (0, 0)),
            pl.BlockSpec((SG_GROUPS, SG_BLOCK, SG_BLOCK), lambda i: (0, 0, 0)),
            pl.BlockSpec((SG_BLOCK, SG_WIDTH), lambda i: (0, 0)),
        ],
        out_specs=pl.BlockSpec((TM_SG, SG_WIDTH), lambda i: (i, 0)),
        out_shape=jax.ShapeDtypeStruct((t, SG_WIDTH), BF16),
        compiler_params=_cparams("parallel"),
        name="spatial_gating",
    )(proj, proj, sg_g.reshape(1, -1), sg_bn.reshape(1, -1), sg_w, bias)


def _mla_prep_kernel(sm_ref, cs_ref, qg_ref, kg_ref, wq_ref, wkv_ref, q_ref, k_ref, vt_ref):
    rows = sm_ref.shape[0]
    mq = sm_ref[:, 0:MLA_Q_RANK].astype(F32)
    qn = _rms_norm(mq, qg_ref[...]).astype(BF16)
    qf = _dot(qn, wq_ref[...])
    ckv = sm_ref[:, P_CKV:P_CKV + MLA_KV_RANK].astype(F32)
    cn = _rms_norm(ckv, kg_ref[...]).astype(BF16)
    kv = _dot(cn, wkv_ref[...])

    cs = cs_ref[...]
    lane = lax.broadcasted_iota(jnp.int32, (rows, LANES), 1)
    low = lane < MLA_ROPE

    def rope(pair):
        t = pair * cs
        return jnp.where(low, t + pltpu.roll(t, MLA_ROPE, 1), 0.0)

    scale = (MLA_QK ** -0.5) * LOG2_E
    kr = rope(sm_ref[:, P_KROPE:P_KROPE + LANES].astype(F32)).astype(BF16)
    for h in range(MLA_HEADS):
        c0 = h * MLA_HEAD_PAD
        q_ref[:, c0:c0 + MLA_NOPE] = (qf[:, c0:c0 + MLA_NOPE] * scale).astype(BF16)
        q_ref[:, c0 + MLA_NOPE:c0 + MLA_HEAD_PAD] = (
            rope(qf[:, c0 + MLA_NOPE:c0 + MLA_HEAD_PAD]) * scale).astype(BF16)
        k_ref[:, c0:c0 + MLA_NOPE] = kv[:, h * MLA_NOPE:(h + 1) * MLA_NOPE].astype(BF16)
        k_ref[:, c0 + MLA_NOPE:c0 + MLA_HEAD_PAD] = kr
    vt = kv[:, MLA_HEADS * MLA_NOPE:].T.astype(BF16)
    ones = jnp.ones((MLA_V_EXT - MLA_V, TK_ATT), BF16)
    for blk in range(rows // TK_ATT):
        ks = slice(blk * TK_ATT, (blk + 1) * TK_ATT)
        for h in range(MLA_HEADS):
            r0 = h * MLA_V_EXT
            vt_ref[blk, r0:r0 + MLA_V, :] = vt[h * MLA_V:(h + 1) * MLA_V, ks]
            vt_ref[blk, r0 + MLA_V:r0 + MLA_V_EXT, :] = ones


def _mla_prep_call(proj, qn_g, w_uq, kvn_g, w_ukv, seq):
    t = proj.shape[0]
    half = MLA_ROPE // 2
    wq = w_uq.reshape(MLA_Q_RANK, MLA_HEADS, MLA_QK)
    wr = wq[:, :, MLA_NOPE:]
    wr_rot = jnp.concatenate([-wr[:, :, half:], wr[:, :, :half]], axis=2)
    wq_ext = jnp.concatenate([wq[:, :, :MLA_NOPE], wr, wr_rot], axis=2)
    wq_ext = wq_ext.reshape(MLA_Q_RANK, MLA_HEADS * MLA_HEAD_PAD).astype(BF16)
    wkv = w_ukv.reshape(MLA_KV_RANK, MLA_HEADS, MLA_NOPE + MLA_V)
    wkv = jnp.concatenate([wkv[:, :, :MLA_NOPE].reshape(MLA_KV_RANK, -1),
                           wkv[:, :, MLA_NOPE:].reshape(MLA_KV_RANK, -1)], axis=1).astype(BF16)
    inv = ROPE_THETA ** (-jnp.arange(half, dtype=F32) / half)
    ang = jnp.arange(seq, dtype=F32)[:, None] * inv
    cs = jnp.concatenate([jnp.cos(ang), jnp.cos(ang), jnp.sin(ang), jnp.sin(ang)], axis=1)
    ns = seq // TM_MLA
    hq = MLA_HEADS * MLA_HEAD_PAD
    hv = MLA_HEADS * MLA_V_EXT
    return pl.pallas_call(
        _mla_prep_kernel,
        grid=(t // TM_MLA,),
        in_specs=[
            pl.BlockSpec((TM_MLA, P_SMALL), lambda i: (i, 0)),
            pl.BlockSpec((TM_MLA, LANES), lambda i: (i % ns, 0)),
            pl.BlockSpec((1, MLA_Q_RANK), lambda i: (0, 0)),
            pl.BlockSpec((1, MLA_KV_RANK), lambda i: (0, 0)),
            pl.BlockSpec((MLA_Q_RANK, hq), lambda i: (0, 0)),
            pl.BlockSpec((MLA_KV_RANK, MLA_HEADS * (MLA_NOPE + MLA_V)), lambda i: (0, 0)),
        ],
        out_specs=[
            pl.BlockSpec((TM_MLA, hq), lambda i: (i, 0)),
            pl.BlockSpec((TM_MLA, hq), lambda i: (i, 0)),
            pl.BlockSpec((TM_MLA // TK_ATT, hv, TK_ATT), lambda i: (i, 0, 0)),
        ],
        out_shape=[jax.ShapeDtypeStruct((t, hq), BF16), jax.ShapeDtypeStruct((t, hq), BF16),
                   jax.ShapeDtypeStruct((t // TK_ATT, hv, TK_ATT), BF16)],
        compiler_params=_cparams("parallel"),
        name="mla_prep",
    )(proj, cs, qn_g.reshape(1, -1), kvn_g.reshape(1, -1), wq_ext, wkv)


def _attn_kernel(q_ref, k_ref, vt_ref, o_ref, acc_ref, sa_ref, sb_ref):
    tq = q_ref.shape[0]
    i = pl.program_id(2)
    acc_ref[...] = jnp.zeros_like(acc_ref)

    def scores(j, s_ref):
        r0 = pl.multiple_of(j * TK_ATT, TK_ATT)
        for hh in range(HEADS_ATT):
            qs = slice(hh * MLA_HEAD_PAD, (hh + 1) * MLA_HEAD_PAD)
            s_ref[hh] = _dot_nt(k_ref[pl.ds(r0, TK_ATT), qs], q_ref[:, qs])

    def consume(j, s_ref, ms, mask):
        new_m = []
        for hh in range(HEADS_ATT):
            s = s_ref[hh]
            if mask is not None:
                s = jnp.where(mask, s, -jnp.inf)
            m_new = jnp.maximum(ms[hh], jnp.max(s, axis=0, keepdims=True))
            alpha = jnp.exp2(ms[hh] - m_new)
            p = jnp.exp2(s - m_new).astype(BF16)
            vt = vt_ref[j, hh * MLA_V_EXT:(hh + 1) * MLA_V_EXT, :]
            acc_ref[hh] = alpha * acc_ref[hh] + _dot(vt, p)
            new_m.append(m_new)
        return tuple(new_m)

    def pair(jj, ms):
        j = 2 * jj
        scores(j + 1, sb_ref)
        ms = consume(j, sa_ref, ms, None)
        scores(j + 2, sa_ref)
        return consume(j + 1, sb_ref, ms, None)

    scores(0, sa_ref)
    m0 = tuple(jnp.full((1, tq), -jnp.inf, F32) for _ in range(HEADS_ATT))
    ms = lax.fori_loop(0, i * (tq // (2 * TK_ATT)), pair, m0)
    jd = i * (tq // TK_ATT)
    scores(jd + 1, sb_ref)
    kk = lax.shift_right_logical(lax.broadcasted_iota(jnp.int32, (TK_ATT, tq), 0), 6)
    qq = lax.shift_right_logical(lax.broadcasted_iota(jnp.int32, (TK_ATT, tq), 1), 6)
    ms = consume(jd, sa_ref, ms, kk <= qq)
    ms = consume(jd + 1, sb_ref, ms, kk + (TK_ATT // CHUNK) <= qq)
    for hh in range(HEADS_ATT):
        acc = acc_ref[hh]
        o = acc[:MLA_V, :] / acc[MLA_V:MLA_V + 1, :]
        o_ref[:, hh * MLA_V:(hh + 1) * MLA_V] = o.T.astype(o_ref.dtype)


def _attn_call(q, k, vt, batch, seq):
    t = q.shape[0]
    nq = seq // TQ_ATT
    qw = HEADS_ATT * MLA_HEAD_PAD
    vw = HEADS_ATT * MLA_V_EXT
    s_buf = pltpu.VMEM((HEADS_ATT, TK_ATT, TQ_ATT), F32)
    return pl.pallas_call(
        _attn_kernel,
        grid=(batch, MLA_HEADS // HEADS_ATT, nq),
        in_specs=[
            pl.BlockSpec((TQ_ATT, qw), lambda b, h, i: (b * nq + i, h)),
            pl.BlockSpec((seq, qw), lambda b, h, i: (b, h)),
            pl.BlockSpec((seq // TK_ATT, vw, TK_ATT), lambda b, h, i: (b, h, 0)),
        ],
        out_specs=pl.BlockSpec((TQ_ATT, HEADS_ATT * MLA_V), lambda b, h, i: (b * nq + i, h)),
        out_shape=jax.ShapeDtypeStruct((t, MLA_HEADS * MLA_V), BF16),
        scratch_shapes=[pltpu.VMEM((HEADS_ATT, MLA_V_EXT, TQ_ATT), F32), s_buf, s_buf],
        compiler_params=_cparams("parallel", "parallel", "arbitrary"),
        name="mla_attention",
    )(q, k, vt)


def _merge_kernel(oa_ref, ob_ref, oc_ref, g0_ref, g1_ref, g2_ref, h_ref, wb_ref, wo_ref,
                  lg_ref, lb_ref, o32_ref, o16_ref):
    merged = _sigmoid(g0_ref[...].astype(F32)) * _dot(oa_ref[...], wb_ref[0])
    merged += _sigmoid(g1_ref[...].astype(F32)) * _dot(ob_ref[...], wb_ref[1])
    merged += _sigmoid(g2_ref[...].astype(F32)) * _dot(oc_ref[...], wb_ref[2])
    mix = _dot(merged.astype(BF16), wo_ref[...])
    y = _layer_norm(ALPHA * h_ref[...] + mix, lg_ref[...], lb_ref[...])
    o32_ref[...] = y
    o16_ref[...] = y.astype(BF16)


def _merge_call(o_a, o_b, o_c, proj, h32, w_branch, w_out, ln_g, ln_b):
    t, d = h32.shape
    row = pl.BlockSpec((TM_MERGE, d), lambda i: (i, 0))
    gate = lambda n: pl.BlockSpec((TM_MERGE, d), lambda i: (i, P_GATES // d + n))
    vec = pl.BlockSpec((1, d), lambda i: (0, 0))
    return pl.pallas_call(
        _merge_kernel,
        grid=(t // TM_MERGE,),
        in_specs=[row, row, row, gate(0), gate(1), gate(2), row,
                  pl.BlockSpec((N_BRANCHES, d, d), lambda i: (0, 0, 0)),
                  pl.BlockSpec((d, d), lambda i: (0, 0)), vec, vec],
        out_specs=[row, row],
        out_shape=[jax.ShapeDtypeStruct((t, d), F32), jax.ShapeDtypeStruct((t, d), BF16)],
        compiler_params=_cparams("parallel"),
        name="merge",
    )(o_a, o_b, o_c, proj, proj, proj, h32, w_branch.astype(BF16), w_out.astype(BF16),
      ln_g.reshape(1, d), ln_b.reshape(1, d))


def _first_argmax_mask(vals, iota, n):
    m = jnp.max(vals, axis=0, keepdims=True)
    idx = jnp.min(jnp.where(vals == m, iota, n), axis=0, keepdims=True)
    return iota == idx


def _router_kernel(h_ref, wt_ref, b_ref, comb_ref, rank_ref, cnt_ref):
    tm = h_ref.shape[0]
    h = h_ref[...]
    h_hi, h_lo = _split_bf16(h)
    w = wt_ref[...]
    w_hi, w_lo = _split_bf16(w)
    logits = _dot_nt(w_hi, h_hi) + _dot_nt(w_hi, h_lo) + _dot_nt(w_lo, h_hi)
    scores = _sigmoid(logits)
    biased = scores + b_ref[...]

    neg = -jnp.inf
    sub = lax.broadcasted_iota(jnp.int32, (GROUP_SIZE, tm), 0)
    grp_rows = []
    for g in range(N_GROUPS):
        blk = biased[g * GROUP_SIZE:(g + 1) * GROUP_SIZE, :]
        m1 = jnp.max(blk, axis=0, keepdims=True)
        first = _first_argmax_mask(blk, sub, GROUP_SIZE)
        m2 = jnp.max(jnp.where(first, neg, blk), axis=0, keepdims=True)
        grp_rows.append(m1 + m2)
    gs = jnp.concatenate(grp_rows, axis=0)
    gsel = jnp.zeros((N_GROUPS, tm), jnp.bool_)
    gi = lax.broadcasted_iota(jnp.int32, (N_GROUPS, tm), 0)
    for _ in range(TOPK_GROUPS):
        pick = _first_argmax_mask(gs, gi, N_GROUPS)
        gsel = gsel | pick
        gs = jnp.where(pick, neg, gs)
    emask = jnp.concatenate(
        [jnp.broadcast_to(gsel[g:g + 1, :], (GROUP_SIZE, tm)) for g in range(N_GROUPS)], axis=0)
    cand = jnp.where(emask, biased, neg)
    ei = lax.broadcasted_iota(jnp.int32, (N_EXPERTS, tm), 0)
    chosen = jnp.zeros((N_EXPERTS, tm), jnp.bool_)
    for _ in range(TOP_K):
        pick = _first_argmax_mask(cand, ei, N_EXPERTS)
        chosen = chosen | pick
        cand = jnp.where(pick, neg, cand)
    wsel = jnp.where(chosen, scores, 0.0)
    comb_ref[...] = wsel / jnp.sum(wsel, axis=0, keepdims=True) * ROUTED_SCALE

    r = lax.broadcasted_iota(jnp.int32, (tm, tm), 0)
    c = lax.broadcasted_iota(jnp.int32, (tm, tm), 1)
    upper = jnp.where(r < c, 1.0, 0.0).astype(BF16)
    sel = jnp.where(chosen, 1.0, 0.0)
    rank_ref[...] = _dot(sel.astype(BF16), upper)
    cnt = jnp.sum(sel, axis=1, keepdims=True)
    cnt_ref[...] = jnp.broadcast_to(cnt, (N_EXPERTS, LANES)).astype(jnp.int32)


def _router_call(h32, router_w, router_b):
    t, d = h32.shape
    nt = t // TM_MOE
    bias = jnp.broadcast_to(router_b.reshape(N_EXPERTS, 1), (N_EXPERTS, TM_MOE))
    comb, rank, cnt = pl.pallas_call(
        _router_kernel,
        grid=(nt,),
        in_specs=[pl.BlockSpec((TM_MOE, d), lambda i: (i, 0)),
                  pl.BlockSpec((N_EXPERTS, d), lambda i: (0, 0)),
                  pl.BlockSpec((N_EXPERTS, TM_MOE), lambda i: (0, 0))],
        out_specs=[pl.BlockSpec((N_EXPERTS, TM_MOE), lambda i: (0, i)),
                   pl.BlockSpec((N_EXPERTS, TM_MOE), lambda i: (0, i)),
                   pl.BlockSpec((N_EXPERTS, LANES), lambda i: (i, 0))],
        out_shape=[jax.ShapeDtypeStruct((N_EXPERTS, t), F32),
                   jax.ShapeDtypeStruct((N_EXPERTS, t), F32),
                   jax.ShapeDtypeStruct((nt * N_EXPERTS, LANES), jnp.int32)],
        compiler_params=_cparams("parallel"),
        name="router",
    )(h32, router_w.T, bias)
    return comb, rank, cnt[:, 0].reshape(nt, N_EXPERTS)


def _moe_plan(cnt, t):
    nt = cnt.shape[0]
    pad = (cnt + SEG_ALIGN - 1) // SEG_ALIGN * SEG_ALIGN
    lseg = jnp.cumsum(pad, axis=1) - pad
    ltot = jnp.sum(pad, axis=1)
    etot = jnp.sum(pad, axis=0)
    region = (etot + ROW_BLK - 1) // ROW_BLK * ROW_BLK
    rend = jnp.cumsum(region)
    gpos = (rend - region)[None, :] + jnp.cumsum(pad, axis=0) - pad
    nblk = (rend[-1] // ROW_BLK).astype(jnp.int32).reshape(1)
    blk_start = jnp.arange(_moe_blocks(t), dtype=jnp.int32) * ROW_BLK
    blk_expert = jnp.minimum(jnp.sum(rend[None, :] <= blk_start[:, None], axis=1), N_EXPERTS - 1)
    nbig = pad // BIG_PIECE
    nsmall = (pad % BIG_PIECE) // SEG_ALIGN
    npiece = jnp.stack([jnp.sum(nbig, axis=1), jnp.sum(nsmall, axis=1)], axis=1)

    def piece_list(count, max_pieces, piece_rows, first_row):
        run = jnp.cumsum(count, axis=1)
        j = jnp.arange(max_pieces, dtype=jnp.int32)[None, :, None]
        owner = jnp.sum(run[:, None, :] <= j, axis=2)
        onehot = owner[:, :, None] == jnp.arange(N_EXPERTS, dtype=jnp.int32)[None, None, :]
        pick = lambda a: jnp.sum(jnp.where(onehot, a[:, None, :], 0), axis=2)
        within = (j[:, :, 0] - pick(run - count)) * piece_rows
        return pick(lseg + first_row) + within, pick(gpos + first_row) + within

    max_big = (TM_MOE * TOP_K + N_EXPERTS * (SEG_ALIGN - 1)) // BIG_PIECE
    max_small = N_EXPERTS * (BIG_PIECE // SEG_ALIGN - 1)
    bsrc, bdst = piece_list(nbig, max_big, BIG_PIECE, jnp.zeros_like(pad))
    ssrc, sdst = piece_list(nsmall, max_small, SEG_ALIGN, nbig * BIG_PIECE)
    i32 = lambda a: a.astype(jnp.int32)
    pieces = (i32(npiece), i32(bsrc), i32(bdst), i32(ssrc), i32(sdst))
    return i32(pad), i32(lseg), i32(ltot), pieces, nblk, i32(blk_expert)


def _moe_blocks(t):
    nt = t // TM_MOE
    rows = t * TOP_K + nt * N_EXPERTS * (SEG_ALIGN - 1) + N_EXPERTS * (ROW_BLK - 1)
    return -(-rows // (2 * ROW_BLK)) * 2


def _zero_uncovered_blocks(ref2d, ntot):
    for cb in range(LOCAL_ROWS // SORT_BLK):
        @pl.when((cb + 1) * SORT_BLK > ntot)
        def _():
            ref2d[cb * SORT_BLK:(cb + 1) * SORT_BLK, :] = jnp.zeros((SORT_BLK, ref2d.shape[1]),
                                                                   ref2d.dtype)


def _build_slot_matrices(p_ref, pw_ref, comb_ref, rank_ref, pad_ref, lseg_ref, ntot, i):
    tm = p_ref.shape[1]
    _zero_uncovered_blocks(p_ref, ntot)
    _zero_uncovered_blocks(pw_ref, ntot)
    rowi = lax.broadcasted_iota(jnp.int32, (SLOT_CHUNK, tm), 0).astype(F32)

    def expert_body(e, carry):
        rrow = rank_ref[pl.ds(e, 1), :]
        wrow = comb_ref[pl.ds(e, 1), :]
        base = lseg_ref[i, e]
        nch = lax.shift_right_logical(pad_ref[i, e] + (SLOT_CHUNK - 1), SLOT_CHUNK_LOG2)

        def chunk_body(c, carry2):
            off = c * SLOT_CHUNK
            hit = (rrow == rowi + off.astype(F32)) & (wrow > 0.0)
            rs = pl.ds(pl.multiple_of(base + off, SEG_ALIGN), SLOT_CHUNK)
            p_ref[rs, :] = jnp.where(hit, 1.0, 0.0).astype(BF16)
            pw_ref[rs, :] = jnp.where(hit, wrow, 0.0).astype(BF16)
            return carry2

        lax.fori_loop(0, nch, chunk_body, 0)
        return carry

    lax.fori_loop(0, N_EXPERTS, expert_body, 0, unroll=2)


def _segment_copies(piece_refs, tile, make_copy):
    npiece_ref, bsrc_ref, bdst_ref, ssrc_ref, sdst_ref = piece_refs
    for col, rows, src_ref, dst_ref in ((0, BIG_PIECE, bsrc_ref, bdst_ref),
                                        (1, SEG_ALIGN, ssrc_ref, sdst_ref)):
        def piece(j, carry, rows=rows, src_ref=src_ref, dst_ref=dst_ref):
            make_copy(pl.multiple_of(src_ref[tile, j], SEG_ALIGN),
                      pl.multiple_of(dst_ref[tile, j], SEG_ALIGN), rows).start()
            return carry

        lax.fori_loop(0, npiece_ref[tile, col], piece, 0)


def _wait_copies(piece_refs, tile, make_copy):
    npiece_ref = piece_refs[0]
    for col, rows in ((0, BIG_PIECE), (1, SEG_ALIGN)):
        def piece(c, carry, rows=rows):
            make_copy(0, 0, rows).wait()
            return carry

        lax.fori_loop(0, npiece_ref[tile, col], piece, 0)


def _dispatch_kernel(pad_ref, lseg_ref, ltot_ref, npiece_ref, bsrc_ref, bdst_ref, ssrc_ref, sdst_ref,
                     x_ref, comb_ref, rank_ref, xs_hbm, pw_ref, p_ref, xs_ref, sem):
    piece_refs = (npiece_ref, bsrc_ref, bdst_ref, ssrc_ref, sdst_ref)
    i = pl.program_id(0)
    last = pl.num_programs(0) - 1
    buf = lax.rem(i, 2)
    ntot = ltot_ref[i]
    _build_slot_matrices(p_ref, pw_ref, comb_ref, rank_ref, pad_ref, lseg_ref, ntot, i)
    for cb in range(LOCAL_ROWS // SORT_BLK):
        @pl.when(cb * SORT_BLK < ntot)
        def _():
            rs = slice(cb * SORT_BLK, (cb + 1) * SORT_BLK)
            xs_ref[buf, rs, :] = _dot(p_ref[rs, :], x_ref[...]).astype(BF16)

    def copy_for(b):
        return lambda l0, g0, rows: pltpu.make_async_copy(
            xs_ref.at[b, pl.ds(l0, rows), :], xs_hbm.at[pl.ds(g0, rows), :], sem.at[b])

    _segment_copies(piece_refs, i, copy_for(buf))

    @pl.when(i > 0)
    def _():
        _wait_copies(piece_refs, i - 1, copy_for(1 - buf))

    @pl.when(i == last)
    def _():
        _wait_copies(piece_refs, i, copy_for(buf))


def _dispatch_call(h16, comb, rank, plan):
    t, d = h16.shape
    nt = t // TM_MOE
    pad, lseg, ltot, pieces, _, _ = plan
    grid_spec = pltpu.PrefetchScalarGridSpec(
        num_scalar_prefetch=3 + len(pieces),
        grid=(nt,),
        in_specs=[
            pl.BlockSpec((TM_MOE, d), lambda i, *_: (i, 0)),
            pl.BlockSpec((N_EXPERTS, TM_MOE), lambda i, *_: (0, i)),
            pl.BlockSpec((N_EXPERTS, TM_MOE), lambda i, *_: (0, i)),
        ],
        out_specs=[pl.BlockSpec(memory_space=pl.ANY),
                   pl.BlockSpec((LOCAL_ROWS, TM_MOE), lambda i, *_: (i, 0))],
        scratch_shapes=[
            pltpu.VMEM((LOCAL_ROWS, TM_MOE), BF16),
            pltpu.VMEM((2, LOCAL_ROWS, d), BF16),
            pltpu.SemaphoreType.DMA((2,)),
        ],
    )
    return pl.pallas_call(
        _dispatch_kernel,
        grid_spec=grid_spec,
        out_shape=[jax.ShapeDtypeStruct((_moe_blocks(t) * ROW_BLK, d), BF16),
                   jax.ShapeDtypeStruct((nt * LOCAL_ROWS, TM_MOE), BF16)],
        compiler_params=_cparams("arbitrary"),
        name="moe_dispatch",
    )(pad, lseg, ltot, *pieces, h16, comb, rank)


def _expert_ffn_kernel(nblk_ref, be_ref, x_ref, wg0_ref, wu0_ref, wd0_ref, wg1_ref, wu1_ref, wd1_ref,
                       o_ref):
    second = 2 * pl.program_id(0) + 1
    weights = ((wg0_ref, wu0_ref, wd0_ref), (wg1_ref, wu1_ref, wd1_ref))

    def chain(half):
        wg_ref, wu_ref, wd_ref = weights[half]
        rs = slice(half * ROW_BLK, (half + 1) * ROW_BLK)
        x = x_ref[rs, :]
        gate = _dot(x, wg_ref[0, 0].astype(BF16))
        up = _dot(x, wu_ref[0, 0].astype(BF16))
        hmid = (_silu(gate) * up).astype(BF16)
        o_ref[rs, :] = _dot(hmid, wd_ref[0, 0].astype(BF16)).astype(o_ref.dtype)

    @pl.when(second < nblk_ref[0])
    def _():
        chain(0)
        chain(1)

    @pl.when(second == nblk_ref[0])
    def _():
        chain(0)


def _expert_ffn_call(xs, plan, w_gate, w_up, w_down, layer):
    rows, d = xs.shape
    nblk, blk_expert = plan[-2:]
    live = lambda s, nblk: jnp.minimum(s, lax.shift_right_logical(nblk[0] - 1, 1))
    wmap = lambda half: (lambda s, nblk, be: (
        layer, be[jnp.minimum(2 * live(s, nblk) + half, nblk[0] - 1)], 0, 0))
    wspecs = []
    for half in range(2):
        wspecs += [pl.BlockSpec((1, 1, d, D_EXPERT), wmap(half)),
                   pl.BlockSpec((1, 1, d, D_EXPERT), wmap(half)),
                   pl.BlockSpec((1, 1, D_EXPERT, d), wmap(half))]
    grid_spec = pltpu.PrefetchScalarGridSpec(
        num_scalar_prefetch=2,
        grid=(rows // (2 * ROW_BLK),),
        in_specs=[pl.BlockSpec((2 * ROW_BLK, d), lambda s, nblk, be: (live(s, nblk), 0))] + wspecs,
        out_specs=pl.BlockSpec((2 * ROW_BLK, d), lambda s, nblk, be: (live(s, nblk), 0)),
    )
    return pl.pallas_call(
        _expert_ffn_kernel,
        grid_spec=grid_spec,
        out_shape=jax.ShapeDtypeStruct((rows, d), BF16),
        compiler_params=_cparams("arbitrary"),
        name="moe_expert_ffn",
    )(nblk, blk_expert, xs, w_gate, w_up, w_down, w_gate, w_up, w_down)


def _combine_kernel(ltot_ref, npiece_ref, bsrc_ref, bdst_ref, ssrc_ref, sdst_ref, ys_hbm, pw_ref,
                    h16_ref, h32_ref, p_ref, swgu_ref, swd_ref, pwi_ref, pwg_ref, lg_ref, lb_ref,
                    o32_ref, o16_ref, ys_ref, sem):
    piece_refs = (npiece_ref, bsrc_ref, bdst_ref, ssrc_ref, sdst_ref)
    i = pl.program_id(0)
    last = pl.num_programs(0) - 1
    buf = lax.rem(i, 2)

    def copy_for(b):
        return lambda l0, g0, rows: pltpu.make_async_copy(
            ys_hbm.at[pl.ds(g0, rows), :], ys_ref.at[b, pl.ds(l0, rows), :], sem.at[b])

    def fetch(tile, b):
        _zero_uncovered_blocks(ys_ref.at[b], ltot_ref[tile])
        _segment_copies(piece_refs, tile, copy_for(b))

    @pl.when(i == 0)
    def _():
        fetch(i, buf)

    @pl.when(i < last)
    def _():
        fetch(i + 1, 1 - buf)

    x = h16_ref[...]
    gu = _dot(x, swgu_ref[...])
    shared = _dot((_silu(gu[:, :D_SHARED]) * gu[:, D_SHARED:]).astype(BF16), swd_ref[...])
    ple = _dot(p_ref[...].astype(BF16), pwi_ref[...]) * _sigmoid(_dot(x, pwg_ref[...]))
    z = ALPHA * h32_ref[...] + shared + ple

    _wait_copies(piece_refs, i, copy_for(buf))
    z += _dot_tn(pw_ref[...], ys_ref[buf])
    y = _layer_norm(z, lg_ref[...], lb_ref[...])
    o32_ref[...] = y
    o16_ref[...] = y.astype(BF16)


def _combine_call(ys, pw, plan, h16, h32, p, sw_gate, sw_up, sw_down, ple_w_in, ple_w_gate,
                  ln_g, ln_b):
    t, d = h32.shape
    nt = t // TM_MOE
    _, _, ltot, pieces, _, _ = plan
    row = pl.BlockSpec((TM_MOE, d), lambda i, *_: (i, 0))
    vec = pl.BlockSpec((1, d), lambda i, *_: (0, 0))
    full = lambda a, b: pl.BlockSpec((a, b), lambda i, *_: (0, 0), pipeline_mode=pl.Buffered(1))
    swgu = jnp.concatenate([sw_gate, sw_up], axis=1).astype(BF16)
    grid_spec = pltpu.PrefetchScalarGridSpec(
        num_scalar_prefetch=1 + len(pieces),
        grid=(nt,),
        in_specs=[pl.BlockSpec(memory_space=pl.ANY),
                  pl.BlockSpec((LOCAL_ROWS, TM_MOE), lambda i, *_: (i, 0)), row, row,
                  pl.BlockSpec((TM_MOE, PLE_DIM), lambda i, *_: (i, 0)),
                  full(d, 2 * D_SHARED), full(D_SHARED, d), full(PLE_DIM, d), full(d, d), vec, vec],
        out_specs=[row, row],
        scratch_shapes=[
            pltpu.VMEM((2, LOCAL_ROWS, d), BF16),
            pltpu.SemaphoreType.DMA((2,)),
        ],
    )
    return pl.pallas_call(
        _combine_kernel,
        grid_spec=grid_spec,
        out_shape=[jax.ShapeDtypeStruct((t, d), F32), jax.ShapeDtypeStruct((t, d), BF16)],
        compiler_params=_cparams("arbitrary"),
        name="moe_combine_tail",
    )(ltot, *pieces, ys, pw, h16, h32, p, swgu, sw_down.astype(BF16),
      ple_w_in.astype(BF16), ple_w_gate.astype(BF16), ln_g.reshape(1, d), ln_b.reshape(1, d))


def kernel(x, p, ln_in_g, ln_in_b, w_in, gla_fg_w, gla_fg_b, gla_norm_g, sg_norm_g, sg_norm_b, sg_w, sg_b, mla_qn_g, mla_w_uq, mla_kvn_g, mla_w_ukv, w_branch, w_out, ln1_g, ln1_b, router_w, router_b, exp_w_gate, exp_w_up, exp_w_down, sh_w_gate, sh_w_up, sh_w_down, ple_w_in, ple_w_gate, ln2_g, ln2_b):
    batch, seq, d = x.shape
    t = batch * seq
    depth = w_in.shape[0]
    h32, h16 = _ln_call(x.reshape(t, d), ln_in_g, ln_in_b)
    for i in range(depth):
        proj = _proj_call(h16, _prep_w_in(w_in[i]))
        o_a = _gla_call(proj, gla_fg_w[i], gla_fg_b[i], gla_norm_g[i], batch, seq)
        o_b = _sg_call(proj, sg_norm_g[i], sg_norm_b[i], sg_w[i], sg_b[i])
        q, k, vt = _mla_prep_call(proj, mla_qn_g[i], mla_w_uq[i], mla_kvn_g[i], mla_w_ukv[i], seq)
        o_c = _attn_call(q, k, vt, batch, seq)
        h32, h16 = _merge_call(o_a, o_b, o_c, proj, h32, w_branch[i], w_out[i], ln1_g[i], ln1_b[i])
        comb, rank, cnt = _router_call(h32, router_w[i], router_b[i])
        plan = _moe_plan(cnt, t)
        xs, pw = _dispatch_call(h16, comb, rank, plan)
        ys = _expert_ffn_call(xs, plan, exp_w_gate, exp_w_up, exp_w_down, i)
        h32, h16 = _combine_call(ys, pw, plan, h16, h32, p[i].reshape(t, -1), sh_w_gate[i],
                                 sh_w_up[i], sh_w_down[i], ple_w_in[i], ple_w_gate[i],
                                 ln2_g[i], ln2_b[i])
    return h32.reshape(batch, seq, d)
```

```python
import functools

import jax
import jax.numpy as jnp
from jax import lax
from jax.experimental import pallas as pl
from jax.experimental.pallas import tpu as pltpu

F32 = jnp.float32
BF16 = jnp.bfloat16

D_MODEL = 1024
DEPTH = 2
CHUNK = 64
GLA_HEADS, GLA_DK, GLA_DV, GLA_GATE_RANK, GLA_TAU = 4, 128, 256, 16, 16.0
SG_WIDTH, SG_GROUPS, SG_BLOCK = 1024, 4, 128
MLA_HEADS, MLA_Q_RANK, MLA_KV_RANK = 8, 384, 256
MLA_NOPE, MLA_ROPE, MLA_V = 128, 64, 128
MLA_QK = MLA_NOPE + MLA_ROPE
ROPE_THETA = 10000.0
N_BRANCHES = 3
N_EXPERTS, N_GROUPS, TOPK_GROUPS, TOP_K = 64, 8, 4, 8
GROUP_SIZE = N_EXPERTS // N_GROUPS
D_EXPERT, D_SHARED = 256, 256
ROUTED_SCALE = 2.5
PLE_DIM = 256
ALPHA = (2 * DEPTH) ** 0.25

LANES = 128
SUBLANES = 8
VMEM_PHYSICAL_BYTES = 64 * 1024 * 1024
VMEM_LIMIT_BYTES = VMEM_PHYSICAL_BYTES - 4 * 1024 * 1024

P_SMALL = 1024
P_GFR = 384
P_CKV = 512
P_KROPE = 768
P_GATES = 1024
P_SG = 4096
P_GV = 6144
P_GR = 7168
P_GQ = 8192
P_GK = 8704
P_TOTAL = 9216
MLA_HEAD_PAD = 256

TM_LN = 512
TM_PROJ, TN_PROJ = 1024, 1536
R_GLA = 512
TM_SG = 256
TQ_ATT = 512
TK_ATT = 256
TM_MLA = TQ_ATT
MLA_V_EXT = MLA_V + 16
HEADS_ATT = 4
LOG2_E = 1.4426950408889634
TM_MERGE = 512
TM_MOE = 512
SEG_ALIGN_LOG2 = 4
SEG_ALIGN = 1 << SEG_ALIGN_LOG2
SLOT_CHUNK_LOG2 = 6
SLOT_CHUNK = 1 << SLOT_CHUNK_LOG2
SORT_BLK = 512
ROW_BLK = 1024
FFN_CHAINS = 2
BIG_PIECE_LOG2 = 6
BIG_PIECE = 1 << BIG_PIECE_LOG2
LOCAL_ROWS = -(-(TM_MOE * TOP_K + N_EXPERTS * (SEG_ALIGN - 1) + SLOT_CHUNK) // SORT_BLK) * SORT_BLK


def _cparams(*sem):
    return pltpu.CompilerParams(dimension_semantics=sem, vmem_limit_bytes=VMEM_LIMIT_BYTES)


def _dot(a, b):
    return jnp.dot(a, b, preferred_element_type=F32)


def _dot_nt(a, b):
    return lax.dot_general(a, b, (((1,), (1,)), ((), ())), preferred_element_type=F32)


def _dot_tn(a, b):
    return lax.dot_general(a, b, (((0,), (0,)), ((), ())), preferred_element_type=F32)


def _layer_norm(x, g, b, eps=1e-5):
    mu = jnp.mean(x, axis=-1, keepdims=True)
    xc = x - mu
    var = jnp.mean(xc * xc, axis=-1, keepdims=True)
    return xc * lax.rsqrt(var + eps) * g + b


def _rms_norm(x, g, eps=1e-6):
    return x * lax.rsqrt(jnp.mean(x * x, axis=-1, keepdims=True) + eps) * g


def _sigmoid(x):
    return 1.0 / (1.0 + jnp.exp(-x))


def _silu(x):
    return x * _sigmoid(x)


def _split_bf16(x):
    hi = x.astype(BF16)
    lo = (x - hi.astype(F32)).astype(BF16)
    return hi, lo


def _ln_kernel(x_ref, g_ref, b_ref, o32_ref, o16_ref):
    y = _layer_norm(x_ref[...], g_ref[...], b_ref[...])
    o32_ref[...] = y
    o16_ref[...] = y.astype(BF16)


def _ln_call(x, g, b):
    t, d = x.shape
    row = pl.BlockSpec((TM_LN, d), lambda i: (i, 0))
    vec = pl.BlockSpec((1, d), lambda i: (0, 0))
    return pl.pallas_call(
        _ln_kernel,
        grid=(t // TM_LN,),
        in_specs=[row, vec, vec],
        out_specs=[row, row],
        out_shape=[jax.ShapeDtypeStruct((t, d), F32), jax.ShapeDtypeStruct((t, d), BF16)],
        compiler_params=_cparams("parallel"),
        name="ln_in",
    )(x, g.reshape(1, d), b.reshape(1, d))


def _proj_kernel(x_ref, w_ref, o_ref):
    o_ref[...] = _dot(x_ref[...], w_ref[...]).astype(o_ref.dtype)


def _proj_call(h16, w):
    t, k = h16.shape
    n = w.shape[1]
    return pl.pallas_call(
        _proj_kernel,
        grid=(t // TM_PROJ, n // TN_PROJ),
        in_specs=[pl.BlockSpec((TM_PROJ, k), lambda i, j: (i, 0)),
                  pl.BlockSpec((k, TN_PROJ), lambda i, j: (0, j))],
        out_specs=pl.BlockSpec((TM_PROJ, TN_PROJ), lambda i, j: (i, j)),
        out_shape=jax.ShapeDtypeStruct((t, n), BF16),
        compiler_params=_cparams("parallel", "arbitrary"),
        name="in_proj",
    )(h16, w)


def _prep_w_in(w_in):
    d = w_in.shape[0]
    gq, gk = w_in[:, 0:512], w_in[:, 512:1024]
    gv, gr = w_in[:, 1024:2048], w_in[:, 2048:3072]
    gfr = w_in[:, 3072:3088]
    sgz = w_in[:, 3088:5136]
    mq = w_in[:, 5136:5520]
    ckv = w_in[:, 5520:5776]
    kr = w_in[:, 5776:5840]
    gates = w_in[:, 5840:8912]
    half = MLA_ROPE // 2
    kr_rot = jnp.concatenate([-kr[:, half:], kr[:, :half]], axis=1)
    z = lambda n: jnp.zeros((d, n), w_in.dtype)
    small = jnp.concatenate([mq, gfr, z(112), ckv, kr, kr_rot, z(128)], axis=1)
    return jnp.concatenate([small, gates, sgz, gv, gr, gq, gk], axis=1).astype(BF16)


def _gla_kernel(q_ref, k_ref, v_ref, gr_ref, gf_ref, fgw_ref, fgb_ref, g_ref, tri_ref, blk_ref,
                o_ref, st_ref, qd_ref, ki_ref, ke_ref, dec_ref):
    rows = q_ref.shape[0]

    @pl.when(pl.program_id(1) == 0)
    def _():
        st_ref[...] = jnp.zeros_like(st_ref)

    pre = _dot(gf_ref[...], fgw_ref[...]) + fgb_ref[...]
    glog = -(jnp.maximum(-pre, 0.0) + jnp.log1p(jnp.exp(-jnp.abs(pre)))) * (1.0 / GLA_TAU)

    hi, lo = _split_bf16(glog)
    bcum = _dot(tri_ref[...], hi) + _dot(tri_ref[...], lo)
    btot = _dot(blk_ref[...], hi) + _dot(blk_ref[...], lo)

    q = q_ref[...].astype(F32) * (GLA_DK ** -0.5)
    k = k_ref[...].astype(F32)
    qd_ref[...] = (q * jnp.exp(bcum)).astype(BF16)
    ki_ref[...] = (k * jnp.exp(-bcum)).astype(BF16)
    ke_ref[...] = (k * jnp.exp(btot - bcum)).astype(BF16)
    dec_ref[...] = jnp.exp(btot)

    tr = lax.broadcasted_iota(jnp.int32, (CHUNK, CHUNK), 0)
    tc = lax.broadcasted_iota(jnp.int32, (CHUNK, CHUNK), 1)
    causal = tc <= tr

    def chunk_body(ci, carry):
        r0 = pl.multiple_of(ci * CHUNK, CHUNK)
        rs = pl.ds(r0, CHUNK)
        for hd in range(GLA_HEADS):
            ks = slice(hd * GLA_DK, (hd + 1) * GLA_DK)
            vs = slice(hd * GLA_DV, (hd + 1) * GLA_DV)
            qd = qd_ref[rs, ks]
            ki = ki_ref[rs, ks]
            ke = ke_ref[rs, ks]
            v = v_ref[rs, vs]
            att = jnp.where(causal, _dot_nt(qd, ki), 0.0)
            st = st_ref[hd]
            o = _dot(att.astype(BF16), v) + _dot_nt(qd, st.astype(BF16))
            dec = dec_ref[pl.ds(r0, 1), ks]
            st_ref[hd] = st * dec + _dot_tn(v, ke)
            on = _rms_norm(o, g_ref[:, vs])
            gate = gr_ref[rs, vs].astype(F32)
            o_ref[rs, vs] = (on * _silu(gate)).astype(o_ref.dtype)
        return carry

    lax.fori_loop(0, rows // CHUNK, chunk_body, 0, unroll=True)


def _gla_call(proj, fg_w, fg_b, gla_g, batch, seq):
    t = proj.shape[0]
    nr = seq // R_GLA
    row = lambda i, j: i * nr + j
    fgw = jnp.zeros((LANES, GLA_HEADS * GLA_DK), F32).at[:GLA_GATE_RANK].set(fg_w).astype(BF16)
    hk = GLA_HEADS * GLA_DK
    hv = GLA_HEADS * GLA_DV
    ridx = jnp.arange(R_GLA, dtype=jnp.int32)
    same = (ridx[:, None] // CHUNK) == (ridx[None, :] // CHUNK)
    blk = same.astype(BF16)
    tri = (same & (ridx[None, :] <= ridx[:, None])).astype(BF16)
    return pl.pallas_call(
        _gla_kernel,
        grid=(batch, nr),
        in_specs=[
            pl.BlockSpec((R_GLA, hk), lambda i, j: (row(i, j), P_GQ // hk)),
            pl.BlockSpec((R_GLA, hk), lambda i, j: (row(i, j), P_GK // hk)),
            pl.BlockSpec((R_GLA, hv), lambda i, j: (row(i, j), P_GV // hv)),
            pl.BlockSpec((R_GLA, hv), lambda i, j: (row(i, j), P_GR // hv)),
            pl.BlockSpec((R_GLA, LANES), lambda i, j: (row(i, j), P_GFR // LANES)),
            pl.BlockSpec((LANES, hk), lambda i, j: (0, 0)),
            pl.BlockSpec((1, hk), lambda i, j: (0, 0)),
            pl.BlockSpec((1, hv), lambda i, j: (0, 0)),
            pl.BlockSpec((R_GLA, R_GLA), lambda i, j: (0, 0)),
            pl.BlockSpec((R_GLA, R_GLA), lambda i, j: (0, 0)),
        ],
        out_specs=pl.BlockSpec((R_GLA, hv), lambda i, j: (row(i, j), 0)),
        out_shape=jax.ShapeDtypeStruct((t, hv), BF16),
        scratch_shapes=[
            pltpu.VMEM((GLA_HEADS, GLA_DV, GLA_DK), F32),
            pltpu.VMEM((R_GLA, hk), BF16),
            pltpu.VMEM((R_GLA, hk), BF16),
            pltpu.VMEM((R_GLA, hk), BF16),
            pltpu.VMEM((R_GLA, hk), F32),
        ],
        compiler_params=_cparams("parallel", "arbitrary"),
        name="gla",
    )(proj, proj, proj, proj, proj, fgw, fg_b.reshape(1, hk), gla_g.reshape(1, hv), tri, blk)


def _gelu(x):
    return 0.5 * x * (1.0 + lax.erf(x * (2.0 ** -0.5)))


def _sg_kernel(u_ref, v_ref, g_ref, b_ref, ws_ref, bias_ref, o_ref):
    rows = u_ref.shape[0]
    gw = SG_WIDTH // SG_GROUPS
    u = _gelu(u_ref[...].astype(F32))
    v = _gelu(v_ref[...].astype(F32))
    vn = _layer_norm(v, g_ref[...], b_ref[...]).astype(BF16)
    tr = lax.broadcasted_iota(jnp.int32, (SG_BLOCK, SG_BLOCK), 0)
    tc = lax.broadcasted_iota(jnp.int32, (SG_BLOCK, SG_BLOCK), 1)
    tril = tc <= tr
    for g in range(SG_GROUPS):
        w = jnp.where(tril, ws_ref[g], 0.0).astype(BF16)
        cs = slice(g * gw, (g + 1) * gw)
        for blk in range(rows // SG_BLOCK):
            rs = slice(blk * SG_BLOCK, (blk + 1) * SG_BLOCK)
            sv = _dot(w, vn[rs, cs]) + bias_ref[:, cs]
            o_ref[rs, cs] = (u[rs, cs] * sv).astype(o_ref.dtype)


def _sg_call(proj, sg_g, sg_bn, sg_w, sg_b):
    t = proj.shape[0]
    gw = SG_WIDTH // SG_GROUPS
    bias = jnp.repeat(sg_b.T, gw, axis=1)
    return pl.pallas_call(
        _sg_kernel,
        grid=(t // TM_SG,),
        in_specs=[
            pl.BlockSpec((TM_SG, SG_WIDTH), lambda i: (i, P_SG // SG_WIDTH)),
            pl.BlockSpec((TM_SG, SG_WIDTH), lambda i: (i, P_SG // SG_WIDTH + 1)),
            pl.BlockSpec((1, SG_WIDTH), lambda i: (0, 0)),
            pl.BlockSpec((1, SG_WIDTH), lambda i: (0, 0)),
            pl.BlockSpec((SG_GROUPS, SG_BLOCK, SG_BLOCK), lambda i: (0, 0, 0)),
            pl.BlockSpec((SG_BLOCK, SG_WIDTH), lambda i: (0, 0)),
        ],
        out_specs=pl.BlockSpec((TM_SG, SG_WIDTH), lambda i: (i, 0)),
        out_shape=jax.ShapeDtypeStruct((t, SG_WIDTH), BF16),
        compiler_params=_cparams("parallel"),
        name="spatial_gating",
    )(proj, proj, sg_g.reshape(1, -1), sg_bn.reshape(1, -1), sg_w, bias)


def _mla_prep_kernel(sm_ref, cs_ref, qg_ref, kg_ref, wq_ref, wkv_ref, q_ref, k_ref, vt_ref):
    rows = sm_ref.shape[0]
    mq = sm_ref[:, 0:MLA_Q_RANK].astype(F32)
    qn = _rms_norm(mq, qg_ref[...]).astype(BF16)
    qf = _dot(qn, wq_ref[...])
    ckv = sm_ref[:, P_CKV:P_CKV + MLA_KV_RANK].astype(F32)
    cn = _rms_norm(ckv, kg_ref[...]).astype(BF16)
    kv = _dot(cn, wkv_ref[...])

    cs = cs_ref[...]
    lane = lax.broadcasted_iota(jnp.int32, (rows, LANES), 1)
    low = lane < MLA_ROPE

    def rope(pair):
        t = pair * cs
        return jnp.where(low, t + pltpu.roll(t, MLA_ROPE, 1), 0.0)

    scale = (MLA_QK ** -0.5) * LOG2_E
    kr = rope(sm_ref[:, P_KROPE:P_KROPE + LANES].astype(F32)).astype(BF16)
    for h in range(MLA_HEADS):
        c0 = h * MLA_HEAD_PAD
        q_ref[:, c0:c0 + MLA_NOPE] = (qf[:, c0:c0 + MLA_NOPE] * scale).astype(BF16)
        q_ref[:, c0 + MLA_NOPE:c0 + MLA_HEAD_PAD] = (
            rope(qf[:, c0 + MLA_NOPE:c0 + MLA_HEAD_PAD]) * scale).astype(BF16)
        k_ref[:, c0:c0 + MLA_NOPE] = kv[:, h * MLA_NOPE:(h + 1) * MLA_NOPE].astype(BF16)
        k_ref[:, c0 + MLA_NOPE:c0 + MLA_HEAD_PAD] = kr
    vt = kv[:, MLA_HEADS * MLA_NOPE:].T.astype(BF16)
    ones = jnp.ones((MLA_V_EXT - MLA_V, TK_ATT), BF16)
    for blk in range(rows // TK_ATT):
        ks = slice(blk * TK_ATT, (blk + 1) * TK_ATT)
        for h in range(MLA_HEADS):
            r0 = h * MLA_V_EXT
            vt_ref[blk, r0:r0 + MLA_V, :] = vt[h * MLA_V:(h + 1) * MLA_V, ks]
            vt_ref[blk, r0 + MLA_V:r0 + MLA_V_EXT, :] = ones


def _mla_prep_call(proj, qn_g, w_uq, kvn_g, w_ukv, seq):
    t = proj.shape[0]
    half = MLA_ROPE // 2
    wq = w_uq.reshape(MLA_Q_RANK, MLA_HEADS, MLA_QK)
    wr = wq[:, :, MLA_NOPE:]
    wr_rot = jnp.concatenate([-wr[:, :, half:], wr[:, :, :half]], axis=2)
    wq_ext = jnp.concatenate([wq[:, :, :MLA_NOPE], wr, wr_rot], axis=2)
    wq_ext = wq_ext.reshape(MLA_Q_RANK, MLA_HEADS * MLA_HEAD_PAD).astype(BF16)
    wkv = w_ukv.reshape(MLA_KV_RANK, MLA_HEADS, MLA_NOPE + MLA_V)
    wkv = jnp.concatenate([wkv[:, :, :MLA_NOPE].reshape(MLA_KV_RANK, -1),
                           wkv[:, :, MLA_NOPE:].reshape(MLA_KV_RANK, -1)], axis=1).astype(BF16)
    inv = ROPE_THETA ** (-jnp.arange(half, dtype=F32) / half)
    ang = jnp.arange(seq, dtype=F32)[:, None] * inv
    cs = jnp.concatenate([jnp.cos(ang), jnp.cos(ang), jnp.sin(ang), jnp.sin(ang)], axis=1)
    ns = seq // TM_MLA
    hq = MLA_HEADS * MLA_HEAD_PAD
    hv = MLA_HEADS * MLA_V_EXT
    return pl.pallas_call(
        _mla_prep_kernel,
        grid=(t // TM_MLA,),
        in_specs=[
            pl.BlockSpec((TM_MLA, P_SMALL), lambda i: (i, 0)),
            pl.BlockSpec((TM_MLA, LANES), lambda i: (i % ns, 0)),
            pl.BlockSpec((1, MLA_Q_RANK), lambda i: (0, 0)),
            pl.BlockSpec((1, MLA_KV_RANK), lambda i: (0, 0)),
            pl.BlockSpec((MLA_Q_RANK, hq), lambda i: (0, 0)),
            pl.BlockSpec((MLA_KV_RANK, MLA_HEADS * (MLA_NOPE + MLA_V)), lambda i: (0, 0)),
        ],
        out_specs=[
            pl.BlockSpec((TM_MLA, hq), lambda i: (i, 0)),
            pl.BlockSpec((TM_MLA, hq), lambda i: (i, 0)),
            pl.BlockSpec((TM_MLA // TK_ATT, hv, TK_ATT), lambda i: (i, 0, 0)),
        ],
        out_shape=[jax.ShapeDtypeStruct((t, hq), BF16), jax.ShapeDtypeStruct((t, hq), BF16),
                   jax.ShapeDtypeStruct((t // TK_ATT, hv, TK_ATT), BF16)],
        compiler_params=_cparams("parallel"),
        name="mla_prep",
    )(proj, cs, qn_g.reshape(1, -1), kvn_g.reshape(1, -1), wq_ext, wkv)


def _attn_kernel(q_ref, k_ref, vt_ref, o_ref, acc_ref, sa_ref, sb_ref):
    tq = q_ref.shape[0]
    i = pl.program_id(2)
    acc_ref[...] = jnp.zeros_like(acc_ref)

    def scores(j, s_ref):
        r0 = pl.multiple_of(j * TK_ATT, TK_ATT)
        for hh in range(HEADS_ATT):
            qs = slice(hh * MLA_HEAD_PAD, (hh + 1) * MLA_HEAD_PAD)
            s_ref[hh] = _dot_nt(k_ref[pl.ds(r0, TK_ATT), qs], q_ref[:, qs])

    def consume(j, s_ref, ms, mask):
        new_m = []
        for hh in range(HEADS_ATT):
            s = s_ref[hh]
            if mask is not None:
                s = jnp.where(mask, s, -jnp.inf)
            m_new = jnp.maximum(ms[hh], jnp.max(s, axis=0, keepdims=True))
            alpha = jnp.exp2(ms[hh] - m_new)
            p = jnp.exp2(s - m_new).astype(BF16)
            vt = vt_ref[j, hh * MLA_V_EXT:(hh + 1) * MLA_V_EXT, :]
            acc_ref[hh] = alpha * acc_ref[hh] + _dot(vt, p)
            new_m.append(m_new)
        return tuple(new_m)

    def pair(jj, ms):
        j = 2 * jj
        scores(j + 1, sb_ref)
        ms = consume(j, sa_ref, ms, None)
        scores(j + 2, sa_ref)
        return consume(j + 1, sb_ref, ms, None)

    scores(0, sa_ref)
    m0 = tuple(jnp.full((1, tq), -jnp.inf, F32) for _ in range(HEADS_ATT))
    ms = lax.fori_loop(0, i * (tq // (2 * TK_ATT)), pair, m0)
    jd = i * (tq // TK_ATT)
    scores(jd + 1, sb_ref)
    kk = lax.shift_right_logical(lax.broadcasted_iota(jnp.int32, (TK_ATT, tq), 0), 6)
    qq = lax.shift_right_logical(lax.broadcasted_iota(jnp.int32, (TK_ATT, tq), 1), 6)
    ms = consume(jd, sa_ref, ms, kk <= qq)
    ms = consume(jd + 1, sb_ref, ms, kk + (TK_ATT // CHUNK) <= qq)
    for hh in range(HEADS_ATT):
        acc = acc_ref[hh]
        o = acc[:MLA_V, :] / acc[MLA_V:MLA_V + 1, :]
        o_ref[:, hh * MLA_V:(hh + 1) * MLA_V] = o.T.astype(o_ref.dtype)


def _attn_call(q, k, vt, batch, seq):
    t = q.shape[0]
    nq = seq // TQ_ATT
    qw = HEADS_ATT * MLA_HEAD_PAD
    vw = HEADS_ATT * MLA_V_EXT
    s_buf = pltpu.VMEM((HEADS_ATT, TK_ATT, TQ_ATT), F32)
    return pl.pallas_call(
        _attn_kernel,
        grid=(batch, MLA_HEADS // HEADS_ATT, nq),
        in_specs=[
            pl.BlockSpec((TQ_ATT, qw), lambda b, h, i: (b * nq + i, h)),
            pl.BlockSpec((seq, qw), lambda b, h, i: (b, h)),
            pl.BlockSpec((seq // TK_ATT, vw, TK_ATT), lambda b, h, i: (b, h, 0)),
        ],
        out_specs=pl.BlockSpec((TQ_ATT, HEADS_ATT * MLA_V), lambda b, h, i: (b * nq + i, h)),
        out_shape=jax.ShapeDtypeStruct((t, MLA_HEADS * MLA_V), BF16),
        scratch_shapes=[pltpu.VMEM((HEADS_ATT, MLA_V_EXT, TQ_ATT), F32), s_buf, s_buf],
        compiler_params=_cparams("parallel", "parallel", "arbitrary"),
        name="mla_attention",
    )(q, k, vt)


def _merge_kernel(oa_ref, ob_ref, oc_ref, g0_ref, g1_ref, g2_ref, h_ref, wb_ref, wo_ref,
                  lg_ref, lb_ref, o32_ref, o16_ref):
    merged = _sigmoid(g0_ref[...].astype(F32)) * _dot(oa_ref[...], wb_ref[0])
    merged += _sigmoid(g1_ref[...].astype(F32)) * _dot(ob_ref[...], wb_ref[1])
    merged += _sigmoid(g2_ref[...].astype(F32)) * _dot(oc_ref[...], wb_ref[2])
    mix = _dot(merged.astype(BF16), wo_ref[...])
    y = _layer_norm(ALPHA * h_ref[...] + mix, lg_ref[...], lb_ref[...])
    o32_ref[...] = y
    o16_ref[...] = y.astype(BF16)


def _merge_call(o_a, o_b, o_c, proj, h32, w_branch, w_out, ln_g, ln_b):
    t, d = h32.shape
    row = pl.BlockSpec((TM_MERGE, d), lambda i: (i, 0))
    gate = lambda n: pl.BlockSpec((TM_MERGE, d), lambda i: (i, P_GATES // d + n))
    vec = pl.BlockSpec((1, d), lambda i: (0, 0))
    return pl.pallas_call(
        _merge_kernel,
        grid=(t // TM_MERGE,),
        in_specs=[row, row, row, gate(0), gate(1), gate(2), row,
                  pl.BlockSpec((N_BRANCHES, d, d), lambda i: (0, 0, 0)),
                  pl.BlockSpec((d, d), lambda i: (0, 0)), vec, vec],
        out_specs=[row, row],
        out_shape=[jax.ShapeDtypeStruct((t, d), F32), jax.ShapeDtypeStruct((t, d), BF16)],
        compiler_params=_cparams("parallel"),
        name="merge",
    )(o_a, o_b, o_c, proj, proj, proj, h32, w_branch.astype(BF16), w_out.astype(BF16),
      ln_g.reshape(1, d), ln_b.reshape(1, d))


def _first_argmax_mask(vals, iota, n):
    m = jnp.max(vals, axis=0, keepdims=True)
    idx = jnp.min(jnp.where(vals == m, iota, n), axis=0, keepdims=True)
    return iota == idx


def _router_kernel(h_ref, wt_ref, b_ref, comb_ref, rank_ref, cnt_ref):
    tm = h_ref.shape[0]
    h = h_ref[...]
    h_hi, h_lo = _split_bf16(h)
    w = wt_ref[...]
    w_hi, w_lo = _split_bf16(w)
    logits = _dot_nt(w_hi, h_hi) + _dot_nt(w_hi, h_lo) + _dot_nt(w_lo, h_hi)
    scores = _sigmoid(logits)
    biased = scores + b_ref[...]

    neg = -jnp.inf
    sub = lax.broadcasted_iota(jnp.int32, (GROUP_SIZE, tm), 0)
    grp_rows = []
    for g in range(N_GROUPS):
        blk = biased[g * GROUP_SIZE:(g + 1) * GROUP_SIZE, :]
        m1 = jnp.max(blk, axis=0, keepdims=True)
        first = _first_argmax_mask(blk, sub, GROUP_SIZE)
        m2 = jnp.max(jnp.where(first, neg, blk), axis=0, keepdims=True)
        grp_rows.append(m1 + m2)
    gs = jnp.concatenate(grp_rows, axis=0)
    gsel = jnp.zeros((N_GROUPS, tm), jnp.bool_)
    gi = lax.broadcasted_iota(jnp.int32, (N_GROUPS, tm), 0)
    for _ in range(TOPK_GROUPS):
        pick = _first_argmax_mask(gs, gi, N_GROUPS)
        gsel = gsel | pick
        gs = jnp.where(pick, neg, gs)
    emask = jnp.concatenate(
        [jnp.broadcast_to(gsel[g:g + 1, :], (GROUP_SIZE, tm)) for g in range(N_GROUPS)], axis=0)
    cand = jnp.where(emask, biased, neg)
    ei = lax.broadcasted_iota(jnp.int32, (N_EXPERTS, tm), 0)
    chosen = jnp.zeros((N_EXPERTS, tm), jnp.bool_)
    for _ in range(TOP_K):
        pick = _first_argmax_mask(cand, ei, N_EXPERTS)
        chosen = chosen | pick
        cand = jnp.where(pick, neg, cand)
    wsel = jnp.where(chosen, scores, 0.0)
    comb_ref[...] = wsel / jnp.sum(wsel, axis=0, keepdims=True) * ROUTED_SCALE

    r = lax.broadcasted_iota(jnp.int32, (tm, tm), 0)
    c = lax.broadcasted_iota(jnp.int32, (tm, tm), 1)
    upper = jnp.where(r < c, 1.0, 0.0).astype(BF16)
    sel = jnp.where(chosen, 1.0, 0.0)
    rank_ref[...] = _dot(sel.astype(BF16), upper)
    cnt = jnp.sum(sel, axis=1, keepdims=True)
    cnt_ref[...] = jnp.broadcast_to(cnt, (N_EXPERTS, LANES)).astype(jnp.int32)


def _router_call(h32, router_w, router_b):
    t, d = h32.shape
    nt = t // TM_MOE
    bias = jnp.broadcast_to(router_b.reshape(N_EXPERTS, 1), (N_EXPERTS, TM_MOE))
    comb, rank, cnt = pl.pallas_call(
        _router_kernel,
        grid=(nt,),
        in_specs=[pl.BlockSpec((TM_MOE, d), lambda i: (i, 0)),
                  pl.BlockSpec((N_EXPERTS, d), lambda i: (0, 0)),
                  pl.BlockSpec((N_EXPERTS, TM_MOE), lambda i: (0, 0))],
        out_specs=[pl.BlockSpec((N_EXPERTS, TM_MOE), lambda i: (0, i)),
                   pl.BlockSpec((N_EXPERTS, TM_MOE), lambda i: (0, i)),
                   pl.BlockSpec((N_EXPERTS, LANES), lambda i: (i, 0))],
        out_shape=[jax.ShapeDtypeStruct((N_EXPERTS, t), F32),
                   jax.ShapeDtypeStruct((N_EXPERTS, t), F32),
                   jax.ShapeDtypeStruct((nt * N_EXPERTS, LANES), jnp.int32)],
        compiler_params=_cparams("parallel"),
        name="router",
    )(h32, router_w.T, bias)
    return comb, rank, cnt[:, 0].reshape(nt, N_EXPERTS)


def _moe_plan(cnt, t):
    nt = cnt.shape[0]
    pad = (cnt + SEG_ALIGN - 1) // SEG_ALIGN * SEG_ALIGN
    lseg = jnp.cumsum(pad, axis=1) - pad
    ltot = jnp.sum(pad, axis=1)
    etot = jnp.sum(pad, axis=0)
    region = (etot + ROW_BLK - 1) // ROW_BLK * ROW_BLK
    rend = jnp.cumsum(region)
    gpos = (rend - region)[None, :] + jnp.cumsum(pad, axis=0) - pad
    nblk = (rend[-1] // ROW_BLK).astype(jnp.int32).reshape(1)
    blk_start = jnp.arange(_moe_blocks(t), dtype=jnp.int32) * ROW_BLK
    blk_expert = jnp.minimum(jnp.sum(rend[None, :] <= blk_start[:, None], axis=1), N_EXPERTS - 1)
    nbig = pad // BIG_PIECE
    nsmall = (pad % BIG_PIECE) // SEG_ALIGN
    npiece = jnp.stack([jnp.sum(nbig, axis=1), jnp.sum(nsmall, axis=1)], axis=1)

    def piece_list(count, max_pieces, piece_rows, first_row):
        run = jnp.cumsum(count, axis=1)
        j = jnp.arange(max_pieces, dtype=jnp.int32)[None, :, None]
        owner = jnp.sum(run[:, None, :] <= j, axis=2)
        onehot = owner[:, :, None] == jnp.arange(N_EXPERTS, dtype=jnp.int32)[None, None, :]
        pick = lambda a: jnp.sum(jnp.where(onehot, a[:, None, :], 0), axis=2)
        within = (j[:, :, 0] - pick(run - count)) * piece_rows
        return pick(lseg + first_row) + within, pick(gpos + first_row) + within

    max_big = (TM_MOE * TOP_K + N_EXPERTS * (SEG_ALIGN - 1)) // BIG_PIECE
    max_small = N_EXPERTS * (BIG_PIECE // SEG_ALIGN - 1)
    bsrc, bdst = piece_list(nbig, max_big, BIG_PIECE, jnp.zeros_like(pad))
    ssrc, sdst = piece_list(nsmall, max_small, SEG_ALIGN, nbig * BIG_PIECE)
    i32 = lambda a: a.astype(jnp.int32)
    pieces = (i32(npiece), i32(bsrc), i32(bdst), i32(ssrc), i32(sdst))
    return i32(pad), i32(lseg), i32(ltot), pieces, nblk, i32(blk_expert)


def _moe_blocks(t):
    nt = t // TM_MOE
    rows = t * TOP_K + nt * N_EXPERTS * (SEG_ALIGN - 1) + N_EXPERTS * (ROW_BLK - 1)
    return -(-rows // ROW_BLK)


def _zero_uncovered_blocks(ref2d, ntot):
    for cb in range(LOCAL_ROWS // SORT_BLK):
        @pl.when((cb + 1) * SORT_BLK > ntot)
        def _():
            ref2d[cb * SORT_BLK:(cb + 1) * SORT_BLK, :] = jnp.zeros((SORT_BLK, ref2d.shape[1]),
                                                                   ref2d.dtype)


def _build_slot_matrices(p_ref, pw_ref, comb_ref, rank_ref, pad_ref, lseg_ref, ntot, i):
    tm = p_ref.shape[1]
    _zero_uncovered_blocks(p_ref, ntot)
    _zero_uncovered_blocks(pw_ref, ntot)
    rowi = lax.broadcasted_iota(jnp.int32, (SLOT_CHUNK, tm), 0).astype(F32)

    def expert_body(e, carry):
        rrow = rank_ref[pl.ds(e, 1), :]
        wrow = comb_ref[pl.ds(e, 1), :]
        base = lseg_ref[i, e]
        nch = lax.shift_right_logical(pad_ref[i, e] + (SLOT_CHUNK - 1), SLOT_CHUNK_LOG2)

        def chunk_body(c, carry2):
            off = c * SLOT_CHUNK
            hit = (rrow == rowi + off.astype(F32)) & (wrow > 0.0)
            rs = pl.ds(pl.multiple_of(base + off, SEG_ALIGN), SLOT_CHUNK)
            p_ref[rs, :] = jnp.where(hit, 1.0, 0.0).astype(BF16)
            pw_ref[rs, :] = jnp.where(hit, wrow, 0.0).astype(BF16)
            return carry2

        lax.fori_loop(0, nch, chunk_body, 0)
        return carry

    lax.fori_loop(0, N_EXPERTS, expert_body, 0, unroll=2)


def _segment_copies(piece_refs, tile, make_copy):
    npiece_ref, bsrc_ref, bdst_ref, ssrc_ref, sdst_ref = piece_refs
    for col, rows, src_ref, dst_ref in ((0, BIG_PIECE, bsrc_ref, bdst_ref),
                                        (1, SEG_ALIGN, ssrc_ref, sdst_ref)):
        def piece(j, carry, rows=rows, src_ref=src_ref, dst_ref=dst_ref):
            make_copy(pl.multiple_of(src_ref[tile, j], SEG_ALIGN),
                      pl.multiple_of(dst_ref[tile, j], SEG_ALIGN), rows).start()
            return carry

        lax.fori_loop(0, npiece_ref[tile, col], piece, 0)


def _wait_copies(piece_refs, tile, make_copy):
    npiece_ref = piece_refs[0]
    for col, rows in ((0, BIG_PIECE), (1, SEG_ALIGN)):
        def piece(c, carry, rows=rows):
            make_copy(0, 0, rows).wait()
            return carry

        lax.fori_loop(0, npiece_ref[tile, col], piece, 0)


def _dispatch_kernel(pad_ref, lseg_ref, ltot_ref, npiece_ref, bsrc_ref, bdst_ref, ssrc_ref, sdst_ref,
                     x_ref, comb_ref, rank_ref, xs_hbm, pw_ref, p_ref, xs_ref, sem):
    piece_refs = (npiece_ref, bsrc_ref, bdst_ref, ssrc_ref, sdst_ref)
    i = pl.program_id(0)
    last = pl.num_programs(0) - 1
    buf = lax.rem(i, 2)
    ntot = ltot_ref[i]
    _build_slot_matrices(p_ref, pw_ref, comb_ref, rank_ref, pad_ref, lseg_ref, ntot, i)
    for cb in range(LOCAL_ROWS // SORT_BLK):
        @pl.when(cb * SORT_BLK < ntot)
        def _():
            rs = slice(cb * SORT_BLK, (cb + 1) * SORT_BLK)
            xs_ref[buf, rs, :] = _dot(p_ref[rs, :], x_ref[...]).astype(BF16)

    def copy_for(b):
        return lambda l0, g0, rows: pltpu.make_async_copy(
            xs_ref.at[b, pl.ds(l0, rows), :], xs_hbm.at[pl.ds(g0, rows), :], sem.at[b])

    _segment_copies(piece_refs, i, copy_for(buf))

    @pl.when(i > 0)
    def _():
        _wait_copies(piece_refs, i - 1, copy_for(1 - buf))

    @pl.when(i == last)
    def _():
        _wait_copies(piece_refs, i, copy_for(buf))


def _dispatch_call(h16, comb, rank, plan):
    t, d = h16.shape
    nt = t // TM_MOE
    pad, lseg, ltot, pieces, _, _ = plan
    grid_spec = pltpu.PrefetchScalarGridSpec(
        num_scalar_prefetch=3 + len(pieces),
        grid=(nt,),
        in_specs=[
            pl.BlockSpec((TM_MOE, d), lambda i, *_: (i, 0)),
            pl.BlockSpec((N_EXPERTS, TM_MOE), lambda i, *_: (0, i)),
            pl.BlockSpec((N_EXPERTS, TM_MOE), lambda i, *_: (0, i)),
        ],
        out_specs=[pl.BlockSpec(memory_space=pl.ANY),
                   pl.BlockSpec((LOCAL_ROWS, TM_MOE), lambda i, *_: (i, 0))],
        scratch_shapes=[
            pltpu.VMEM((LOCAL_ROWS, TM_MOE), BF16),
            pltpu.VMEM((2, LOCAL_ROWS, d), BF16),
            pltpu.SemaphoreType.DMA((2,)),
        ],
    )
    return pl.pallas_call(
        _dispatch_kernel,
        grid_spec=grid_spec,
        out_shape=[jax.ShapeDtypeStruct((_moe_blocks(t) * ROW_BLK, d), BF16),
                   jax.ShapeDtypeStruct((nt * LOCAL_ROWS, TM_MOE), BF16)],
        compiler_params=_cparams("arbitrary"),
        name="moe_dispatch",
    )(pad, lseg, ltot, *pieces, h16, comb, rank)


def _expert_ffn_kernel(nblk_ref, be_ref, x_ref, wg_ref, wu_ref, wd_ref, o_ref):
    @pl.when(pl.program_id(0) < nblk_ref[0])
    def _():
        wg = wg_ref[0, 0].astype(BF16)
        wu = wu_ref[0, 0].astype(BF16)
        wd = wd_ref[0, 0].astype(BF16)
        sub = ROW_BLK // FFN_CHAINS
        for c in range(FFN_CHAINS):
            rs = slice(c * sub, (c + 1) * sub)
            x = x_ref[rs, :]
            hmid = (_silu(_dot(x, wg)) * _dot(x, wu)).astype(BF16)
            o_ref[rs, :] = _dot(hmid, wd).astype(o_ref.dtype)


def _expert_ffn_call(xs, plan, w_gate, w_up, w_down, layer):
    rows, d = xs.shape
    nblk, blk_expert = plan[-2:]
    live = lambda b, nblk, be: jnp.minimum(b, nblk[0] - 1)
    wmap = lambda b, nblk, be: (layer, be[live(b, nblk, be)], 0, 0)
    grid_spec = pltpu.PrefetchScalarGridSpec(
        num_scalar_prefetch=2,
        grid=(rows // ROW_BLK,),
        in_specs=[
            pl.BlockSpec((ROW_BLK, d), lambda b, nblk, be: (live(b, nblk, be), 0)),
            pl.BlockSpec((1, 1, d, D_EXPERT), wmap),
            pl.BlockSpec((1, 1, d, D_EXPERT), wmap),
            pl.BlockSpec((1, 1, D_EXPERT, d), wmap),
        ],
        out_specs=pl.BlockSpec((ROW_BLK, d), lambda b, nblk, be: (live(b, nblk, be), 0)),
    )
    return pl.pallas_call(
        _expert_ffn_kernel,
        grid_spec=grid_spec,
        out_shape=jax.ShapeDtypeStruct((rows, d), BF16),
        compiler_params=_cparams("arbitrary"),
        name="moe_expert_ffn",
    )(nblk, blk_expert, xs, w_gate, w_up, w_down)


def _combine_kernel(ltot_ref, npiece_ref, bsrc_ref, bdst_ref, ssrc_ref, sdst_ref, ys_hbm, pw_ref,
                    h16_ref, h32_ref, p_ref, swgu_ref, swd_ref, pwi_ref, pwg_ref, lg_ref, lb_ref,
                    o32_ref, o16_ref, ys_ref, sem):
    piece_refs = (npiece_ref, bsrc_ref, bdst_ref, ssrc_ref, sdst_ref)
    i = pl.program_id(0)
    last = pl.num_programs(0) - 1
    buf = lax.rem(i, 2)

    def copy_for(b):
        return lambda l0, g0, rows: pltpu.make_async_copy(
            ys_hbm.at[pl.ds(g0, rows), :], ys_ref.at[b, pl.ds(l0, rows), :], sem.at[b])

    def fetch(tile, b):
        _zero_uncovered_blocks(ys_ref.at[b], ltot_ref[tile])
        _segment_copies(piece_refs, tile, copy_for(b))

    @pl.when(i == 0)
    def _():
        fetch(i, buf)

    @pl.when(i < last)
    def _():
        fetch(i + 1, 1 - buf)

    x = h16_ref[...]
    gu = _dot(x, swgu_ref[...])
    shared = _dot((_silu(gu[:, :D_SHARED]) * gu[:, D_SHARED:]).astype(BF16), swd_ref[...])
    ple = _dot(p_ref[...].astype(BF16), pwi_ref[...]) * _sigmoid(_dot(x, pwg_ref[...]))
    z = ALPHA * h32_ref[...] + shared + ple

    _wait_copies(piece_refs, i, copy_for(buf))
    z += _dot_tn(pw_ref[...], ys_ref[buf])
    y = _layer_norm(z, lg_ref[...], lb_ref[...])
    o32_ref[...] = y
    o16_ref[...] = y.astype(BF16)


def _combine_call(ys, pw, plan, h16, h32, p, sw_gate, sw_up, sw_down, ple_w_in, ple_w_gate,
                  ln_g, ln_b):
    t, d = h32.shape
    nt = t // TM_MOE
    _, _, ltot, pieces, _, _ = plan
    row = pl.BlockSpec((TM_MOE, d), lambda i, *_: (i, 0))
    vec = pl.BlockSpec((1, d), lambda i, *_: (0, 0))
    full = lambda a, b: pl.BlockSpec((a, b), lambda i, *_: (0, 0), pipeline_mode=pl.Buffered(1))
    swgu = jnp.concatenate([sw_gate, sw_up], axis=1).astype(BF16)
    grid_spec = pltpu.PrefetchScalarGridSpec(
        num_scalar_prefetch=1 + len(pieces),
        grid=(nt,),
        in_specs=[pl.BlockSpec(memory_space=pl.ANY),
                  pl.BlockSpec((LOCAL_ROWS, TM_MOE), lambda i, *_: (i, 0)), row, row,
                  pl.BlockSpec((TM_MOE, PLE_DIM), lambda i, *_: (i, 0)),
                  full(d, 2 * D_SHARED), full(D_SHARED, d), full(PLE_DIM, d), full(d, d), vec, vec],
        out_specs=[row, row],
        scratch_shapes=[
            pltpu.VMEM((2, LOCAL_ROWS, d), BF16),
            pltpu.SemaphoreType.DMA((2,)),
        ],
    )
    return pl.pallas_call(
        _combine_kernel,
        grid_spec=grid_spec,
        out_shape=[jax.ShapeDtypeStruct((t, d), F32), jax.ShapeDtypeStruct((t, d), BF16)],
        compiler_params=_cparams("arbitrary"),
        name="moe_combine_tail",
    )(ltot, *pieces, ys, pw, h16, h32, p, swgu, sw_down.astype(BF16),
      ple_w_in.astype(BF16), ple_w_gate.astype(BF16), ln_g.reshape(1, d), ln_b.reshape(1, d))


def kernel(x, p, ln_in_g, ln_in_b, w_in, gla_fg_w, gla_fg_b, gla_norm_g, sg_norm_g, sg_norm_b, sg_w, sg_b, mla_qn_g, mla_w_uq, mla_kvn_g, mla_w_ukv, w_branch, w_out, ln1_g, ln1_b, router_w, router_b, exp_w_gate, exp_w_up, exp_w_down, sh_w_gate, sh_w_up, sh_w_down, ple_w_in, ple_w_gate, ln2_g, ln2_b):
    batch, seq, d = x.shape
    t = batch * seq
    depth = w_in.shape[0]
    h32, h16 = _ln_call(x.reshape(t, d), ln_in_g, ln_in_b)
    for i in range(depth):
        proj = _proj_call(h16, _prep_w_in(w_in[i]))
        o_a = _gla_call(proj, gla_fg_w[i], gla_fg_b[i], gla_norm_g[i], batch, seq)
        o_b = _sg_call(proj, sg_norm_g[i], sg_norm_b[i], sg_w[i], sg_b[i])
        q, k, vt = _mla_prep_call(proj, mla_qn_g[i], mla_w_uq[i], mla_kvn_g[i], mla_w_ukv[i], seq)
        o_c = _attn_call(q, k, vt, batch, seq)
        h32, h16 = _merge_call(o_a, o_b, o_c, proj, h32, w_branch[i], w_out[i], ln1_g[i], ln1_b[i])
        comb, rank, cnt = _router_call(h32, router_w[i], router_b[i])
        plan = _moe_plan(cnt, t)
        xs, pw = _dispatch_call(h16, comb, rank, plan)
        ys = _expert_ffn_call(xs, plan, exp_w_gate, exp_w_up, exp_w_down, i)
        h32, h16 = _combine_call(ys, pw, plan, h16, h32, p[i].reshape(t, -1), sh_w_gate[i],
                                 sh_w_up[i], sh_w_down[i], ple_w_in[i], ple_w_gate[i],
                                 ln2_g[i], ln2_b[i])
    return h32.reshape(batch, seq, d)
```

```python
import functools

import jax
import jax.numpy as jnp
from jax import lax
from jax.experimental import pallas as pl
from jax.experimental.pallas import tpu as pltpu

F32 = jnp.float32
BF16 = jnp.bfloat16

D_MODEL = 1024
DEPTH = 2
CHUNK = 64
GLA_HEADS, GLA_DK, GLA_DV, GLA_GATE_RANK, GLA_TAU = 4, 128, 256, 16, 16.0
SG_WIDTH, SG_GROUPS, SG_BLOCK = 1024, 4, 128
MLA_HEADS, MLA_Q_RANK, MLA_KV_RANK = 8, 384, 256
MLA_NOPE, MLA_ROPE, MLA_V = 128, 64, 128
MLA_QK = MLA_NOPE + MLA_ROPE
ROPE_THETA = 10000.0
N_BRANCHES = 3
N_EXPERTS, N_GROUPS, TOPK_GROUPS, TOP_K = 64, 8, 4, 8
GROUP_SIZE = N_EXPERTS // N_GROUPS
D_EXPERT, D_SHARED = 256, 256
ROUTED_SCALE = 2.5
PLE_DIM = 256
ALPHA = (2 * DEPTH) ** 0.25

LANES = 128
SUBLANES = 8
VMEM_PHYSICAL_BYTES = 64 * 1024 * 1024
VMEM_LIMIT_BYTES = VMEM_PHYSICAL_BYTES - 4 * 1024 * 1024

P_SMALL = 1024
P_GFR = 384
P_CKV = 512
P_KROPE = 768
P_GATES = 1024
P_SG = 4096
P_GV = 6144
P_GR = 7168
P_GQ = 8192
P_GK = 8704
P_TOTAL = 9216
MLA_HEAD_PAD = 256

TM_LN = 512
TM_PROJ, TN_PROJ = 1024, 1536
R_GLA = 512
TM_SG = 256
TQ_ATT = 512
TK_ATT = 256
TM_MLA = TQ_ATT
MLA_V_EXT = MLA_V + 16
HEADS_ATT = 4
LOG2_E = 1.4426950408889634
TM_MERGE = 512
TM_MOE = 512
SEG_ALIGN_LOG2 = 4
SEG_ALIGN = 1 << SEG_ALIGN_LOG2
SLOT_CHUNK_LOG2 = 6
SLOT_CHUNK = 1 << SLOT_CHUNK_LOG2
SORT_BLK = 512
ROW_BLK = 1024
FFN_CHAINS = 2
BIG_PIECE_LOG2 = 6
BIG_PIECE = 1 << BIG_PIECE_LOG2
LOCAL_ROWS = -(-(TM_MOE * TOP_K + N_EXPERTS * (SEG_ALIGN - 1) + SLOT_CHUNK) // SORT_BLK) * SORT_BLK


def _cparams(*sem):
    return pltpu.CompilerParams(dimension_semantics=sem, vmem_limit_bytes=VMEM_LIMIT_BYTES)


def _dot(a, b):
    return jnp.dot(a, b, preferred_element_type=F32)


def _dot_nt(a, b):
    return lax.dot_general(a, b, (((1,), (1,)), ((), ())), preferred_element_type=F32)


def _dot_tn(a, b):
    return lax.dot_general(a, b, (((0,), (0,)), ((), ())), preferred_element_type=F32)


def _layer_norm(x, g, b, eps=1e-5):
    mu = jnp.mean(x, axis=-1, keepdims=True)
    xc = x - mu
    var = jnp.mean(xc * xc, axis=-1, keepdims=True)
    return xc * lax.rsqrt(var + eps) * g + b


def _rms_norm(x, g, eps=1e-6):
    return x * lax.rsqrt(jnp.mean(x * x, axis=-1, keepdims=True) + eps) * g


def _sigmoid(x):
    return 1.0 / (1.0 + jnp.exp(-x))


def _silu(x):
    return x * _sigmoid(x)


def _split_bf16(x):
    hi = x.astype(BF16)
    lo = (x - hi.astype(F32)).astype(BF16)
    return hi, lo


def _ln_kernel(x_ref, g_ref, b_ref, o32_ref, o16_ref):
    y = _layer_norm(x_ref[...], g_ref[...], b_ref[...])
    o32_ref[...] = y
    o16_ref[...] = y.astype(BF16)


def _ln_call(x, g, b):
    t, d = x.shape
    row = pl.BlockSpec((TM_LN, d), lambda i: (i, 0))
    vec = pl.BlockSpec((1, d), lambda i: (0, 0))
    return pl.pallas_call(
        _ln_kernel,
        grid=(t // TM_LN,),
        in_specs=[row, vec, vec],
        out_specs=[row, row],
        out_shape=[jax.ShapeDtypeStruct((t, d), F32), jax.ShapeDtypeStruct((t, d), BF16)],
        compiler_params=_cparams("parallel"),
        name="ln_in",
    )(x, g.reshape(1, d), b.reshape(1, d))


def _proj_kernel(x_ref, w_ref, o_ref):
    o_ref[...] = _dot(x_ref[...], w_ref[...]).astype(o_ref.dtype)


def _proj_call(h16, w, layer):
    t, k = h16.shape
    n = w.shape[2]
    return pl.pallas_call(
        _proj_kernel,
        grid=(t // TM_PROJ, n // TN_PROJ),
        in_specs=[pl.BlockSpec((TM_PROJ, k), lambda i, j: (i, 0)),
                  pl.BlockSpec((None, k, TN_PROJ), lambda i, j: (layer, 0, j))],
        out_specs=pl.BlockSpec((TM_PROJ, TN_PROJ), lambda i, j: (i, j)),
        out_shape=jax.ShapeDtypeStruct((t, n), BF16),
        compiler_params=_cparams("parallel", "arbitrary"),
        name="in_proj",
    )(h16, w)


def _prep_w_in(w_in):
    gq, gk = w_in[..., 0:512], w_in[..., 512:1024]
    gv, gr = w_in[..., 1024:2048], w_in[..., 2048:3072]
    gfr = w_in[..., 3072:3088]
    sgz = w_in[..., 3088:5136]
    mq = w_in[..., 5136:5520]
    ckv = w_in[..., 5520:5776]
    kr = w_in[..., 5776:5840]
    gates = w_in[..., 5840:8912]
    half = MLA_ROPE // 2
    kr_rot = jnp.concatenate([-kr[..., half:], kr[..., :half]], axis=-1)
    z = lambda n: jnp.zeros(w_in.shape[:-1] + (n,), w_in.dtype)
    small = jnp.concatenate([mq, gfr, z(112), ckv, kr, kr_rot, z(128)], axis=-1)
    return jnp.concatenate([small, gates, sgz, gv, gr, gq, gk], axis=-1).astype(BF16)


def _gla_kernel(q_ref, k_ref, v_ref, gr_ref, gf_ref, fgw_ref, fgb_ref, g_ref, tri_ref, blk_ref,
                o_ref, st_ref, qd_ref, ki_ref, ke_ref, dec_ref):
    rows = q_ref.shape[0]

    @pl.when(pl.program_id(1) == 0)
    def _():
        st_ref[...] = jnp.zeros_like(st_ref)

    pre = _dot(gf_ref[...], fgw_ref[...]) + fgb_ref[...]
    glog = -(jnp.maximum(-pre, 0.0) + jnp.log1p(jnp.exp(-jnp.abs(pre)))) * (1.0 / GLA_TAU)

    hi, lo = _split_bf16(glog)
    bcum = _dot(tri_ref[...], hi) + _dot(tri_ref[...], lo)
    btot = _dot(blk_ref[...], hi) + _dot(blk_ref[...], lo)

    q = q_ref[...].astype(F32) * (GLA_DK ** -0.5)
    k = k_ref[...].astype(F32)
    qd_ref[...] = (q * jnp.exp(bcum)).astype(BF16)
    ki_ref[...] = (k * jnp.exp(-bcum)).astype(BF16)
    ke_ref[...] = (k * jnp.exp(btot - bcum)).astype(BF16)
    dec_ref[...] = jnp.exp(btot)

    tr = lax.broadcasted_iota(jnp.int32, (CHUNK, CHUNK), 0)
    tc = lax.broadcasted_iota(jnp.int32, (CHUNK, CHUNK), 1)
    causal = tc <= tr

    def chunk_body(ci, carry):
        r0 = pl.multiple_of(ci * CHUNK, CHUNK)
        rs = pl.ds(r0, CHUNK)
        for hd in range(GLA_HEADS):
            ks = slice(hd * GLA_DK, (hd + 1) * GLA_DK)
            vs = slice(hd * GLA_DV, (hd + 1) * GLA_DV)
            qd = qd_ref[rs, ks]
            ki = ki_ref[rs, ks]
            ke = ke_ref[rs, ks]
            v = v_ref[rs, vs]
            att = jnp.where(causal, _dot_nt(qd, ki), 0.0)
            st = st_ref[hd]
            o = _dot(att.astype(BF16), v) + _dot_nt(qd, st.astype(BF16))
            dec = dec_ref[pl.ds(r0, 1), ks]
            st_ref[hd] = st * dec + _dot_tn(v, ke)
            on = _rms_norm(o, g_ref[:, vs])
            gate = gr_ref[rs, vs].astype(F32)
            o_ref[rs, vs] = (on * _silu(gate)).astype(o_ref.dtype)
        return carry

    lax.fori_loop(0, rows // CHUNK, chunk_body, 0, unroll=True)


def _gla_call(proj, fg_w, fg_b, gla_g, batch, seq):
    t = proj.shape[0]
    nr = seq // R_GLA
    row = lambda i, j: i * nr + j
    fgw = jnp.zeros((LANES, GLA_HEADS * GLA_DK), F32).at[:GLA_GATE_RANK].set(fg_w).astype(BF16)
    hk = GLA_HEADS * GLA_DK
    hv = GLA_HEADS * GLA_DV
    ridx = jnp.arange(R_GLA, dtype=jnp.int32)
    same = (ridx[:, None] // CHUNK) == (ridx[None, :] // CHUNK)
    blk = same.astype(BF16)
    tri = (same & (ridx[None, :] <= ridx[:, None])).astype(BF16)
    return pl.pallas_call(
        _gla_kernel,
        grid=(batch, nr),
        in_specs=[
            pl.BlockSpec((R_GLA, hk), lambda i, j: (row(i, j), P_GQ // hk)),
            pl.BlockSpec((R_GLA, hk), lambda i, j: (row(i, j), P_GK // hk)),
            pl.BlockSpec((R_GLA, hv), lambda i, j: (row(i, j), P_GV // hv)),
            pl.BlockSpec((R_GLA, hv), lambda i, j: (row(i, j), P_GR // hv)),
            pl.BlockSpec((R_GLA, LANES), lambda i, j: (row(i, j), P_GFR // LANES)),
            pl.BlockSpec((LANES, hk), lambda i, j: (0, 0)),
            pl.BlockSpec((1, hk), lambda i, j: (0, 0)),
            pl.BlockSpec((1, hv), lambda i, j: (0, 0)),
            pl.BlockSpec((R_GLA, R_GLA), lambda i, j: (0, 0)),
            pl.BlockSpec((R_GLA, R_GLA), lambda i, j: (0, 0)),
        ],
        out_specs=pl.BlockSpec((R_GLA, hv), lambda i, j: (row(i, j), 0)),
        out_shape=jax.ShapeDtypeStruct((t, hv), BF16),
        scratch_shapes=[
            pltpu.VMEM((GLA_HEADS, GLA_DV, GLA_DK), F32),
            pltpu.VMEM((R_GLA, hk), BF16),
            pltpu.VMEM((R_GLA, hk), BF16),
            pltpu.VMEM((R_GLA, hk), BF16),
            pltpu.VMEM((R_GLA, hk), F32),
        ],
        compiler_params=_cparams("parallel", "arbitrary"),
        name="gla",
    )(proj, proj, proj, proj, proj, fgw, fg_b.reshape(1, hk), gla_g.reshape(1, hv), tri, blk)


def _gelu(x):
    return 0.5 * x * (1.0 + lax.erf(x * (2.0 ** -0.5)))


def _sg_kernel(u_ref, v_ref, g_ref, b_ref, ws_ref, bias_ref, o_ref):
    rows = u_ref.shape[0]
    gw = SG_WIDTH // SG_GROUPS
    u = _gelu(u_ref[...].astype(F32))
    v = _gelu(v_ref[...].astype(F32))
    vn = _layer_norm(v, g_ref[...], b_ref[...]).astype(BF16)
    tr = lax.broadcasted_iota(jnp.int32, (SG_BLOCK, SG_BLOCK), 0)
    tc = lax.broadcasted_iota(jnp.int32, (SG_BLOCK, SG_BLOCK), 1)
    tril = tc <= tr
    for g in range(SG_GROUPS):
        w = jnp.where(tril, ws_ref[g], 0.0).astype(BF16)
        cs = slice(g * gw, (g + 1) * gw)
        for blk in range(rows // SG_BLOCK):
            rs = slice(blk * SG_BLOCK, (blk + 1) * SG_BLOCK)
            sv = _dot(w, vn[rs, cs]) + bias_ref[:, cs]
            o_ref[rs, cs] = (u[rs, cs] * sv).astype(o_ref.dtype)


def _sg_call(proj, sg_g, sg_bn, sg_w, sg_b):
    t = proj.shape[0]
    gw = SG_WIDTH // SG_GROUPS
    bias = jnp.repeat(sg_b.T, gw, axis=1)
    return pl.pallas_call(
        _sg_kernel,
        grid=(t // TM_SG,),
        in_specs=[
            pl.BlockSpec((TM_SG, SG_WIDTH), lambda i: (i, P_SG // SG_WIDTH)),
            pl.BlockSpec((TM_SG, SG_WIDTH), lambda i: (i, P_SG // SG_WIDTH + 1)),
            pl.BlockSpec((1, SG_WIDTH), lambda i: (0, 0)),
            pl.BlockSpec((1, SG_WIDTH), lambda i: (0, 0)),
            pl.BlockSpec((SG_GROUPS, SG_BLOCK, SG_BLOCK), lambda i: (0, 0, 0)),
            pl.BlockSpec((SG_BLOCK, SG_WIDTH), lambda i: (0, 0)),
        ],
        out_specs=pl.BlockSpec((TM_SG, SG_WIDTH), lambda i: (i, 0)),
        out_shape=jax.ShapeDtypeStruct((t, SG_WIDTH), BF16),
        compiler_params=_cparams("parallel"),
        name="spatial_gating",
    )(proj, proj, sg_g.reshape(1, -1), sg_bn.reshape(1, -1), sg_w, bias)


def _mla_prep_kernel(sm_ref, cs_ref, qg_ref, kg_ref, wq_ref, wkv_ref, q_ref, k_ref, vt_ref):
    rows = sm_ref.shape[0]
    mq = sm_ref[:, 0:MLA_Q_RANK].astype(F32)
    qn = _rms_norm(mq, qg_ref[...]).astype(BF16)
    qf = _dot(qn, wq_ref[...])
    ckv = sm_ref[:, P_CKV:P_CKV + MLA_KV_RANK].astype(F32)
    cn = _rms_norm(ckv, kg_ref[...]).astype(BF16)
    kv = _dot(cn, wkv_ref[...])

    cs = cs_ref[...]
    lane = lax.broadcasted_iota(jnp.int32, (rows, LANES), 1)
    low = lane < MLA_ROPE

    def rope(pair):
        t = pair * cs
        return jnp.where(low, t + pltpu.roll(t, MLA_ROPE, 1), 0.0)

    scale = (MLA_QK ** -0.5) * LOG2_E
    kr = rope(sm_ref[:, P_KROPE:P_KROPE + LANES].astype(F32)).astype(BF16)
    for h in range(MLA_HEADS):
        c0 = h * MLA_HEAD_PAD
        q_ref[:, c0:c0 + MLA_NOPE] = (qf[:, c0:c0 + MLA_NOPE] * scale).astype(BF16)
        q_ref[:, c0 + MLA_NOPE:c0 + MLA_HEAD_PAD] = (
            rope(qf[:, c0 + MLA_NOPE:c0 + MLA_HEAD_PAD]) * scale).astype(BF16)
        k_ref[:, c0:c0 + MLA_NOPE] = kv[:, h * MLA_NOPE:(h + 1) * MLA_NOPE].astype(BF16)
        k_ref[:, c0 + MLA_NOPE:c0 + MLA_HEAD_PAD] = kr
    vt = kv[:, MLA_HEADS * MLA_NOPE:].T.astype(BF16)
    ones = jnp.ones((MLA_V_EXT - MLA_V, TK_ATT), BF16)
    for blk in range(rows // TK_ATT):
        ks = slice(blk * TK_ATT, (blk + 1) * TK_ATT)
        for h in range(MLA_HEADS):
            r0 = h * MLA_V_EXT
            vt_ref[blk, r0:r0 + MLA_V, :] = vt[h * MLA_V:(h + 1) * MLA_V, ks]
            vt_ref[blk, r0 + MLA_V:r0 + MLA_V_EXT, :] = ones


def _mla_prep_call(proj, qn_g, w_uq, kvn_g, w_ukv, seq):
    t = proj.shape[0]
    half = MLA_ROPE // 2
    wq = w_uq.reshape(MLA_Q_RANK, MLA_HEADS, MLA_QK)
    wr = wq[:, :, MLA_NOPE:]
    wr_rot = jnp.concatenate([-wr[:, :, half:], wr[:, :, :half]], axis=2)
    wq_ext = jnp.concatenate([wq[:, :, :MLA_NOPE], wr, wr_rot], axis=2)
    wq_ext = wq_ext.reshape(MLA_Q_RANK, MLA_HEADS * MLA_HEAD_PAD).astype(BF16)
    wkv = w_ukv.reshape(MLA_KV_RANK, MLA_HEADS, MLA_NOPE + MLA_V)
    wkv = jnp.concatenate([wkv[:, :, :MLA_NOPE].reshape(MLA_KV_RANK, -1),
                           wkv[:, :, MLA_NOPE:].reshape(MLA_KV_RANK, -1)], axis=1).astype(BF16)
    inv = ROPE_THETA ** (-jnp.arange(half, dtype=F32) / half)
    ang = jnp.arange(seq, dtype=F32)[:, None] * inv
    cs = jnp.concatenate([jnp.cos(ang), jnp.cos(ang), jnp.sin(ang), jnp.sin(ang)], axis=1)
    ns = seq // TM_MLA
    hq = MLA_HEADS * MLA_HEAD_PAD
    hv = MLA_HEADS * MLA_V_EXT
    return pl.pallas_call(
        _mla_prep_kernel,
        grid=(t // TM_MLA,),
        in_specs=[
            pl.BlockSpec((TM_MLA, P_SMALL), lambda i: (i, 0)),
            pl.BlockSpec((TM_MLA, LANES), lambda i: (i % ns, 0)),
            pl.BlockSpec((1, MLA_Q_RANK), lambda i: (0, 0)),
            pl.BlockSpec((1, MLA_KV_RANK), lambda i: (0, 0)),
            pl.BlockSpec((MLA_Q_RANK, hq), lambda i: (0, 0)),
            pl.BlockSpec((MLA_KV_RANK, MLA_HEADS * (MLA_NOPE + MLA_V)), lambda i: (0, 0)),
        ],
        out_specs=[
            pl.BlockSpec((TM_MLA, hq), lambda i: (i, 0)),
            pl.BlockSpec((TM_MLA, hq), lambda i: (i, 0)),
            pl.BlockSpec((TM_MLA // TK_ATT, hv, TK_ATT), lambda i: (i, 0, 0)),
        ],
        out_shape=[jax.ShapeDtypeStruct((t, hq), BF16), jax.ShapeDtypeStruct((t, hq), BF16),
                   jax.ShapeDtypeStruct((t // TK_ATT, hv, TK_ATT), BF16)],
        compiler_params=_cparams("parallel"),
        name="mla_prep",
    )(proj, cs, qn_g.reshape(1, -1), kvn_g.reshape(1, -1), wq_ext, wkv)


def _attn_kernel(q_ref, k_ref, vt_ref, o_ref, acc_ref, sa_ref, sb_ref):
    tq = q_ref.shape[0]
    i = pl.program_id(2)
    acc_ref[...] = jnp.zeros_like(acc_ref)

    def scores(j, s_ref):
        r0 = pl.multiple_of(j * TK_ATT, TK_ATT)
        for hh in range(HEADS_ATT):
            qs = slice(hh * MLA_HEAD_PAD, (hh + 1) * MLA_HEAD_PAD)
            s_ref[hh] = _dot_nt(k_ref[pl.ds(r0, TK_ATT), qs], q_ref[:, qs])

    def consume(j, s_ref, ms, mask):
        new_m = []
        for hh in range(HEADS_ATT):
            s = s_ref[hh]
            if mask is not None:
                s = jnp.where(mask, s, -jnp.inf)
            m_new = jnp.maximum(ms[hh], jnp.max(s, axis=0, keepdims=True))
            alpha = jnp.exp2(ms[hh] - m_new)
            p = jnp.exp2(s - m_new).astype(BF16)
            vt = vt_ref[j, hh * MLA_V_EXT:(hh + 1) * MLA_V_EXT, :]
            acc_ref[hh] = alpha * acc_ref[hh] + _dot(vt, p)
            new_m.append(m_new)
        return tuple(new_m)

    def pair(jj, ms):
        j = 2 * jj
        scores(j + 1, sb_ref)
        ms = consume(j, sa_ref, ms, None)
        scores(j + 2, sa_ref)
        return consume(j + 1, sb_ref, ms, None)

    scores(0, sa_ref)
    m0 = tuple(jnp.full((1, tq), -jnp.inf, F32) for _ in range(HEADS_ATT))
    ms = lax.fori_loop(0, i * (tq // (2 * TK_ATT)), pair, m0)
    jd = i * (tq // TK_ATT)
    scores(jd + 1, sb_ref)
    kk = lax.shift_right_logical(lax.broadcasted_iota(jnp.int32, (TK_ATT, tq), 0), 6)
    qq = lax.shift_right_logical(lax.broadcasted_iota(jnp.int32, (TK_ATT, tq), 1), 6)
    ms = consume(jd, sa_ref, ms, kk <= qq)
    ms = consume(jd + 1, sb_ref, ms, kk + (TK_ATT // CHUNK) <= qq)
    for hh in range(HEADS_ATT):
        acc = acc_ref[hh]
        o = acc[:MLA_V, :] / acc[MLA_V:MLA_V + 1, :]
        o_ref[:, hh * MLA_V:(hh + 1) * MLA_V] = o.T.astype(o_ref.dtype)


def _attn_call(q, k, vt, batch, seq):
    t = q.shape[0]
    nq = seq // TQ_ATT
    qw = HEADS_ATT * MLA_HEAD_PAD
    vw = HEADS_ATT * MLA_V_EXT
    s_buf = pltpu.VMEM((HEADS_ATT, TK_ATT, TQ_ATT), F32)
    return pl.pallas_call(
        _attn_kernel,
        grid=(batch, MLA_HEADS // HEADS_ATT, nq),
        in_specs=[
            pl.BlockSpec((TQ_ATT, qw), lambda b, h, i: (b * nq + i, h)),
            pl.BlockSpec((seq, qw), lambda b, h, i: (b, h)),
            pl.BlockSpec((seq // TK_ATT, vw, TK_ATT), lambda b, h, i: (b, h, 0)),
        ],
        out_specs=pl.BlockSpec((TQ_ATT, HEADS_ATT * MLA_V), lambda b, h, i: (b * nq + i, h)),
        out_shape=jax.ShapeDtypeStruct((t, MLA_HEADS * MLA_V), BF16),
        scratch_shapes=[pltpu.VMEM((HEADS_ATT, MLA_V_EXT, TQ_ATT), F32), s_buf, s_buf],
        compiler_params=_cparams("parallel", "parallel", "arbitrary"),
        name="mla_attention",
    )(q, k, vt)


def _merge_kernel(oa_ref, ob_ref, oc_ref, g0_ref, g1_ref, g2_ref, h_ref, wb_ref, wo_ref,
                  lg_ref, lb_ref, o32_ref, o16_ref):
    merged = _sigmoid(g0_ref[...].astype(F32)) * _dot(oa_ref[...], wb_ref[0])
    merged += _sigmoid(g1_ref[...].astype(F32)) * _dot(ob_ref[...], wb_ref[1])
    merged += _sigmoid(g2_ref[...].astype(F32)) * _dot(oc_ref[...], wb_ref[2])
    mix = _dot(merged.astype(BF16), wo_ref[...])
    y = _layer_norm(ALPHA * h_ref[...] + mix, lg_ref[...], lb_ref[...])
    o32_ref[...] = y
    o16_ref[...] = y.astype(BF16)


def _merge_call(o_a, o_b, o_c, proj, h32, w_branch, w_out, ln_g, ln_b, layer):
    t, d = h32.shape
    row = pl.BlockSpec((TM_MERGE, d), lambda i: (i, 0))
    gate = lambda n: pl.BlockSpec((TM_MERGE, d), lambda i: (i, P_GATES // d + n))
    vec = pl.BlockSpec((1, d), lambda i: (0, 0))
    return pl.pallas_call(
        _merge_kernel,
        grid=(t // TM_MERGE,),
        in_specs=[row, row, row, gate(0), gate(1), gate(2), row,
                  pl.BlockSpec((None, N_BRANCHES, d, d), lambda i: (layer, 0, 0, 0)),
                  pl.BlockSpec((None, d, d), lambda i: (layer, 0, 0)), vec, vec],
        out_specs=[row, row],
        out_shape=[jax.ShapeDtypeStruct((t, d), F32), jax.ShapeDtypeStruct((t, d), BF16)],
        compiler_params=_cparams("parallel"),
        name="merge",
    )(o_a, o_b, o_c, proj, proj, proj, h32, w_branch, w_out, ln_g.reshape(1, d), ln_b.reshape(1, d))


def _first_argmax_mask(vals, iota, n):
    m = jnp.max(vals, axis=0, keepdims=True)
    idx = jnp.min(jnp.where(vals == m, iota, n), axis=0, keepdims=True)
    return iota == idx


def _router_kernel(h_ref, wt_ref, b_ref, comb_ref, rank_ref, cnt_ref):
    tm = h_ref.shape[0]
    h = h_ref[...]
    h_hi, h_lo = _split_bf16(h)
    w = wt_ref[...]
    w_hi, w_lo = _split_bf16(w)
    logits = _dot_nt(w_hi, h_hi) + _dot_nt(w_hi, h_lo) + _dot_nt(w_lo, h_hi)
    scores = _sigmoid(logits)
    biased = scores + b_ref[...]

    neg = -jnp.inf
    sub = lax.broadcasted_iota(jnp.int32, (GROUP_SIZE, tm), 0)
    grp_rows = []
    for g in range(N_GROUPS):
        blk = biased[g * GROUP_SIZE:(g + 1) * GROUP_SIZE, :]
        m1 = jnp.max(blk, axis=0, keepdims=True)
        first = _first_argmax_mask(blk, sub, GROUP_SIZE)
        m2 = jnp.max(jnp.where(first, neg, blk), axis=0, keepdims=True)
        grp_rows.append(m1 + m2)
    gs = jnp.concatenate(grp_rows, axis=0)
    gsel = jnp.zeros((N_GROUPS, tm), jnp.bool_)
    gi = lax.broadcasted_iota(jnp.int32, (N_GROUPS, tm), 0)
    for _ in range(TOPK_GROUPS):
        pick = _first_argmax_mask(gs, gi, N_GROUPS)
        gsel = gsel | pick
        gs = jnp.where(pick, neg, gs)
    emask = jnp.concatenate(
        [jnp.broadcast_to(gsel[g:g + 1, :], (GROUP_SIZE, tm)) for g in range(N_GROUPS)], axis=0)
    cand = jnp.where(emask, biased, neg)
    ei = lax.broadcasted_iota(jnp.int32, (N_EXPERTS, tm), 0)
    chosen = jnp.zeros((N_EXPERTS, tm), jnp.bool_)
    for _ in range(TOP_K):
        pick = _first_argmax_mask(cand, ei, N_EXPERTS)
        chosen = chosen | pick
        cand = jnp.where(pick, neg, cand)
    wsel = jnp.where(chosen, scores, 0.0)
    comb_ref[...] = wsel / jnp.sum(wsel, axis=0, keepdims=True) * ROUTED_SCALE

    r = lax.broadcasted_iota(jnp.int32, (tm, tm), 0)
    c = lax.broadcasted_iota(jnp.int32, (tm, tm), 1)
    upper = jnp.where(r < c, 1.0, 0.0).astype(BF16)
    sel = jnp.where(chosen, 1.0, 0.0)
    rank_ref[...] = _dot(sel.astype(BF16), upper)
    cnt = jnp.sum(sel, axis=1, keepdims=True)
    cnt_ref[...] = jnp.broadcast_to(cnt, (N_EXPERTS, LANES)).astype(jnp.int32)


def _router_call(h32, router_w, router_b):
    t, d = h32.shape
    nt = t // TM_MOE
    bias = jnp.broadcast_to(router_b.reshape(N_EXPERTS, 1), (N_EXPERTS, TM_MOE))
    comb, rank, cnt = pl.pallas_call(
        _router_kernel,
        grid=(nt,),
        in_specs=[pl.BlockSpec((TM_MOE, d), lambda i: (i, 0)),
                  pl.BlockSpec((N_EXPERTS, d), lambda i: (0, 0)),
                  pl.BlockSpec((N_EXPERTS, TM_MOE), lambda i: (0, 0))],
        out_specs=[pl.BlockSpec((N_EXPERTS, TM_MOE), lambda i: (0, i)),
                   pl.BlockSpec((N_EXPERTS, TM_MOE), lambda i: (0, i)),
                   pl.BlockSpec((N_EXPERTS, LANES), lambda i: (i, 0))],
        out_shape=[jax.ShapeDtypeStruct((N_EXPERTS, t), F32),
                   jax.ShapeDtypeStruct((N_EXPERTS, t), F32),
                   jax.ShapeDtypeStruct((nt * N_EXPERTS, LANES), jnp.int32)],
        compiler_params=_cparams("parallel"),
        name="router",
    )(h32, router_w.T, bias)
    return comb, rank, cnt[:, 0].reshape(nt, N_EXPERTS)


def _moe_plan(cnt, t):
    nt = cnt.shape[0]
    pad = (cnt + SEG_ALIGN - 1) // SEG_ALIGN * SEG_ALIGN
    lseg = jnp.cumsum(pad, axis=1) - pad
    ltot = jnp.sum(pad, axis=1)
    etot = jnp.sum(pad, axis=0)
    region = (etot + ROW_BLK - 1) // ROW_BLK * ROW_BLK
    rend = jnp.cumsum(region)
    gpos = (rend - region)[None, :] + jnp.cumsum(pad, axis=0) - pad
    nblk = (rend[-1] // ROW_BLK).astype(jnp.int32).reshape(1)
    blk_start = jnp.arange(_moe_blocks(t), dtype=jnp.int32) * ROW_BLK
    blk_expert = jnp.minimum(jnp.sum(rend[None, :] <= blk_start[:, None], axis=1), N_EXPERTS - 1)
    nbig = pad // BIG_PIECE
    nsmall = (pad % BIG_PIECE) // SEG_ALIGN
    npiece = jnp.stack([jnp.sum(nbig, axis=1), jnp.sum(nsmall, axis=1)], axis=1)

    def piece_list(count, max_pieces, piece_rows, first_row):
        run = jnp.cumsum(count, axis=1)
        j = jnp.arange(max_pieces, dtype=jnp.int32)[None, :, None]
        owner = jnp.sum(run[:, None, :] <= j, axis=2)
        onehot = owner[:, :, None] == jnp.arange(N_EXPERTS, dtype=jnp.int32)[None, None, :]
        pick = lambda a: jnp.sum(jnp.where(onehot, a[:, None, :], 0), axis=2)
        within = (j[:, :, 0] - pick(run - count)) * piece_rows
        return pick(lseg + first_row) + within, pick(gpos + first_row) + within

    max_big = (TM_MOE * TOP_K + N_EXPERTS * (SEG_ALIGN - 1)) // BIG_PIECE
    max_small = N_EXPERTS * (BIG_PIECE // SEG_ALIGN - 1)
    bsrc, bdst = piece_list(nbig, max_big, BIG_PIECE, jnp.zeros_like(pad))
    ssrc, sdst = piece_list(nsmall, max_small, SEG_ALIGN, nbig * BIG_PIECE)
    i32 = lambda a: a.astype(jnp.int32)
    pieces = (i32(npiece), i32(bsrc), i32(bdst), i32(ssrc), i32(sdst))
    return i32(pad), i32(lseg), i32(ltot), pieces, nblk, i32(blk_expert)


def _moe_blocks(t):
    nt = t // TM_MOE
    rows = t * TOP_K + nt * N_EXPERTS * (SEG_ALIGN - 1) + N_EXPERTS * (ROW_BLK - 1)
    return -(-rows // ROW_BLK)


def _zero_uncovered_blocks(ref2d, ntot):
    for cb in range(LOCAL_ROWS // SORT_BLK):
        @pl.when((cb + 1) * SORT_BLK > ntot)
        def _():
            ref2d[cb * SORT_BLK:(cb + 1) * SORT_BLK, :] = jnp.zeros((SORT_BLK, ref2d.shape[1]),
                                                                   ref2d.dtype)


def _build_slot_matrices(p_ref, pw_ref, comb_ref, rank_ref, pad_ref, lseg_ref, ntot, i):
    tm = p_ref.shape[1]
    _zero_uncovered_blocks(p_ref, ntot)
    _zero_uncovered_blocks(pw_ref, ntot)
    rowi = lax.broadcasted_iota(jnp.int32, (SLOT_CHUNK, tm), 0).astype(F32)

    def expert_body(e, carry):
        rrow = rank_ref[pl.ds(e, 1), :]
        wrow = comb_ref[pl.ds(e, 1), :]
        base = lseg_ref[i, e]
        nch = lax.shift_right_logical(pad_ref[i, e] + (SLOT_CHUNK - 1), SLOT_CHUNK_LOG2)

        def chunk_body(c, carry2):
            off = c * SLOT_CHUNK
            hit = (rrow == rowi + off.astype(F32)) & (wrow > 0.0)
            rs = pl.ds(pl.multiple_of(base + off, SEG_ALIGN), SLOT_CHUNK)
            p_ref[rs, :] = jnp.where(hit, 1.0, 0.0).astype(BF16)
            pw_ref[rs, :] = jnp.where(hit, wrow, 0.0).astype(BF16)
            return carry2

        lax.fori_loop(0, nch, chunk_body, 0)
        return carry

    lax.fori_loop(0, N_EXPERTS, expert_body, 0, unroll=2)


def _segment_copies(piece_refs, tile, make_copy):
    npiece_ref, bsrc_ref, bdst_ref, ssrc_ref, sdst_ref = piece_refs
    for col, rows, src_ref, dst_ref in ((0, BIG_PIECE, bsrc_ref, bdst_ref),
                                        (1, SEG_ALIGN, ssrc_ref, sdst_ref)):
        def piece(j, carry, rows=rows, src_ref=src_ref, dst_ref=dst_ref):
            make_copy(pl.multiple_of(src_ref[tile, j], SEG_ALIGN),
                      pl.multiple_of(dst_ref[tile, j], SEG_ALIGN), rows).start()
            return carry

        lax.fori_loop(0, npiece_ref[tile, col], piece, 0)


def _wait_copies(piece_refs, tile, make_copy):
    npiece_ref = piece_refs[0]
    for col, rows in ((0, BIG_PIECE), (1, SEG_ALIGN)):
        def piece(c, carry, rows=rows):
            make_copy(0, 0, rows).wait()
            return carry

        lax.fori_loop(0, npiece_ref[tile, col], piece, 0)


def _dispatch_kernel(pad_ref, lseg_ref, ltot_ref, npiece_ref, bsrc_ref, bdst_ref, ssrc_ref, sdst_ref,
                     x_ref, comb_ref, rank_ref, xs_hbm, pw_ref, p_ref, xs_ref, sem):
    piece_refs = (npiece_ref, bsrc_ref, bdst_ref, ssrc_ref, sdst_ref)
    i = pl.program_id(0)
    last = pl.num_programs(0) - 1
    buf = lax.rem(i, 2)
    ntot = ltot_ref[i]
    _build_slot_matrices(p_ref, pw_ref, comb_ref, rank_ref, pad_ref, lseg_ref, ntot, i)
    for cb in range(LOCAL_ROWS // SORT_BLK):
        @pl.when(cb * SORT_BLK < ntot)
        def _():
            rs = slice(cb * SORT_BLK, (cb + 1) * SORT_BLK)
            xs_ref[buf, rs, :] = _dot(p_ref[rs, :], x_ref[...]).astype(BF16)

    def copy_for(b):
        return lambda l0, g0, rows: pltpu.make_async_copy(
            xs_ref.at[b, pl.ds(l0, rows), :], xs_hbm.at[pl.ds(g0, rows), :], sem.at[b])

    _segment_copies(piece_refs, i, copy_for(buf))

    @pl.when(i > 0)
    def _():
        _wait_copies(piece_refs, i - 1, copy_for(1 - buf))

    @pl.when(i == last)
    def _():
        _wait_copies(piece_refs, i, copy_for(buf))


def _dispatch_call(h16, comb, rank, plan):
    t, d = h16.shape
    nt = t // TM_MOE
    pad, lseg, ltot, pieces, _, _ = plan
    grid_spec = pltpu.PrefetchScalarGridSpec(
        num_scalar_prefetch=3 + len(pieces),
        grid=(nt,),
        in_specs=[
            pl.BlockSpec((TM_MOE, d), lambda i, *_: (i, 0)),
            pl.BlockSpec((N_EXPERTS, TM_MOE), lambda i, *_: (0, i)),
            pl.BlockSpec((N_EXPERTS, TM_MOE), lambda i, *_: (0, i)),
        ],
        out_specs=[pl.BlockSpec(memory_space=pl.ANY),
                   pl.BlockSpec((LOCAL_ROWS, TM_MOE), lambda i, *_: (i, 0))],
        scratch_shapes=[
            pltpu.VMEM((LOCAL_ROWS, TM_MOE), BF16),
            pltpu.VMEM((2, LOCAL_ROWS, d), BF16),
            pltpu.SemaphoreType.DMA((2,)),
        ],
    )
    return pl.pallas_call(
        _dispatch_kernel,
        grid_spec=grid_spec,
        out_shape=[jax.ShapeDtypeStruct((_moe_blocks(t) * ROW_BLK, d), BF16),
                   jax.ShapeDtypeStruct((nt * LOCAL_ROWS, TM_MOE), BF16)],
        compiler_params=_cparams("arbitrary"),
        name="moe_dispatch",
    )(pad, lseg, ltot, *pieces, h16, comb, rank)


def _expert_ffn_kernel(nblk_ref, be_ref, x_ref, wg_ref, wu_ref, wd_ref, o_ref):
    @pl.when(pl.program_id(0) < nblk_ref[0])
    def _():
        wg = wg_ref[0, 0].astype(BF16)
        wu = wu_ref[0, 0].astype(BF16)
        wd = wd_ref[0, 0].astype(BF16)
        sub = ROW_BLK // FFN_CHAINS
        for c in range(FFN_CHAINS):
            rs = slice(c * sub, (c + 1) * sub)
            x = x_ref[rs, :]
            hmid = (_silu(_dot(x, wg)) * _dot(x, wu)).astype(BF16)
            o_ref[rs, :] = _dot(hmid, wd).astype(o_ref.dtype)


def _expert_ffn_call(xs, plan, w_gate, w_up, w_down, layer):
    rows, d = xs.shape
    nblk, blk_expert = plan[-2:]
    live = lambda b, nblk, be: jnp.minimum(b, nblk[0] - 1)
    wmap = lambda b, nblk, be: (layer, be[live(b, nblk, be)], 0, 0)
    grid_spec = pltpu.PrefetchScalarGridSpec(
        num_scalar_prefetch=2,
        grid=(rows // ROW_BLK,),
        in_specs=[
            pl.BlockSpec((ROW_BLK, d), lambda b, nblk, be: (live(b, nblk, be), 0)),
            pl.BlockSpec((1, 1, d, D_EXPERT), wmap),
            pl.BlockSpec((1, 1, d, D_EXPERT), wmap),
            pl.BlockSpec((1, 1, D_EXPERT, d), wmap),
        ],
        out_specs=pl.BlockSpec((ROW_BLK, d), lambda b, nblk, be: (live(b, nblk, be), 0)),
    )
    return pl.pallas_call(
        _expert_ffn_kernel,
        grid_spec=grid_spec,
        out_shape=jax.ShapeDtypeStruct((rows, d), BF16),
        compiler_params=_cparams("arbitrary"),
        name="moe_expert_ffn",
    )(nblk, blk_expert, xs, w_gate, w_up, w_down)


def _combine_kernel(ltot_ref, npiece_ref, bsrc_ref, bdst_ref, ssrc_ref, sdst_ref, ys_hbm, pw_ref,
                    h16_ref, h32_ref, p_ref, swgu_ref, swd_ref, pwi_ref, pwg_ref, lg_ref, lb_ref,
                    o32_ref, o16_ref, ys_ref, sem):
    piece_refs = (npiece_ref, bsrc_ref, bdst_ref, ssrc_ref, sdst_ref)
    i = pl.program_id(0)
    last = pl.num_programs(0) - 1
    buf = lax.rem(i, 2)

    def copy_for(b):
        return lambda l0, g0, rows: pltpu.make_async_copy(
            ys_hbm.at[pl.ds(g0, rows), :], ys_ref.at[b, pl.ds(l0, rows), :], sem.at[b])

    def fetch(tile, b):
        _zero_uncovered_blocks(ys_ref.at[b], ltot_ref[tile])
        _segment_copies(piece_refs, tile, copy_for(b))

    @pl.when(i == 0)
    def _():
        fetch(i, buf)

    @pl.when(i < last)
    def _():
        fetch(i + 1, 1 - buf)

    x = h16_ref[...]
    gu = _dot(x, swgu_ref[...])
    shared = _dot((_silu(gu[:, :D_SHARED]) * gu[:, D_SHARED:]).astype(BF16), swd_ref[...])
    ple = _dot(p_ref[...].astype(BF16), pwi_ref[...]) * _sigmoid(_dot(x, pwg_ref[...]))
    z = ALPHA * h32_ref[...] + shared + ple

    _wait_copies(piece_refs, i, copy_for(buf))
    z += _dot_tn(pw_ref[...], ys_ref[buf])
    y = _layer_norm(z, lg_ref[...], lb_ref[...])
    o32_ref[...] = y
    o16_ref[...] = y.astype(BF16)


def _combine_call(ys, pw, plan, h16, h32, p, sw_gate, sw_up, sw_down, ple_w_in, ple_w_gate,
                  ln_g, ln_b, layer):
    t, d = h32.shape
    nt = t // TM_MOE
    _, _, ltot, pieces, _, _ = plan
    row = pl.BlockSpec((TM_MOE, d), lambda i, *_: (i, 0))
    vec = pl.BlockSpec((1, d), lambda i, *_: (0, 0))
    full = lambda a, b: pl.BlockSpec((a, b), lambda i, *_: (0, 0), pipeline_mode=pl.Buffered(1))
    swgu = jnp.concatenate([sw_gate, sw_up], axis=1).astype(BF16)
    grid_spec = pltpu.PrefetchScalarGridSpec(
        num_scalar_prefetch=1 + len(pieces),
        grid=(nt,),
        in_specs=[pl.BlockSpec(memory_space=pl.ANY),
                  pl.BlockSpec((LOCAL_ROWS, TM_MOE), lambda i, *_: (i, 0)), row, row,
                  pl.BlockSpec((TM_MOE, PLE_DIM), lambda i, *_: (layer * nt + i, 0)),
                  full(d, 2 * D_SHARED), full(D_SHARED, d), full(PLE_DIM, d), full(d, d), vec, vec],
        out_specs=[row, row],
        scratch_shapes=[
            pltpu.VMEM((2, LOCAL_ROWS, d), BF16),
            pltpu.SemaphoreType.DMA((2,)),
        ],
    )
    return pl.pallas_call(
        _combine_kernel,
        grid_spec=grid_spec,
        out_shape=[jax.ShapeDtypeStruct((t, d), F32), jax.ShapeDtypeStruct((t, d), BF16)],
        compiler_params=_cparams("arbitrary"),
        name="moe_combine_tail",
    )(ltot, *pieces, ys, pw, h16, h32, p, swgu, sw_down.astype(BF16),
      ple_w_in.astype(BF16), ple_w_gate.astype(BF16), ln_g.reshape(1, d), ln_b.reshape(1, d))


def kernel(x, p, ln_in_g, ln_in_b, w_in, gla_fg_w, gla_fg_b, gla_norm_g, sg_norm_g, sg_norm_b, sg_w, sg_b, mla_qn_g, mla_w_uq, mla_kvn_g, mla_w_ukv, w_branch, w_out, ln1_g, ln1_b, router_w, router_b, exp_w_gate, exp_w_up, exp_w_down, sh_w_gate, sh_w_up, sh_w_down, ple_w_in, ple_w_gate, ln2_g, ln2_b):
    batch, seq, d = x.shape
    t = batch * seq
    depth = w_in.shape[0]
    h32, h16 = _ln_call(x.reshape(t, d), ln_in_g, ln_in_b)
    w_proj = _prep_w_in(w_in)
    w_branch16 = w_branch.astype(BF16)
    w_out16 = w_out.astype(BF16)
    for i in range(depth):
        proj = _proj_call(h16, w_proj, i)
        o_a = _gla_call(proj, gla_fg_w[i], gla_fg_b[i], gla_norm_g[i], batch, seq)
        o_b = _sg_call(proj, sg_norm_g[i], sg_norm_b[i], sg_w[i], sg_b[i])
        q, k, vt = _mla_prep_call(proj, mla_qn_g[i], mla_w_uq[i], mla_kvn_g[i], mla_w_ukv[i], seq)
        o_c = _attn_call(q, k, vt, batch, seq)
        h32, h16 = _merge_call(o_a, o_b, o_c, proj, h32, w_branch16, w_out16, ln1_g[i], ln1_b[i], i)
        comb, rank, cnt = _router_call(h32, router_w[i], router_b[i])
        plan = _moe_plan(cnt, t)
        xs, pw = _dispatch_call(h16, comb, rank, plan)
        ys = _expert_ffn_call(xs, plan, exp_w_gate, exp_w_up, exp_w_down, i)
        h32, h16 = _combine_call(ys, pw, plan, h16, h32, p.reshape(depth * t, -1), sh_w_gate[i],
                                 sh_w_up[i], sh_w_down[i], ple_w_in[i], ple_w_gate[i],
                                 ln2_g[i], ln2_b[i], i)
    return h32.reshape(batch, seq, d)
```

```python
import functools

import jax
import jax.numpy as jnp
from jax import lax
from jax.experimental import pallas as pl
from jax.experimental.pallas import tpu as pltpu

F32 = jnp.float32
BF16 = jnp.bfloat16

D_MODEL = 1024
DEPTH = 2
CHUNK = 64
GLA_HEADS, GLA_DK, GLA_DV, GLA_GATE_RANK, GLA_TAU = 4, 128, 256, 16, 16.0
SG_WIDTH, SG_GROUPS, SG_BLOCK = 1024, 4, 128
MLA_HEADS, MLA_Q_RANK, MLA_KV_RANK = 8, 384, 256
MLA_NOPE, MLA_ROPE, MLA_V = 128, 64, 128
MLA_QK = MLA_NOPE + MLA_ROPE
ROPE_THETA = 10000.0
N_BRANCHES = 3
N_EXPERTS, N_GROUPS, TOPK_GROUPS, TOP_K = 64, 8, 4, 8
GROUP_SIZE = N_EXPERTS // N_GROUPS
D_EXPERT, D_SHARED = 256, 256
ROUTED_SCALE = 2.5
PLE_DIM = 256
ALPHA = (2 * DEPTH) ** 0.25

LANES = 128
SUBLANES = 8
VMEM_PHYSICAL_BYTES = 64 * 1024 * 1024
VMEM_LIMIT_BYTES = VMEM_PHYSICAL_BYTES - 4 * 1024 * 1024

P_SMALL = 1024
P_GFR = 384
P_CKV = 512
P_KROPE = 768
P_GATES = 1024
P_SG = 4096
P_GV = 6144
P_GR = 7168
P_GQ = 8192
P_GK = 8704
P_TOTAL = 9216
MLA_HEAD_PAD = 256

TM_LN = 512
TM_PROJ, TN_PROJ = 1024, 1536
R_GLA = 512
TM_SG = 256
TQ_ATT = 512
TK_ATT = 256
TM_MLA = TQ_ATT
MLA_V_EXT = MLA_V + 16
HEADS_ATT = 4
LOG2_E = 1.4426950408889634
TM_MERGE = 512
TM_MOE = 512
SEG_ALIGN_LOG2 = 4
SEG_ALIGN = 1 << SEG_ALIGN_LOG2
SLOT_CHUNK_LOG2 = 6
SLOT_CHUNK = 1 << SLOT_CHUNK_LOG2
SORT_BLK = 512
ROW_BLK = 1024
FFN_CHAINS = 2
BIG_PIECE_LOG2 = 6
BIG_PIECE = 1 << BIG_PIECE_LOG2
LOCAL_ROWS = -(-(TM_MOE * TOP_K + N_EXPERTS * (SEG_ALIGN - 1) + SLOT_CHUNK) // SORT_BLK) * SORT_BLK


def _cparams(*sem):
    return pltpu.CompilerParams(dimension_semantics=sem, vmem_limit_bytes=VMEM_LIMIT_BYTES)


def _dot(a, b):
    return jnp.dot(a, b, preferred_element_type=F32)


def _dot_nt(a, b):
    return lax.dot_general(a, b, (((1,), (1,)), ((), ())), preferred_element_type=F32)


def _dot_tn(a, b):
    return lax.dot_general(a, b, (((0,), (0,)), ((), ())), preferred_element_type=F32)


def _layer_norm(x, g, b, eps=1e-5):
    mu = jnp.mean(x, axis=-1, keepdims=True)
    xc = x - mu
    var = jnp.mean(xc * xc, axis=-1, keepdims=True)
    return xc * lax.rsqrt(var + eps) * g + b


def _rms_norm(x, g, eps=1e-6):
    return x * lax.rsqrt(jnp.mean(x * x, axis=-1, keepdims=True) + eps) * g


def _sigmoid(x):
    return 1.0 / (1.0 + jnp.exp(-x))


def _silu(x):
    return x * _sigmoid(x)


def _split_bf16(x):
    hi = x.astype(BF16)
    lo = (x - hi.astype(F32)).astype(BF16)
    return hi, lo


def _ln_kernel(x_ref, g_ref, b_ref, o32_ref, o16_ref):
    y = _layer_norm(x_ref[...], g_ref[...], b_ref[...])
    o32_ref[...] = y
    o16_ref[...] = y.astype(BF16)


def _ln_call(x, g, b):
    t, d = x.shape
    row = pl.BlockSpec((TM_LN, d), lambda i: (i, 0))
    vec = pl.BlockSpec((1, d), lambda i: (0, 0))
    return pl.pallas_call(
        _ln_kernel,
        grid=(t // TM_LN,),
        in_specs=[row, vec, vec],
        out_specs=[row, row],
        out_shape=[jax.ShapeDtypeStruct((t, d), F32), jax.ShapeDtypeStruct((t, d), BF16)],
        compiler_params=_cparams("parallel"),
        name="ln_in",
    )(x, g.reshape(1, d), b.reshape(1, d))


def _proj_kernel(x_ref, w_ref, o_ref):
    o_ref[...] = _dot(x_ref[...], w_ref[...]).astype(o_ref.dtype)


def _proj_call(h16, w, layer):
    t, k = h16.shape
    n = w.shape[2]
    return pl.pallas_call(
        _proj_kernel,
        grid=(t // TM_PROJ, n // TN_PROJ),
        in_specs=[pl.BlockSpec((TM_PROJ, k), lambda i, j: (i, 0)),
                  pl.BlockSpec((None, k, TN_PROJ), lambda i, j: (layer, 0, j))],
        out_specs=pl.BlockSpec((TM_PROJ, TN_PROJ), lambda i, j: (i, j)),
        out_shape=jax.ShapeDtypeStruct((t, n), BF16),
        compiler_params=_cparams("parallel", "arbitrary"),
        name="in_proj",
    )(h16, w)


def _prep_w_in(w_in):
    gq, gk = w_in[..., 0:512], w_in[..., 512:1024]
    gv, gr = w_in[..., 1024:2048], w_in[..., 2048:3072]
    gfr = w_in[..., 3072:3088]
    sgz = w_in[..., 3088:5136]
    mq = w_in[..., 5136:5520]
    ckv = w_in[..., 5520:5776]
    kr = w_in[..., 5776:5840]
    gates = w_in[..., 5840:8912]
    half = MLA_ROPE // 2
    kr_rot = jnp.concatenate([-kr[..., half:], kr[..., :half]], axis=-1)
    z = lambda n: jnp.zeros(w_in.shape[:-1] + (n,), w_in.dtype)
    small = jnp.concatenate([mq, gfr, z(112), ckv, kr, kr_rot, z(128)], axis=-1)
    return jnp.concatenate([small, gates, sgz, gv, gr, gq, gk], axis=-1).astype(BF16)


def _gla_kernel(q_ref, k_ref, v_ref, gr_ref, gf_ref, fgw_ref, fgb_ref, g_ref, tri_ref, blk_ref,
                o_ref, st_ref, qd_ref, ki_ref, ke_ref, dec_ref):
    rows = q_ref.shape[0]

    @pl.when(pl.program_id(1) == 0)
    def _():
        st_ref[...] = jnp.zeros_like(st_ref)

    pre = _dot(gf_ref[...], fgw_ref[...]) + fgb_ref[...]
    glog = -(jnp.maximum(-pre, 0.0) + jnp.log1p(jnp.exp(-jnp.abs(pre)))) * (1.0 / GLA_TAU)

    hi, lo = _split_bf16(glog)
    bcum = _dot(tri_ref[...], hi) + _dot(tri_ref[...], lo)
    btot = _dot(blk_ref[...], hi) + _dot(blk_ref[...], lo)

    q = q_ref[...].astype(F32) * (GLA_DK ** -0.5)
    k = k_ref[...].astype(F32)
    qd_ref[...] = (q * jnp.exp(bcum)).astype(BF16)
    ki_ref[...] = (k * jnp.exp(-bcum)).astype(BF16)
    ke_ref[...] = (k * jnp.exp(btot - bcum)).astype(BF16)
    dec_ref[...] = jnp.exp(btot)

    tr = lax.broadcasted_iota(jnp.int32, (CHUNK, CHUNK), 0)
    tc = lax.broadcasted_iota(jnp.int32, (CHUNK, CHUNK), 1)
    causal = tc <= tr

    def chunk_body(ci, carry):
        r0 = pl.multiple_of(ci * CHUNK, CHUNK)
        rs = pl.ds(r0, CHUNK)
        for hd in range(GLA_HEADS):
            ks = slice(hd * GLA_DK, (hd + 1) * GLA_DK)
            vs = slice(hd * GLA_DV, (hd + 1) * GLA_DV)
            qd = qd_ref[rs, ks]
            ki = ki_ref[rs, ks]
            ke = ke_ref[rs, ks]
            v = v_ref[rs, vs]
            att = jnp.where(causal, _dot_nt(qd, ki), 0.0)
            st = st_ref[hd]
            o = _dot(att.astype(BF16), v) + _dot_nt(qd, st.astype(BF16))
            dec = dec_ref[pl.ds(r0, 1), ks]
            st_ref[hd] = st * dec + _dot_tn(v, ke)
            on = _rms_norm(o, g_ref[:, vs])
            gate = gr_ref[rs, vs].astype(F32)
            o_ref[rs, vs] = (on * _silu(gate)).astype(o_ref.dtype)
        return carry

    lax.fori_loop(0, rows // CHUNK, chunk_body, 0, unroll=True)


def _gla_call(proj, fg_w, fg_b, gla_g, batch, seq):
    t = proj.shape[0]
    nr = seq // R_GLA
    row = lambda i, j: i * nr + j
    fgw = jnp.zeros((LANES, GLA_HEADS * GLA_DK), F32).at[:GLA_GATE_RANK].set(fg_w).astype(BF16)
    hk = GLA_HEADS * GLA_DK
    hv = GLA_HEADS * GLA_DV
    ridx = jnp.arange(R_GLA, dtype=jnp.int32)
    same = (ridx[:, None] // CHUNK) == (ridx[None, :] // CHUNK)
    blk = same.astype(BF16)
    tri = (same & (ridx[None, :] <= ridx[:, None])).astype(BF16)
    return pl.pallas_call(
        _gla_kernel,
        grid=(batch, nr),
        in_specs=[
            pl.BlockSpec((R_GLA, hk), lambda i, j: (row(i, j), P_GQ // hk)),
            pl.BlockSpec((R_GLA, hk), lambda i, j: (row(i, j), P_GK // hk)),
            pl.BlockSpec((R_GLA, hv), lambda i, j: (row(i, j), P_GV // hv)),
            pl.BlockSpec((R_GLA, hv), lambda i, j: (row(i, j), P_GR // hv)),
            pl.BlockSpec((R_GLA, LANES), lambda i, j: (row(i, j), P_GFR // LANES)),
            pl.BlockSpec((LANES, hk), lambda i, j: (0, 0)),
            pl.BlockSpec((1, hk), lambda i, j: (0, 0)),
            pl.BlockSpec((1, hv), lambda i, j: (0, 0)),
            pl.BlockSpec((R_GLA, R_GLA), lambda i, j: (0, 0)),
            pl.BlockSpec((R_GLA, R_GLA), lambda i, j: (0, 0)),
        ],
        out_specs=pl.BlockSpec((R_GLA, hv), lambda i, j: (row(i, j), 0)),
        out_shape=jax.ShapeDtypeStruct((t, hv), BF16),
        scratch_shapes=[
            pltpu.VMEM((GLA_HEADS, GLA_DV, GLA_DK), F32),
            pltpu.VMEM((R_GLA, hk), BF16),
            pltpu.VMEM((R_GLA, hk), BF16),
            pltpu.VMEM((R_GLA, hk), BF16),
            pltpu.VMEM((R_GLA, hk), F32),
        ],
        compiler_params=_cparams("parallel", "arbitrary"),
        name="gla",
    )(proj, proj, proj, proj, proj, fgw, fg_b.reshape(1, hk), gla_g.reshape(1, hv), tri, blk)


def _gelu(x):
    return 0.5 * x * (1.0 + lax.erf(x * (2.0 ** -0.5)))


def _sg_kernel(u_ref, v_ref, g_ref, b_ref, ws_ref, bias_ref, o_ref):
    rows = u_ref.shape[0]
    gw = SG_WIDTH // SG_GROUPS
    u = _gelu(u_ref[...].astype(F32))
    v = _gelu(v_ref[...].astype(F32))
    vn = _layer_norm(v, g_ref[...], b_ref[...]).astype(BF16)
    tr = lax.broadcasted_iota(jnp.int32, (SG_BLOCK, SG_BLOCK), 0)
    tc = lax.broadcasted_iota(jnp.int32, (SG_BLOCK, SG_BLOCK), 1)
    tril = tc <= tr
    for g in range(SG_GROUPS):
        w = jnp.where(tril, ws_ref[g], 0.0).astype(BF16)
        cs = slice(g * gw, (g + 1) * gw)
        for blk in range(rows // SG_BLOCK):
            rs = slice(blk * SG_BLOCK, (blk + 1) * SG_BLOCK)
            sv = _dot(w, vn[rs, cs]) + bias_ref[:, cs]
            o_ref[rs, cs] = (u[rs, cs] * sv).astype(o_ref.dtype)


def _sg_call(proj, sg_g, sg_bn, sg_w, sg_b):
    t = proj.shape[0]
    gw = SG_WIDTH // SG_GROUPS
    bias = jnp.repeat(sg_b.T, gw, axis=1)
    return pl.pallas_call(
        _sg_kernel,
        grid=(t // TM_SG,),
        in_specs=[
            pl.BlockSpec((TM_SG, SG_WIDTH), lambda i: (i, P_SG // SG_WIDTH)),
            pl.BlockSpec((TM_SG, SG_WIDTH), lambda i: (i, P_SG // SG_WIDTH + 1)),
            pl.BlockSpec((1, SG_WIDTH), lambda i: (0, 0)),
            pl.BlockSpec((1, SG_WIDTH), lambda i: (0, 0)),
            pl.BlockSpec((SG_GROUPS, SG_BLOCK, SG_BLOCK), lambda i: (0, 0, 0)),
            pl.BlockSpec((SG_BLOCK, SG_WIDTH), lambda i: (0, 0)),
        ],
        out_specs=pl.BlockSpec((TM_SG, SG_WIDTH), lambda i: (i, 0)),
        out_shape=jax.ShapeDtypeStruct((t, SG_WIDTH), BF16),
        compiler_params=_cparams("parallel"),
        name="spatial_gating",
    )(proj, proj, sg_g.reshape(1, -1), sg_bn.reshape(1, -1), sg_w, bias)


def _mla_prep_kernel(sm_ref, cs_ref, qg_ref, kg_ref, wq_ref, wkv_ref, q_ref, k_ref, vt_ref):
    rows = sm_ref.shape[0]
    mq = sm_ref[:, 0:MLA_Q_RANK].astype(F32)
    qn = _rms_norm(mq, qg_ref[...]).astype(BF16)
    qf = _dot(qn, wq_ref[...])
    ckv = sm_ref[:, P_CKV:P_CKV + MLA_KV_RANK].astype(F32)
    cn = _rms_norm(ckv, kg_ref[...]).astype(BF16)
    kv = _dot(cn, wkv_ref[...])

    cs = cs_ref[...]
    lane = lax.broadcasted_iota(jnp.int32, (rows, LANES), 1)
    low = lane < MLA_ROPE

    def rope(pair):
        t = pair * cs
        return jnp.where(low, t + pltpu.roll(t, MLA_ROPE, 1), 0.0)

    scale = (MLA_QK ** -0.5) * LOG2_E
    kr = rope(sm_ref[:, P_KROPE:P_KROPE + LANES].astype(F32)).astype(BF16)
    for h in range(MLA_HEADS):
        c0 = h * MLA_HEAD_PAD
        q_ref[:, c0:c0 + MLA_NOPE] = (qf[:, c0:c0 + MLA_NOPE] * scale).astype(BF16)
        q_ref[:, c0 + MLA_NOPE:c0 + MLA_HEAD_PAD] = (
            rope(qf[:, c0 + MLA_NOPE:c0 + MLA_HEAD_PAD]) * scale).astype(BF16)
        k_ref[:, c0:c0 + MLA_NOPE] = kv[:, h * MLA_NOPE:(h + 1) * MLA_NOPE].astype(BF16)
        k_ref[:, c0 + MLA_NOPE:c0 + MLA_HEAD_PAD] = kr
    vt = kv[:, MLA_HEADS * MLA_NOPE:].T.astype(BF16)
    ones = jnp.ones((MLA_V_EXT - MLA_V, TK_ATT), BF16)
    for blk in range(rows // TK_ATT):
        ks = slice(blk * TK_ATT, (blk + 1) * TK_ATT)
        for h in range(MLA_HEADS):
            r0 = h * MLA_V_EXT
            vt_ref[blk, r0:r0 + MLA_V, :] = vt[h * MLA_V:(h + 1) * MLA_V, ks]
            vt_ref[blk, r0 + MLA_V:r0 + MLA_V_EXT, :] = ones


def _prep_mla_weights(w_uq, w_ukv):
    depth = w_uq.shape[0]
    half = MLA_ROPE // 2
    wq = w_uq.reshape(depth, MLA_Q_RANK, MLA_HEADS, MLA_QK)
    wr = wq[..., MLA_NOPE:]
    wr_rot = jnp.concatenate([-wr[..., half:], wr[..., :half]], axis=-1)
    wq_ext = jnp.concatenate([wq[..., :MLA_NOPE], wr, wr_rot], axis=-1)
    wq_ext = wq_ext.reshape(depth, MLA_Q_RANK, MLA_HEADS * MLA_HEAD_PAD).astype(BF16)
    wkv = w_ukv.reshape(depth, MLA_KV_RANK, MLA_HEADS, MLA_NOPE + MLA_V)
    wkv = jnp.concatenate([wkv[..., :MLA_NOPE].reshape(depth, MLA_KV_RANK, -1),
                           wkv[..., MLA_NOPE:].reshape(depth, MLA_KV_RANK, -1)], axis=-1).astype(BF16)
    return wq_ext, wkv


def _mla_prep_call(proj, qn_g, wq_ext, kvn_g, wkv, seq, layer):
    t = proj.shape[0]
    half = MLA_ROPE // 2
    inv = ROPE_THETA ** (-jnp.arange(half, dtype=F32) / half)
    ang = jnp.arange(seq, dtype=F32)[:, None] * inv
    cs = jnp.concatenate([jnp.cos(ang), jnp.cos(ang), jnp.sin(ang), jnp.sin(ang)], axis=1)
    ns = seq // TM_MLA
    hq = MLA_HEADS * MLA_HEAD_PAD
    hv = MLA_HEADS * MLA_V_EXT
    return pl.pallas_call(
        _mla_prep_kernel,
        grid=(t // TM_MLA,),
        in_specs=[
            pl.BlockSpec((TM_MLA, P_SMALL), lambda i: (i, 0)),
            pl.BlockSpec((TM_MLA, LANES), lambda i: (i % ns, 0)),
            pl.BlockSpec((1, MLA_Q_RANK), lambda i: (0, 0)),
            pl.BlockSpec((1, MLA_KV_RANK), lambda i: (0, 0)),
            pl.BlockSpec((None, MLA_Q_RANK, hq), lambda i: (layer, 0, 0)),
            pl.BlockSpec((None, MLA_KV_RANK, MLA_HEADS * (MLA_NOPE + MLA_V)), lambda i: (layer, 0, 0)),
        ],
        out_specs=[
            pl.BlockSpec((TM_MLA, hq), lambda i: (i, 0)),
            pl.BlockSpec((TM_MLA, hq), lambda i: (i, 0)),
            pl.BlockSpec((TM_MLA // TK_ATT, hv, TK_ATT), lambda i: (i, 0, 0)),
        ],
        out_shape=[jax.ShapeDtypeStruct((t, hq), BF16), jax.ShapeDtypeStruct((t, hq), BF16),
                   jax.ShapeDtypeStruct((t // TK_ATT, hv, TK_ATT), BF16)],
        compiler_params=_cparams("parallel"),
        name="mla_prep",
    )(proj, cs, qn_g.reshape(1, -1), kvn_g.reshape(1, -1), wq_ext, wkv)


def _attn_kernel(q_ref, k_ref, vt_ref, o_ref, acc_ref, sa_ref, sb_ref):
    tq = q_ref.shape[0]
    i = pl.program_id(2)
    acc_ref[...] = jnp.zeros_like(acc_ref)

    def scores(j, s_ref):
        r0 = pl.multiple_of(j * TK_ATT, TK_ATT)
        for hh in range(HEADS_ATT):
            qs = slice(hh * MLA_HEAD_PAD, (hh + 1) * MLA_HEAD_PAD)
            s_ref[hh] = _dot_nt(k_ref[pl.ds(r0, TK_ATT), qs], q_ref[:, qs])

    def consume(j, s_ref, ms, mask):
        new_m = []
        for hh in range(HEADS_ATT):
            s = s_ref[hh]
            if mask is not None:
                s = jnp.where(mask, s, -jnp.inf)
            m_new = jnp.maximum(ms[hh], jnp.max(s, axis=0, keepdims=True))
            alpha = jnp.exp2(ms[hh] - m_new)
            p = jnp.exp2(s - m_new).astype(BF16)
            vt = vt_ref[j, hh * MLA_V_EXT:(hh + 1) * MLA_V_EXT, :]
            acc_ref[hh] = alpha * acc_ref[hh] + _dot(vt, p)
            new_m.append(m_new)
        return tuple(new_m)

    def pair(jj, ms):
        j = 2 * jj
        scores(j + 1, sb_ref)
        ms = consume(j, sa_ref, ms, None)
        scores(j + 2, sa_ref)
        return consume(j + 1, sb_ref, ms, None)

    scores(0, sa_ref)
    m0 = tuple(jnp.full((1, tq), -jnp.inf, F32) for _ in range(HEADS_ATT))
    ms = lax.fori_loop(0, i * (tq // (2 * TK_ATT)), pair, m0)
    jd = i * (tq // TK_ATT)
    scores(jd + 1, sb_ref)
    kk = lax.shift_right_logical(lax.broadcasted_iota(jnp.int32, (TK_ATT, tq), 0), 6)
    qq = lax.shift_right_logical(lax.broadcasted_iota(jnp.int32, (TK_ATT, tq), 1), 6)
    ms = consume(jd, sa_ref, ms, kk <= qq)
    ms = consume(jd + 1, sb_ref, ms, kk + (TK_ATT // CHUNK) <= qq)
    for hh in range(HEADS_ATT):
        acc = acc_ref[hh]
        o = acc[:MLA_V, :] / acc[MLA_V:MLA_V + 1, :]
        o_ref[:, hh * MLA_V:(hh + 1) * MLA_V] = o.T.astype(o_ref.dtype)


def _attn_call(q, k, vt, batch, seq):
    t = q.shape[0]
    nq = seq // TQ_ATT
    qw = HEADS_ATT * MLA_HEAD_PAD
    vw = HEADS_ATT * MLA_V_EXT
    s_buf = pltpu.VMEM((HEADS_ATT, TK_ATT, TQ_ATT), F32)
    return pl.pallas_call(
        _attn_kernel,
        grid=(batch, MLA_HEADS // HEADS_ATT, nq),
        in_specs=[
            pl.BlockSpec((TQ_ATT, qw), lambda b, h, i: (b * nq + i, h)),
            pl.BlockSpec((seq, qw), lambda b, h, i: (b, h)),
            pl.BlockSpec((seq // TK_ATT, vw, TK_ATT), lambda b, h, i: (b, h, 0)),
        ],
        out_specs=pl.BlockSpec((TQ_ATT, HEADS_ATT * MLA_V), lambda b, h, i: (b * nq + i, h)),
        out_shape=jax.ShapeDtypeStruct((t, MLA_HEADS * MLA_V), BF16),
        scratch_shapes=[pltpu.VMEM((HEADS_ATT, MLA_V_EXT, TQ_ATT), F32), s_buf, s_buf],
        compiler_params=_cparams("parallel", "parallel", "arbitrary"),
        name="mla_attention",
    )(q, k, vt)


def _merge_kernel(oa_ref, ob_ref, oc_ref, g0_ref, g1_ref, g2_ref, h_ref, wb_ref, wo_ref,
                  lg_ref, lb_ref, o32_ref, o16_ref):
    merged = _sigmoid(g0_ref[...].astype(F32)) * _dot(oa_ref[...], wb_ref[0])
    merged += _sigmoid(g1_ref[...].astype(F32)) * _dot(ob_ref[...], wb_ref[1])
    merged += _sigmoid(g2_ref[...].astype(F32)) * _dot(oc_ref[...], wb_ref[2])
    mix = _dot(merged.astype(BF16), wo_ref[...])
    y = _layer_norm(ALPHA * h_ref[...] + mix, lg_ref[...], lb_ref[...])
    o32_ref[...] = y
    o16_ref[...] = y.astype(BF16)


def _merge_call(o_a, o_b, o_c, proj, h32, w_branch, w_out, ln_g, ln_b, layer):
    t, d = h32.shape
    row = pl.BlockSpec((TM_MERGE, d), lambda i: (i, 0))
    gate = lambda n: pl.BlockSpec((TM_MERGE, d), lambda i: (i, P_GATES // d + n))
    vec = pl.BlockSpec((1, d), lambda i: (0, 0))
    return pl.pallas_call(
        _merge_kernel,
        grid=(t // TM_MERGE,),
        in_specs=[row, row, row, gate(0), gate(1), gate(2), row,
                  pl.BlockSpec((None, N_BRANCHES, d, d), lambda i: (layer, 0, 0, 0)),
                  pl.BlockSpec((None, d, d), lambda i: (layer, 0, 0)), vec, vec],
        out_specs=[row, row],
        out_shape=[jax.ShapeDtypeStruct((t, d), F32), jax.ShapeDtypeStruct((t, d), BF16)],
        compiler_params=_cparams("parallel"),
        name="merge",
    )(o_a, o_b, o_c, proj, proj, proj, h32, w_branch, w_out, ln_g.reshape(1, d), ln_b.reshape(1, d))


def _first_argmax_mask(vals, iota, n):
    m = jnp.max(vals, axis=0, keepdims=True)
    idx = jnp.min(jnp.where(vals == m, iota, n), axis=0, keepdims=True)
    return iota == idx


def _router_kernel(h_ref, wt_ref, b_ref, comb_ref, rank_ref, cnt_ref):
    tm = h_ref.shape[0]
    h = h_ref[...]
    h_hi, h_lo = _split_bf16(h)
    w = wt_ref[...]
    w_hi, w_lo = _split_bf16(w)
    logits = _dot_nt(w_hi, h_hi) + _dot_nt(w_hi, h_lo) + _dot_nt(w_lo, h_hi)
    scores = _sigmoid(logits)
    biased = scores + b_ref[...]

    neg = -jnp.inf
    sub = lax.broadcasted_iota(jnp.int32, (GROUP_SIZE, tm), 0)
    grp_rows = []
    for g in range(N_GROUPS):
        blk = biased[g * GROUP_SIZE:(g + 1) * GROUP_SIZE, :]
        m1 = jnp.max(blk, axis=0, keepdims=True)
        first = _first_argmax_mask(blk, sub, GROUP_SIZE)
        m2 = jnp.max(jnp.where(first, neg, blk), axis=0, keepdims=True)
        grp_rows.append(m1 + m2)
    gs = jnp.concatenate(grp_rows, axis=0)
    gsel = jnp.zeros((N_GROUPS, tm), jnp.bool_)
    gi = lax.broadcasted_iota(jnp.int32, (N_GROUPS, tm), 0)
    for _ in range(TOPK_GROUPS):
        pick = _first_argmax_mask(gs, gi, N_GROUPS)
        gsel = gsel | pick
        gs = jnp.where(pick, neg, gs)
    emask = jnp.concatenate(
        [jnp.broadcast_to(gsel[g:g + 1, :], (GROUP_SIZE, tm)) for g in range(N_GROUPS)], axis=0)
    cand = jnp.where(emask, biased, neg)
    ei = lax.broadcasted_iota(jnp.int32, (N_EXPERTS, tm), 0)
    chosen = jnp.zeros((N_EXPERTS, tm), jnp.bool_)
    for _ in range(TOP_K):
        pick = _first_argmax_mask(cand, ei, N_EXPERTS)
        chosen = chosen | pick
        cand = jnp.where(pick, neg, cand)
    wsel = jnp.where(chosen, scores, 0.0)
    comb_ref[...] = wsel / jnp.sum(wsel, axis=0, keepdims=True) * ROUTED_SCALE

    r = lax.broadcasted_iota(jnp.int32, (tm, tm), 0)
    c = lax.broadcasted_iota(jnp.int32, (tm, tm), 1)
    upper = jnp.where(r < c, 1.0, 0.0).astype(BF16)
    sel = jnp.where(chosen, 1.0, 0.0)
    rank_ref[...] = _dot(sel.astype(BF16), upper)
    cnt = jnp.sum(sel, axis=1, keepdims=True)
    cnt_ref[...] = jnp.broadcast_to(cnt, (N_EXPERTS, LANES)).astype(jnp.int32)


def _router_call(h32, router_w, router_b):
    t, d = h32.shape
    nt = t // TM_MOE
    bias = jnp.broadcast_to(router_b.reshape(N_EXPERTS, 1), (N_EXPERTS, TM_MOE))
    comb, rank, cnt = pl.pallas_call(
        _router_kernel,
        grid=(nt,),
        in_specs=[pl.BlockSpec((TM_MOE, d), lambda i: (i, 0)),
                  pl.BlockSpec((N_EXPERTS, d), lambda i: (0, 0)),
                  pl.BlockSpec((N_EXPERTS, TM_MOE), lambda i: (0, 0))],
        out_specs=[pl.BlockSpec((N_EXPERTS, TM_MOE), lambda i: (0, i)),
                   pl.BlockSpec((N_EXPERTS, TM_MOE), lambda i: (0, i)),
                   pl.BlockSpec((N_EXPERTS, LANES), lambda i: (i, 0))],
        out_shape=[jax.ShapeDtypeStruct((N_EXPERTS, t), F32),
                   jax.ShapeDtypeStruct((N_EXPERTS, t), F32),
                   jax.ShapeDtypeStruct((nt * N_EXPERTS, LANES), jnp.int32)],
        compiler_params=_cparams("parallel"),
        name="router",
    )(h32, router_w.T, bias)
    return comb, rank, cnt[:, 0].reshape(nt, N_EXPERTS)


def _moe_plan(cnt, t):
    nt = cnt.shape[0]
    pad = (cnt + SEG_ALIGN - 1) // SEG_ALIGN * SEG_ALIGN
    lseg = jnp.cumsum(pad, axis=1) - pad
    ltot = jnp.sum(pad, axis=1)
    etot = jnp.sum(pad, axis=0)
    region = (etot + ROW_BLK - 1) // ROW_BLK * ROW_BLK
    rend = jnp.cumsum(region)
    gpos = (rend - region)[None, :] + jnp.cumsum(pad, axis=0) - pad
    nblk = (rend[-1] // ROW_BLK).astype(jnp.int32).reshape(1)
    blk_start = jnp.arange(_moe_blocks(t), dtype=jnp.int32) * ROW_BLK
    blk_expert = jnp.minimum(jnp.sum(rend[None, :] <= blk_start[:, None], axis=1), N_EXPERTS - 1)
    nbig = pad // BIG_PIECE
    nsmall = (pad % BIG_PIECE) // SEG_ALIGN
    npiece = jnp.stack([jnp.sum(nbig, axis=1), jnp.sum(nsmall, axis=1)], axis=1)

    def piece_list(count, max_pieces, piece_rows, first_row):
        run = jnp.cumsum(count, axis=1)
        j = jnp.arange(max_pieces, dtype=jnp.int32)[None, :, None]
        owner = jnp.sum(run[:, None, :] <= j, axis=2)
        onehot = owner[:, :, None] == jnp.arange(N_EXPERTS, dtype=jnp.int32)[None, None, :]
        pick = lambda a: jnp.sum(jnp.where(onehot, a[:, None, :], 0), axis=2)
        within = (j[:, :, 0] - pick(run - count)) * piece_rows
        return pick(lseg + first_row) + within, pick(gpos + first_row) + within

    max_big = (TM_MOE * TOP_K + N_EXPERTS * (SEG_ALIGN - 1)) // BIG_PIECE
    max_small = N_EXPERTS * (BIG_PIECE // SEG_ALIGN - 1)
    bsrc, bdst = piece_list(nbig, max_big, BIG_PIECE, jnp.zeros_like(pad))
    ssrc, sdst = piece_list(nsmall, max_small, SEG_ALIGN, nbig * BIG_PIECE)
    i32 = lambda a: a.astype(jnp.int32)
    pieces = (i32(npiece), i32(bsrc), i32(bdst), i32(ssrc), i32(sdst))
    return i32(pad), i32(lseg), i32(ltot), pieces, nblk, i32(blk_expert)


def _moe_blocks(t):
    nt = t // TM_MOE
    rows = t * TOP_K + nt * N_EXPERTS * (SEG_ALIGN - 1) + N_EXPERTS * (ROW_BLK - 1)
    return -(-rows // ROW_BLK)


def _zero_uncovered_blocks(ref2d, ntot):
    for cb in range(LOCAL_ROWS // SORT_BLK):
        @pl.when((cb + 1) * SORT_BLK > ntot)
        def _():
            ref2d[cb * SORT_BLK:(cb + 1) * SORT_BLK, :] = jnp.zeros((SORT_BLK, ref2d.shape[1]),
                                                                   ref2d.dtype)


def _build_slot_matrices(p_ref, pw_ref, comb_ref, rank_ref, pad_ref, lseg_ref, ntot, i):
    tm = p_ref.shape[1]
    _zero_uncovered_blocks(p_ref, ntot)
    _zero_uncovered_blocks(pw_ref, ntot)
    rowi = lax.broadcasted_iota(jnp.int32, (SLOT_CHUNK, tm), 0).astype(F32)

    def expert_body(e, carry):
        rrow = rank_ref[pl.ds(e, 1), :]
        wrow = comb_ref[pl.ds(e, 1), :]
        base = lseg_ref[i, e]
        nch = lax.shift_right_logical(pad_ref[i, e] + (SLOT_CHUNK - 1), SLOT_CHUNK_LOG2)

        def chunk_body(c, carry2):
            off = c * SLOT_CHUNK
            hit = (rrow == rowi + off.astype(F32)) & (wrow > 0.0)
            rs = pl.ds(pl.multiple_of(base + off, SEG_ALIGN), SLOT_CHUNK)
            p_ref[rs, :] = jnp.where(hit, 1.0, 0.0).astype(BF16)
            pw_ref[rs, :] = jnp.where(hit, wrow, 0.0).astype(BF16)
            return carry2

        lax.fori_loop(0, nch, chunk_body, 0)
        return carry

    lax.fori_loop(0, N_EXPERTS, expert_body, 0, unroll=2)


def _segment_copies(piece_refs, tile, make_copy):
    npiece_ref, bsrc_ref, bdst_ref, ssrc_ref, sdst_ref = piece_refs
    for col, rows, src_ref, dst_ref in ((0, BIG_PIECE, bsrc_ref, bdst_ref),
                                        (1, SEG_ALIGN, ssrc_ref, sdst_ref)):
        def piece(j, carry, rows=rows, src_ref=src_ref, dst_ref=dst_ref):
            make_copy(pl.multiple_of(src_ref[tile, j], SEG_ALIGN),
                      pl.multiple_of(dst_ref[tile, j], SEG_ALIGN), rows).start()
            return carry

        lax.fori_loop(0, npiece_ref[tile, col], piece, 0)


def _wait_copies(piece_refs, tile, make_copy):
    npiece_ref = piece_refs[0]
    for col, rows in ((0, BIG_PIECE), (1, SEG_ALIGN)):
        def piece(c, carry, rows=rows):
            make_copy(0, 0, rows).wait()
            return carry

        lax.fori_loop(0, npiece_ref[tile, col], piece, 0)


def _dispatch_kernel(pad_ref, lseg_ref, ltot_ref, npiece_ref, bsrc_ref, bdst_ref, ssrc_ref, sdst_ref,
                     x_ref, comb_ref, rank_ref, xs_hbm, pw_ref, p_ref, xs_ref, sem):
    piece_refs = (npiece_ref, bsrc_ref, bdst_ref, ssrc_ref, sdst_ref)
    i = pl.program_id(0)
    last = pl.num_programs(0) - 1
    buf = lax.rem(i, 2)
    ntot = ltot_ref[i]
    _build_slot_matrices(p_ref, pw_ref, comb_ref, rank_ref, pad_ref, lseg_ref, ntot, i)
    for cb in range(LOCAL_ROWS // SORT_BLK):
        @pl.when(cb * SORT_BLK < ntot)
        def _():
            rs = slice(cb * SORT_BLK, (cb + 1) * SORT_BLK)
            xs_ref[buf, rs, :] = _dot(p_ref[rs, :], x_ref[...]).astype(BF16)

    def copy_for(b):
        return lambda l0, g0, rows: pltpu.make_async_copy(
            xs_ref.at[b, pl.ds(l0, rows), :], xs_hbm.at[pl.ds(g0, rows), :], sem.at[b])

    _segment_copies(piece_refs, i, copy_for(buf))

    @pl.when(i > 0)
    def _():
        _wait_copies(piece_refs, i - 1, copy_for(1 - buf))

    @pl.when(i == last)
    def _():
        _wait_copies(piece_refs, i, copy_for(buf))


def _dispatch_call(h16, comb, rank, plan):
    t, d = h16.shape
    nt = t // TM_MOE
    pad, lseg, ltot, pieces, _, _ = plan
    grid_spec = pltpu.PrefetchScalarGridSpec(
        num_scalar_prefetch=3 + len(pieces),
        grid=(nt,),
        in_specs=[
            pl.BlockSpec((TM_MOE, d), lambda i, *_: (i, 0)),
            pl.BlockSpec((N_EXPERTS, TM_MOE), lambda i, *_: (0, i)),
            pl.BlockSpec((N_EXPERTS, TM_MOE), lambda i, *_: (0, i)),
        ],
        out_specs=[pl.BlockSpec(memory_space=pl.ANY),
                   pl.BlockSpec((LOCAL_ROWS, TM_MOE), lambda i, *_: (i, 0))],
        scratch_shapes=[
            pltpu.VMEM((LOCAL_ROWS, TM_MOE), BF16),
            pltpu.VMEM((2, LOCAL_ROWS, d), BF16),
            pltpu.SemaphoreType.DMA((2,)),
        ],
    )
    return pl.pallas_call(
        _dispatch_kernel,
        grid_spec=grid_spec,
        out_shape=[jax.ShapeDtypeStruct((_moe_blocks(t) * ROW_BLK, d), BF16),
                   jax.ShapeDtypeStruct((nt * LOCAL_ROWS, TM_MOE), BF16)],
        compiler_params=_cparams("arbitrary"),
        name="moe_dispatch",
    )(pad, lseg, ltot, *pieces, h16, comb, rank)


def _expert_ffn_kernel(nblk_ref, be_ref, x_ref, wg_ref, wu_ref, wd_ref, o_ref):
    @pl.when(pl.program_id(0) < nblk_ref[0])
    def _():
        wg = wg_ref[0, 0].astype(BF16)
        wu = wu_ref[0, 0].astype(BF16)
        wd = wd_ref[0, 0].astype(BF16)
        sub = ROW_BLK // FFN_CHAINS
        for c in range(FFN_CHAINS):
            rs = slice(c * sub, (c + 1) * sub)
            x = x_ref[rs, :]
            hmid = (_silu(_dot(x, wg)) * _dot(x, wu)).astype(BF16)
            o_ref[rs, :] = _dot(hmid, wd).astype(o_ref.dtype)


def _expert_ffn_call(xs, plan, w_gate, w_up, w_down, layer):
    rows, d = xs.shape
    nblk, blk_expert = plan[-2:]
    live = lambda b, nblk, be: jnp.minimum(b, nblk[0] - 1)
    wmap = lambda b, nblk, be: (layer, be[live(b, nblk, be)], 0, 0)
    grid_spec = pltpu.PrefetchScalarGridSpec(
        num_scalar_prefetch=2,
        grid=(rows // ROW_BLK,),
        in_specs=[
            pl.BlockSpec((ROW_BLK, d), lambda b, nblk, be: (live(b, nblk, be), 0)),
            pl.BlockSpec((1, 1, d, D_EXPERT), wmap),
            pl.BlockSpec((1, 1, d, D_EXPERT), wmap),
            pl.BlockSpec((1, 1, D_EXPERT, d), wmap),
        ],
        out_specs=pl.BlockSpec((ROW_BLK, d), lambda b, nblk, be: (live(b, nblk, be), 0)),
    )
    return pl.pallas_call(
        _expert_ffn_kernel,
        grid_spec=grid_spec,
        out_shape=jax.ShapeDtypeStruct((rows, d), BF16),
        compiler_params=_cparams("arbitrary"),
        name="moe_expert_ffn",
    )(nblk, blk_expert, xs, w_gate, w_up, w_down)


def _combine_kernel(ltot_ref, npiece_ref, bsrc_ref, bdst_ref, ssrc_ref, sdst_ref, ys_hbm, pw_ref,
                    h16_ref, h32_ref, p_ref, swgu_ref, swd_ref, pwi_ref, pwg_ref, lg_ref, lb_ref,
                    o32_ref, o16_ref, ys_ref, sem):
    piece_refs = (npiece_ref, bsrc_ref, bdst_ref, ssrc_ref, sdst_ref)
    i = pl.program_id(0)
    last = pl.num_programs(0) - 1
    buf = lax.rem(i, 2)

    def copy_for(b):
        return lambda l0, g0, rows: pltpu.make_async_copy(
            ys_hbm.at[pl.ds(g0, rows), :], ys_ref.at[b, pl.ds(l0, rows), :], sem.at[b])

    def fetch(tile, b):
        _zero_uncovered_blocks(ys_ref.at[b], ltot_ref[tile])
        _segment_copies(piece_refs, tile, copy_for(b))

    @pl.when(i == 0)
    def _():
        fetch(i, buf)

    @pl.when(i < last)
    def _():
        fetch(i + 1, 1 - buf)

    x = h16_ref[...]
    gu = _dot(x, swgu_ref[...])
    shared = _dot((_silu(gu[:, :D_SHARED]) * gu[:, D_SHARED:]).astype(BF16), swd_ref[...])
    ple = _dot(p_ref[...].astype(BF16), pwi_ref[...]) * _sigmoid(_dot(x, pwg_ref[...]))
    z = ALPHA * h32_ref[...] + shared + ple

    _wait_copies(piece_refs, i, copy_for(buf))
    z += _dot_tn(pw_ref[...], ys_ref[buf])
    y = _layer_norm(z, lg_ref[...], lb_ref[...])
    o32_ref[...] = y
    o16_ref[...] = y.astype(BF16)


def _combine_call(ys, pw, plan, h16, h32, p, swgu, sw_down, ple_w_in, ple_w_gate, ln_g, ln_b, layer):
    t, d = h32.shape
    nt = t // TM_MOE
    _, _, ltot, pieces, _, _ = plan
    row = pl.BlockSpec((TM_MOE, d), lambda i, *_: (i, 0))
    vec = pl.BlockSpec((1, d), lambda i, *_: (0, 0))
    full = lambda a, b: pl.BlockSpec((None, a, b), lambda i, *_: (layer, 0, 0),
                                     pipeline_mode=pl.Buffered(1))
    grid_spec = pltpu.PrefetchScalarGridSpec(
        num_scalar_prefetch=1 + len(pieces),
        grid=(nt,),
        in_specs=[pl.BlockSpec(memory_space=pl.ANY),
                  pl.BlockSpec((LOCAL_ROWS, TM_MOE), lambda i, *_: (i, 0)), row, row,
                  pl.BlockSpec((TM_MOE, PLE_DIM), lambda i, *_: (layer * nt + i, 0)),
                  full(d, 2 * D_SHARED), full(D_SHARED, d), full(PLE_DIM, d), full(d, d), vec, vec],
        out_specs=[row, row],
        scratch_shapes=[
            pltpu.VMEM((2, LOCAL_ROWS, d), BF16),
            pltpu.SemaphoreType.DMA((2,)),
        ],
    )
    return pl.pallas_call(
        _combine_kernel,
        grid_spec=grid_spec,
        out_shape=[jax.ShapeDtypeStruct((t, d), F32), jax.ShapeDtypeStruct((t, d), BF16)],
        compiler_params=_cparams("arbitrary"),
        name="moe_combine_tail",
    )(ltot, *pieces, ys, pw, h16, h32, p, swgu, sw_down, ple_w_in, ple_w_gate,
      ln_g.reshape(1, d), ln_b.reshape(1, d))


def kernel(x, p, ln_in_g, ln_in_b, w_in, gla_fg_w, gla_fg_b, gla_norm_g, sg_norm_g, sg_norm_b, sg_w, sg_b, mla_qn_g, mla_w_uq, mla_kvn_g, mla_w_ukv, w_branch, w_out, ln1_g, ln1_b, router_w, router_b, exp_w_gate, exp_w_up, exp_w_down, sh_w_gate, sh_w_up, sh_w_down, ple_w_in, ple_w_gate, ln2_g, ln2_b):
    batch, seq, d = x.shape
    t = batch * seq
    depth = w_in.shape[0]
    h32, h16 = _ln_call(x.reshape(t, d), ln_in_g, ln_in_b)
    w_proj = _prep_w_in(w_in)
    w_branch16 = w_branch.astype(BF16)
    w_out16 = w_out.astype(BF16)
    sh_wgu16 = jnp.concatenate([sh_w_gate, sh_w_up], axis=-1).astype(BF16)
    sh_wd16 = sh_w_down.astype(BF16)
    ple_wi16 = ple_w_in.astype(BF16)
    ple_wg16 = ple_w_gate.astype(BF16)
    wq_ext, wkv = _prep_mla_weights(mla_w_uq, mla_w_ukv)
    for i in range(depth):
        proj = _proj_call(h16, w_proj, i)
        o_a = _gla_call(proj, gla_fg_w[i], gla_fg_b[i], gla_norm_g[i], batch, seq)
        o_b = _sg_call(proj, sg_norm_g[i], sg_norm_b[i], sg_w[i], sg_b[i])
        q, k, vt = _mla_prep_call(proj, mla_qn_g[i], wq_ext, mla_kvn_g[i], wkv, seq, i)
        o_c = _attn_call(q, k, vt, batch, seq)
        h32, h16 = _merge_call(o_a, o_b, o_c, proj, h32, w_branch16, w_out16, ln1_g[i], ln1_b[i], i)
        comb, rank, cnt = _router_call(h32, router_w[i], router_b[i])
        plan = _moe_plan(cnt, t)
        xs, pw = _dispatch_call(h16, comb, rank, plan)
        ys = _expert_ffn_call(xs, plan, exp_w_gate, exp_w_up, exp_w_down, i)
        h32, h16 = _combine_call(ys, pw, plan, h16, h32, p.reshape(depth * t, -1), sh_wgu16,
                                 sh_wd16, ple_wi16, ple_wg16, ln2_g[i], ln2_b[i], i)
    return h32.reshape(batch, seq, d)
```

```python
import functools

import jax
import jax.numpy as jnp
from jax import lax
from jax.experimental import pallas as pl
from jax.experimental.pallas import tpu as pltpu

F32 = jnp.float32
BF16 = jnp.bfloat16

D_MODEL = 1024
DEPTH = 2
CHUNK = 64
GLA_HEADS, GLA_DK, GLA_DV, GLA_GATE_RANK, GLA_TAU = 4, 128, 256, 16, 16.0
SG_WIDTH, SG_GROUPS, SG_BLOCK = 1024, 4, 128
MLA_HEADS, MLA_Q_RANK, MLA_KV_RANK = 8, 384, 256
MLA_NOPE, MLA_ROPE, MLA_V = 128, 64, 128
MLA_QK = MLA_NOPE + MLA_ROPE
ROPE_THETA = 10000.0
N_BRANCHES = 3
N_EXPERTS, N_GROUPS, TOPK_GROUPS, TOP_K = 64, 8, 4, 8
GROUP_SIZE = N_EXPERTS // N_GROUPS
D_EXPERT, D_SHARED = 256, 256
ROUTED_SCALE = 2.5
PLE_DIM = 256
ALPHA = (2 * DEPTH) ** 0.25

LANES = 128
SUBLANES = 8
VMEM_PHYSICAL_BYTES = 64 * 1024 * 1024
VMEM_LIMIT_BYTES = VMEM_PHYSICAL_BYTES - 4 * 1024 * 1024

P_SMALL = 1024
P_GFR = 384
P_CKV = 512
P_KROPE = 768
P_GATES = 1024
P_SG = 4096
P_GV = 6144
P_GR = 7168
P_GQ = 8192
P_GK = 8704
P_TOTAL = 9216
MLA_HEAD_PAD = 256

TM_LN = 512
TM_PROJ, TN_PROJ = 1024, 1536
R_GLA = 512
TM_SG = 256
TQ_ATT = 512
TK_ATT = 256
TM_MLA = TQ_ATT
MLA_V_EXT = MLA_V + 16
HEADS_ATT = 4
LOG2_E = 1.4426950408889634
TM_MERGE = 512
TM_MOE = 512
SEG_ALIGN_LOG2 = 4
SEG_ALIGN = 1 << SEG_ALIGN_LOG2
SLOT_CHUNK_LOG2 = 6
SLOT_CHUNK = 1 << SLOT_CHUNK_LOG2
SORT_BLK = 512
ROW_BLK = 1024
FFN_CHAINS = 2
BIG_PIECE_LOG2 = 6
BIG_PIECE = 1 << BIG_PIECE_LOG2
LOCAL_ROWS = -(-(TM_MOE * TOP_K + N_EXPERTS * (SEG_ALIGN - 1) + SLOT_CHUNK) // SORT_BLK) * SORT_BLK


def _cparams(*sem):
    return pltpu.CompilerParams(dimension_semantics=sem, vmem_limit_bytes=VMEM_LIMIT_BYTES)


def _dot(a, b):
    return jnp.dot(a, b, preferred_element_type=F32)


def _dot_nt(a, b):
    return lax.dot_general(a, b, (((1,), (1,)), ((), ())), preferred_element_type=F32)


def _dot_tn(a, b):
    return lax.dot_general(a, b, (((0,), (0,)), ((), ())), preferred_element_type=F32)


def _layer_norm(x, g, b, eps=1e-5):
    mu = jnp.mean(x, axis=-1, keepdims=True)
    xc = x - mu
    var = jnp.mean(xc * xc, axis=-1, keepdims=True)
    return xc * lax.rsqrt(var + eps) * g + b


def _rms_norm(x, g, eps=1e-6):
    return x * lax.rsqrt(jnp.mean(x * x, axis=-1, keepdims=True) + eps) * g


def _sigmoid(x):
    return 1.0 / (1.0 + jnp.exp(-x))


def _silu(x):
    return x * _sigmoid(x)


def _split_bf16(x):
    hi = x.astype(BF16)
    lo = (x - hi.astype(F32)).astype(BF16)
    return hi, lo


def _ln_kernel(x_ref, g_ref, b_ref, o32_ref, o16_ref):
    y = _layer_norm(x_ref[...], g_ref[...], b_ref[...])
    o32_ref[...] = y
    o16_ref[...] = y.astype(BF16)


def _ln_call(x, g, b):
    t, d = x.shape
    row = pl.BlockSpec((TM_LN, d), lambda i: (i, 0))
    vec = pl.BlockSpec((1, d), lambda i: (0, 0))
    return pl.pallas_call(
        _ln_kernel,
        grid=(t // TM_LN,),
        in_specs=[row, vec, vec],
        out_specs=[row, row],
        out_shape=[jax.ShapeDtypeStruct((t, d), F32), jax.ShapeDtypeStruct((t, d), BF16)],
        compiler_params=_cparams("parallel"),
        name="ln_in",
    )(x, g.reshape(1, d), b.reshape(1, d))


def _proj_kernel(x_ref, w_ref, o_ref):
    o_ref[...] = _dot(x_ref[...], w_ref[...]).astype(o_ref.dtype)


def _proj_call(h16, w, layer):
    t, k = h16.shape
    n = w.shape[2]
    return pl.pallas_call(
        _proj_kernel,
        grid=(t // TM_PROJ, n // TN_PROJ),
        in_specs=[pl.BlockSpec((TM_PROJ, k), lambda i, j: (i, 0)),
                  pl.BlockSpec((None, k, TN_PROJ), lambda i, j: (layer, 0, j))],
        out_specs=pl.BlockSpec((TM_PROJ, TN_PROJ), lambda i, j: (i, j)),
        out_shape=jax.ShapeDtypeStruct((t, n), BF16),
        compiler_params=_cparams("parallel", "arbitrary"),
        name="in_proj",
    )(h16, w)


def _prep_w_in(w_in):
    gq, gk = w_in[..., 0:512], w_in[..., 512:1024]
    gv, gr = w_in[..., 1024:2048], w_in[..., 2048:3072]
    gfr = w_in[..., 3072:3088]
    sgz = w_in[..., 3088:5136]
    mq = w_in[..., 5136:5520]
    ckv = w_in[..., 5520:5776]
    kr = w_in[..., 5776:5840]
    gates = w_in[..., 5840:8912]
    half = MLA_ROPE // 2
    kr_rot = jnp.concatenate([-kr[..., half:], kr[..., :half]], axis=-1)
    z = lambda n: jnp.zeros(w_in.shape[:-1] + (n,), w_in.dtype)
    small = jnp.concatenate([mq, gfr, z(112), ckv, kr, kr_rot, z(128)], axis=-1)
    return jnp.concatenate([small, gates, sgz, gv, gr, gq, gk], axis=-1).astype(BF16)


def _gla_kernel(q_ref, k_ref, v_ref, gr_ref, gf_ref, fgw_ref, fgb_ref, g_ref, tri_ref, blk_ref,
                o_ref, st_ref, qd_ref, ki_ref, ke_ref, dec_ref):
    rows = q_ref.shape[0]

    @pl.when(pl.program_id(1) == 0)
    def _():
        st_ref[...] = jnp.zeros_like(st_ref)

    pre = _dot(gf_ref[...], fgw_ref[...]) + fgb_ref[...]
    glog = -(jnp.maximum(-pre, 0.0) + jnp.log1p(jnp.exp(-jnp.abs(pre)))) * (1.0 / GLA_TAU)

    hi, lo = _split_bf16(glog)
    bcum = _dot(tri_ref[...], hi) + _dot(tri_ref[...], lo)
    btot = _dot(blk_ref[...], hi) + _dot(blk_ref[...], lo)

    q = q_ref[...].astype(F32) * (GLA_DK ** -0.5)
    k = k_ref[...].astype(F32)
    qd_ref[...] = (q * jnp.exp(bcum)).astype(BF16)
    ki_ref[...] = (k * jnp.exp(-bcum)).astype(BF16)
    ke_ref[...] = (k * jnp.exp(btot - bcum)).astype(BF16)
    dec_ref[...] = jnp.exp(btot)

    tr = lax.broadcasted_iota(jnp.int32, (CHUNK, CHUNK), 0)
    tc = lax.broadcasted_iota(jnp.int32, (CHUNK, CHUNK), 1)
    causal = tc <= tr

    def chunk_body(ci, carry):
        r0 = pl.multiple_of(ci * CHUNK, CHUNK)
        rs = pl.ds(r0, CHUNK)
        for hd in range(GLA_HEADS):
            ks = slice(hd * GLA_DK, (hd + 1) * GLA_DK)
            vs = slice(hd * GLA_DV, (hd + 1) * GLA_DV)
            qd = qd_ref[rs, ks]
            ki = ki_ref[rs, ks]
            ke = ke_ref[rs, ks]
            v = v_ref[rs, vs]
            att = jnp.where(causal, _dot_nt(qd, ki), 0.0)
            st = st_ref[hd]
            o = _dot(att.astype(BF16), v) + _dot_nt(qd, st.astype(BF16))
            dec = dec_ref[pl.ds(r0, 1), ks]
            st_ref[hd] = st * dec + _dot_tn(v, ke)
            on = _rms_norm(o, g_ref[:, vs])
            gate = gr_ref[rs, vs].astype(F32)
            o_ref[rs, vs] = (on * _silu(gate)).astype(o_ref.dtype)
        return carry

    lax.fori_loop(0, rows // CHUNK, chunk_body, 0, unroll=True)


def _gla_call(proj, fg_w, fg_b, gla_g, batch, seq):
    t = proj.shape[0]
    nr = seq // R_GLA
    row = lambda i, j: i * nr + j
    fgw = jnp.zeros((LANES, GLA_HEADS * GLA_DK), F32).at[:GLA_GATE_RANK].set(fg_w).astype(BF16)
    hk = GLA_HEADS * GLA_DK
    hv = GLA_HEADS * GLA_DV
    ridx = jnp.arange(R_GLA, dtype=jnp.int32)
    same = (ridx[:, None] // CHUNK) == (ridx[None, :] // CHUNK)
    blk = same.astype(BF16)
    tri = (same & (ridx[None, :] <= ridx[:, None])).astype(BF16)
    return pl.pallas_call(
        _gla_kernel,
        grid=(batch, nr),
        in_specs=[
            pl.BlockSpec((R_GLA, hk), lambda i, j: (row(i, j), P_GQ // hk)),
            pl.BlockSpec((R_GLA, hk), lambda i, j: (row(i, j), P_GK // hk)),
            pl.BlockSpec((R_GLA, hv), lambda i, j: (row(i, j), P_GV // hv)),
            pl.BlockSpec((R_GLA, hv), lambda i, j: (row(i, j), P_GR // hv)),
            pl.BlockSpec((R_GLA, LANES), lambda i, j: (row(i, j), P_GFR // LANES)),
            pl.BlockSpec((LANES, hk), lambda i, j: (0, 0)),
            pl.BlockSpec((1, hk), lambda i, j: (0, 0)),
            pl.BlockSpec((1, hv), lambda i, j: (0, 0)),
            pl.BlockSpec((R_GLA, R_GLA), lambda i, j: (0, 0)),
            pl.BlockSpec((R_GLA, R_GLA), lambda i, j: (0, 0)),
        ],
        out_specs=pl.BlockSpec((R_GLA, hv), lambda i, j: (row(i, j), 0)),
        out_shape=jax.ShapeDtypeStruct((t, hv), BF16),
        scratch_shapes=[
            pltpu.VMEM((GLA_HEADS, GLA_DV, GLA_DK), F32),
            pltpu.VMEM((R_GLA, hk), BF16),
            pltpu.VMEM((R_GLA, hk), BF16),
            pltpu.VMEM((R_GLA, hk), BF16),
            pltpu.VMEM((R_GLA, hk), F32),
        ],
        compiler_params=_cparams("parallel", "arbitrary"),
        name="gla",
    )(proj, proj, proj, proj, proj, fgw, fg_b.reshape(1, hk), gla_g.reshape(1, hv), tri, blk)


def _gelu(x):
    return 0.5 * x * (1.0 + lax.erf(x * (2.0 ** -0.5)))


def _sg_kernel(u_ref, v_ref, g_ref, b_ref, ws_ref, bias_ref, o_ref):
    rows = u_ref.shape[0]
    gw = SG_WIDTH // SG_GROUPS
    u = _gelu(u_ref[...].astype(F32))
    v = _gelu(v_ref[...].astype(F32))
    vn = _layer_norm(v, g_ref[...], b_ref[...]).astype(BF16)
    tr = lax.broadcasted_iota(jnp.int32, (SG_BLOCK, SG_BLOCK), 0)
    tc = lax.broadcasted_iota(jnp.int32, (SG_BLOCK, SG_BLOCK), 1)
    tril = tc <= tr
    for g in range(SG_GROUPS):
        w = jnp.where(tril, ws_ref[g], 0.0).astype(BF16)
        cs = slice(g * gw, (g + 1) * gw)
        for blk in range(rows // SG_BLOCK):
            rs = slice(blk * SG_BLOCK, (blk + 1) * SG_BLOCK)
            sv = _dot(w, vn[rs, cs]) + bias_ref[:, cs]
            o_ref[rs, cs] = (u[rs, cs] * sv).astype(o_ref.dtype)


def _sg_call(proj, sg_g, sg_bn, sg_w, sg_b):
    t = proj.shape[0]
    gw = SG_WIDTH // SG_GROUPS
    bias = jnp.repeat(sg_b.T, gw, axis=1)
    return pl.pallas_call(
        _sg_kernel,
        grid=(t // TM_SG,),
        in_specs=[
            pl.BlockSpec((TM_SG, SG_WIDTH), lambda i: (i, P_SG // SG_WIDTH)),
            pl.BlockSpec((TM_SG, SG_WIDTH), lambda i: (i, P_SG // SG_WIDTH + 1)),
            pl.BlockSpec((1, SG_WIDTH), lambda i: (0, 0)),
            pl.BlockSpec((1, SG_WIDTH), lambda i: (0, 0)),
            pl.BlockSpec((SG_GROUPS, SG_BLOCK, SG_BLOCK), lambda i: (0, 0, 0)),
            pl.BlockSpec((SG_BLOCK, SG_WIDTH), lambda i: (0, 0)),
        ],
        out_specs=pl.BlockSpec((TM_SG, SG_WIDTH), lambda i: (i, 0)),
        out_shape=jax.ShapeDtypeStruct((t, SG_WIDTH), BF16),
        compiler_params=_cparams("parallel"),
        name="spatial_gating",
    )(proj, proj, sg_g.reshape(1, -1), sg_bn.reshape(1, -1), sg_w, bias)


def _mla_prep_kernel(sm_ref, cs_ref, qg_ref, kg_ref, wq_ref, wkv_ref, q_ref, k_ref, vt_ref):
    rows = sm_ref.shape[0]
    mq = sm_ref[:, 0:MLA_Q_RANK].astype(F32)
    qn = _rms_norm(mq, qg_ref[...]).astype(BF16)
    qf = _dot(qn, wq_ref[...])
    ckv = sm_ref[:, P_CKV:P_CKV + MLA_KV_RANK].astype(F32)
    cn = _rms_norm(ckv, kg_ref[...]).astype(BF16)
    kv = _dot(cn, wkv_ref[...])

    cs = cs_ref[...]
    lane = lax.broadcasted_iota(jnp.int32, (rows, LANES), 1)
    low = lane < MLA_ROPE

    def rope(pair):
        t = pair * cs
        return jnp.where(low, t + pltpu.roll(t, MLA_ROPE, 1), 0.0)

    scale = (MLA_QK ** -0.5) * LOG2_E
    kr = rope(sm_ref[:, P_KROPE:P_KROPE + LANES].astype(F32)).astype(BF16)
    for h in range(MLA_HEADS):
        c0 = h * MLA_HEAD_PAD
        q_ref[:, c0:c0 + MLA_NOPE] = (qf[:, c0:c0 + MLA_NOPE] * scale).astype(BF16)
        q_ref[:, c0 + MLA_NOPE:c0 + MLA_HEAD_PAD] = (
            rope(qf[:, c0 + MLA_NOPE:c0 + MLA_HEAD_PAD]) * scale).astype(BF16)
        k_ref[:, c0:c0 + MLA_NOPE] = kv[:, h * MLA_NOPE:(h + 1) * MLA_NOPE].astype(BF16)
        k_ref[:, c0 + MLA_NOPE:c0 + MLA_HEAD_PAD] = kr
    vt = kv[:, MLA_HEADS * MLA_NOPE:].T.astype(BF16)
    ones = jnp.ones((MLA_V_EXT - MLA_V, TK_ATT), BF16)
    for blk in range(rows // TK_ATT):
        ks = slice(blk * TK_ATT, (blk + 1) * TK_ATT)
        for h in range(MLA_HEADS):
            r0 = h * MLA_V_EXT
            vt_ref[blk, r0:r0 + MLA_V, :] = vt[h * MLA_V:(h + 1) * MLA_V, ks]
            vt_ref[blk, r0 + MLA_V:r0 + MLA_V_EXT, :] = ones


def _prep_mla_weights(w_uq, w_ukv):
    depth = w_uq.shape[0]
    half = MLA_ROPE // 2
    wq = w_uq.reshape(depth, MLA_Q_RANK, MLA_HEADS, MLA_QK)
    wr = wq[..., MLA_NOPE:]
    wr_rot = jnp.concatenate([-wr[..., half:], wr[..., :half]], axis=-1)
    wq_ext = jnp.concatenate([wq[..., :MLA_NOPE], wr, wr_rot], axis=-1)
    wq_ext = wq_ext.reshape(depth, MLA_Q_RANK, MLA_HEADS * MLA_HEAD_PAD).astype(BF16)
    wkv = w_ukv.reshape(depth, MLA_KV_RANK, MLA_HEADS, MLA_NOPE + MLA_V)
    wkv = jnp.concatenate([wkv[..., :MLA_NOPE].reshape(depth, MLA_KV_RANK, -1),
                           wkv[..., MLA_NOPE:].reshape(depth, MLA_KV_RANK, -1)], axis=-1).astype(BF16)
    return wq_ext, wkv


def _mla_prep_call(proj, qn_g, wq_ext, kvn_g, wkv, seq, layer):
    t = proj.shape[0]
    half = MLA_ROPE // 2
    inv = ROPE_THETA ** (-jnp.arange(half, dtype=F32) / half)
    ang = jnp.arange(seq, dtype=F32)[:, None] * inv
    cs = jnp.concatenate([jnp.cos(ang), jnp.cos(ang), jnp.sin(ang), jnp.sin(ang)], axis=1)
    ns = seq // TM_MLA
    hq = MLA_HEADS * MLA_HEAD_PAD
    hv = MLA_HEADS * MLA_V_EXT
    return pl.pallas_call(
        _mla_prep_kernel,
        grid=(t // TM_MLA,),
        in_specs=[
            pl.BlockSpec((TM_MLA, P_SMALL), lambda i: (i, 0)),
            pl.BlockSpec((TM_MLA, LANES), lambda i: (i % ns, 0)),
            pl.BlockSpec((1, MLA_Q_RANK), lambda i: (0, 0)),
            pl.BlockSpec((1, MLA_KV_RANK), lambda i: (0, 0)),
            pl.BlockSpec((None, MLA_Q_RANK, hq), lambda i: (layer, 0, 0)),
            pl.BlockSpec((None, MLA_KV_RANK, MLA_HEADS * (MLA_NOPE + MLA_V)), lambda i: (layer, 0, 0)),
        ],
        out_specs=[
            pl.BlockSpec((TM_MLA, hq), lambda i: (i, 0)),
            pl.BlockSpec((TM_MLA, hq), lambda i: (i, 0)),
            pl.BlockSpec((TM_MLA // TK_ATT, hv, TK_ATT), lambda i: (i, 0, 0)),
        ],
        out_shape=[jax.ShapeDtypeStruct((t, hq), BF16), jax.ShapeDtypeStruct((t, hq), BF16),
                   jax.ShapeDtypeStruct((t // TK_ATT, hv, TK_ATT), BF16)],
        compiler_params=_cparams("parallel"),
        name="mla_prep",
    )(proj, cs, qn_g.reshape(1, -1), kvn_g.reshape(1, -1), wq_ext, wkv)


def _attn_kernel(q_ref, k_ref, vt_ref, o_ref, acc_ref, sa_ref, sb_ref):
    tq = q_ref.shape[0]
    i = pl.program_id(2)
    acc_ref[...] = jnp.zeros_like(acc_ref)

    def scores(j, s_ref):
        r0 = pl.multiple_of(j * TK_ATT, TK_ATT)
        for hh in range(HEADS_ATT):
            qs = slice(hh * MLA_HEAD_PAD, (hh + 1) * MLA_HEAD_PAD)
            s_ref[hh] = _dot_nt(k_ref[pl.ds(r0, TK_ATT), qs], q_ref[:, qs])

    def consume(j, s_ref, ms, mask):
        new_m = []
        for hh in range(HEADS_ATT):
            s = s_ref[hh]
            if mask is not None:
                s = jnp.where(mask, s, -jnp.inf)
            m_new = jnp.maximum(ms[hh], jnp.max(s, axis=0, keepdims=True))
            alpha = jnp.exp2(ms[hh] - m_new)
            p = jnp.exp2(s - m_new).astype(BF16)
            vt = vt_ref[j, hh * MLA_V_EXT:(hh + 1) * MLA_V_EXT, :]
            acc_ref[hh] = alpha * acc_ref[hh] + _dot(vt, p)
            new_m.append(m_new)
        return tuple(new_m)

    def pair(jj, ms):
        j = 2 * jj
        scores(j + 1, sb_ref)
        ms = consume(j, sa_ref, ms, None)
        scores(j + 2, sa_ref)
        return consume(j + 1, sb_ref, ms, None)

    scores(0, sa_ref)
    m0 = tuple(jnp.full((1, tq), -jnp.inf, F32) for _ in range(HEADS_ATT))
    ms = lax.fori_loop(0, i * (tq // (2 * TK_ATT)), pair, m0)
    jd = i * (tq // TK_ATT)
    scores(jd + 1, sb_ref)
    kk = lax.shift_right_logical(lax.broadcasted_iota(jnp.int32, (TK_ATT, tq), 0), 6)
    qq = lax.shift_right_logical(lax.broadcasted_iota(jnp.int32, (TK_ATT, tq), 1), 6)
    ms = consume(jd, sa_ref, ms, kk <= qq)
    ms = consume(jd + 1, sb_ref, ms, kk + (TK_ATT // CHUNK) <= qq)
    for hh in range(HEADS_ATT):
        acc = acc_ref[hh]
        o = acc[:MLA_V, :] / acc[MLA_V:MLA_V + 1, :]
        o_ref[:, hh * MLA_V:(hh + 1) * MLA_V] = o.T.astype(o_ref.dtype)


def _attn_call(q, k, vt, batch, seq):
    t = q.shape[0]
    nq = seq // TQ_ATT
    qw = HEADS_ATT * MLA_HEAD_PAD
    vw = HEADS_ATT * MLA_V_EXT
    s_buf = pltpu.VMEM((HEADS_ATT, TK_ATT, TQ_ATT), F32)
    return pl.pallas_call(
        _attn_kernel,
        grid=(batch, MLA_HEADS // HEADS_ATT, nq),
        in_specs=[
            pl.BlockSpec((TQ_ATT, qw), lambda b, h, i: (b * nq + i, h)),
            pl.BlockSpec((seq, qw), lambda b, h, i: (b, h)),
            pl.BlockSpec((seq // TK_ATT, vw, TK_ATT), lambda b, h, i: (b, h, 0)),
        ],
        out_specs=pl.BlockSpec((TQ_ATT, HEADS_ATT * MLA_V), lambda b, h, i: (b * nq + i, h)),
        out_shape=jax.ShapeDtypeStruct((t, MLA_HEADS * MLA_V), BF16),
        scratch_shapes=[pltpu.VMEM((HEADS_ATT, MLA_V_EXT, TQ_ATT), F32), s_buf, s_buf],
        compiler_params=_cparams("parallel", "parallel", "arbitrary"),
        name="mla_attention",
    )(q, k, vt)


def _merge_kernel(oa_ref, ob_ref, oc_ref, g0_ref, g1_ref, g2_ref, h_ref, wb_ref, wo_ref,
                  lg_ref, lb_ref, o32_ref, o16_ref):
    merged = _sigmoid(g0_ref[...].astype(F32)) * _dot(oa_ref[...], wb_ref[0])
    merged += _sigmoid(g1_ref[...].astype(F32)) * _dot(ob_ref[...], wb_ref[1])
    merged += _sigmoid(g2_ref[...].astype(F32)) * _dot(oc_ref[...], wb_ref[2])
    mix = _dot(merged.astype(BF16), wo_ref[...])
    y = _layer_norm(ALPHA * h_ref[...] + mix, lg_ref[...], lb_ref[...])
    o32_ref[...] = y
    o16_ref[...] = y.astype(BF16)


def _merge_call(o_a, o_b, o_c, proj, h32, w_branch, w_out, ln_g, ln_b, layer):
    t, d = h32.shape
    row = pl.BlockSpec((TM_MERGE, d), lambda i: (i, 0))
    gate = lambda n: pl.BlockSpec((TM_MERGE, d), lambda i: (i, P_GATES // d + n))
    vec = pl.BlockSpec((1, d), lambda i: (0, 0))
    return pl.pallas_call(
        _merge_kernel,
        grid=(t // TM_MERGE,),
        in_specs=[row, row, row, gate(0), gate(1), gate(2), row,
                  pl.BlockSpec((None, N_BRANCHES, d, d), lambda i: (layer, 0, 0, 0)),
                  pl.BlockSpec((None, d, d), lambda i: (layer, 0, 0)), vec, vec],
        out_specs=[row, row],
        out_shape=[jax.ShapeDtypeStruct((t, d), F32), jax.ShapeDtypeStruct((t, d), BF16)],
        compiler_params=_cparams("parallel"),
        name="merge",
    )(o_a, o_b, o_c, proj, proj, proj, h32, w_branch, w_out, ln_g.reshape(1, d), ln_b.reshape(1, d))


def _first_argmax_mask(vals, iota, n):
    m = jnp.max(vals, axis=0, keepdims=True)
    idx = jnp.min(jnp.where(vals == m, iota, n), axis=0, keepdims=True)
    return iota == idx


def _router_kernel(h_ref, wt_ref, b_ref, comb_ref, rank_ref, cnt_ref):
    tm = h_ref.shape[0]
    h = h_ref[...]
    h_hi, h_lo = _split_bf16(h)
    w = wt_ref[...]
    w_hi, w_lo = _split_bf16(w)
    logits = _dot_nt(w_hi, h_hi) + _dot_nt(w_hi, h_lo) + _dot_nt(w_lo, h_hi)
    scores = _sigmoid(logits)
    biased = scores + b_ref[...]

    neg = -jnp.inf
    sub = lax.broadcasted_iota(jnp.int32, (GROUP_SIZE, tm), 0)
    grp_rows = []
    for g in range(N_GROUPS):
        blk = biased[g * GROUP_SIZE:(g + 1) * GROUP_SIZE, :]
        m1 = jnp.max(blk, axis=0, keepdims=True)
        first = _first_argmax_mask(blk, sub, GROUP_SIZE)
        m2 = jnp.max(jnp.where(first, neg, blk), axis=0, keepdims=True)
        grp_rows.append(m1 + m2)
    gs = jnp.concatenate(grp_rows, axis=0)
    gsel = jnp.zeros((N_GROUPS, tm), jnp.bool_)
    gi = lax.broadcasted_iota(jnp.int32, (N_GROUPS, tm), 0)
    for _ in range(TOPK_GROUPS):
        pick = _first_argmax_mask(gs, gi, N_GROUPS)
        gsel = gsel | pick
        gs = jnp.where(pick, neg, gs)
    emask = jnp.concatenate(
        [jnp.broadcast_to(gsel[g:g + 1, :], (GROUP_SIZE, tm)) for g in range(N_GROUPS)], axis=0)
    cand = jnp.where(emask, biased, neg)
    ei = lax.broadcasted_iota(jnp.int32, (N_EXPERTS, tm), 0)
    chosen = jnp.zeros((N_EXPERTS, tm), jnp.bool_)
    for _ in range(TOP_K):
        pick = _first_argmax_mask(cand, ei, N_EXPERTS)
        chosen = chosen | pick
        cand = jnp.where(pick, neg, cand)
    wsel = jnp.where(chosen, scores, 0.0)
    comb_ref[...] = wsel / jnp.sum(wsel, axis=0, keepdims=True) * ROUTED_SCALE

    r = lax.broadcasted_iota(jnp.int32, (tm, tm), 0)
    c = lax.broadcasted_iota(jnp.int32, (tm, tm), 1)
    upper = jnp.where(r < c, 1.0, 0.0).astype(BF16)
    sel = jnp.where(chosen, 1.0, 0.0)
    rank_ref[...] = _dot(sel.astype(BF16), upper)
    cnt = jnp.sum(sel, axis=1, keepdims=True)
    cnt_ref[...] = jnp.broadcast_to(cnt, (N_EXPERTS, LANES)).astype(jnp.int32)


def _router_call(h32, router_w, router_b):
    t, d = h32.shape
    nt = t // TM_MOE
    bias = jnp.broadcast_to(router_b.reshape(N_EXPERTS, 1), (N_EXPERTS, TM_MOE))
    comb, rank, cnt = pl.pallas_call(
        _router_kernel,
        grid=(nt,),
        in_specs=[pl.BlockSpec((TM_MOE, d), lambda i: (i, 0)),
                  pl.BlockSpec((N_EXPERTS, d), lambda i: (0, 0)),
                  pl.BlockSpec((N_EXPERTS, TM_MOE), lambda i: (0, 0))],
        out_specs=[pl.BlockSpec((N_EXPERTS, TM_MOE), lambda i: (0, i)),
                   pl.BlockSpec((N_EXPERTS, TM_MOE), lambda i: (0, i)),
                   pl.BlockSpec((N_EXPERTS, LANES), lambda i: (i, 0))],
        out_shape=[jax.ShapeDtypeStruct((N_EXPERTS, t), F32),
                   jax.ShapeDtypeStruct((N_EXPERTS, t), F32),
                   jax.ShapeDtypeStruct((nt * N_EXPERTS, LANES), jnp.int32)],
        compiler_params=_cparams("parallel"),
        name="router",
    )(h32, router_w.T, bias)
    return comb, rank, cnt[:, 0].reshape(nt, N_EXPERTS)


def _moe_plan(cnt, t):
    nt = cnt.shape[0]
    pad = (cnt + SEG_ALIGN - 1) // SEG_ALIGN * SEG_ALIGN
    lseg = jnp.cumsum(pad, axis=1) - pad
    ltot = jnp.sum(pad, axis=1)
    etot = jnp.sum(pad, axis=0)
    region = (etot + ROW_BLK - 1) // ROW_BLK * ROW_BLK
    rend = jnp.cumsum(region)
    gpos = (rend - region)[None, :] + jnp.cumsum(pad, axis=0) - pad
    nblk = (rend[-1] // ROW_BLK).astype(jnp.int32).reshape(1)
    blk_start = jnp.arange(_moe_blocks(t), dtype=jnp.int32) * ROW_BLK
    blk_expert = jnp.minimum(jnp.sum(rend[None, :] <= blk_start[:, None], axis=1), N_EXPERTS - 1)
    nbig = pad // BIG_PIECE
    nsmall = (pad % BIG_PIECE) // SEG_ALIGN
    npiece = jnp.stack([jnp.sum(nbig, axis=1), jnp.sum(nsmall, axis=1)], axis=1)

    def piece_list(count, max_pieces, piece_rows, first_row):
        run = jnp.cumsum(count, axis=1)
        j = jnp.arange(max_pieces, dtype=jnp.int32)[None, :, None]
        owner = jnp.sum(run[:, None, :] <= j, axis=2)
        onehot = owner[:, :, None] == jnp.arange(N_EXPERTS, dtype=jnp.int32)[None, None, :]
        pick = lambda a: jnp.sum(jnp.where(onehot, a[:, None, :], 0), axis=2)
        within = (j[:, :, 0] - pick(run - count)) * piece_rows
        return pick(lseg + first_row) + within, pick(gpos + first_row) + within

    max_big = (TM_MOE * TOP_K + N_EXPERTS * (SEG_ALIGN - 1)) // BIG_PIECE
    max_small = N_EXPERTS * (BIG_PIECE // SEG_ALIGN - 1)
    bsrc, bdst = piece_list(nbig, max_big, BIG_PIECE, jnp.zeros_like(pad))
    ssrc, sdst = piece_list(nsmall, max_small, SEG_ALIGN, nbig * BIG_PIECE)
    i32 = lambda a: a.astype(jnp.int32)
    pieces = (i32(npiece), i32(bsrc), i32(bdst), i32(ssrc), i32(sdst))
    return i32(pad), i32(lseg), i32(ltot), pieces, nblk, i32(blk_expert)


def _moe_blocks(t):
    nt = t // TM_MOE
    rows = t * TOP_K + nt * N_EXPERTS * (SEG_ALIGN - 1) + N_EXPERTS * (ROW_BLK - 1)
    return -(-rows // ROW_BLK)


def _zero_uncovered_blocks(ref2d, ntot):
    for cb in range(LOCAL_ROWS // SORT_BLK):
        @pl.when((cb + 1) * SORT_BLK > ntot)
        def _():
            ref2d[cb * SORT_BLK:(cb + 1) * SORT_BLK, :] = jnp.zeros((SORT_BLK, ref2d.shape[1]),
                                                                   ref2d.dtype)


def _build_slot_matrices(p_ref, pw_ref, comb_ref, rank_ref, pad_ref, lseg_ref, ntot, i):
    tm = p_ref.shape[1]
    _zero_uncovered_blocks(p_ref, ntot)
    _zero_uncovered_blocks(pw_ref, ntot)
    rowi = lax.broadcasted_iota(jnp.int32, (SLOT_CHUNK, tm), 0).astype(F32)

    def expert_body(e, carry):
        rrow = rank_ref[pl.ds(e, 1), :]
        wrow = comb_ref[pl.ds(e, 1), :]
        base = lseg_ref[i, e]
        nch = lax.shift_right_logical(pad_ref[i, e] + (SLOT_CHUNK - 1), SLOT_CHUNK_LOG2)

        def chunk_body(c, carry2):
            off = c * SLOT_CHUNK
            hit = (rrow == rowi + off.astype(F32)) & (wrow > 0.0)
            rs = pl.ds(pl.multiple_of(base + off, SEG_ALIGN), SLOT_CHUNK)
            p_ref[rs, :] = jnp.where(hit, 1.0, 0.0).astype(BF16)
            pw_ref[rs, :] = jnp.where(hit, wrow, 0.0).astype(BF16)
            return carry2

        lax.fori_loop(0, nch, chunk_body, 0)
        return carry

    lax.fori_loop(0, N_EXPERTS, expert_body, 0, unroll=2)


def _segment_copies(piece_refs, tile, make_copy):
    npiece_ref, bsrc_ref, bdst_ref, ssrc_ref, sdst_ref = piece_refs
    for col, rows, src_ref, dst_ref in ((0, BIG_PIECE, bsrc_ref, bdst_ref),
                                        (1, SEG_ALIGN, ssrc_ref, sdst_ref)):
        def piece(j, carry, rows=rows, src_ref=src_ref, dst_ref=dst_ref):
            make_copy(pl.multiple_of(src_ref[tile, j], SEG_ALIGN),
                      pl.multiple_of(dst_ref[tile, j], SEG_ALIGN), rows).start()
            return carry

        lax.fori_loop(0, npiece_ref[tile, col], piece, 0)


def _wait_copies(piece_refs, tile, make_copy):
    npiece_ref = piece_refs[0]
    for col, rows in ((0, BIG_PIECE), (1, SEG_ALIGN)):
        def piece(c, carry, rows=rows):
            make_copy(0, 0, rows).wait()
            return carry

        lax.fori_loop(0, npiece_ref[tile, col], piece, 0)


def _dispatch_kernel(pad_ref, lseg_ref, ltot_ref, npiece_ref, bsrc_ref, bdst_ref, ssrc_ref, sdst_ref,
                     x_ref, comb_ref, rank_ref, xs_hbm, pw_ref, p_ref, xs_ref, sem):
    piece_refs = (npiece_ref, bsrc_ref, bdst_ref, ssrc_ref, sdst_ref)
    i = pl.program_id(0)
    last = pl.num_programs(0) - 1
    buf = lax.rem(i, 2)
    ntot = ltot_ref[i]
    _build_slot_matrices(p_ref, pw_ref, comb_ref, rank_ref, pad_ref, lseg_ref, ntot, i)
    for cb in range(LOCAL_ROWS // SORT_BLK):
        @pl.when(cb * SORT_BLK < ntot)
        def _():
            rs = slice(cb * SORT_BLK, (cb + 1) * SORT_BLK)
            xs_ref[buf, rs, :] = _dot(p_ref[rs, :], x_ref[...]).astype(BF16)

    def copy_for(b):
        return lambda l0, g0, rows: pltpu.make_async_copy(
            xs_ref.at[b, pl.ds(l0, rows), :], xs_hbm.at[pl.ds(g0, rows), :], sem.at[b])

    _segment_copies(piece_refs, i, copy_for(buf))

    @pl.when(i > 0)
    def _():
        _wait_copies(piece_refs, i - 1, copy_for(1 - buf))

    @pl.when(i == last)
    def _():
        _wait_copies(piece_refs, i, copy_for(buf))


def _dispatch_call(h16, comb, rank, plan):
    t, d = h16.shape
    nt = t // TM_MOE
    pad, lseg, ltot, pieces, _, _ = plan
    grid_spec = pltpu.PrefetchScalarGridSpec(
        num_scalar_prefetch=3 + len(pieces),
        grid=(nt,),
        in_specs=[
            pl.BlockSpec((TM_MOE, d), lambda i, *_: (i, 0)),
            pl.BlockSpec((N_EXPERTS, TM_MOE), lambda i, *_: (0, i)),
            pl.BlockSpec((N_EXPERTS, TM_MOE), lambda i, *_: (0, i)),
        ],
        out_specs=[pl.BlockSpec(memory_space=pl.ANY),
                   pl.BlockSpec((LOCAL_ROWS, TM_MOE), lambda i, *_: (i, 0))],
        scratch_shapes=[
            pltpu.VMEM((LOCAL_ROWS, TM_MOE), BF16),
            pltpu.VMEM((2, LOCAL_ROWS, d), BF16),
            pltpu.SemaphoreType.DMA((2,)),
        ],
    )
    return pl.pallas_call(
        _dispatch_kernel,
        grid_spec=grid_spec,
        out_shape=[jax.ShapeDtypeStruct((_moe_blocks(t) * ROW_BLK, d), BF16),
                   jax.ShapeDtypeStruct((nt * LOCAL_ROWS, TM_MOE), BF16)],
        compiler_params=_cparams("arbitrary"),
        name="moe_dispatch",
    )(pad, lseg, ltot, *pieces, h16, comb, rank)


def _expert_ffn_kernel(nblk_ref, be_ref, x_ref, wg_ref, wu_ref, wd_ref, o_ref):
    @pl.when(pl.program_id(0) < nblk_ref[0])
    def _():
        wg = wg_ref[0, 0].astype(BF16)
        wu = wu_ref[0, 0].astype(BF16)
        wd = wd_ref[0, 0].astype(BF16)
        sub = ROW_BLK // FFN_CHAINS
        for c in range(FFN_CHAINS):
            rs = slice(c * sub, (c + 1) * sub)
            x = x_ref[rs, :]
            hmid = (_silu(_dot(x, wg)) * _dot(x, wu)).astype(BF16)
            o_ref[rs, :] = _dot(hmid, wd).astype(o_ref.dtype)


def _expert_ffn_call(xs, plan, w_gate, w_up, w_down, layer):
    rows, d = xs.shape
    nblk, blk_expert = plan[-2:]
    live = lambda b, nblk, be: jnp.minimum(b, nblk[0] - 1)
    wmap = lambda b, nblk, be: (layer, be[live(b, nblk, be)], 0, 0)
    grid_spec = pltpu.PrefetchScalarGridSpec(
        num_scalar_prefetch=2,
        grid=(rows // ROW_BLK,),
        in_specs=[
            pl.BlockSpec((ROW_BLK, d), lambda b, nblk, be: (live(b, nblk, be), 0)),
            pl.BlockSpec((1, 1, d, D_EXPERT), wmap),
            pl.BlockSpec((1, 1, d, D_EXPERT), wmap),
            pl.BlockSpec((1, 1, D_EXPERT, d), wmap),
        ],
        out_specs=pl.BlockSpec((ROW_BLK, d), lambda b, nblk, be: (live(b, nblk, be), 0)),
    )
    return pl.pallas_call(
        _expert_ffn_kernel,
        grid_spec=grid_spec,
        out_shape=jax.ShapeDtypeStruct((rows, d), BF16),
        compiler_params=_cparams("arbitrary"),
        name="moe_expert_ffn",
    )(nblk, blk_expert, xs, w_gate, w_up, w_down)


def _combine_kernel(ltot_ref, npiece_ref, bsrc_ref, bdst_ref, ssrc_ref, sdst_ref, ys_hbm, pw_ref,
                    h16_ref, h32_ref, p_ref, swgu_ref, swd_ref, pwi_ref, pwg_ref, lg_ref, lb_ref,
                    o32_ref, o16_ref, ys_ref, sem):
    piece_refs = (npiece_ref, bsrc_ref, bdst_ref, ssrc_ref, sdst_ref)
    i = pl.program_id(0)
    last = pl.num_programs(0) - 1
    buf = lax.rem(i, 2)

    def copy_for(b):
        return lambda l0, g0, rows: pltpu.make_async_copy(
            ys_hbm.at[pl.ds(g0, rows), :], ys_ref.at[b, pl.ds(l0, rows), :], sem.at[b])

    def fetch(tile, b):
        _zero_uncovered_blocks(ys_ref.at[b], ltot_ref[tile])
        _segment_copies(piece_refs, tile, copy_for(b))

    @pl.when(i == 0)
    def _():
        fetch(i, buf)

    @pl.when(i < last)
    def _():
        fetch(i + 1, 1 - buf)

    x = h16_ref[...]
    gu = _dot(x, swgu_ref[...])
    shared = _dot((_silu(gu[:, :D_SHARED]) * gu[:, D_SHARED:]).astype(BF16), swd_ref[...])
    ple = _dot(p_ref[...].astype(BF16), pwi_ref[...]) * _sigmoid(_dot(x, pwg_ref[...]))
    z = ALPHA * h32_ref[...] + shared + ple

    _wait_copies(piece_refs, i, copy_for(buf))
    z += _dot_tn(pw_ref[...], ys_ref[buf])
    y = _layer_norm(z, lg_ref[...], lb_ref[...])
    o32_ref[...] = y
    o16_ref[...] = y.astype(BF16)


def _combine_call(ys, pw, plan, h16, h32, p, swgu, sw_down, ple_w_in, ple_w_gate, ln_g, ln_b, layer):
    t, d = h32.shape
    nt = t // TM_MOE
    _, _, ltot, pieces, _, _ = plan
    row = pl.BlockSpec((TM_MOE, d), lambda i, *_: (i, 0))
    vec = pl.BlockSpec((1, d), lambda i, *_: (0, 0))
    full = lambda a, b: pl.BlockSpec((None, a, b), lambda i, *_: (layer, 0, 0),
                                     pipeline_mode=pl.Buffered(1))
    grid_spec = pltpu.PrefetchScalarGridSpec(
        num_scalar_prefetch=1 + len(pieces),
        grid=(nt,),
        in_specs=[pl.BlockSpec(memory_space=pl.ANY),
                  pl.BlockSpec((LOCAL_ROWS, TM_MOE), lambda i, *_: (i, 0)), row, row,
                  pl.BlockSpec((TM_MOE, PLE_DIM), lambda i, *_: (layer * nt + i, 0)),
                  full(d, 2 * D_SHARED), full(D_SHARED, d), full(PLE_DIM, d), full(d, d), vec, vec],
        out_specs=[row, row],
        scratch_shapes=[
            pltpu.VMEM((2, LOCAL_ROWS, d), BF16),
            pltpu.SemaphoreType.DMA((2,)),
        ],
    )
    return pl.pallas_call(
        _combine_kernel,
        grid_spec=grid_spec,
        out_shape=[jax.ShapeDtypeStruct((t, d), F32), jax.ShapeDtypeStruct((t, d), BF16)],
        compiler_params=_cparams("arbitrary"),
        name="moe_combine_tail",
    )(ltot, *pieces, ys, pw, h16, h32, p, swgu, sw_down, ple_w_in, ple_w_gate,
      ln_g.reshape(1, d), ln_b.reshape(1, d))


def kernel(x, p, ln_in_g, ln_in_b, w_in, gla_fg_w, gla_fg_b, gla_norm_g, sg_norm_g, sg_norm_b, sg_w, sg_b, mla_qn_g, mla_w_uq, mla_kvn_g, mla_w_ukv, w_branch, w_out, ln1_g, ln1_b, router_w, router_b, exp_w_gate, exp_w_up, exp_w_down, sh_w_gate, sh_w_up, sh_w_down, ple_w_in, ple_w_gate, ln2_g, ln2_b):
    batch, seq, d = x.shape
    t = batch * seq
    depth = w_in.shape[0]
    h32, h16 = _ln_call(x.reshape(t, d), ln_in_g, ln_in_b)
    w_proj = _prep_w_in(w_in)
    w_branch16 = w_branch.astype(BF16)
    w_out16 = w_out.astype(BF16)
    sh_wgu16 = jnp.concatenate([sh_w_gate, sh_w_up], axis=-1).astype(BF16)
    sh_wd16 = sh_w_down.astype(BF16)
    ple_wi16 = ple_w_in.astype(BF16)
    ple_wg16 = ple_w_gate.astype(BF16)
    for i in range(depth):
        proj = _proj_call(h16, w_proj, i)
        o_a = _gla_call(proj, gla_fg_w[i], gla_fg_b[i], gla_norm_g[i], batch, seq)
        o_b = _sg_call(proj, sg_norm_g[i], sg_norm_b[i], sg_w[i], sg_b[i])
        wq_ext, wkv = _prep_mla_weights(mla_w_uq[i][None], mla_w_ukv[i][None])
        q, k, vt = _mla_prep_call(proj, mla_qn_g[i], wq_ext, mla_kvn_g[i], wkv, seq, 0)
        o_c = _attn_call(q, k, vt, batch, seq)
        h32, h16 = _merge_call(o_a, o_b, o_c, proj, h32, w_branch16, w_out16, ln1_g[i], ln1_b[i], i)
        comb, rank, cnt = _router_call(h32, router_w[i], router_b[i])
        plan = _moe_plan(cnt, t)
        xs, pw = _dispatch_call(h16, comb, rank, plan)
        ys = _expert_ffn_call(xs, plan, exp_w_gate, exp_w_up, exp_w_down, i)
        h32, h16 = _combine_call(ys, pw, plan, h16, h32, p.reshape(depth * t, -1), sh_wgu16,
                                 sh_wd16, ple_wi16, ple_wg16, ln2_g[i], ln2_b[i], i)
    return h32.reshape(batch, seq, d)
```
